```python
import math, functools
import jax, jax.numpy as jnp
from jax import lax
import numpy as np

D_MODEL = 2048
BATCH = 2
SEQ = 8192
DEPTH = 2

GRID_W = 64
CTX_LEN = 256
N_BRANCH = 4
BRANCH_WIDTH = D_MODEL // 2
NORM_EPS = 1e-6
HEAD_DIM = 128
ATT_HEADS = BRANCH_WIDTH // HEAD_DIM
ATT_KV_HEADS = ATT_HEADS // 4
ATT_BLOCK = 128
ROPE_THETA = 10000.0
SSD_HEAD_DIM = 64
SSD_HEADS = BRANCH_WIDTH // SSD_HEAD_DIM
SSD_GROUPS = 2
SSD_HPG = SSD_HEADS // SSD_GROUPS
SSD_STATE = 128
SSD_CONV = 5
SSD_CHUNK = 128
GLA_HEADS = 4
GLA_DV = BRANCH_WIDTH // GLA_HEADS
GLA_DK = GLA_DV // 2
GLA_RANK = 16
GLA_TAU = 16.0
GLA_CHUNK = 64
SC_CONV = 3

IN_WIDTHS = (
    ATT_HEADS * HEAD_DIM, ATT_KV_HEADS * HEAD_DIM, ATT_KV_HEADS * HEAD_DIM, BRANCH_WIDTH,
    BRANCH_WIDTH, BRANCH_WIDTH, SSD_GROUPS * SSD_STATE, SSD_GROUPS * SSD_STATE, 2 * SSD_HEADS,
    GLA_HEADS * GLA_DK, GLA_HEADS * GLA_DK, GLA_HEADS * GLA_DV, BRANCH_WIDTH, 2 * GLA_RANK,
    BRANCH_WIDTH, BRANCH_WIDTH, BRANCH_WIDTH, BRANCH_WIDTH,
    N_BRANCH * D_MODEL,
)

kernel_name = "hybrid_parallel_flow_block"


def rmsnorm(x, g):
    x32 = x.astype(jnp.float32)
    y = x32 * lax.rsqrt(jnp.mean(x32 * x32, axis=-1, keepdims=True) + NORM_EPS)
    return y.astype(x.dtype) * g


def dwconv(u, w):
    k = w.shape[0]
    return lax.conv_general_dilated(u, w[:, None, :].astype(u.dtype), (1,), [(k // 2, k // 2)],
                                    dimension_numbers=("NWC", "WIO", "NWC"),
                                    feature_group_count=u.shape[-1])


def axial_rope_tables(rows):
    t_row = jnp.repeat(jnp.arange(rows, dtype=jnp.float32), GRID_W)
    t_col = jnp.tile(jnp.arange(GRID_W, dtype=jnp.float32), rows)
    half = HEAD_DIM // 2
    freqs = ROPE_THETA ** (-(jnp.arange(0, half, 2, dtype=jnp.float32) / half))
    ang = jnp.concatenate([t_row[:, None] * freqs, t_col[:, None] * freqs], axis=-1)
    return jnp.cos(ang), jnp.sin(ang)


def apply_rope(x, cos, sin):
    xp = x.astype(jnp.float32).reshape(*x.shape[:-1], HEAD_DIM // 2, 2)
    xe, xo = xp[..., 0], xp[..., 1]
    c, s = cos[None, :, None, :], sin[None, :, None, :]
    out = jnp.stack([xe * c - xo * s, xe * s + xo * c], axis=-1)
    return out.reshape(x.shape).astype(x.dtype)


def _attend(qb, k, v):
    s = jnp.einsum("bqgrd,bkgd->bgrqk", qb, k).astype(jnp.float32) * (HEAD_DIM ** -0.5)
    p = jax.nn.softmax(s, axis=-1).astype(v.dtype)
    return jnp.einsum("bgrqk,bkgd->bqgrd", p, v)


def attention_branch(pc, pl, g_q, g_k, cos, sin, need_ctx):
    rep = ATT_HEADS // ATT_KV_HEADS

    def heads(p, rope):
        q, k, v, _ = p
        bsz, n = q.shape[:2]
        q = rmsnorm(q.reshape(bsz, n, ATT_HEADS, HEAD_DIM), g_q)
        k = rmsnorm(k.reshape(bsz, n, ATT_KV_HEADS, HEAD_DIM), g_k)
        if rope:
            q, k = apply_rope(q, cos, sin), apply_rope(k, cos, sin)
        return (q.reshape(bsz, n, ATT_KV_HEADS, rep, HEAD_DIM), k,
                v.reshape(bsz, n, ATT_KV_HEADS, HEAD_DIM))

    q_c, k_c, v_c = heads(pc, False)
    q_l, k_l, v_l = heads(pl, True)
    k_all = jnp.concatenate([k_c, k_l], axis=1)
    v_all = jnp.concatenate([v_c, v_l], axis=1)
    bsz, n_lat = q_l.shape[:2]
    nb = n_lat // ATT_BLOCK
    q_blocks = jnp.moveaxis(q_l.reshape(bsz, nb, ATT_BLOCK, ATT_KV_HEADS, rep, HEAD_DIM), 1, 0)
    o_l = lax.map(lambda qb: _attend(qb, k_all, v_all), q_blocks)
    o_l = jnp.moveaxis(o_l, 0, 1).reshape(bsz, n_lat, BRANCH_WIDTH) * jax.nn.silu(pl[3])
    o_c = None
    if need_ctx:
        o_c = _attend(q_c, k_c, v_c).reshape(bsz, q_c.shape[1], BRANCH_WIDTH) * jax.nn.silu(pc[3])
    return o_c, o_l


def prefix_then_latent(scan_fn, ctx_in, lat_in, h0, reverse):
    f = (lambda t: jnp.flip(t, axis=1)) if reverse else (lambda t: t)
    y_c, h_c = scan_fn(*[f(t) for t in ctx_in], h0)
    y_l, _ = scan_fn(*[f(t) for t in lat_in], h_c)
    return f(y_c), f(y_l)


def ssd_scan(x, dt, bm, cm, h0, a):
    bsz, n = x.shape[:2]
    nc = n // SSD_CHUNK
    ch = lambda t: t.reshape(bsz, nc, SSD_CHUNK, *t.shape[2:])
    x, dt, bm, cm = ch(x), ch(dt), ch(bm), ch(cm)
    cs = jnp.cumsum(dt * a, axis=2)
    tri = jnp.tril(jnp.ones((SSD_CHUNK, SSD_CHUNK), bool))[:, :, None, None]
    seg = cs[:, :, :, None] - cs[:, :, None, :]
    decay = jnp.exp(jnp.where(tri, seg, -jnp.inf))
    cb = jnp.einsum("bctgn,bcsgn->bctsg", cm, bm)
    dtx = dt[..., None] * x
    y_diag = jnp.einsum("bctsg,bctsgr,bcsgrp->bctgrp", cb, decay, dtx)
    to_end = jnp.exp(cs[:, :, -1:] - cs)
    states = jnp.einsum("bcsgn,bcsgrp->bcgrnp", bm, to_end[..., None] * dtx)
    chunk_decay = jnp.exp(cs[:, :, -1])

    def step(h, inp):
        st, dec = inp
        return dec[..., None, None] * h + st, h

    h_fin, h_in = lax.scan(step, h0, (jnp.moveaxis(states, 1, 0), jnp.moveaxis(chunk_decay, 1, 0)))
    y_off = jnp.einsum("bctgn,bcgrnp->bctgrp", cm, jnp.moveaxis(h_in, 0, 1)) * jnp.exp(cs)[..., None]
    return (y_diag + y_off).reshape(bsz, n, *x.shape[3:]), h_fin


def ssd_prep(p, conv_w, conv_b, dt_bias):
    xs, _, bm, cm, dt_raw = p
    bsz, n = xs.shape[:2]
    xbc = jax.nn.silu(dwconv(jnp.concatenate([xs, bm, cm], axis=-1), conv_w) + conv_b)
    xs, bm, cm = jnp.split(xbc, [BRANCH_WIDTH, BRANCH_WIDTH + SSD_GROUPS * SSD_STATE], axis=-1)
    xh = xs.reshape(bsz, n, SSD_GROUPS, SSD_HPG, SSD_HEAD_DIM)
    bm = bm.reshape(bsz, n, SSD_GROUPS, SSD_STATE)
    cm = cm.reshape(bsz, n, SSD_GROUPS, SSD_STATE)
    dt = jax.nn.softplus(dt_raw.astype(jnp.float32).reshape(bsz, n, 2, SSD_HEADS) + dt_bias)
    dt = dt.reshape(bsz, n, 2, SSD_GROUPS, SSD_HPG)
    return xh, bm, cm, dt[:, :, 0], dt[:, :, 1]


def ssd_branch(pc, pl, conv_w, conv_b, a_log, dt_bias, d_skip, norm_g, need_ctx):
    a = -jnp.exp(a_log.astype(jnp.float32)).reshape(2, SSD_GROUPS, SSD_HPG)
    s_c = ssd_prep(pc, conv_w, conv_b, dt_bias)
    s_l = ssd_prep(pl, conv_w, conv_b, dt_bias)
    bsz = pl[0].shape[0]
    h0 = jnp.zeros((bsz, SSD_GROUPS, SSD_HPG, SSD_STATE, SSD_HEAD_DIM), jnp.float32)
    yf_c, yf_l = prefix_then_latent(functools.partial(ssd_scan, a=a[0]),
                                    (s_c[0], s_c[3], s_c[1], s_c[2]), (s_l[0], s_l[3], s_l[1], s_l[2]), h0, False)
    yb_c, yb_l = prefix_then_latent(functools.partial(ssd_scan, a=a[1]),
                                    (s_c[0], s_c[4], s_c[1], s_c[2]), (s_l[0], s_l[4], s_l[1], s_l[2]), h0, True)
    skip = d_skip.reshape(SSD_GROUPS, SSD_HPG, 1)

    def finish(xh, y, z):
        y = (y + skip * xh).reshape(z.shape).astype(z.dtype)
        return rmsnorm(y * jax.nn.silu(z), norm_g)

    o_l = finish(s_l[0], yf_l + yb_l, pl[1])
    o_c = finish(s_c[0], yf_c + yb_c, pc[1]) if need_ctx else None
    return o_c, o_l


def gla_scan(q, k, v, g, h0):
    bsz, n = q.shape[:2]
    nc = n // GLA_CHUNK
    ch = lambda t: t.reshape(bsz, nc, GLA_CHUNK, *t.shape[2:])
    q, k, v, g = ch(q), ch(k), ch(v), ch(g)
    b = jnp.cumsum(g, axis=2)
    qe = q * jnp.exp(b)
    ke = k * jnp.exp(-b)
    kd = k * jnp.exp(b[:, :, -1:] - b)
    tri = jnp.tril(jnp.ones((GLA_CHUNK, GLA_CHUNK), bool))
    att = jnp.where(tri, jnp.einsum("bcthd,bcshd->bchts", qe, ke), 0.0)
    o_intra = jnp.einsum("bchts,bcshv->bcthv", att, v)
    upd = jnp.einsum("bcshd,bcshv->bchdv", kd, v)
    dec = jnp.exp(b[:, :, -1])

    def step(h, inp):
        u, d = inp
        return d[..., None] * h + u, h

    h_fin, h_in = lax.scan(step, h0, (jnp.moveaxis(upd, 1, 0), jnp.moveaxis(dec, 1, 0)))
    o_inter = jnp.einsum("bcthd,bchdv->bcthv", qe, jnp.moveaxis(h_in, 0, 1))
    return (o_intra + o_inter).reshape(bsz, n, *v.shape[3:]), h_fin


def gla_prep(p, w_f2, b_f):
    q, k, v, _, f1 = p
    bsz, n = q.shape[:2]
    q = q.reshape(bsz, n, GLA_HEADS, GLA_DK) * (GLA_DK ** -0.5)
    k = k.reshape(bsz, n, GLA_HEADS, GLA_DK)
    v = v.reshape(bsz, n, GLA_HEADS, GLA_DV)
    logit = jnp.einsum("bndr,drk->bndk", f1.reshape(bsz, n, 2, GLA_RANK), w_f2) + b_f
    g = (jax.nn.log_sigmoid(logit.astype(jnp.float32)) / GLA_TAU).reshape(bsz, n, 2, GLA_HEADS, GLA_DK)
    return q, k, v, g[:, :, 0], g[:, :, 1]


def gla_branch(pc, pl, w_f2, b_f, norm_g, need_ctx):
    s_c = gla_prep(pc, w_f2, b_f)
    s_l = gla_prep(pl, w_f2, b_f)
    bsz = pl[0].shape[0]
    h0 = jnp.zeros((bsz, GLA_HEADS, GLA_DK, GLA_DV), jnp.float32)
    of_c, of_l = prefix_then_latent(gla_scan, (s_c[0], s_c[1], s_c[2], s_c[3]),
                                    (s_l[0], s_l[1], s_l[2], s_l[3]), h0, False)
    ob_c, ob_l = prefix_then_latent(gla_scan, (s_c[0], s_c[1], s_c[2], s_c[4]),
                                    (s_l[0], s_l[1], s_l[2], s_l[4]), h0, True)

    def finish(o, r):
        o = rmsnorm(o.astype(r.dtype), norm_g).reshape(r.shape)
        return o * jax.nn.silu(r)

    o_l = finish(of_l + ob_l, pl[3])
    o_c = finish(of_c + ob_c, pc[3]) if need_ctx else None
    return o_c, o_l


def shortconv_branch(pc, pl, conv_w, need_ctx):
    def f(p):
        bg, cg, xd, gate = p
        return bg * dwconv(cg * xd, conv_w) * jax.nn.silu(gate)
    return (f(pc) if need_ctx else None), f(pl)


def merge_branches(ys, gates, w_branch, w_out):
    gs = jnp.split(gates, N_BRANCH, axis=-1)
    m = jax.nn.sigmoid(gs[0]) * (ys[0] @ w_branch[0])
    for i in range(1, N_BRANCH):
        m = m + jax.nn.sigmoid(gs[i]) * (ys[i] @ w_branch[i])
    return m @ w_out


def setup_inputs(seed: int = 0) -> dict:
    key = jax.random.key(seed)
    ks = jax.random.split(key, 24)
    f32 = jnp.float32
    nrm = lambda k, shape, s: jax.random.normal(k, shape, f32) * s
    L, D = DEPTH, D_MODEL
    in_total = sum(IN_WIDTHS)
    ssd_conv_ch = BRANCH_WIDTH + 2 * SSD_GROUPS * SSD_STATE
    dt0 = jnp.exp(jax.random.uniform(ks[12], (L, 2, SSD_HEADS), f32, math.log(1e-3), math.log(1e-1)))
    return {
        "x": nrm(ks[0], (BATCH, SEQ, D), 1.0),
        "c": nrm(ks[1], (BATCH, D), 1.0),
        "ctx": nrm(ks[2], (BATCH, CTX_LEN, D), 1.0),
        "c_ctx": nrm(ks[3], (D,), 1.0),
        "w_mod": nrm(ks[4], (L, D, 3 * D), 0.5 * D ** -0.5),
        "b_mod": nrm(ks[5], (L, 3 * D), 0.01),
        "g_pre": 1.0 + nrm(ks[6], (L, D), 0.02),
        "g_post": 1.0 + nrm(ks[7], (L, D), 0.02),
        "w_in": nrm(ks[8], (L, D, in_total), D ** -0.5),
        "g_q": 1.0 + nrm(ks[9], (L, HEAD_DIM), 0.02),
        "g_k": 1.0 + nrm(ks[10], (L, HEAD_DIM), 0.02),
        "ssd_conv_w": nrm(ks[11], (L, SSD_CONV, ssd_conv_ch), SSD_CONV ** -0.5),
        "ssd_conv_b": nrm(ks[13], (L, ssd_conv_ch), 0.01),
        "ssd_a_log": jnp.log(jax.random.uniform(ks[14], (L, 2, SSD_HEADS), f32, 1.0, 16.0)),
        "ssd_dt_bias": dt0 + jnp.log(-jnp.expm1(-dt0)),
        "ssd_d": 1.0 + nrm(ks[15], (L, SSD_HEADS), 0.02),
        "ssd_norm_g": 1.0 + nrm(ks[16], (L, BRANCH_WIDTH), 0.02),
        "gla_w_f2": nrm(ks[17], (L, 2, GLA_RANK, GLA_HEADS * GLA_DK), GLA_RANK ** -0.5),
        "gla_b_f": nrm(ks[18], (L, 2, GLA_HEADS * GLA_DK), 0.01),
        "gla_norm_g": 1.0 + nrm(ks[19], (L, GLA_DV), 0.02),
        "sc_conv_w": nrm(ks[20], (L, SC_CONV, BRANCH_WIDTH), SC_CONV ** -0.5),
        "w_branch": nrm(ks[21], (L, N_BRANCH, BRANCH_WIDTH, D), BRANCH_WIDTH ** -0.5),
        "w_out": nrm(ks[22], (L, D, D), D ** -0.5),
    }


def reference(x, c, ctx, c_ctx, w_mod, b_mod, g_pre, g_post, w_in, g_q, g_k, ssd_conv_w, ssd_conv_b,
              ssd_a_log, ssd_dt_bias, ssd_d, ssd_norm_g, gla_w_f2, gla_b_f, gla_norm_g, sc_conv_w,
              w_branch, w_out):
    rows = x.shape[1] // GRID_W
    cos, sin = axial_rope_tables(rows)
    split_at = [int(s) for s in np.cumsum(IN_WIDTHS)[:-1]]
    silu_c = jax.nn.silu(c)
    silu_cc = jax.nn.silu(c_ctx)
    xc = ctx
    for l in range(DEPTH):
        need_ctx = l < DEPTH - 1
        sh, sc, gt = jnp.split(silu_c @ w_mod[l] + b_mod[l], 3, axis=-1)
        sh_c, sc_c, gt_c = jnp.split(silu_cc @ w_mod[l] + b_mod[l], 3, axis=-1)
        h = rmsnorm(x, g_pre[l]) * (1.0 + sc[:, None]) + sh[:, None]
        hc = rmsnorm(xc, g_pre[l]) * (1.0 + sc_c) + sh_c
        pl = jnp.split(h @ w_in[l], split_at, axis=-1)
        pc = jnp.split(hc @ w_in[l], split_at, axis=-1)
        ya_c, ya = attention_branch(pc[0:4], pl[0:4], g_q[l], g_k[l], cos, sin, need_ctx)
        yb_c, yb = ssd_branch(pc[4:9], pl[4:9], ssd_conv_w[l], ssd_conv_b[l], ssd_a_log[l],
                              ssd_dt_bias[l], ssd_d[l], ssd_norm_g[l], need_ctx)
        yg_c, yg = gla_branch(pc[9:14], pl[9:14], gla_w_f2[l], gla_b_f[l], gla_norm_g[l], need_ctx)
        yd_c, yd = shortconv_branch(pc[14:18], pl[14:18], sc_conv_w[l], need_ctx)
        out = merge_branches((ya, yb, yg, yd), pl[18], w_branch[l], w_out[l])
        x = x + gt[:, None] * rmsnorm(out, g_post[l])
        if need_ctx:
            out_c = merge_branches((ya_c, yb_c, yg_c, yd_c), pc[18], w_branch[l], w_out[l])
            xc = xc + gt_c * rmsnorm(out_c, g_post[l])
    return x
```

```python
import functools

import numpy as np
import jax
import jax.numpy as jnp
from jax import lax
from jax.experimental import pallas as pl
from jax.experimental.pallas import tpu as pltpu

F32 = jnp.float32
BF16 = jnp.bfloat16

D_MODEL = 2048
GRID_W = 64
BRANCH_WIDTH = D_MODEL // 2
N_BRANCH = 4
NORM_EPS = 1e-6
HEAD_DIM = 128
ATT_HEADS = BRANCH_WIDTH // HEAD_DIM
ATT_KV_HEADS = ATT_HEADS // 4
ATT_REP = ATT_HEADS // ATT_KV_HEADS
ROPE_THETA = 10000.0
SSD_HEAD_DIM = 64
SSD_HEADS = BRANCH_WIDTH // SSD_HEAD_DIM
SSD_GROUPS = 2
SSD_HPG = SSD_HEADS // SSD_GROUPS
SSD_STATE = 128
SSD_CONV = 5
SSD_CHUNK = 128
GLA_HEADS = 4
GLA_DV = BRANCH_WIDTH // GLA_HEADS
GLA_DK = GLA_DV // 2
GLA_RANK = 16
GLA_TAU = 16.0
GLA_CHUNK = 64
SC_CONV = 3

LANES = 128
SUBLANES = 8
ROW_TILE = 256
VMEM_LIMIT = 56 * 1024 * 1024

_COL = dict(
    A_Q=0, A_G=1024, B_Z=2048, B_X=3072, B_B=4096, B_C=4352, A_K=4608, A_V=4864,
    C_V=5120, C_G=6144, C_Q=7168, C_K=7680,
    D_ALL=8192, MG=12288, NARROW=20480,
)
N_PROJ = 20608
PROJ_TN = 2944
DT_LANE0 = 0
F1_LANE0 = 32


def _cparams(sem, vmem=VMEM_LIMIT):
    return pltpu.CompilerParams(dimension_semantics=sem, vmem_limit_bytes=vmem)


def _silu(x):
    return x * jax.nn.sigmoid(x)


def _softplus(x):
    return jnp.maximum(x, 0.0) + jnp.log1p(jnp.exp(-jnp.abs(x)))


def _log_sigmoid(x):
    return jnp.minimum(x, 0.0) - jnp.log1p(jnp.exp(-jnp.abs(x)))


def _mod_kernel(c_ref, w_ref, b_ref, o_ref):
    a = _silu(c_ref[...]).astype(BF16)
    o_ref[...] = jnp.dot(a, w_ref[...].astype(BF16), preferred_element_type=F32) + b_ref[...]


def _modulation(c_rows, w_mod, b_mod):
    d, n = w_mod.shape
    tn = 1024
    return pl.pallas_call(
        _mod_kernel,
        grid=(n // tn,),
        in_specs=[pl.BlockSpec((SUBLANES, d), lambda j: (0, 0)),
                  pl.BlockSpec((d, tn), lambda j: (0, j)),
                  pl.BlockSpec((1, tn), lambda j: (0, j))],
        out_specs=pl.BlockSpec((SUBLANES, tn), lambda j: (0, j)),
        out_shape=jax.ShapeDtypeStruct((SUBLANES, n), F32),
        compiler_params=_cparams(("parallel",)),
        name="modulation",
    )(c_rows, w_mod, b_mod)


class _Geom:
    def __init__(self, batch, ctx_len, seq):
        self.batch, self.ctx_len, self.seq = batch, ctx_len, seq
        self.lt = ctx_len + seq
        self.rows = batch * self.lt
        assert ctx_len % ROW_TILE == 0 and seq % ROW_TILE == 0
        self.tpb = self.lt // ROW_TILE
        self.ctx_tiles = ctx_len // ROW_TILE
        self.lat_tiles = seq // ROW_TILE

    def span(self, with_ctx):
        return (0, self.tpb) if with_ctx else (self.ctx_tiles, self.lat_tiles)

    def mod_row(self, b, j):
        return jnp.where(j < self.ctx_tiles, self.batch, b)


def _prenorm_kernel(x_ref, g_ref, sh_ref, sc_ref, o_ref):
    x = x_ref[...]
    y = x * lax.rsqrt(jnp.mean(x * x, axis=-1, keepdims=True) + NORM_EPS)
    o_ref[...] = ((y * g_ref[...]) * (1.0 + sc_ref[0]) + sh_ref[0]).astype(o_ref.dtype)


def _prenorm(x_all, g_pre, mod3, geo):
    d = x_all.shape[1]
    row = lambda b, j: (b * geo.tpb + j, 0)
    return pl.pallas_call(
        _prenorm_kernel,
        grid=(geo.batch, geo.tpb),
        in_specs=[pl.BlockSpec((ROW_TILE, d), row),
                  pl.BlockSpec((1, d), lambda b, j: (0, 0)),
                  pl.BlockSpec((1, 1, d), lambda b, j: (geo.mod_row(b, j), 0, 0)),
                  pl.BlockSpec((1, 1, d), lambda b, j: (geo.mod_row(b, j), 0, 1))],
        out_specs=pl.BlockSpec((ROW_TILE, d), row),
        out_shape=jax.ShapeDtypeStruct(x_all.shape, BF16),
        compiler_params=_cparams(("parallel", "parallel")),
        name="prenorm",
    )(x_all, g_pre, mod3, mod3)


def _matmul_kernel(a_ref, w_ref, o_ref):
    o_ref[...] = jnp.dot(a_ref[...], w_ref[...], preferred_element_type=F32).astype(o_ref.dtype)


def _matmul(a, w, tn, out_dtype=BF16):
    m, k = a.shape
    n = w.shape[1]
    tm = 512 if m % 512 == 0 else 256
    assert m % tm == 0 and n % tn == 0
    return pl.pallas_call(
        _matmul_kernel,
        grid=(n // tn, m // tm),
        in_specs=[pl.BlockSpec((tm, k), lambda j, i: (i, 0)),
                  pl.BlockSpec((k, tn), lambda j, i: (0, j))],
        out_specs=pl.BlockSpec((tm, tn), lambda j, i: (i, j)),
        out_shape=jax.ShapeDtypeStruct((m, n), out_dtype),
        compiler_params=_cparams(("parallel", "parallel")),
        name="in_proj",
    )(a, w)


def _rope_tables(geo):
    rows = geo.seq // GRID_W
    t_row = jnp.repeat(jnp.arange(rows, dtype=F32), GRID_W)
    t_col = jnp.tile(jnp.arange(GRID_W, dtype=F32), rows)
    half = HEAD_DIM // 2
    freqs = ROPE_THETA ** (-(jnp.arange(0, half, 2, dtype=F32) / half))
    ang = jnp.concatenate([t_row[:, None] * freqs, t_col[:, None] * freqs], axis=-1)
    cos, sin = jnp.cos(ang), jnp.sin(ang)
    cos_l = jnp.concatenate([cos, cos], axis=-1)
    sin_l = jnp.concatenate([-sin, sin], axis=-1)
    cos_c = jnp.ones((geo.ctx_len, HEAD_DIM), F32)
    sin_c = jnp.zeros((geo.ctx_len, HEAD_DIM), F32)
    return jnp.concatenate([cos_c, cos_l], axis=0), jnp.concatenate([sin_c, sin_l], axis=0)


def _qk_prep_kernel(q_ref, k_ref, cos_ref, sin_ref, gq_ref, gk_ref, qo_ref, ko_ref):
    cos, sin = cos_ref[...], sin_ref[...]

    def norm_rope(x, g):
        x = x.astype(F32)
        y = x * lax.rsqrt(jnp.mean(x * x, axis=-1, keepdims=True) + NORM_EPS) * g
        return y * cos + pltpu.roll(y, HEAD_DIM // 2, 1) * sin

    gq, gk = gq_ref[...], gk_ref[...]
    for h in range(ATT_HEADS):
        y = norm_rope(q_ref[:, h * HEAD_DIM:(h + 1) * HEAD_DIM], gq) * (HEAD_DIM ** -0.5)
        qo_ref[h] = y.astype(qo_ref.dtype)
    for h in range(ATT_KV_HEADS):
        ko_ref[h] = norm_rope(k_ref[:, h * HEAD_DIM:(h + 1) * HEAD_DIM], gk).astype(ko_ref.dtype)


def _qk_prep(proj, cos_t, sin_t, g_q, g_k, geo):
    row = lambda b, j: b * geo.tpb + j
    nq, nk = ATT_HEADS * HEAD_DIM, ATT_KV_HEADS * HEAD_DIM
    return pl.pallas_call(
        _qk_prep_kernel,
        grid=(geo.batch, geo.tpb),
        in_specs=[pl.BlockSpec((ROW_TILE, nq), lambda b, j: (row(b, j), _COL["A_Q"] // nq)),
                  pl.BlockSpec((ROW_TILE, nk), lambda b, j: (row(b, j), _COL["A_K"] // nk)),
                  pl.BlockSpec((ROW_TILE, HEAD_DIM), lambda b, j: (j, 0)),
                  pl.BlockSpec((ROW_TILE, HEAD_DIM), lambda b, j: (j, 0)),
                  pl.BlockSpec((1, HEAD_DIM), lambda b, j: (0, 0)),
                  pl.BlockSpec((1, HEAD_DIM), lambda b, j: (0, 0))],
        out_specs=[pl.BlockSpec((ATT_HEADS, ROW_TILE, HEAD_DIM), lambda b, j: (b, j, 0)),
                   pl.BlockSpec((ATT_KV_HEADS, ROW_TILE, HEAD_DIM), lambda b, j: (b, j, 0))],
        out_shape=[jax.ShapeDtypeStruct((geo.batch * ATT_HEADS, geo.lt, HEAD_DIM), BF16),
                   jax.ShapeDtypeStruct((geo.batch * ATT_KV_HEADS, geo.lt, HEAD_DIM), BF16)],
        compiler_params=_cparams(("parallel", "parallel")),
        name="qk_prep",
    )(proj, proj, cos_t, sin_t, g_q, g_k)


def _attn_kernel(q_ref, k_ref, v_ref, gate_ref, o_ref, *, tk):
    tq = q_ref.shape[1]
    q = q_ref[...].reshape(ATT_REP * tq, HEAD_DIM)
    n_chunks = k_ref.shape[0] // tk

    def body(c, carry):
        m, l, acc = carry
        start = pl.multiple_of(c * tk, tk)
        k = k_ref[pl.ds(start, tk), :]
        v = v_ref[pl.ds(start, tk), :]
        s = lax.dot_general(q, k, (((1,), (1,)), ((), ())), preferred_element_type=F32)
        m_new = jnp.maximum(m, jnp.max(s, axis=-1, keepdims=True))
        alpha = jnp.exp(m - m_new)
        p = jnp.exp(s - m_new)
        l = alpha * l + jnp.sum(p, axis=-1, keepdims=True)
        acc = alpha * acc + jnp.dot(p.astype(BF16), v, preferred_element_type=F32)
        return m_new, l, acc

    init = (jnp.full((ATT_REP * tq, 1), -jnp.inf, F32), jnp.zeros((ATT_REP * tq, 1), F32),
            jnp.zeros((ATT_REP * tq, HEAD_DIM), F32))
    _, l, acc = lax.fori_loop(0, n_chunks, body, init)
    o = (acc / l).reshape(ATT_REP, tq, HEAD_DIM)
    o = jnp.concatenate([o[r] for r in range(ATT_REP)], axis=1)
    o_ref[...] = (o * _silu(gate_ref[...].astype(F32))).astype(o_ref.dtype)


def _attention(qh, kh, proj, ya_prev, geo, q_first_row, q_rows, kv_rows, tq=128, tk=256):
    assert q_first_row % tq == 0 and q_rows % tq == 0 and kv_rows % tk == 0 and geo.lt % kv_rows == 0
    q0 = q_first_row // tq
    tiles_b = geo.lt // tq
    gw = ATT_REP * HEAD_DIM
    kernel = functools.partial(_attn_kernel, tk=tk)
    in_specs = [
        pl.BlockSpec((ATT_REP, tq, HEAD_DIM), lambda b, g, i: (b * ATT_KV_HEADS + g, q0 + i, 0)),
        pl.BlockSpec((None, kv_rows, HEAD_DIM), lambda b, g, i: (b * ATT_KV_HEADS + g, 0, 0)),
        pl.BlockSpec((kv_rows, HEAD_DIM), lambda b, g, i: (b * (geo.lt // kv_rows), _COL["A_V"] // HEAD_DIM + g)),
        pl.BlockSpec((tq, gw), lambda b, g, i: (b * tiles_b + q0 + i, _COL["A_G"] // gw + g)),
    ]
    out_spec = pl.BlockSpec((tq, gw), lambda b, g, i: (b * tiles_b + q0 + i, g))
    args = [qh, kh, proj, proj]
    aliases = {}
    if ya_prev is not None:
        in_specs.append(pl.BlockSpec(memory_space=pl.ANY))
        args.append(ya_prev)
        aliases = {4: 0}
        kernel = functools.partial(_drop_last_input, kernel, 4)
    return pl.pallas_call(
        kernel,
        grid=(geo.batch, ATT_KV_HEADS, q_rows // tq),
        in_specs=in_specs,
        out_specs=out_spec,
        out_shape=jax.ShapeDtypeStruct((geo.rows, BRANCH_WIDTH), BF16),
        input_output_aliases=aliases,
        compiler_params=_cparams(("parallel", "parallel", "arbitrary")),
        name="attention",
    )(*args)


def _drop_last_input(kernel, n_in, *refs):
    return kernel(*refs[:n_in], *refs[n_in + 1:])


def _halo_specs(width, col_block, geo, first, row):
    per = ROW_TILE // SUBLANES
    last = geo.rows // SUBLANES - 1
    prev = pl.BlockSpec((SUBLANES, width), lambda b, j: (jnp.maximum(row(b, j) * per - 1, 0), col_block))
    nxt = pl.BlockSpec((SUBLANES, width), lambda b, j: (jnp.minimum((row(b, j) + 1) * per, last), col_block))
    return prev, nxt


def _fill_ext(ext_ref, cur, prev, nxt, j, geo):
    t = cur.shape[0]
    seg_first = jnp.logical_or(j == 0, j == geo.ctx_tiles)
    seg_last = jnp.logical_or(j == geo.ctx_tiles - 1, j == geo.tpb - 1)
    ext_ref[0:SUBLANES, :] = jnp.where(seg_first, 0.0, prev)
    ext_ref[SUBLANES:SUBLANES + t, :] = cur
    ext_ref[SUBLANES + t:2 * SUBLANES + t, :] = jnp.where(seg_last, 0.0, nxt)


def _ssd_conv_kernel(cur_ref, prev_ref, next_ref, nar_ref, w_ref, b_ref, dtb_ref, xbc_ref, dt_ref, ext_ref, *, geo):
    j = pl.program_id(1)
    _fill_ext(ext_ref, cur_ref[...].astype(F32), prev_ref[...].astype(F32), next_ref[...].astype(F32), j, geo)
    t = cur_ref.shape[0]
    acc = jnp.zeros(cur_ref.shape, F32) + b_ref[...]
    for k in range(SSD_CONV):
        acc = acc + w_ref[k:k + 1, :] * ext_ref[SUBLANES + k - SSD_CONV // 2:SUBLANES + k - SSD_CONV // 2 + t, :]
    xbc_ref[...] = _silu(acc).astype(xbc_ref.dtype)
    lane = lax.broadcasted_iota(jnp.int32, dt_ref.shape, 1)
    dt = _softplus(nar_ref[...].astype(F32) + dtb_ref[...])
    dt_ref[...] = jnp.where(lane < 2 * SSD_HEADS, dt, 0.0)


def _ssd_conv(proj, conv_w, conv_b, dt_bias_row, geo):
    width = BRANCH_WIDTH + 2 * SSD_GROUPS * SSD_STATE
    row = lambda b, j: b * geo.tpb + j
    cb = _COL["B_X"] // width
    prev, nxt = _halo_specs(width, cb, geo, 0, row)
    return pl.pallas_call(
        functools.partial(_ssd_conv_kernel, geo=geo),
        grid=(geo.batch, geo.tpb),
        in_specs=[pl.BlockSpec((ROW_TILE, width), lambda b, j: (row(b, j), cb)), prev, nxt,
                  pl.BlockSpec((ROW_TILE, LANES), lambda b, j: (row(b, j), _COL["NARROW"] // LANES)),
                  pl.BlockSpec((SSD_CONV, width), lambda b, j: (0, 0)),
                  pl.BlockSpec((1, width), lambda b, j: (0, 0)),
                  pl.BlockSpec((1, LANES), lambda b, j: (0, 0))],
        out_specs=[pl.BlockSpec((ROW_TILE, width), lambda b, j: (row(b, j), 0)),
                   pl.BlockSpec((ROW_TILE, LANES), lambda b, j: (row(b, j), 0))],
        out_shape=[jax.ShapeDtypeStruct((geo.rows, width), BF16),
                   jax.ShapeDtypeStruct((geo.rows, LANES), F32)],
        scratch_shapes=[pltpu.VMEM((ROW_TILE + 2 * SUBLANES, width), F32)],
        compiler_params=_cparams(("parallel", "parallel")),
        name="ssd_conv",
    )(proj, proj, proj, proj, conv_w, conv_b, dt_bias_row)


def _scan_chunk(s, n_chunks, n_ctx_chunks, reverse):
    if not reverse:
        return s
    return jnp.where(s < n_ctx_chunks, n_ctx_chunks - 1 - s, n_chunks + n_ctx_chunks - 1 - s)


def _tri(n, reverse):
    t = lax.broadcasted_iota(jnp.int32, (n, n), 0)
    s = lax.broadcasted_iota(jnp.int32, (n, n), 1)
    return (s >= t) if reverse else (s <= t)


def _ssd_scan_kernel(xbc_ref, dt_ref, alog_ref, e_ref, y_ref, h_ref, *, reverse):
    q = SSD_CHUNK
    gw = SSD_HPG * SSD_HEAD_DIM
    lane0 = SSD_HEADS if reverse else 0

    @pl.when(pl.program_id(1) == 0)
    def _():
        h_ref[...] = jnp.zeros_like(h_ref)

    lane = lax.broadcasted_iota(jnp.int32, (1, LANES), 1)
    a = jnp.where(lane < 2 * SSD_HEADS, -jnp.exp(alog_ref[...]), 0.0)
    dt = dt_ref[...]
    mask = _tri(q, reverse)
    cs = jnp.dot(mask.astype(F32), dt * a, preferred_element_type=F32, precision=lax.Precision.HIGHEST)
    cs_last = cs[0:1, :] if reverse else cs[q - 1:q, :]
    e = e_ref[...]
    expand = lambda t: jnp.dot(t, e, preferred_element_type=F32, precision=lax.Precision.HIGHEST)
    x = xbc_ref[:, 0:BRANCH_WIDTH].astype(F32)
    dtx = expand(dt) * x
    wx = (expand(jnp.exp(cs_last - cs)) * dtx).astype(BF16)
    dtx = dtx.astype(BF16)
    ecs = expand(jnp.exp(cs))
    chunk_decay = expand(jnp.broadcast_to(jnp.exp(cs_last), (SUBLANES, LANES)))[0:1, :]
    cs_t = cs.T
    for g in range(SSD_GROUPS):
        bm = xbc_ref[:, BRANCH_WIDTH + g * SSD_STATE:BRANCH_WIDTH + (g + 1) * SSD_STATE]
        c0 = BRANCH_WIDTH + SSD_GROUPS * SSD_STATE + g * SSD_STATE
        cm = xbc_ref[:, c0:c0 + SSD_STATE]
        cb = lax.dot_general(cm, bm, (((1,), (1,)), ((), ())), preferred_element_type=F32)
        h_in = h_ref[g]
        y_off = jnp.dot(cm, h_in.astype(BF16), preferred_element_type=F32) * ecs[:, g * gw:(g + 1) * gw]
        states = jnp.dot(bm.astype(F32).T.astype(BF16), wx[:, g * gw:(g + 1) * gw], preferred_element_type=F32)
        h_ref[g] = chunk_decay[:, g * gw:(g + 1) * gw] * h_in + states
        for r in range(SSD_HPG):
            col = lane0 + g * SSD_HPG + r
            seg = cs[:, col:col + 1] - cs_t[col:col + 1, :]
            decay = jnp.exp(jnp.where(mask, seg, -1e30))
            lo = g * gw + r * SSD_HEAD_DIM
            y_diag = jnp.dot((cb * decay).astype(BF16), dtx[:, lo:lo + SSD_HEAD_DIM], preferred_element_type=F32)
            y_ref[:, lo:lo + SSD_HEAD_DIM] = (y_diag + y_off[:, r * SSD_HEAD_DIM:(r + 1) * SSD_HEAD_DIM]).astype(y_ref.dtype)


def _ssd_scan(xbc, dt, a_log_row, expand_mat, geo, reverse):
    nc, ncc = geo.lt // SSD_CHUNK, geo.ctx_len // SSD_CHUNK
    width = xbc.shape[1]
    row = lambda b, s: b * nc + _scan_chunk(s, nc, ncc, reverse)
    return pl.pallas_call(
        functools.partial(_ssd_scan_kernel, reverse=reverse),
        grid=(geo.batch, nc),
        in_specs=[pl.BlockSpec((SSD_CHUNK, width), lambda b, s: (row(b, s), 0)),
                  pl.BlockSpec((SSD_CHUNK, LANES), lambda b, s: (row(b, s), 0)),
                  pl.BlockSpec((1, LANES), lambda b, s: (0, 0)),
                  pl.BlockSpec((LANES, BRANCH_WIDTH), lambda b, s: (0, 0))],
        out_specs=pl.BlockSpec((SSD_CHUNK, BRANCH_WIDTH), lambda b, s: (row(b, s), 0)),
        out_shape=jax.ShapeDtypeStruct((geo.rows, BRANCH_WIDTH), F32),
        scratch_shapes=[pltpu.VMEM((SSD_GROUPS, SSD_STATE, SSD_HPG * SSD_HEAD_DIM), F32)],
        compiler_params=_cparams(("parallel", "arbitrary")),
        name="ssd_scan_bwd" if reverse else "ssd_scan_fwd",
    )(xbc, dt, a_log_row, expand_mat)


def _ssd_finish_kernel(yf_ref, yb_ref, xbc_ref, z_ref, skip_ref, g_ref, o_ref):
    xh = xbc_ref[:, 0:BRANCH_WIDTH].astype(F32)
    y = yf_ref[...] + yb_ref[...] + skip_ref[...] * xh
    y = y * _silu(z_ref[...].astype(F32))
    y = y * lax.rsqrt(jnp.mean(y * y, axis=-1, keepdims=True) + NORM_EPS)
    o_ref[...] = (y * g_ref[...]).astype(o_ref.dtype)


def _ssd_finish(yf, yb, xbc, proj, skip_row, norm_g, geo, with_ctx):
    first, nt = geo.span(with_ctx)
    row = lambda b, j: b * geo.tpb + first + j
    w = BRANCH_WIDTH
    return pl.pallas_call(
        _ssd_finish_kernel,
        grid=(geo.batch, nt),
        in_specs=[pl.BlockSpec((ROW_TILE, w), lambda b, j: (row(b, j), 0)),
                  pl.BlockSpec((ROW_TILE, w), lambda b, j: (row(b, j), 0)),
                  pl.BlockSpec((ROW_TILE, xbc.shape[1]), lambda b, j: (row(b, j), 0)),
                  pl.BlockSpec((ROW_TILE, w), lambda b, j: (row(b, j), _COL["B_Z"] // w)),
                  pl.BlockSpec((1, w), lambda b, j: (0, 0)),
                  pl.BlockSpec((1, w), lambda b, j: (0, 0))],
        out_specs=pl.BlockSpec((ROW_TILE, w), lambda b, j: (row(b, j), 0)),
        out_shape=jax.ShapeDtypeStruct((geo.rows, w), BF16),
        compiler_params=_cparams(("parallel", "parallel")),
        name="ssd_finish",
    )(yf, yb, xbc, proj, skip_row, norm_g)


def _gla_scan_kernel(q_ref, k_ref, v_ref, nar_ref, w2_ref, bf_ref, o_ref, h_ref, *, reverse):
    n = GLA_CHUNK

    @pl.when(pl.program_id(1) == 0)
    def _():
        h_ref[...] = jnp.zeros_like(h_ref)

    logit = jnp.dot(nar_ref[...], w2_ref[...], preferred_element_type=F32) + bf_ref[...]
    gl = _log_sigmoid(logit) / GLA_TAU
    mask = _tri(n, reverse)
    b = jnp.dot(mask.astype(F32), gl, preferred_element_type=F32, precision=lax.Precision.HIGHEST)
    b_last = b[0:1, :] if reverse else b[n - 1:n, :]
    eb = jnp.exp(b)
    q = q_ref[...].astype(F32) * (GLA_DK ** -0.5)
    k = k_ref[...].astype(F32)
    qe = (q * eb).astype(BF16)
    ke = (k * jnp.exp(-b)).astype(BF16)
    kd = (k * jnp.exp(b_last - b)).astype(BF16)
    dec = jnp.exp(b_last)
    for h in range(GLA_HEADS):
        ks = slice(h * GLA_DK, (h + 1) * GLA_DK)
        vs = slice(h * GLA_DV, (h + 1) * GLA_DV)
        v = v_ref[:, vs]
        att = lax.dot_general(qe[:, ks], ke[:, ks], (((1,), (1,)), ((), ())), preferred_element_type=F32)
        att = jnp.where(mask, att, 0.0).astype(BF16)
        h_in = h_ref[h]
        o_inter = lax.dot_general(qe[:, ks], h_in.astype(BF16), (((1,), (1,)), ((), ())), preferred_element_type=F32)
        o_ref[:, vs] = (jnp.dot(att, v, preferred_element_type=F32) + o_inter).astype(o_ref.dtype)
        upd = jnp.dot(v.astype(F32).T.astype(BF16), kd[:, ks], preferred_element_type=F32)
        h_ref[h] = dec[:, ks] * h_in + upd


def _gla_scan(proj, w2, b_f, geo, reverse):
    nc, ncc = geo.lt // GLA_CHUNK, geo.ctx_len // GLA_CHUNK
    row = lambda b, s: b * nc + _scan_chunk(s, nc, ncc, reverse)
    kw, vw = GLA_HEADS * GLA_DK, GLA_HEADS * GLA_DV
    return pl.pallas_call(
        functools.partial(_gla_scan_kernel, reverse=reverse),
        grid=(geo.batch, nc),
        in_specs=[pl.BlockSpec((GLA_CHUNK, kw), lambda b, s: (row(b, s), _COL["C_Q"] // kw)),
                  pl.BlockSpec((GLA_CHUNK, kw), lambda b, s: (row(b, s), _COL["C_K"] // kw)),
                  pl.BlockSpec((GLA_CHUNK, vw), lambda b, s: (row(b, s), _COL["C_V"] // vw)),
                  pl.BlockSpec((GLA_CHUNK, LANES), lambda b, s: (row(b, s), _COL["NARROW"] // LANES)),
                  pl.BlockSpec((LANES, kw), lambda b, s: (0, 0)),
                  pl.BlockSpec((1, kw), lambda b, s: (0, 0))],
        out_specs=pl.BlockSpec((GLA_CHUNK, vw), lambda b, s: (row(b, s), 0)),
        out_shape=jax.ShapeDtypeStruct((geo.rows, vw), F32),
        scratch_shapes=[pltpu.VMEM((GLA_HEADS, GLA_DV, GLA_DK), F32)],
        compiler_params=_cparams(("parallel", "arbitrary")),
        name="gla_scan_bwd" if reverse else "gla_scan_fwd",
    )(proj, proj, proj, proj, w2, b_f)


def _gla_finish_kernel(of_ref, ob_ref, gate_ref, g_ref, o_ref):
    g = g_ref[...]
    for h in range(GLA_HEADS):
        vs = slice(h * GLA_DV, (h + 1) * GLA_DV)
        o = (of_ref[:, vs] + ob_ref[:, vs])
        o = o * lax.rsqrt(jnp.mean(o * o, axis=-1, keepdims=True) + NORM_EPS) * g
        o_ref[:, vs] = (o * _silu(gate_ref[:, vs].astype(F32))).astype(o_ref.dtype)


def _gla_finish(of, ob, proj, norm_g, geo, with_ctx):
    first, nt = geo.span(with_ctx)
    row = lambda b, j: b * geo.tpb + first + j
    w = BRANCH_WIDTH
    return pl.pallas_call(
        _gla_finish_kernel,
        grid=(geo.batch, nt),
        in_specs=[pl.BlockSpec((ROW_TILE, w), lambda b, j: (row(b, j), 0)),
                  pl.BlockSpec((ROW_TILE, w), lambda b, j: (row(b, j), 0)),
                  pl.BlockSpec((ROW_TILE, w), lambda b, j: (row(b, j), _COL["C_G"] // w)),
                  pl.BlockSpec((1, GLA_DV), lambda b, j: (0, 0))],
        out_specs=pl.BlockSpec((ROW_TILE, w), lambda b, j: (row(b, j), 0)),
        out_shape=jax.ShapeDtypeStruct((geo.rows, w), BF16),
        compiler_params=_cparams(("parallel", "parallel")),
        name="gla_finish",
    )(of, ob, proj, norm_g)


def _shortconv_kernel(cur_ref, prev_ref, next_ref, w_ref, o_ref, ext_ref, *, geo, first):
    j = pl.program_id(1) + first
    w = BRANCH_WIDTH
    u = lambda ref: ref[:, w:2 * w].astype(F32) * ref[:, 2 * w:3 * w].astype(F32)
    _fill_ext(ext_ref, u(cur_ref), u(prev_ref), u(next_ref), j, geo)
    t = cur_ref.shape[0]
    acc = jnp.zeros((t, w), F32)
    for k in range(SC_CONV):
        acc = acc + w_ref[k:k + 1, :] * ext_ref[SUBLANES + k - SC_CONV // 2:SUBLANES + k - SC_CONV // 2 + t, :]
    y = cur_ref[:, 0:w].astype(F32) * acc * _silu(cur_ref[:, 3 * w:4 * w].astype(F32))
    o_ref[...] = y.astype(o_ref.dtype)


def _shortconv(proj, conv_w, geo, with_ctx):
    first, nt = geo.span(with_ctx)
    row = lambda b, j: b * geo.tpb + first + j
    width = 4 * BRANCH_WIDTH
    cb = _COL["D_ALL"] // width
    prev, nxt = _halo_specs(width, cb, geo, first, row)
    return pl.pallas_call(
        functools.partial(_shortconv_kernel, geo=geo, first=first),
        grid=(geo.batch, nt),
        in_specs=[pl.BlockSpec((ROW_TILE, width), lambda b, j: (row(b, j), cb)), prev, nxt,
                  pl.BlockSpec((SC_CONV, BRANCH_WIDTH), lambda b, j: (0, 0))],
        out_specs=pl.BlockSpec((ROW_TILE, BRANCH_WIDTH), lambda b, j: (row(b, j), 0)),
        out_shape=jax.ShapeDtypeStruct((geo.rows, BRANCH_WIDTH), BF16),
        scratch_shapes=[pltpu.VMEM((ROW_TILE + 2 * SUBLANES, BRANCH_WIDTH), F32)],
        compiler_params=_cparams(("parallel", "parallel")),
        name="shortconv",
    )(proj, proj, proj, conv_w)


def _merge_kernel(ya_ref, yb_ref, yg_ref, yd_ref, gate_ref, w_ref, o_ref, acc_ref):
    i = pl.program_id(2)
    gate = jax.nn.sigmoid(gate_ref[...].astype(F32))
    for n, y_ref in enumerate((ya_ref, yb_ref, yg_ref, yd_ref)):
        @pl.when(i == n)
        def _(y_ref=y_ref, n=n):
            term = gate * jnp.dot(y_ref[...], w_ref[...], preferred_element_type=F32)
            if n == 0:
                acc_ref[...] = term
            else:
                acc_ref[...] += term

    @pl.when(i == N_BRANCH - 1)
    def _():
        o_ref[...] = acc_ref[...].astype(o_ref.dtype)


def _merge(ys, proj, w_branch, geo, with_ctx):
    first, nt = geo.span(with_ctx)
    row = lambda b, j, i: b * geo.tpb + first + j
    w, d = BRANCH_WIDTH, D_MODEL
    y_spec = pl.BlockSpec((ROW_TILE, w), lambda b, j, i: (row(b, j, i), 0))
    return pl.pallas_call(
        _merge_kernel,
        grid=(geo.batch, nt, N_BRANCH),
        in_specs=[y_spec, y_spec, y_spec, y_spec,
                  pl.BlockSpec((ROW_TILE, d), lambda b, j, i: (row(b, j, i), _COL["MG"] // d + i)),
                  pl.BlockSpec((None, w, d), lambda b, j, i: (i, 0, 0))],
        out_specs=pl.BlockSpec((ROW_TILE, d), lambda b, j, i: (row(b, j, i), 0)),
        out_shape=jax.ShapeDtypeStruct((geo.rows, d), BF16),
        scratch_shapes=[pltpu.VMEM((ROW_TILE, d), F32)],
        compiler_params=_cparams(("parallel", "parallel", "arbitrary")),
        name="merge",
    )(*ys, proj, w_branch)


def _out_kernel(m_ref, w_ref, x_ref, gt_ref, g_ref, o_ref):
    out = jnp.dot(m_ref[...], w_ref[...], preferred_element_type=F32)
    y = out * lax.rsqrt(jnp.mean(out * out, axis=-1, keepdims=True) + NORM_EPS) * g_ref[...]
    o_ref[...] = x_ref[...] + gt_ref[0] * y


def _out_proj(m, w_out, x_all, mod3, g_post, geo, with_ctx):
    first, nt = geo.span(with_ctx)
    d = D_MODEL
    row = lambda b, j: b * geo.tpb + first + j
    out_rows = geo.rows if with_ctx else geo.batch * geo.seq
    out_row = row if with_ctx else (lambda b, j: b * nt + j)
    return pl.pallas_call(
        _out_kernel,
        grid=(geo.batch, nt),
        in_specs=[pl.BlockSpec((ROW_TILE, d), lambda b, j: (row(b, j), 0)),
                  pl.BlockSpec((d, d), lambda b, j: (0, 0)),
                  pl.BlockSpec((ROW_TILE, d), lambda b, j: (row(b, j), 0)),
                  pl.BlockSpec((1, 1, d), lambda b, j: (geo.mod_row(b, first + j), 0, 2)),
                  pl.BlockSpec((1, d), lambda b, j: (0, 0))],
        out_specs=pl.BlockSpec((ROW_TILE, d), lambda b, j: (out_row(b, j), 0)),
        out_shape=jax.ShapeDtypeStruct((out_rows, d), F32),
        compiler_params=_cparams(("parallel", "parallel")),
        name="out_proj",
    )(m, w_out, x_all, mod3, g_post)


def _deinterleave_cols(w, heads):
    k = w.shape[0]
    return w.reshape(k, heads, HEAD_DIM // 2, 2).swapaxes(2, 3).reshape(k, heads * HEAD_DIM)


def _regroup_w_in(w):
    bw = BRANCH_WIDTH
    widths = (ATT_HEADS * HEAD_DIM, ATT_KV_HEADS * HEAD_DIM, ATT_KV_HEADS * HEAD_DIM, bw,
              bw, bw, SSD_GROUPS * SSD_STATE, SSD_GROUPS * SSD_STATE, 2 * SSD_HEADS,
              GLA_HEADS * GLA_DK, GLA_HEADS * GLA_DK, GLA_HEADS * GLA_DV, bw, 2 * GLA_RANK,
              bw, bw, bw, bw, N_BRANCH * D_MODEL)
    offs = np.concatenate([[0], np.cumsum(widths)])
    (a_q, a_k, a_v, a_g, b_x, b_z, b_b, b_c, b_dt, c_q, c_k, c_v, c_g, c_f,
     d_b, d_c, d_x, d_g, mg) = [w[:, int(offs[i]):int(offs[i + 1])] for i in range(len(widths))]
    a_q = _deinterleave_cols(a_q, ATT_HEADS)
    a_k = _deinterleave_cols(a_k, ATT_KV_HEADS)
    pad = jnp.zeros((w.shape[0], LANES - 2 * SSD_HEADS - 2 * GLA_RANK), w.dtype)
    parts = [a_q, a_g, b_z, b_x, b_b, b_c, a_k, a_v, c_v, c_g, c_q, c_k, d_b, d_c, d_x, d_g, mg, b_dt, c_f, pad]
    out = jnp.concatenate(parts, axis=1).astype(BF16)
    assert out.shape[1] == N_PROJ
    return out


def _deinterleave_vec(g):
    return g.reshape(HEAD_DIM // 2, 2).T.reshape(1, HEAD_DIM)


def _pad_lanes(v, lane0=0):
    return jnp.zeros((1, LANES), F32).at[0, lane0:lane0 + v.shape[0]].set(v.astype(F32))


def _head_expand_matrix(reverse):
    e = np.zeros((LANES, BRANCH_WIDTH), np.float32)
    lane0 = SSD_HEADS if reverse else 0
    for r in range(SSD_HEADS):
        e[lane0 + r, r * SSD_HEAD_DIM:(r + 1) * SSD_HEAD_DIM] = 1.0
    return jnp.asarray(e)


def _forget_weight(w_f2_dir, direction):
    lane0 = F1_LANE0 + direction * GLA_RANK
    return jnp.zeros((LANES, w_f2_dir.shape[1]), F32).at[lane0:lane0 + GLA_RANK].set(w_f2_dir).astype(BF16)


def kernel(x, c, ctx, c_ctx, w_mod, b_mod, g_pre, g_post, w_in, g_q, g_k, ssd_conv_w, ssd_conv_b,
           ssd_a_log, ssd_dt_bias, ssd_d, ssd_norm_g, gla_w_f2, gla_b_f, gla_norm_g, sc_conv_w,
           w_branch, w_out):
    batch, seq, d = x.shape
    ctx_len = ctx.shape[1]
    depth = w_in.shape[0]
    geo = _Geom(batch, ctx_len, seq)
    assert d == D_MODEL and batch + 1 <= SUBLANES and seq % GRID_W == 0

    cos_t, sin_t = _rope_tables(geo)
    c_rows = jnp.zeros((SUBLANES, d), F32).at[:batch].set(c).at[batch].set(c_ctx)
    x_all = jnp.concatenate([ctx, x], axis=1).reshape(geo.rows, d)
    e_fwd, e_bwd = _head_expand_matrix(False), _head_expand_matrix(True)

    for l in range(depth):
        need_ctx = l < depth - 1
        mod = _modulation(c_rows, w_mod[l], b_mod[l][None, :])
        mod3 = mod.reshape(SUBLANES, 1, 3 * d)
        h = _prenorm(x_all, g_pre[l][None, :], mod3, geo)
        proj = _matmul(h, _regroup_w_in(w_in[l]), PROJ_TN)

        qh, kh = _qk_prep(proj, cos_t, sin_t, _deinterleave_vec(g_q[l]), _deinterleave_vec(g_k[l]), geo)
        ya = _attention(qh, kh, proj, None, geo, ctx_len, seq, geo.lt)
        if need_ctx:
            ya = _attention(qh, kh, proj, ya, geo, 0, ctx_len, ctx_len)

        xbc, dt = _ssd_conv(proj, ssd_conv_w[l], ssd_conv_b[l][None, :], _pad_lanes(ssd_dt_bias[l].reshape(-1)), geo)
        a_log_row = _pad_lanes(ssd_a_log[l].reshape(-1))
        ysf = _ssd_scan(xbc, dt, a_log_row, e_fwd, geo, False)
        ysb = _ssd_scan(xbc, dt, a_log_row, e_bwd, geo, True)
        skip_row = jnp.repeat(ssd_d[l], SSD_HEAD_DIM)[None, :]
        yb = _ssd_finish(ysf, ysb, xbc, proj, skip_row, ssd_norm_g[l][None, :], geo, need_ctx)

        ogf = _gla_scan(proj, _forget_weight(gla_w_f2[l, 0], 0), gla_b_f[l, 0][None, :], geo, False)
        ogb = _gla_scan(proj, _forget_weight(gla_w_f2[l, 1], 1), gla_b_f[l, 1][None, :], geo, True)
        yg = _gla_finish(ogf, ogb, proj, gla_norm_g[l][None, :], geo, need_ctx)

        yd = _shortconv(proj, sc_conv_w[l], geo, need_ctx)

        m = _merge((ya, yb, yg, yd), proj, w_branch[l].astype(BF16), geo, need_ctx)
        x_all = _out_proj(m, w_out[l].astype(BF16), x_all, mod3, g_post[l][None, :], geo, need_ctx)

    return x_all.reshape(batch, seq, d)
```

```python
import functools

import numpy as np
import jax
import jax.numpy as jnp
from jax import lax
from jax.experimental import pallas as pl
from jax.experimental.pallas import tpu as pltpu

F32 = jnp.float32
BF16 = jnp.bfloat16

D_MODEL = 2048
GRID_W = 64
BRANCH_WIDTH = D_MODEL // 2
N_BRANCH = 4
NORM_EPS = 1e-6
HEAD_DIM = 128
ATT_HEADS = BRANCH_WIDTH // HEAD_DIM
ATT_KV_HEADS = ATT_HEADS // 4
ATT_REP = ATT_HEADS // ATT_KV_HEADS
ROPE_THETA = 10000.0
SSD_HEAD_DIM = 64
SSD_HEADS = BRANCH_WIDTH // SSD_HEAD_DIM
SSD_GROUPS = 2
SSD_HPG = SSD_HEADS // SSD_GROUPS
SSD_STATE = 128
SSD_CONV = 5
SSD_CHUNK = 128
GLA_HEADS = 4
GLA_DV = BRANCH_WIDTH // GLA_HEADS
GLA_DK = GLA_DV // 2
GLA_RANK = 16
GLA_TAU = 16.0
GLA_CHUNK = 64
SC_CONV = 3

LANES = 128
SUBLANES = 8
ROW_TILE = 256
VMEM_LIMIT = 56 * 1024 * 1024

_COL = dict(
    A_Q=0, A_G=1024, B_Z=2048, B_X=3072, B_B=4096, B_C=4352, A_K=4608, A_V=4864,
    C_V=5120, C_G=6144, C_Q=7168, C_K=7680,
    D_ALL=8192, MG=12288, NARROW=20480,
)
N_PROJ = 20608
PROJ_TN = 2944
BF16_SUBLANES = 16
VT_ROWS = HEAD_DIM + BF16_SUBLANES
Q_SCALE = HEAD_DIM ** -0.5 * float(np.log2(np.e))
DT_LANE0 = 0
F1_LANE0 = 32


def _cparams(sem, vmem=VMEM_LIMIT):
    return pltpu.CompilerParams(dimension_semantics=sem, vmem_limit_bytes=vmem)


def _silu(x):
    return x * jax.nn.sigmoid(x)


def _softplus(x):
    return jnp.maximum(x, 0.0) + jnp.log1p(jnp.exp(-jnp.abs(x)))


def _log_sigmoid(x):
    return jnp.minimum(x, 0.0) - jnp.log1p(jnp.exp(-jnp.abs(x)))


def _mod_kernel(c_ref, w_ref, b_ref, o_ref):
    a = _silu(c_ref[...]).astype(BF16)
    o_ref[...] = jnp.dot(a, w_ref[...].astype(BF16), preferred_element_type=F32) + b_ref[...]


def _modulation(c_rows, w_mod, b_mod):
    d, n = w_mod.shape
    tn = 1024
    return pl.pallas_call(
        _mod_kernel,
        grid=(n // tn,),
        in_specs=[pl.BlockSpec((SUBLANES, d), lambda j: (0, 0)),
                  pl.BlockSpec((d, tn), lambda j: (0, j)),
                  pl.BlockSpec((1, tn), lambda j: (0, j))],
        out_specs=pl.BlockSpec((SUBLANES, tn), lambda j: (0, j)),
        out_shape=jax.ShapeDtypeStruct((SUBLANES, n), F32),
        compiler_params=_cparams(("parallel",)),
        name="modulation",
    )(c_rows, w_mod, b_mod)


class _Geom:
    def __init__(self, batch, ctx_len, seq):
        self.batch, self.ctx_len, self.seq = batch, ctx_len, seq
        self.lt = ctx_len + seq
        self.rows = batch * self.lt
        assert ctx_len % ROW_TILE == 0 and seq % ROW_TILE == 0
        self.tpb = self.lt // ROW_TILE
        self.ctx_tiles = ctx_len // ROW_TILE
        self.lat_tiles = seq // ROW_TILE

    def span(self, with_ctx):
        return (0, self.tpb) if with_ctx else (self.ctx_tiles, self.lat_tiles)

    def mod_row(self, b, j):
        return jnp.where(j < self.ctx_tiles, self.batch, b)


def _prenorm_kernel(x_ref, g_ref, sh_ref, sc_ref, o_ref):
    x = x_ref[...]
    y = x * lax.rsqrt(jnp.mean(x * x, axis=-1, keepdims=True) + NORM_EPS)
    o_ref[...] = ((y * g_ref[...]) * (1.0 + sc_ref[0]) + sh_ref[0]).astype(o_ref.dtype)


def _prenorm(x_all, g_pre, mod3, geo):
    d = x_all.shape[1]
    row = lambda b, j: (b * geo.tpb + j, 0)
    return pl.pallas_call(
        _prenorm_kernel,
        grid=(geo.batch, geo.tpb),
        in_specs=[pl.BlockSpec((ROW_TILE, d), row),
                  pl.BlockSpec((1, d), lambda b, j: (0, 0)),
                  pl.BlockSpec((1, 1, d), lambda b, j: (geo.mod_row(b, j), 0, 0)),
                  pl.BlockSpec((1, 1, d), lambda b, j: (geo.mod_row(b, j), 0, 1))],
        out_specs=pl.BlockSpec((ROW_TILE, d), row),
        out_shape=jax.ShapeDtypeStruct(x_all.shape, BF16),
        compiler_params=_cparams(("parallel", "parallel")),
        name="prenorm",
    )(x_all, g_pre, mod3, mod3)


def _matmul_kernel(a_ref, w_ref, o_ref):
    o_ref[...] = jnp.dot(a_ref[...], w_ref[...], preferred_element_type=F32).astype(o_ref.dtype)


def _matmul(a, w, tn, out_dtype=BF16):
    m, k = a.shape
    n = w.shape[1]
    tm = 512 if m % 512 == 0 else 256
    assert m % tm == 0 and n % tn == 0
    return pl.pallas_call(
        _matmul_kernel,
        grid=(n // tn, m // tm),
        in_specs=[pl.BlockSpec((tm, k), lambda j, i: (i, 0)),
                  pl.BlockSpec((k, tn), lambda j, i: (0, j))],
        out_specs=pl.BlockSpec((tm, tn), lambda j, i: (i, j)),
        out_shape=jax.ShapeDtypeStruct((m, n), out_dtype),
        compiler_params=_cparams(("parallel", "parallel")),
        name="in_proj",
    )(a, w)


def _rope_tables(geo):
    rows = geo.seq // GRID_W
    t_row = jnp.repeat(jnp.arange(rows, dtype=F32), GRID_W)
    t_col = jnp.tile(jnp.arange(GRID_W, dtype=F32), rows)
    half = HEAD_DIM // 2
    freqs = ROPE_THETA ** (-(jnp.arange(0, half, 2, dtype=F32) / half))
    ang = jnp.concatenate([t_row[:, None] * freqs, t_col[:, None] * freqs], axis=-1)
    cos, sin = jnp.cos(ang), jnp.sin(ang)
    cos_l = jnp.concatenate([cos, cos], axis=-1)
    sin_l = jnp.concatenate([-sin, sin], axis=-1)
    cos_c = jnp.ones((geo.ctx_len, HEAD_DIM), F32)
    sin_c = jnp.zeros((geo.ctx_len, HEAD_DIM), F32)
    return jnp.concatenate([cos_c, cos_l], axis=0), jnp.concatenate([sin_c, sin_l], axis=0)


def _qk_prep_kernel(q_ref, k_ref, v_ref, cos_ref, sin_ref, gq_ref, gk_ref, qt_ref, ko_ref, vt_ref):
    cos, sin = cos_ref[...], sin_ref[...]
    t = q_ref.shape[0]

    def norm_rope(x, g):
        x = x.astype(F32)
        y = x * lax.rsqrt(jnp.mean(x * x, axis=-1, keepdims=True) + NORM_EPS) * g
        return y * cos + pltpu.roll(y, HEAD_DIM // 2, 1) * sin

    gq, gk = gq_ref[...], gk_ref[...]
    for h in range(ATT_HEADS):
        g, r = divmod(h, ATT_REP)
        y = norm_rope(q_ref[:, h * HEAD_DIM:(h + 1) * HEAD_DIM], gq) * Q_SCALE
        qt_ref[g, 0, :, r * t:(r + 1) * t] = y.T.astype(qt_ref.dtype)
    for g in range(ATT_KV_HEADS):
        hs = slice(g * HEAD_DIM, (g + 1) * HEAD_DIM)
        ko_ref[g] = norm_rope(k_ref[:, hs], gk).astype(ko_ref.dtype)
        vt_ref[g, 0, 0:HEAD_DIM, :] = v_ref[:, hs].astype(F32).T.astype(vt_ref.dtype)
        vt_ref[g, 0, HEAD_DIM:VT_ROWS, :] = jnp.ones((VT_ROWS - HEAD_DIM, t), vt_ref.dtype)


def _qk_prep(proj, cos_t, sin_t, g_q, g_k, geo):
    row = lambda b, j: b * geo.tpb + j
    nq, nk = ATT_HEADS * HEAD_DIM, ATT_KV_HEADS * HEAD_DIM
    kv, t = ATT_KV_HEADS, ROW_TILE
    return pl.pallas_call(
        _qk_prep_kernel,
        grid=(geo.batch, geo.tpb),
        in_specs=[pl.BlockSpec((t, nq), lambda b, j: (row(b, j), _COL["A_Q"] // nq)),
                  pl.BlockSpec((t, nk), lambda b, j: (row(b, j), _COL["A_K"] // nk)),
                  pl.BlockSpec((t, nk), lambda b, j: (row(b, j), _COL["A_V"] // nk)),
                  pl.BlockSpec((t, HEAD_DIM), lambda b, j: (j, 0)),
                  pl.BlockSpec((t, HEAD_DIM), lambda b, j: (j, 0)),
                  pl.BlockSpec((1, HEAD_DIM), lambda b, j: (0, 0)),
                  pl.BlockSpec((1, HEAD_DIM), lambda b, j: (0, 0))],
        out_specs=[pl.BlockSpec((kv, 1, HEAD_DIM, ATT_REP * t), lambda b, j: (b, j, 0, 0)),
                   pl.BlockSpec((kv, t, HEAD_DIM), lambda b, j: (b, j, 0)),
                   pl.BlockSpec((kv, 1, VT_ROWS, t), lambda b, j: (b, j, 0, 0))],
        out_shape=[jax.ShapeDtypeStruct((geo.batch * kv, geo.tpb, HEAD_DIM, ATT_REP * t), BF16),
                   jax.ShapeDtypeStruct((geo.batch * kv, geo.lt, HEAD_DIM), BF16),
                   jax.ShapeDtypeStruct((geo.batch * kv, geo.tpb, VT_ROWS, t), BF16)],
        compiler_params=_cparams(("parallel", "parallel")),
        name="qk_prep",
    )(proj, proj, proj, cos_t, sin_t, g_q, g_k)


def _attn_kernel(qt_ref, k_ref, vt_ref, gate_ref, o_ref, m_ref, acc_ref, s0_ref, s1_ref):
    tk = vt_ref.shape[2]
    n_chunks = vt_ref.shape[0]
    tq = o_ref.shape[0]
    qt = qt_ref[...]
    m_ref[...] = jnp.full(m_ref.shape, -jnp.inf, F32)
    acc_ref[...] = jnp.zeros(acc_ref.shape, F32)

    def scores(c, s_ref):
        k = k_ref[pl.ds(pl.multiple_of(c * tk, tk), tk), :]
        s_ref[...] = jnp.dot(k, qt, preferred_element_type=F32)

    def absorb(c, s_ref):
        s = s_ref[...]
        m_old = m_ref[...]
        m_new = jnp.maximum(m_old, jnp.max(s, axis=0, keepdims=True))
        alpha = jnp.exp2(m_old - m_new)
        p = jnp.exp2((s - m_new).astype(BF16))
        acc_ref[...] = alpha * acc_ref[...] + jnp.dot(vt_ref[c], p, preferred_element_type=F32)
        m_ref[...] = m_new

    def pair(i, carry):
        c = 2 * i
        scores(c + 1, s1_ref)
        absorb(c, s0_ref)
        scores(c + 2, s0_ref)
        absorb(c + 1, s1_ref)
        return carry

    scores(0, s0_ref)
    n_pairs = (n_chunks - 1) // 2
    if n_pairs > 0:
        lax.fori_loop(0, n_pairs, pair, 0)
    if n_chunks % 2 == 0:
        scores(n_chunks - 1, s1_ref)
        absorb(n_chunks - 2, s0_ref)
        absorb(n_chunks - 1, s1_ref)
    else:
        absorb(n_chunks - 1, s0_ref)
    o_t = acc_ref[0:HEAD_DIM, :] / acc_ref[HEAD_DIM:HEAD_DIM + 1, :]
    o = jnp.concatenate([o_t[:, r * tq:(r + 1) * tq].T for r in range(ATT_REP)], axis=1)
    o_ref[...] = (o * _silu(gate_ref[...].astype(F32))).astype(o_ref.dtype)


def _attention(qt, kh, vt, proj, ya_prev, geo, q_first_row, q_rows, kv_rows):
    tq = ROW_TILE
    assert q_first_row % tq == 0 and q_rows % tq == 0 and kv_rows % ROW_TILE == 0
    q0 = q_first_row // tq
    gw = ATT_REP * HEAD_DIM
    kernel = _attn_kernel
    in_specs = [
        pl.BlockSpec((None, None, HEAD_DIM, ATT_REP * tq), lambda b, g, i: (b * ATT_KV_HEADS + g, q0 + i, 0, 0)),
        pl.BlockSpec((None, kv_rows, HEAD_DIM), lambda b, g, i: (b * ATT_KV_HEADS + g, 0, 0)),
        pl.BlockSpec((None, kv_rows // ROW_TILE, VT_ROWS, ROW_TILE), lambda b, g, i: (b * ATT_KV_HEADS + g, 0, 0, 0)),
        pl.BlockSpec((tq, gw), lambda b, g, i: (b * geo.tpb + q0 + i, _COL["A_G"] // gw + g)),
    ]
    out_spec = pl.BlockSpec((tq, gw), lambda b, g, i: (b * geo.tpb + q0 + i, g))
    args = [qt, kh, vt, proj]
    aliases = {}
    if ya_prev is not None:
        in_specs.append(pl.BlockSpec(memory_space=pl.ANY))
        args.append(ya_prev)
        aliases = {4: 0}
        kernel = functools.partial(_drop_last_input, kernel, 4)
    return pl.pallas_call(
        kernel,
        grid=(geo.batch, ATT_KV_HEADS, q_rows // tq),
        in_specs=in_specs,
        out_specs=out_spec,
        out_shape=jax.ShapeDtypeStruct((geo.rows, BRANCH_WIDTH), BF16),
        scratch_shapes=[pltpu.VMEM((1, ATT_REP * tq), F32), pltpu.VMEM((VT_ROWS, ATT_REP * tq), F32),
                        pltpu.VMEM((ROW_TILE, ATT_REP * tq), F32), pltpu.VMEM((ROW_TILE, ATT_REP * tq), F32)],
        input_output_aliases=aliases,
        compiler_params=_cparams(("parallel", "parallel", "arbitrary")),
        name="attention",
    )(*args)


def _drop_last_input(kernel, n_in, *refs):
    return kernel(*refs[:n_in], *refs[n_in + 1:])


def _halo_specs(width, col_block, geo, first, row):
    per = ROW_TILE // SUBLANES
    last = geo.rows // SUBLANES - 1
    prev = pl.BlockSpec((SUBLANES, width), lambda b, j: (jnp.maximum(row(b, j) * per - 1, 0), col_block))
    nxt = pl.BlockSpec((SUBLANES, width), lambda b, j: (jnp.minimum((row(b, j) + 1) * per, last), col_block))
    return prev, nxt


def _fill_ext(ext_ref, cur, prev, nxt, j, geo):
    t = cur.shape[0]
    seg_first = jnp.logical_or(j == 0, j == geo.ctx_tiles)
    seg_last = jnp.logical_or(j == geo.ctx_tiles - 1, j == geo.tpb - 1)
    ext_ref[0:SUBLANES, :] = jnp.where(seg_first, 0.0, prev)
    ext_ref[SUBLANES:SUBLANES + t, :] = cur
    ext_ref[SUBLANES + t:2 * SUBLANES + t, :] = jnp.where(seg_last, 0.0, nxt)


def _ssd_conv_kernel(cur_ref, prev_ref, next_ref, nar_ref, w_ref, b_ref, dtb_ref, xbc_ref, dt_ref, ext_ref, *, geo):
    j = pl.program_id(1)
    _fill_ext(ext_ref, cur_ref[...].astype(F32), prev_ref[...].astype(F32), next_ref[...].astype(F32), j, geo)
    t = cur_ref.shape[0]
    acc = jnp.zeros(cur_ref.shape, F32) + b_ref[...]
    for k in range(SSD_CONV):
        acc = acc + w_ref[k:k + 1, :] * ext_ref[SUBLANES + k - SSD_CONV // 2:SUBLANES + k - SSD_CONV // 2 + t, :]
    xbc_ref[...] = _silu(acc).astype(xbc_ref.dtype)
    lane = lax.broadcasted_iota(jnp.int32, dt_ref.shape, 1)
    dt = _softplus(nar_ref[...].astype(F32) + dtb_ref[...])
    dt_ref[...] = jnp.where(lane < 2 * SSD_HEADS, dt, 0.0)


def _ssd_conv(proj, conv_w, conv_b, dt_bias_row, geo):
    width = BRANCH_WIDTH + 2 * SSD_GROUPS * SSD_STATE
    row = lambda b, j: b * geo.tpb + j
    cb = _COL["B_X"] // width
    prev, nxt = _halo_specs(width, cb, geo, 0, row)
    return pl.pallas_call(
        functools.partial(_ssd_conv_kernel, geo=geo),
        grid=(geo.batch, geo.tpb),
        in_specs=[pl.BlockSpec((ROW_TILE, width), lambda b, j: (row(b, j), cb)), prev, nxt,
                  pl.BlockSpec((ROW_TILE, LANES), lambda b, j: (row(b, j), _COL["NARROW"] // LANES)),
                  pl.BlockSpec((SSD_CONV, width), lambda b, j: (0, 0)),
                  pl.BlockSpec((1, width), lambda b, j: (0, 0)),
                  pl.BlockSpec((1, LANES), lambda b, j: (0, 0))],
        out_specs=[pl.BlockSpec((ROW_TILE, width), lambda b, j: (row(b, j), 0)),
                   pl.BlockSpec((ROW_TILE, LANES), lambda b, j: (row(b, j), 0))],
        out_shape=[jax.ShapeDtypeStruct((geo.rows, width), BF16),
                   jax.ShapeDtypeStruct((geo.rows, LANES), F32)],
        scratch_shapes=[pltpu.VMEM((ROW_TILE + 2 * SUBLANES, width), F32)],
        compiler_params=_cparams(("parallel", "parallel")),
        name="ssd_conv",
    )(proj, proj, proj, proj, conv_w, conv_b, dt_bias_row)


def _scan_chunk(s, n_chunks, n_ctx_chunks, reverse):
    if not reverse:
        return s
    return jnp.where(s < n_ctx_chunks, n_ctx_chunks - 1 - s, n_chunks + n_ctx_chunks - 1 - s)


def _tri(n, reverse):
    t = lax.broadcasted_iota(jnp.int32, (n, n), 0)
    s = lax.broadcasted_iota(jnp.int32, (n, n), 1)
    return (s >= t) if reverse else (s <= t)


def _ssd_scan_kernel(xbc_ref, dt_ref, alog_ref, e_ref, y_ref, h_ref, *, reverse):
    q = SSD_CHUNK
    gw = SSD_HPG * SSD_HEAD_DIM
    lane0 = SSD_HEADS if reverse else 0

    @pl.when(pl.program_id(1) == 0)
    def _():
        h_ref[...] = jnp.zeros_like(h_ref)

    lane = lax.broadcasted_iota(jnp.int32, (1, LANES), 1)
    a = jnp.where(lane < 2 * SSD_HEADS, -jnp.exp(alog_ref[...]), 0.0)
    dt = dt_ref[...]
    mask = _tri(q, reverse)
    cs = jnp.dot(mask.astype(F32), dt * a, preferred_element_type=F32, precision=lax.Precision.HIGHEST)
    cs_last = cs[0:1, :] if reverse else cs[q - 1:q, :]
    e = e_ref[...]
    expand = lambda t: jnp.dot(t, e, preferred_element_type=F32, precision=lax.Precision.HIGHEST)
    x = xbc_ref[:, 0:BRANCH_WIDTH].astype(F32)
    dtx = expand(dt) * x
    wx = (expand(jnp.exp(cs_last - cs)) * dtx).astype(BF16)
    dtx = dtx.astype(BF16)
    ecs = expand(jnp.exp(cs))
    chunk_decay = expand(jnp.broadcast_to(jnp.exp(cs_last), (SUBLANES, LANES)))[0:1, :]
    cs_t = cs.T
    for g in range(SSD_GROUPS):
        bm = xbc_ref[:, BRANCH_WIDTH + g * SSD_STATE:BRANCH_WIDTH + (g + 1) * SSD_STATE]
        c0 = BRANCH_WIDTH + SSD_GROUPS * SSD_STATE + g * SSD_STATE
        cm = xbc_ref[:, c0:c0 + SSD_STATE]
        cb = lax.dot_general(cm, bm, (((1,), (1,)), ((), ())), preferred_element_type=F32)
        h_in = h_ref[g]
        y_off = jnp.dot(cm, h_in.astype(BF16), preferred_element_type=F32) * ecs[:, g * gw:(g + 1) * gw]
        states = jnp.dot(bm.astype(F32).T.astype(BF16), wx[:, g * gw:(g + 1) * gw], preferred_element_type=F32)
        h_ref[g] = chunk_decay[:, g * gw:(g + 1) * gw] * h_in + states
        for r in range(SSD_HPG):
            col = lane0 + g * SSD_HPG + r
            seg = cs[:, col:col + 1] - cs_t[col:col + 1, :]
            decay = jnp.exp(jnp.where(mask, seg, -1e30))
            lo = g * gw + r * SSD_HEAD_DIM
            y_diag = jnp.dot((cb * decay).astype(BF16), dtx[:, lo:lo + SSD_HEAD_DIM], preferred_element_type=F32)
            y_ref[:, lo:lo + SSD_HEAD_DIM] = (y_diag + y_off[:, r * SSD_HEAD_DIM:(r + 1) * SSD_HEAD_DIM]).astype(y_ref.dtype)


def _ssd_scan(xbc, dt, a_log_row, expand_mat, geo, reverse):
    nc, ncc = geo.lt // SSD_CHUNK, geo.ctx_len // SSD_CHUNK
    width = xbc.shape[1]
    row = lambda b, s: b * nc + _scan_chunk(s, nc, ncc, reverse)
    return pl.pallas_call(
        functools.partial(_ssd_scan_kernel, reverse=reverse),
        grid=(geo.batch, nc),
        in_specs=[pl.BlockSpec((SSD_CHUNK, width), lambda b, s: (row(b, s), 0)),
                  pl.BlockSpec((SSD_CHUNK, LANES), lambda b, s: (row(b, s), 0)),
                  pl.BlockSpec((1, LANES), lambda b, s: (0, 0)),
                  pl.BlockSpec((LANES, BRANCH_WIDTH), lambda b, s: (0, 0))],
        out_specs=pl.BlockSpec((SSD_CHUNK, BRANCH_WIDTH), lambda b, s: (row(b, s), 0)),
        out_shape=jax.ShapeDtypeStruct((geo.rows, BRANCH_WIDTH), F32),
        scratch_shapes=[pltpu.VMEM((SSD_GROUPS, SSD_STATE, SSD_HPG * SSD_HEAD_DIM), F32)],
        compiler_params=_cparams(("parallel", "arbitrary")),
        name="ssd_scan_bwd" if reverse else "ssd_scan_fwd",
    )(xbc, dt, a_log_row, expand_mat)


def _ssd_finish_kernel(yf_ref, yb_ref, xbc_ref, z_ref, skip_ref, g_ref, o_ref):
    xh = xbc_ref[:, 0:BRANCH_WIDTH].astype(F32)
    y = yf_ref[...] + yb_ref[...] + skip_ref[...] * xh
    y = y * _silu(z_ref[...].astype(F32))
    y = y * lax.rsqrt(jnp.mean(y * y, axis=-1, keepdims=True) + NORM_EPS)
    o_ref[...] = (y * g_ref[...]).astype(o_ref.dtype)


def _ssd_finish(yf, yb, xbc, proj, skip_row, norm_g, geo, with_ctx):
    first, nt = geo.span(with_ctx)
    row = lambda b, j: b * geo.tpb + first + j
    w = BRANCH_WIDTH
    return pl.pallas_call(
        _ssd_finish_kernel,
        grid=(geo.batch, nt),
        in_specs=[pl.BlockSpec((ROW_TILE, w), lambda b, j: (row(b, j), 0)),
                  pl.BlockSpec((ROW_TILE, w), lambda b, j: (row(b, j), 0)),
                  pl.BlockSpec((ROW_TILE, xbc.shape[1]), lambda b, j: (row(b, j), 0)),
                  pl.BlockSpec((ROW_TILE, w), lambda b, j: (row(b, j), _COL["B_Z"] // w)),
                  pl.BlockSpec((1, w), lambda b, j: (0, 0)),
                  pl.BlockSpec((1, w), lambda b, j: (0, 0))],
        out_specs=pl.BlockSpec((ROW_TILE, w), lambda b, j: (row(b, j), 0)),
        out_shape=jax.ShapeDtypeStruct((geo.rows, w), BF16),
        compiler_params=_cparams(("parallel", "parallel")),
        name="ssd_finish",
    )(yf, yb, xbc, proj, skip_row, norm_g)


def _gla_scan_kernel(q_ref, k_ref, v_ref, nar_ref, w2_ref, bf_ref, o_ref, h_ref, *, reverse):
    n = GLA_CHUNK

    @pl.when(pl.program_id(1) == 0)
    def _():
        h_ref[...] = jnp.zeros_like(h_ref)

    logit = jnp.dot(nar_ref[...], w2_ref[...], preferred_element_type=F32) + bf_ref[...]
    gl = _log_sigmoid(logit) / GLA_TAU
    mask = _tri(n, reverse)
    b = jnp.dot(mask.astype(F32), gl, preferred_element_type=F32, precision=lax.Precision.HIGHEST)
    b_last = b[0:1, :] if reverse else b[n - 1:n, :]
    eb = jnp.exp(b)
    q = q_ref[...].astype(F32) * (GLA_DK ** -0.5)
    k = k_ref[...].astype(F32)
    qe = (q * eb).astype(BF16)
    ke = (k * jnp.exp(-b)).astype(BF16)
    kd = (k * jnp.exp(b_last - b)).astype(BF16)
    dec = jnp.exp(b_last)
    for h in range(GLA_HEADS):
        ks = slice(h * GLA_DK, (h + 1) * GLA_DK)
        vs = slice(h * GLA_DV, (h + 1) * GLA_DV)
        v = v_ref[:, vs]
        att = lax.dot_general(qe[:, ks], ke[:, ks], (((1,), (1,)), ((), ())), preferred_element_type=F32)
        att = jnp.where(mask, att, 0.0).astype(BF16)
        h_in = h_ref[h]
        o_inter = lax.dot_general(qe[:, ks], h_in.astype(BF16), (((1,), (1,)), ((), ())), preferred_element_type=F32)
        o_ref[:, vs] = (jnp.dot(att, v, preferred_element_type=F32) + o_inter).astype(o_ref.dtype)
        upd = jnp.dot(v.astype(F32).T.astype(BF16), kd[:, ks], preferred_element_type=F32)
        h_ref[h] = dec[:, ks] * h_in + upd


def _gla_scan(proj, w2, b_f, geo, reverse):
    nc, ncc = geo.lt // GLA_CHUNK, geo.ctx_len // GLA_CHUNK
    row = lambda b, s: b * nc + _scan_chunk(s, nc, ncc, reverse)
    kw, vw = GLA_HEADS * GLA_DK, GLA_HEADS * GLA_DV
    return pl.pallas_call(
        functools.partial(_gla_scan_kernel, reverse=reverse),
        grid=(geo.batch, nc),
        in_specs=[pl.BlockSpec((GLA_CHUNK, kw), lambda b, s: (row(b, s), _COL["C_Q"] // kw)),
                  pl.BlockSpec((GLA_CHUNK, kw), lambda b, s: (row(b, s), _COL["C_K"] // kw)),
                  pl.BlockSpec((GLA_CHUNK, vw), lambda b, s: (row(b, s), _COL["C_V"] // vw)),
                  pl.BlockSpec((GLA_CHUNK, LANES), lambda b, s: (row(b, s), _COL["NARROW"] // LANES)),
                  pl.BlockSpec((LANES, kw), lambda b, s: (0, 0)),
                  pl.BlockSpec((1, kw), lambda b, s: (0, 0))],
        out_specs=pl.BlockSpec((GLA_CHUNK, vw), lambda b, s: (row(b, s), 0)),
        out_shape=jax.ShapeDtypeStruct((geo.rows, vw), F32),
        scratch_shapes=[pltpu.VMEM((GLA_HEADS, GLA_DV, GLA_DK), F32)],
        compiler_params=_cparams(("parallel", "arbitrary")),
        name="gla_scan_bwd" if reverse else "gla_scan_fwd",
    )(proj, proj, proj, proj, w2, b_f)


def _gla_finish_kernel(of_ref, ob_ref, gate_ref, g_ref, o_ref):
    g = g_ref[...]
    for h in range(GLA_HEADS):
        vs = slice(h * GLA_DV, (h + 1) * GLA_DV)
        o = (of_ref[:, vs] + ob_ref[:, vs])
        o = o * lax.rsqrt(jnp.mean(o * o, axis=-1, keepdims=True) + NORM_EPS) * g
        o_ref[:, vs] = (o * _silu(gate_ref[:, vs].astype(F32))).astype(o_ref.dtype)


def _gla_finish(of, ob, proj, norm_g, geo, with_ctx):
    first, nt = geo.span(with_ctx)
    row = lambda b, j: b * geo.tpb + first + j
    w = BRANCH_WIDTH
    return pl.pallas_call(
        _gla_finish_kernel,
        grid=(geo.batch, nt),
        in_specs=[pl.BlockSpec((ROW_TILE, w), lambda b, j: (row(b, j), 0)),
                  pl.BlockSpec((ROW_TILE, w), lambda b, j: (row(b, j), 0)),
                  pl.BlockSpec((ROW_TILE, w), lambda b, j: (row(b, j), _COL["C_G"] // w)),
                  pl.BlockSpec((1, GLA_DV), lambda b, j: (0, 0))],
        out_specs=pl.BlockSpec((ROW_TILE, w), lambda b, j: (row(b, j), 0)),
        out_shape=jax.ShapeDtypeStruct((geo.rows, w), BF16),
        compiler_params=_cparams(("parallel", "parallel")),
        name="gla_finish",
    )(of, ob, proj, norm_g)


def _shortconv_kernel(cur_ref, prev_ref, next_ref, w_ref, o_ref, ext_ref, *, geo, first):
    j = pl.program_id(1) + first
    w = BRANCH_WIDTH
    u = lambda ref: ref[:, w:2 * w].astype(F32) * ref[:, 2 * w:3 * w].astype(F32)
    _fill_ext(ext_ref, u(cur_ref), u(prev_ref), u(next_ref), j, geo)
    t = cur_ref.shape[0]
    acc = jnp.zeros((t, w), F32)
    for k in range(SC_CONV):
        acc = acc + w_ref[k:k + 1, :] * ext_ref[SUBLANES + k - SC_CONV // 2:SUBLANES + k - SC_CONV // 2 + t, :]
    y = cur_ref[:, 0:w].astype(F32) * acc * _silu(cur_ref[:, 3 * w:4 * w].astype(F32))
    o_ref[...] = y.astype(o_ref.dtype)


def _shortconv(proj, conv_w, geo, with_ctx):
    first, nt = geo.span(with_ctx)
    row = lambda b, j: b * geo.tpb + first + j
    width = 4 * BRANCH_WIDTH
    cb = _COL["D_ALL"] // width
    prev, nxt = _halo_specs(width, cb, geo, first, row)
    return pl.pallas_call(
        functools.partial(_shortconv_kernel, geo=geo, first=first),
        grid=(geo.batch, nt),
        in_specs=[pl.BlockSpec((ROW_TILE, width), lambda b, j: (row(b, j), cb)), prev, nxt,
                  pl.BlockSpec((SC_CONV, BRANCH_WIDTH), lambda b, j: (0, 0))],
        out_specs=pl.BlockSpec((ROW_TILE, BRANCH_WIDTH), lambda b, j: (row(b, j), 0)),
        out_shape=jax.ShapeDtypeStruct((geo.rows, BRANCH_WIDTH), BF16),
        scratch_shapes=[pltpu.VMEM((ROW_TILE + 2 * SUBLANES, BRANCH_WIDTH), F32)],
        compiler_params=_cparams(("parallel", "parallel")),
        name="shortconv",
    )(proj, proj, proj, conv_w)


def _merge_kernel(ya_ref, yb_ref, yg_ref, yd_ref, gate_ref, w_ref, o_ref, acc_ref):
    i = pl.program_id(2)
    gate = jax.nn.sigmoid(gate_ref[...].astype(F32))
    for n, y_ref in enumerate((ya_ref, yb_ref, yg_ref, yd_ref)):
        @pl.when(i == n)
        def _(y_ref=y_ref, n=n):
            term = gate * jnp.dot(y_ref[...], w_ref[...], preferred_element_type=F32)
            if n == 0:
                acc_ref[...] = term
            else:
                acc_ref[...] += term

    @pl.when(i == N_BRANCH - 1)
    def _():
        o_ref[...] = acc_ref[...].astype(o_ref.dtype)


def _merge(ys, proj, w_branch, geo, with_ctx):
    first, nt = geo.span(with_ctx)
    row = lambda b, j, i: b * geo.tpb + first + j
    w, d = BRANCH_WIDTH, D_MODEL
    y_spec = pl.BlockSpec((ROW_TILE, w), lambda b, j, i: (row(b, j, i), 0))
    return pl.pallas_call(
        _merge_kernel,
        grid=(geo.batch, nt, N_BRANCH),
        in_specs=[y_spec, y_spec, y_spec, y_spec,
                  pl.BlockSpec((ROW_TILE, d), lambda b, j, i: (row(b, j, i), _COL["MG"] // d + i)),
                  pl.BlockSpec((None, w, d), lambda b, j, i: (i, 0, 0))],
        out_specs=pl.BlockSpec((ROW_TILE, d), lambda b, j, i: (row(b, j, i), 0)),
        out_shape=jax.ShapeDtypeStruct((geo.rows, d), BF16),
        scratch_shapes=[pltpu.VMEM((ROW_TILE, d), F32)],
        compiler_params=_cparams(("parallel", "parallel", "arbitrary")),
        name="merge",
    )(*ys, proj, w_branch)


def _out_kernel(m_ref, w_ref, x_ref, gt_ref, g_ref, o_ref):
    out = jnp.dot(m_ref[...], w_ref[...], preferred_element_type=F32)
    y = out * lax.rsqrt(jnp.mean(out * out, axis=-1, keepdims=True) + NORM_EPS) * g_ref[...]
    o_ref[...] = x_ref[...] + gt_ref[0] * y


def _out_proj(m, w_out, x_all, mod3, g_post, geo, with_ctx):
    first, nt = geo.span(with_ctx)
    d = D_MODEL
    row = lambda b, j: b * geo.tpb + first + j
    out_rows = geo.rows if with_ctx else geo.batch * geo.seq
    out_row = row if with_ctx else (lambda b, j: b * nt + j)
    return pl.pallas_call(
        _out_kernel,
        grid=(geo.batch, nt),
        in_specs=[pl.BlockSpec((ROW_TILE, d), lambda b, j: (row(b, j), 0)),
                  pl.BlockSpec((d, d), lambda b, j: (0, 0)),
                  pl.BlockSpec((ROW_TILE, d), lambda b, j: (row(b, j), 0)),
                  pl.BlockSpec((1, 1, d), lambda b, j: (geo.mod_row(b, first + j), 0, 2)),
                  pl.BlockSpec((1, d), lambda b, j: (0, 0))],
        out_specs=pl.BlockSpec((ROW_TILE, d), lambda b, j: (out_row(b, j), 0)),
        out_shape=jax.ShapeDtypeStruct((out_rows, d), F32),
        compiler_params=_cparams(("parallel", "parallel")),
        name="out_proj",
    )(m, w_out, x_all, mod3, g_post)


def _deinterleave_cols(w, heads):
    k = w.shape[0]
    return w.reshape(k, heads, HEAD_DIM // 2, 2).swapaxes(2, 3).reshape(k, heads * HEAD_DIM)


def _regroup_w_in(w):
    bw = BRANCH_WIDTH
    widths = (ATT_HEADS * HEAD_DIM, ATT_KV_HEADS * HEAD_DIM, ATT_KV_HEADS * HEAD_DIM, bw,
              bw, bw, SSD_GROUPS * SSD_STATE, SSD_GROUPS * SSD_STATE, 2 * SSD_HEADS,
              GLA_HEADS * GLA_DK, GLA_HEADS * GLA_DK, GLA_HEADS * GLA_DV, bw, 2 * GLA_RANK,
              bw, bw, bw, bw, N_BRANCH * D_MODEL)
    offs = np.concatenate([[0], np.cumsum(widths)])
    (a_q, a_k, a_v, a_g, b_x, b_z, b_b, b_c, b_dt, c_q, c_k, c_v, c_g, c_f,
     d_b, d_c, d_x, d_g, mg) = [w[:, int(offs[i]):int(offs[i + 1])] for i in range(len(widths))]
    a_q = _deinterleave_cols(a_q, ATT_HEADS)
    a_k = _deinterleave_cols(a_k, ATT_KV_HEADS)
    pad = jnp.zeros((w.shape[0], LANES - 2 * SSD_HEADS - 2 * GLA_RANK), w.dtype)
    parts = [a_q, a_g, b_z, b_x, b_b, b_c, a_k, a_v, c_v, c_g, c_q, c_k, d_b, d_c, d_x, d_g, mg, b_dt, c_f, pad]
    out = jnp.concatenate(parts, axis=1).astype(BF16)
    assert out.shape[1] == N_PROJ
    return out


def _deinterleave_vec(g):
    return g.reshape(HEAD_DIM // 2, 2).T.reshape(1, HEAD_DIM)


def _pad_lanes(v, lane0=0):
    return jnp.zeros((1, LANES), F32).at[0, lane0:lane0 + v.shape[0]].set(v.astype(F32))


def _head_expand_matrix(reverse):
    e = np.zeros((LANES, BRANCH_WIDTH), np.float32)
    lane0 = SSD_HEADS if reverse else 0
    for r in range(SSD_HEADS):
        e[lane0 + r, r * SSD_HEAD_DIM:(r + 1) * SSD_HEAD_DIM] = 1.0
    return jnp.asarray(e)


def _forget_weight(w_f2_dir, direction):
    lane0 = F1_LANE0 + direction * GLA_RANK
    return jnp.zeros((LANES, w_f2_dir.shape[1]), F32).at[lane0:lane0 + GLA_RANK].set(w_f2_dir).astype(BF16)


def kernel(x, c, ctx, c_ctx, w_mod, b_mod, g_pre, g_post, w_in, g_q, g_k, ssd_conv_w, ssd_conv_b,
           ssd_a_log, ssd_dt_bias, ssd_d, ssd_norm_g, gla_w_f2, gla_b_f, gla_norm_g, sc_conv_w,
           w_branch, w_out):
    batch, seq, d = x.shape
    ctx_len = ctx.shape[1]
    depth = w_in.shape[0]
    geo = _Geom(batch, ctx_len, seq)
    assert d == D_MODEL and batch + 1 <= SUBLANES and seq % GRID_W == 0

    cos_t, sin_t = _rope_tables(geo)
    c_rows = jnp.zeros((SUBLANES, d), F32).at[:batch].set(c).at[batch].set(c_ctx)
    x_all = jnp.concatenate([ctx, x], axis=1).reshape(geo.rows, d)
    e_fwd, e_bwd = _head_expand_matrix(False), _head_expand_matrix(True)

    for l in range(depth):
        need_ctx = l < depth - 1
        mod = _modulation(c_rows, w_mod[l], b_mod[l][None, :])
        mod3 = mod.reshape(SUBLANES, 1, 3 * d)
        h = _prenorm(x_all, g_pre[l][None, :], mod3, geo)
        proj = _matmul(h, _regroup_w_in(w_in[l]), PROJ_TN)

        qt, kh, vt = _qk_prep(proj, cos_t, sin_t, _deinterleave_vec(g_q[l]), _deinterleave_vec(g_k[l]), geo)
        ya = _attention(qt, kh, vt, proj, None, geo, ctx_len, seq, geo.lt)
        if need_ctx:
            ya = _attention(qt, kh, vt, proj, ya, geo, 0, ctx_len, ctx_len)

        xbc, dt = _ssd_conv(proj, ssd_conv_w[l], ssd_conv_b[l][None, :], _pad_lanes(ssd_dt_bias[l].reshape(-1)), geo)
        a_log_row = _pad_lanes(ssd_a_log[l].reshape(-1))
        ysf = _ssd_scan(xbc, dt, a_log_row, e_fwd, geo, False)
        ysb = _ssd_scan(xbc, dt, a_log_row, e_bwd, geo, True)
        skip_row = jnp.repeat(ssd_d[l], SSD_HEAD_DIM)[None, :]
        yb = _ssd_finish(ysf, ysb, xbc, proj, skip_row, ssd_norm_g[l][None, :], geo, need_ctx)

        ogf = _gla_scan(proj, _forget_weight(gla_w_f2[l, 0], 0), gla_b_f[l, 0][None, :], geo, False)
        ogb = _gla_scan(proj, _forget_weight(gla_w_f2[l, 1], 1), gla_b_f[l, 1][None, :], geo, True)
        yg = _gla_finish(ogf, ogb, proj, gla_norm_g[l][None, :], geo, need_ctx)

        yd = _shortconv(proj, sc_conv_w[l], geo, need_ctx)

        m = _merge((ya, yb, yg, yd), proj, w_branch[l].astype(BF16), geo, need_ctx)
        x_all = _out_proj(m, w_out[l].astype(BF16), x_all, mod3, g_post[l][None, :], geo, need_ctx)

    return x_all.reshape(batch, seq, d)
```

```python
import functools

import numpy as np
import jax
import jax.numpy as jnp
from jax import lax
from jax.experimental import pallas as pl
from jax.experimental.pallas import tpu as pltpu

F32 = jnp.float32
BF16 = jnp.bfloat16

D_MODEL = 2048
GRID_W = 64
BRANCH_WIDTH = D_MODEL // 2
N_BRANCH = 4
NORM_EPS = 1e-6
HEAD_DIM = 128
ATT_HEADS = BRANCH_WIDTH // HEAD_DIM
ATT_KV_HEADS = ATT_HEADS // 4
ATT_REP = ATT_HEADS // ATT_KV_HEADS
ROPE_THETA = 10000.0
SSD_HEAD_DIM = 64
SSD_HEADS = BRANCH_WIDTH // SSD_HEAD_DIM
SSD_GROUPS = 2
SSD_HPG = SSD_HEADS // SSD_GROUPS
SSD_STATE = 128
SSD_CONV = 5
SSD_CHUNK = 128
GLA_HEADS = 4
GLA_DV = BRANCH_WIDTH // GLA_HEADS
GLA_DK = GLA_DV // 2
GLA_RANK = 16
GLA_TAU = 16.0
GLA_CHUNK = 64
SC_CONV = 3

LANES = 128
SUBLANES = 8
ROW_TILE = 256
SCAN_BLOCK = ROW_TILE
VMEM_LIMIT = 56 * 1024 * 1024

_COL = dict(
    A_Q=0, A_G=1024, B_Z=2048, B_X=3072, B_B=4096, B_C=4352, A_K=4608, A_V=4864,
    C_V=5120, C_G=6144, C_Q=7168, C_K=7680,
    D_ALL=8192, MG=12288, NARROW=20480,
)
N_PROJ = 20608
PROJ_TN = 2944
BF16_SUBLANES = 16
VT_ROWS = HEAD_DIM + BF16_SUBLANES
Q_SCALE = HEAD_DIM ** -0.5 * float(np.log2(np.e))
DT_LANE0 = 0
F1_LANE0 = 32


def _cparams(sem, vmem=VMEM_LIMIT):
    return pltpu.CompilerParams(dimension_semantics=sem, vmem_limit_bytes=vmem)


def _silu(x):
    return x * jax.nn.sigmoid(x)


def _softplus(x):
    return jnp.maximum(x, 0.0) + jnp.log1p(jnp.exp(-jnp.abs(x)))


def _log_sigmoid(x):
    return jnp.minimum(x, 0.0) - jnp.log1p(jnp.exp(-jnp.abs(x)))


def _mod_kernel(c_ref, w_ref, b_ref, o_ref):
    a = _silu(c_ref[...]).astype(BF16)
    o_ref[...] = jnp.dot(a, w_ref[...].astype(BF16), preferred_element_type=F32) + b_ref[...]


def _modulation(c_rows, w_mod, b_mod):
    d, n = w_mod.shape
    tn = 1024
    return pl.pallas_call(
        _mod_kernel,
        grid=(n // tn,),
        in_specs=[pl.BlockSpec((SUBLANES, d), lambda j: (0, 0)),
                  pl.BlockSpec((d, tn), lambda j: (0, j)),
                  pl.BlockSpec((1, tn), lambda j: (0, j))],
        out_specs=pl.BlockSpec((SUBLANES, tn), lambda j: (0, j)),
        out_shape=jax.ShapeDtypeStruct((SUBLANES, n), F32),
        compiler_params=_cparams(("parallel",)),
        name="modulation",
    )(c_rows, w_mod, b_mod)


class _Geom:
    def __init__(self, batch, ctx_len, seq):
        self.batch, self.ctx_len, self.seq = batch, ctx_len, seq
        self.lt = ctx_len + seq
        self.rows = batch * self.lt
        assert ctx_len % ROW_TILE == 0 and seq % ROW_TILE == 0
        self.tpb = self.lt // ROW_TILE
        self.ctx_tiles = ctx_len // ROW_TILE
        self.lat_tiles = seq // ROW_TILE

    def span(self, with_ctx):
        return (0, self.tpb) if with_ctx else (self.ctx_tiles, self.lat_tiles)

    def mod_row(self, b, j):
        return jnp.where(j < self.ctx_tiles, self.batch, b)


def _prenorm_kernel(x_ref, g_ref, sh_ref, sc_ref, o_ref):
    x = x_ref[...]
    y = x * lax.rsqrt(jnp.mean(x * x, axis=-1, keepdims=True) + NORM_EPS)
    o_ref[...] = ((y * g_ref[...]) * (1.0 + sc_ref[0]) + sh_ref[0]).astype(o_ref.dtype)


def _prenorm(x_all, g_pre, mod3, geo):
    d = x_all.shape[1]
    row = lambda b, j: (b * geo.tpb + j, 0)
    return pl.pallas_call(
        _prenorm_kernel,
        grid=(geo.batch, geo.tpb),
        in_specs=[pl.BlockSpec((ROW_TILE, d), row),
                  pl.BlockSpec((1, d), lambda b, j: (0, 0)),
                  pl.BlockSpec((1, 1, d), lambda b, j: (geo.mod_row(b, j), 0, 0)),
                  pl.BlockSpec((1, 1, d), lambda b, j: (geo.mod_row(b, j), 0, 1))],
        out_specs=pl.BlockSpec((ROW_TILE, d), row),
        out_shape=jax.ShapeDtypeStruct(x_all.shape, BF16),
        compiler_params=_cparams(("parallel", "parallel")),
        name="prenorm",
    )(x_all, g_pre, mod3, mod3)


def _matmul_kernel(a_ref, w_ref, o_ref):
    o_ref[...] = jnp.dot(a_ref[...], w_ref[...], preferred_element_type=F32).astype(o_ref.dtype)


def _matmul(a, w, tn, out_dtype=BF16):
    m, k = a.shape
    n = w.shape[1]
    tm = 512 if m % 512 == 0 else 256
    assert m % tm == 0 and n % tn == 0
    return pl.pallas_call(
        _matmul_kernel,
        grid=(n // tn, m // tm),
        in_specs=[pl.BlockSpec((tm, k), lambda j, i: (i, 0)),
                  pl.BlockSpec((k, tn), lambda j, i: (0, j))],
        out_specs=pl.BlockSpec((tm, tn), lambda j, i: (i, j)),
        out_shape=jax.ShapeDtypeStruct((m, n), out_dtype),
        compiler_params=_cparams(("parallel", "parallel")),
        name="in_proj",
    )(a, w)


def _rope_tables(geo):
    rows = geo.seq // GRID_W
    t_row = jnp.repeat(jnp.arange(rows, dtype=F32), GRID_W)
    t_col = jnp.tile(jnp.arange(GRID_W, dtype=F32), rows)
    half = HEAD_DIM // 2
    freqs = ROPE_THETA ** (-(jnp.arange(0, half, 2, dtype=F32) / half))
    ang = jnp.concatenate([t_row[:, None] * freqs, t_col[:, None] * freqs], axis=-1)
    cos, sin = jnp.cos(ang), jnp.sin(ang)
    cos_l = jnp.concatenate([cos, cos], axis=-1)
    sin_l = jnp.concatenate([-sin, sin], axis=-1)
    cos_c = jnp.ones((geo.ctx_len, HEAD_DIM), F32)
    sin_c = jnp.zeros((geo.ctx_len, HEAD_DIM), F32)
    return jnp.concatenate([cos_c, cos_l], axis=0), jnp.concatenate([sin_c, sin_l], axis=0)


def _qk_prep_kernel(q_ref, k_ref, v_ref, cos_ref, sin_ref, gq_ref, gk_ref, qt_ref, ko_ref, vt_ref):
    cos, sin = cos_ref[...], sin_ref[...]
    t = q_ref.shape[0]

    def norm_rope(x, g):
        x = x.astype(F32)
        y = x * lax.rsqrt(jnp.mean(x * x, axis=-1, keepdims=True) + NORM_EPS) * g
        return y * cos + pltpu.roll(y, HEAD_DIM // 2, 1) * sin

    gq, gk = gq_ref[...], gk_ref[...]
    for h in range(ATT_HEADS):
        g, r = divmod(h, ATT_REP)
        y = norm_rope(q_ref[:, h * HEAD_DIM:(h + 1) * HEAD_DIM], gq) * Q_SCALE
        qt_ref[g, 0, :, r * t:(r + 1) * t] = y.T.astype(qt_ref.dtype)
    for g in range(ATT_KV_HEADS):
        hs = slice(g * HEAD_DIM, (g + 1) * HEAD_DIM)
        ko_ref[g] = norm_rope(k_ref[:, hs], gk).astype(ko_ref.dtype)
        vt_ref[g, 0, 0:HEAD_DIM, :] = v_ref[:, hs].astype(F32).T.astype(vt_ref.dtype)
        vt_ref[g, 0, HEAD_DIM:VT_ROWS, :] = jnp.ones((VT_ROWS - HEAD_DIM, t), vt_ref.dtype)


def _qk_prep(proj, cos_t, sin_t, g_q, g_k, geo):
    row = lambda b, j: b * geo.tpb + j
    nq, nk = ATT_HEADS * HEAD_DIM, ATT_KV_HEADS * HEAD_DIM
    kv, t = ATT_KV_HEADS, ROW_TILE
    return pl.pallas_call(
        _qk_prep_kernel,
        grid=(geo.batch, geo.tpb),
        in_specs=[pl.BlockSpec((t, nq), lambda b, j: (row(b, j), _COL["A_Q"] // nq)),
                  pl.BlockSpec((t, nk), lambda b, j: (row(b, j), _COL["A_K"] // nk)),
                  pl.BlockSpec((t, nk), lambda b, j: (row(b, j), _COL["A_V"] // nk)),
                  pl.BlockSpec((t, HEAD_DIM), lambda b, j: (j, 0)),
                  pl.BlockSpec((t, HEAD_DIM), lambda b, j: (j, 0)),
                  pl.BlockSpec((1, HEAD_DIM), lambda b, j: (0, 0)),
                  pl.BlockSpec((1, HEAD_DIM), lambda b, j: (0, 0))],
        out_specs=[pl.BlockSpec((kv, 1, HEAD_DIM, ATT_REP * t), lambda b, j: (b, j, 0, 0)),
                   pl.BlockSpec((kv, t, HEAD_DIM), lambda b, j: (b, j, 0)),
                   pl.BlockSpec((kv, 1, VT_ROWS, t), lambda b, j: (b, j, 0, 0))],
        out_shape=[jax.ShapeDtypeStruct((geo.batch * kv, geo.tpb, HEAD_DIM, ATT_REP * t), BF16),
                   jax.ShapeDtypeStruct((geo.batch * kv, geo.lt, HEAD_DIM), BF16),
                   jax.ShapeDtypeStruct((geo.batch * kv, geo.tpb, VT_ROWS, t), BF16)],
        compiler_params=_cparams(("parallel", "parallel")),
        name="qk_prep",
    )(proj, proj, proj, cos_t, sin_t, g_q, g_k)


def _attn_kernel(qt_ref, k_ref, vt_ref, gate_ref, o_ref, m_ref, acc_ref, s0_ref, s1_ref):
    tk = vt_ref.shape[2]
    n_chunks = vt_ref.shape[0]
    tq = o_ref.shape[0]
    qt = qt_ref[...]
    m_ref[...] = jnp.full(m_ref.shape, -jnp.inf, F32)
    acc_ref[...] = jnp.zeros(acc_ref.shape, F32)

    def scores(c, s_ref):
        k = k_ref[pl.ds(pl.multiple_of(c * tk, tk), tk), :]
        s_ref[...] = jnp.dot(k, qt, preferred_element_type=F32)

    def absorb(c, s_ref):
        s = s_ref[...]
        m_old = m_ref[...]
        m_new = jnp.maximum(m_old, jnp.max(s, axis=0, keepdims=True))
        alpha = jnp.exp2(m_old - m_new)
        p = jnp.exp2((s - m_new).astype(BF16))
        acc_ref[...] = alpha * acc_ref[...] + jnp.dot(vt_ref[c], p, preferred_element_type=F32)
        m_ref[...] = m_new

    def pair(i, carry):
        c = 2 * i
        scores(c + 1, s1_ref)
        absorb(c, s0_ref)
        scores(c + 2, s0_ref)
        absorb(c + 1, s1_ref)
        return carry

    scores(0, s0_ref)
    n_pairs = (n_chunks - 1) // 2
    if n_pairs > 0:
        lax.fori_loop(0, n_pairs, pair, 0)
    if n_chunks % 2 == 0:
        scores(n_chunks - 1, s1_ref)
        absorb(n_chunks - 2, s0_ref)
        absorb(n_chunks - 1, s1_ref)
    else:
        absorb(n_chunks - 1, s0_ref)
    o_t = acc_ref[0:HEAD_DIM, :] / acc_ref[HEAD_DIM:HEAD_DIM + 1, :]
    o = jnp.concatenate([o_t[:, r * tq:(r + 1) * tq].T for r in range(ATT_REP)], axis=1)
    o_ref[...] = (o * _silu(gate_ref[...].astype(F32))).astype(o_ref.dtype)


def _attention(qt, kh, vt, proj, ya_prev, geo, q_first_row, q_rows, kv_rows):
    tq = ROW_TILE
    assert q_first_row % tq == 0 and q_rows % tq == 0 and kv_rows % ROW_TILE == 0
    q0 = q_first_row // tq
    gw = ATT_REP * HEAD_DIM
    kernel = _attn_kernel
    in_specs = [
        pl.BlockSpec((None, None, HEAD_DIM, ATT_REP * tq), lambda b, g, i: (b * ATT_KV_HEADS + g, q0 + i, 0, 0)),
        pl.BlockSpec((None, kv_rows, HEAD_DIM), lambda b, g, i: (b * ATT_KV_HEADS + g, 0, 0)),
        pl.BlockSpec((None, kv_rows // ROW_TILE, VT_ROWS, ROW_TILE), lambda b, g, i: (b * ATT_KV_HEADS + g, 0, 0, 0)),
        pl.BlockSpec((tq, gw), lambda b, g, i: (b * geo.tpb + q0 + i, _COL["A_G"] // gw + g)),
    ]
    out_spec = pl.BlockSpec((tq, gw), lambda b, g, i: (b * geo.tpb + q0 + i, g))
    args = [qt, kh, vt, proj]
    aliases = {}
    if ya_prev is not None:
        in_specs.append(pl.BlockSpec(memory_space=pl.ANY))
        args.append(ya_prev)
        aliases = {4: 0}
        kernel = functools.partial(_drop_last_input, kernel, 4)
    return pl.pallas_call(
        kernel,
        grid=(geo.batch, ATT_KV_HEADS, q_rows // tq),
        in_specs=in_specs,
        out_specs=out_spec,
        out_shape=jax.ShapeDtypeStruct((geo.rows, BRANCH_WIDTH), BF16),
        scratch_shapes=[pltpu.VMEM((1, ATT_REP * tq), F32), pltpu.VMEM((VT_ROWS, ATT_REP * tq), F32),
                        pltpu.VMEM((ROW_TILE, ATT_REP * tq), F32), pltpu.VMEM((ROW_TILE, ATT_REP * tq), F32)],
        input_output_aliases=aliases,
        compiler_params=_cparams(("parallel", "parallel", "arbitrary")),
        name="attention",
    )(*args)


def _drop_last_input(kernel, n_in, *refs):
    return kernel(*refs[:n_in], *refs[n_in + 1:])


def _halo_specs(width, col_block, geo, first, row):
    per = ROW_TILE // SUBLANES
    last = geo.rows // SUBLANES - 1
    prev = pl.BlockSpec((SUBLANES, width), lambda b, j: (jnp.maximum(row(b, j) * per - 1, 0), col_block))
    nxt = pl.BlockSpec((SUBLANES, width), lambda b, j: (jnp.minimum((row(b, j) + 1) * per, last), col_block))
    return prev, nxt


def _fill_ext(ext_ref, cur, prev, nxt, j, geo):
    t = cur.shape[0]
    seg_first = jnp.logical_or(j == 0, j == geo.ctx_tiles)
    seg_last = jnp.logical_or(j == geo.ctx_tiles - 1, j == geo.tpb - 1)
    ext_ref[0:SUBLANES, :] = jnp.where(seg_first, 0.0, prev)
    ext_ref[SUBLANES:SUBLANES + t, :] = cur
    ext_ref[SUBLANES + t:2 * SUBLANES + t, :] = jnp.where(seg_last, 0.0, nxt)


def _ssd_conv_kernel(cur_ref, prev_ref, next_ref, nar_ref, w_ref, b_ref, dtb_ref, xbc_ref, dt_ref, ext_ref, *, geo):
    j = pl.program_id(1)
    _fill_ext(ext_ref, cur_ref[...].astype(F32), prev_ref[...].astype(F32), next_ref[...].astype(F32), j, geo)
    t = cur_ref.shape[0]
    acc = jnp.zeros(cur_ref.shape, F32) + b_ref[...]
    for k in range(SSD_CONV):
        acc = acc + w_ref[k:k + 1, :] * ext_ref[SUBLANES + k - SSD_CONV // 2:SUBLANES + k - SSD_CONV // 2 + t, :]
    xbc_ref[...] = _silu(acc).astype(xbc_ref.dtype)
    lane = lax.broadcasted_iota(jnp.int32, dt_ref.shape, 1)
    dt = _softplus(nar_ref[...].astype(F32) + dtb_ref[...])
    dt_ref[...] = jnp.where(lane < 2 * SSD_HEADS, dt, 0.0)


def _ssd_conv(proj, conv_w, conv_b, dt_bias_row, geo):
    width = BRANCH_WIDTH + 2 * SSD_GROUPS * SSD_STATE
    row = lambda b, j: b * geo.tpb + j
    cb = _COL["B_X"] // width
    prev, nxt = _halo_specs(width, cb, geo, 0, row)
    return pl.pallas_call(
        functools.partial(_ssd_conv_kernel, geo=geo),
        grid=(geo.batch, geo.tpb),
        in_specs=[pl.BlockSpec((ROW_TILE, width), lambda b, j: (row(b, j), cb)), prev, nxt,
                  pl.BlockSpec((ROW_TILE, LANES), lambda b, j: (row(b, j), _COL["NARROW"] // LANES)),
                  pl.BlockSpec((SSD_CONV, width), lambda b, j: (0, 0)),
                  pl.BlockSpec((1, width), lambda b, j: (0, 0)),
                  pl.BlockSpec((1, LANES), lambda b, j: (0, 0))],
        out_specs=[pl.BlockSpec((ROW_TILE, width), lambda b, j: (row(b, j), 0)),
                   pl.BlockSpec((ROW_TILE, LANES), lambda b, j: (row(b, j), 0))],
        out_shape=[jax.ShapeDtypeStruct((geo.rows, width), BF16),
                   jax.ShapeDtypeStruct((geo.rows, LANES), F32)],
        scratch_shapes=[pltpu.VMEM((ROW_TILE + 2 * SUBLANES, width), F32)],
        compiler_params=_cparams(("parallel", "parallel")),
        name="ssd_conv",
    )(proj, proj, proj, proj, conv_w, conv_b, dt_bias_row)


def _scan_chunk(s, n_chunks, n_ctx_chunks, reverse):
    if not reverse:
        return s
    return jnp.where(s < n_ctx_chunks, n_ctx_chunks - 1 - s, n_chunks + n_ctx_chunks - 1 - s)


def _tri(n, reverse):
    t = lax.broadcasted_iota(jnp.int32, (n, n), 0)
    s = lax.broadcasted_iota(jnp.int32, (n, n), 1)
    return (s >= t) if reverse else (s <= t)


def _split_bf16(x, pieces):
    out, rest = [], x
    for _ in range(pieces):
        p = rest.astype(BF16)
        out.append(p)
        rest = rest - p.astype(F32)
    return out


def _cumsum_rows(mask_bf16, x):
    w = x.shape[1]
    parts = jnp.dot(mask_bf16, jnp.concatenate(_split_bf16(x, 3), axis=1), preferred_element_type=F32)
    return parts[:, 0:w] + parts[:, w:2 * w] + parts[:, 2 * w:3 * w]


def _ssd_scan_kernel(xbc_ref, dt_ref, alog_ref, e_ref, y_ref, h_ref, *, reverse):
    q = SSD_CHUNK
    gw = SSD_HPG * SSD_HEAD_DIM
    quad = 4
    qw = quad * SSD_HEAD_DIM
    lane0 = SSD_HEADS if reverse else 0
    n_sub = xbc_ref.shape[0] // q

    @pl.when(pl.program_id(1) == 0)
    def _():
        h_ref[...] = jnp.zeros_like(h_ref)

    lane = lax.broadcasted_iota(jnp.int32, (1, LANES), 1)
    a = jnp.where(lane < 2 * SSD_HEADS, -jnp.exp(alog_ref[...]), 0.0)
    mask = _tri(q, reverse)
    mask_bf = jnp.where(mask, 1.0, 0.0).astype(BF16)
    e = e_ref[...]
    head_of_lane = lax.broadcasted_iota(jnp.int32, (1, qw), 1) // SSD_HEAD_DIM
    zero_bf = jnp.zeros((), BF16)

    for ci in (range(n_sub - 1, -1, -1) if reverse else range(n_sub)):
        rows = slice(ci * q, (ci + 1) * q)
        dt = dt_ref[rows, :]
        cs = _cumsum_rows(mask_bf, dt * a)
        cs_last = cs[0:1, :] if reverse else cs[q - 1:q, :]
        ecs_hi, ecs_lo = _split_bf16(jnp.exp(cs), 2)
        cd = _split_bf16(jnp.broadcast_to(jnp.exp(cs_last), (BF16_SUBLANES, LANES)), 3)
        stack = jnp.concatenate([dt.astype(BF16), (jnp.exp(cs_last - cs) * dt).astype(BF16), ecs_hi, ecs_lo, *cd], axis=0)
        big = jnp.dot(stack, e, preferred_element_type=F32)
        x = xbc_ref[rows, 0:BRANCH_WIDTH].astype(F32)
        dtx = (big[0:q] * x).astype(BF16)
        wx = (big[q:2 * q] * x).astype(BF16)
        ecs = big[2 * q:3 * q] + big[3 * q:4 * q]
        r0 = 4 * q
        chunk_decay = (big[r0:r0 + 1] + big[r0 + BF16_SUBLANES:r0 + BF16_SUBLANES + 1]
                       + big[r0 + 2 * BF16_SUBLANES:r0 + 2 * BF16_SUBLANES + 1])
        cs_t = cs.T
        for g in range(SSD_GROUPS):
            bm = xbc_ref[rows, BRANCH_WIDTH + g * SSD_STATE:BRANCH_WIDTH + (g + 1) * SSD_STATE]
            c0 = BRANCH_WIDTH + SSD_GROUPS * SSD_STATE + g * SSD_STATE
            cm = xbc_ref[rows, c0:c0 + SSD_STATE]
            cb = lax.dot_general(cm, bm, (((1,), (1,)), ((), ())), preferred_element_type=F32)
            h_in = h_ref[g]
            y_off = jnp.dot(cm, h_in.astype(BF16), preferred_element_type=F32) * ecs[:, g * gw:(g + 1) * gw]
            states = jnp.dot(bm.astype(F32).T.astype(BF16), wx[:, g * gw:(g + 1) * gw], preferred_element_type=F32)
            h_ref[g] = chunk_decay[:, g * gw:(g + 1) * gw] * h_in + states
            for hq in range(SSD_HPG // quad):
                ms = []
                for r in range(quad):
                    col = lane0 + g * SSD_HPG + hq * quad + r
                    seg = cs[:, col:col + 1] - cs_t[col:col + 1, :]
                    ms.append((cb * jnp.exp(jnp.where(mask, seg, -1e30))).astype(BF16))
                lo = g * gw + hq * qw
                slab = dtx[:, lo:lo + qw]
                rhs = jnp.concatenate([jnp.where(head_of_lane == r, slab, zero_bf) for r in range(quad)], axis=0)
                y_diag = jnp.dot(jnp.concatenate(ms, axis=1), rhs, preferred_element_type=F32)
                y_ref[rows, lo:lo + qw] = (y_diag + y_off[:, hq * qw:(hq + 1) * qw]).astype(y_ref.dtype)


def _ssd_scan(xbc, dt, a_log_row, expand_mat, geo, reverse):
    nb, nbc = geo.lt // SCAN_BLOCK, geo.ctx_len // SCAN_BLOCK
    width = xbc.shape[1]
    row = lambda b, s: b * nb + _scan_chunk(s, nb, nbc, reverse)
    return pl.pallas_call(
        functools.partial(_ssd_scan_kernel, reverse=reverse),
        grid=(geo.batch, nb),
        in_specs=[pl.BlockSpec((SCAN_BLOCK, width), lambda b, s: (row(b, s), 0)),
                  pl.BlockSpec((SCAN_BLOCK, LANES), lambda b, s: (row(b, s), 0)),
                  pl.BlockSpec((1, LANES), lambda b, s: (0, 0)),
                  pl.BlockSpec((LANES, BRANCH_WIDTH), lambda b, s: (0, 0))],
        out_specs=pl.BlockSpec((SCAN_BLOCK, BRANCH_WIDTH), lambda b, s: (row(b, s), 0)),
        out_shape=jax.ShapeDtypeStruct((geo.rows, BRANCH_WIDTH), F32),
        scratch_shapes=[pltpu.VMEM((SSD_GROUPS, SSD_STATE, SSD_HPG * SSD_HEAD_DIM), F32)],
        compiler_params=_cparams(("parallel", "arbitrary")),
        name="ssd_scan_bwd" if reverse else "ssd_scan_fwd",
    )(xbc, dt, a_log_row, expand_mat)


def _ssd_finish_kernel(yf_ref, yb_ref, xbc_ref, z_ref, skip_ref, g_ref, o_ref):
    xh = xbc_ref[:, 0:BRANCH_WIDTH].astype(F32)
    y = yf_ref[...] + yb_ref[...] + skip_ref[...] * xh
    y = y * _silu(z_ref[...].astype(F32))
    y = y * lax.rsqrt(jnp.mean(y * y, axis=-1, keepdims=True) + NORM_EPS)
    o_ref[...] = (y * g_ref[...]).astype(o_ref.dtype)


def _ssd_finish(yf, yb, xbc, proj, skip_row, norm_g, geo, with_ctx):
    first, nt = geo.span(with_ctx)
    row = lambda b, j: b * geo.tpb + first + j
    w = BRANCH_WIDTH
    return pl.pallas_call(
        _ssd_finish_kernel,
        grid=(geo.batch, nt),
        in_specs=[pl.BlockSpec((ROW_TILE, w), lambda b, j: (row(b, j), 0)),
                  pl.BlockSpec((ROW_TILE, w), lambda b, j: (row(b, j), 0)),
                  pl.BlockSpec((ROW_TILE, xbc.shape[1]), lambda b, j: (row(b, j), 0)),
                  pl.BlockSpec((ROW_TILE, w), lambda b, j: (row(b, j), _COL["B_Z"] // w)),
                  pl.BlockSpec((1, w), lambda b, j: (0, 0)),
                  pl.BlockSpec((1, w), lambda b, j: (0, 0))],
        out_specs=pl.BlockSpec((ROW_TILE, w), lambda b, j: (row(b, j), 0)),
        out_shape=jax.ShapeDtypeStruct((geo.rows, w), BF16),
        compiler_params=_cparams(("parallel", "parallel")),
        name="ssd_finish",
    )(yf, yb, xbc, proj, skip_row, norm_g)


def _gla_scan_kernel(q_ref, k_ref, v_ref, nar_ref, w2_ref, bf_ref, o_ref, h_ref, *, reverse):
    n = GLA_CHUNK

    @pl.when(pl.program_id(1) == 0)
    def _():
        h_ref[...] = jnp.zeros_like(h_ref)

    t = q_ref.shape[0]
    n_sub = t // n
    ri = lax.broadcasted_iota(jnp.int32, (t, t), 0)
    ci = lax.broadcasted_iota(jnp.int32, (t, t), 1)
    mask = jnp.logical_and(ri // n == ci // n, (ci >= ri) if reverse else (ci <= ri))
    logit = jnp.dot(nar_ref[...], w2_ref[...], preferred_element_type=F32) + bf_ref[...]
    gl = _log_sigmoid(logit) / GLA_TAU
    b = _cumsum_rows(jnp.where(mask, 1.0, 0.0).astype(BF16), gl)
    last = [b[c * n:c * n + 1, :] if reverse else b[(c + 1) * n - 1:(c + 1) * n, :] for c in range(n_sub)]
    b_last = jnp.concatenate([jnp.broadcast_to(r, (n, r.shape[1])) for r in last], axis=0)
    q = q_ref[...].astype(F32) * (GLA_DK ** -0.5)
    k = k_ref[...].astype(F32)
    qe = (q * jnp.exp(b)).astype(BF16)
    ke = (k * jnp.exp(-b)).astype(BF16)
    kd = (k * jnp.exp(b_last - b)).astype(BF16)
    for h in range(GLA_HEADS):
        ks = slice(h * GLA_DK, (h + 1) * GLA_DK)
        vs = slice(h * GLA_DV, (h + 1) * GLA_DV)
        v = v_ref[:, vs]
        att = lax.dot_general(qe[:, ks], ke[:, ks], (((1,), (1,)), ((), ())), preferred_element_type=F32)
        o_intra = jnp.dot(jnp.where(mask, att, 0.0).astype(BF16), v, preferred_element_type=F32)
        for c in (range(n_sub - 1, -1, -1) if reverse else range(n_sub)):
            rows = slice(c * n, (c + 1) * n)
            h_in = h_ref[h]
            o_inter = lax.dot_general(qe[rows, ks], h_in.astype(BF16), (((1,), (1,)), ((), ())),
                                      preferred_element_type=F32)
            o_ref[rows, vs] = (o_intra[rows] + o_inter).astype(o_ref.dtype)
            upd = jnp.dot(v[rows].astype(F32).T.astype(BF16), kd[rows, ks], preferred_element_type=F32)
            h_ref[h] = jnp.exp(last[c][:, ks]) * h_in + upd


def _gla_scan(proj, w2, b_f, geo, reverse):
    nb, nbc = geo.lt // SCAN_BLOCK, geo.ctx_len // SCAN_BLOCK
    row = lambda b, s: b * nb + _scan_chunk(s, nb, nbc, reverse)
    kw, vw = GLA_HEADS * GLA_DK, GLA_HEADS * GLA_DV
    return pl.pallas_call(
        functools.partial(_gla_scan_kernel, reverse=reverse),
        grid=(geo.batch, nb),
        in_specs=[pl.BlockSpec((SCAN_BLOCK, kw), lambda b, s: (row(b, s), _COL["C_Q"] // kw)),
                  pl.BlockSpec((SCAN_BLOCK, kw), lambda b, s: (row(b, s), _COL["C_K"] // kw)),
                  pl.BlockSpec((SCAN_BLOCK, vw), lambda b, s: (row(b, s), _COL["C_V"] // vw)),
                  pl.BlockSpec((SCAN_BLOCK, LANES), lambda b, s: (row(b, s), _COL["NARROW"] // LANES)),
                  pl.BlockSpec((LANES, kw), lambda b, s: (0, 0)),
                  pl.BlockSpec((1, kw), lambda b, s: (0, 0))],
        out_specs=pl.BlockSpec((SCAN_BLOCK, vw), lambda b, s: (row(b, s), 0)),
        out_shape=jax.ShapeDtypeStruct((geo.rows, vw), F32),
        scratch_shapes=[pltpu.VMEM((GLA_HEADS, GLA_DV, GLA_DK), F32)],
        compiler_params=_cparams(("parallel", "arbitrary")),
        name="gla_scan_bwd" if reverse else "gla_scan_fwd",
    )(proj, proj, proj, proj, w2, b_f)


def _gla_finish_kernel(of_ref, ob_ref, gate_ref, g_ref, o_ref):
    g = g_ref[...]
    for h in range(GLA_HEADS):
        vs = slice(h * GLA_DV, (h + 1) * GLA_DV)
        o = (of_ref[:, vs] + ob_ref[:, vs])
        o = o * lax.rsqrt(jnp.mean(o * o, axis=-1, keepdims=True) + NORM_EPS) * g
        o_ref[:, vs] = (o * _silu(gate_ref[:, vs].astype(F32))).astype(o_ref.dtype)


def _gla_finish(of, ob, proj, norm_g, geo, with_ctx):
    first, nt = geo.span(with_ctx)
    row = lambda b, j: b * geo.tpb + first + j
    w = BRANCH_WIDTH
    return pl.pallas_call(
        _gla_finish_kernel,
        grid=(geo.batch, nt),
        in_specs=[pl.BlockSpec((ROW_TILE, w), lambda b, j: (row(b, j), 0)),
                  pl.BlockSpec((ROW_TILE, w), lambda b, j: (row(b, j), 0)),
                  pl.BlockSpec((ROW_TILE, w), lambda b, j: (row(b, j), _COL["C_G"] // w)),
                  pl.BlockSpec((1, GLA_DV), lambda b, j: (0, 0))],
        out_specs=pl.BlockSpec((ROW_TILE, w), lambda b, j: (row(b, j), 0)),
        out_shape=jax.ShapeDtypeStruct((geo.rows, w), BF16),
        compiler_params=_cparams(("parallel", "parallel")),
        name="gla_finish",
    )(of, ob, proj, norm_g)


def _shortconv_kernel(cur_ref, prev_ref, next_ref, w_ref, o_ref, ext_ref, *, geo, first):
    j = pl.program_id(1) + first
    w = BRANCH_WIDTH
    u = lambda ref: ref[:, w:2 * w].astype(F32) * ref[:, 2 * w:3 * w].astype(F32)
    _fill_ext(ext_ref, u(cur_ref), u(prev_ref), u(next_ref), j, geo)
    t = cur_ref.shape[0]
    acc = jnp.zeros((t, w), F32)
    for k in range(SC_CONV):
        acc = acc + w_ref[k:k + 1, :] * ext_ref[SUBLANES + k - SC_CONV // 2:SUBLANES + k - SC_CONV // 2 + t, :]
    y = cur_ref[:, 0:w].astype(F32) * acc * _silu(cur_ref[:, 3 * w:4 * w].astype(F32))
    o_ref[...] = y.astype(o_ref.dtype)


def _shortconv(proj, conv_w, geo, with_ctx):
    first, nt = geo.span(with_ctx)
    row = lambda b, j: b * geo.tpb + first + j
    width = 4 * BRANCH_WIDTH
    cb = _COL["D_ALL"] // width
    prev, nxt = _halo_specs(width, cb, geo, first, row)
    return pl.pallas_call(
        functools.partial(_shortconv_kernel, geo=geo, first=first),
        grid=(geo.batch, nt),
        in_specs=[pl.BlockSpec((ROW_TILE, width), lambda b, j: (row(b, j), cb)), prev, nxt,
                  pl.BlockSpec((SC_CONV, BRANCH_WIDTH), lambda b, j: (0, 0))],
        out_specs=pl.BlockSpec((ROW_TILE, BRANCH_WIDTH), lambda b, j: (row(b, j), 0)),
        out_shape=jax.ShapeDtypeStruct((geo.rows, BRANCH_WIDTH), BF16),
        scratch_shapes=[pltpu.VMEM((ROW_TILE + 2 * SUBLANES, BRANCH_WIDTH), F32)],
        compiler_params=_cparams(("parallel", "parallel")),
        name="shortconv",
    )(proj, proj, proj, conv_w)


def _merge_out_kernel(ya_ref, yb_ref, yg_ref, yd_ref, g0_ref, g1_ref, g2_ref, g3_ref, wb_ref, wo_ref,
                      x_ref, gt_ref, gp_ref, o_ref):
    m = None
    for n, (y_ref, gate_ref) in enumerate(zip((ya_ref, yb_ref, yg_ref, yd_ref), (g0_ref, g1_ref, g2_ref, g3_ref))):
        term = jax.nn.sigmoid(gate_ref[...].astype(F32)) * jnp.dot(y_ref[...], wb_ref[n], preferred_element_type=F32)
        m = term if m is None else m + term
    out = jnp.dot(m.astype(BF16), wo_ref[...], preferred_element_type=F32)
    y = out * lax.rsqrt(jnp.mean(out * out, axis=-1, keepdims=True) + NORM_EPS) * gp_ref[...]
    o_ref[...] = x_ref[...] + gt_ref[0] * y


def _merge_out(ys, proj, w_branch, w_out, x_all, mod3, g_post, geo, with_ctx):
    first, nt = geo.span(with_ctx)
    w, d = BRANCH_WIDTH, D_MODEL
    row = lambda b, j: b * geo.tpb + first + j
    out_rows = geo.rows if with_ctx else geo.batch * geo.seq
    out_row = row if with_ctx else (lambda b, j: b * nt + j)
    y_spec = pl.BlockSpec((ROW_TILE, w), lambda b, j: (row(b, j), 0))
    gate_specs = [pl.BlockSpec((ROW_TILE, d), functools.partial(lambda b, j, n: (row(b, j), _COL["MG"] // d + n), n=n))
                  for n in range(N_BRANCH)]
    resident = pl.Buffered(1)
    return pl.pallas_call(
        _merge_out_kernel,
        grid=(geo.batch, nt),
        in_specs=[y_spec, y_spec, y_spec, y_spec, *gate_specs,
                  pl.BlockSpec((N_BRANCH, w, d), lambda b, j: (0, 0, 0), pipeline_mode=resident),
                  pl.BlockSpec((d, d), lambda b, j: (0, 0), pipeline_mode=resident),
                  pl.BlockSpec((ROW_TILE, d), lambda b, j: (row(b, j), 0)),
                  pl.BlockSpec((1, 1, d), lambda b, j: (geo.mod_row(b, first + j), 0, 2)),
                  pl.BlockSpec((1, d), lambda b, j: (0, 0))],
        out_specs=pl.BlockSpec((ROW_TILE, d), lambda b, j: (out_row(b, j), 0)),
        out_shape=jax.ShapeDtypeStruct((out_rows, d), F32),
        compiler_params=_cparams(("parallel", "parallel")),
        name="merge_out",
    )(*ys, proj, proj, proj, proj, w_branch, w_out, x_all, mod3, g_post)


def _deinterleave_cols(w, heads):
    k = w.shape[0]
    return w.reshape(k, heads, HEAD_DIM // 2, 2).swapaxes(2, 3).reshape(k, heads * HEAD_DIM)


def _regroup_w_in(w):
    bw = BRANCH_WIDTH
    widths = (ATT_HEADS * HEAD_DIM, ATT_KV_HEADS * HEAD_DIM, ATT_KV_HEADS * HEAD_DIM, bw,
              bw, bw, SSD_GROUPS * SSD_STATE, SSD_GROUPS * SSD_STATE, 2 * SSD_HEADS,
              GLA_HEADS * GLA_DK, GLA_HEADS * GLA_DK, GLA_HEADS * GLA_DV, bw, 2 * GLA_RANK,
              bw, bw, bw, bw, N_BRANCH * D_MODEL)
    offs = np.concatenate([[0], np.cumsum(widths)])
    (a_q, a_k, a_v, a_g, b_x, b_z, b_b, b_c, b_dt, c_q, c_k, c_v, c_g, c_f,
     d_b, d_c, d_x, d_g, mg) = [w[:, int(offs[i]):int(offs[i + 1])] for i in range(len(widths))]
    a_q = _deinterleave_cols(a_q, ATT_HEADS)
    a_k = _deinterleave_cols(a_k, ATT_KV_HEADS)
    pad = jnp.zeros((w.shape[0], LANES - 2 * SSD_HEADS - 2 * GLA_RANK), w.dtype)
    parts = [a_q, a_g, b_z, b_x, b_b, b_c, a_k, a_v, c_v, c_g, c_q, c_k, d_b, d_c, d_x, d_g, mg, b_dt, c_f, pad]
    out = jnp.concatenate(parts, axis=1).astype(BF16)
    assert out.shape[1] == N_PROJ
    return out


def _deinterleave_vec(g):
    return g.reshape(HEAD_DIM // 2, 2).T.reshape(1, HEAD_DIM)


def _pad_lanes(v, lane0=0):
    return jnp.zeros((1, LANES), F32).at[0, lane0:lane0 + v.shape[0]].set(v.astype(F32))


def _head_expand_matrix(reverse):
    e = np.zeros((LANES, BRANCH_WIDTH), np.float32)
    lane0 = SSD_HEADS if reverse else 0
    for r in range(SSD_HEADS):
        e[lane0 + r, r * SSD_HEAD_DIM:(r + 1) * SSD_HEAD_DIM] = 1.0
    return jnp.asarray(e, BF16)


def _forget_weight(w_f2_dir, direction):
    lane0 = F1_LANE0 + direction * GLA_RANK
    return jnp.zeros((LANES, w_f2_dir.shape[1]), F32).at[lane0:lane0 + GLA_RANK].set(w_f2_dir).astype(BF16)


def kernel(x, c, ctx, c_ctx, w_mod, b_mod, g_pre, g_post, w_in, g_q, g_k, ssd_conv_w, ssd_conv_b,
           ssd_a_log, ssd_dt_bias, ssd_d, ssd_norm_g, gla_w_f2, gla_b_f, gla_norm_g, sc_conv_w,
           w_branch, w_out):
    batch, seq, d = x.shape
    ctx_len = ctx.shape[1]
    depth = w_in.shape[0]
    geo = _Geom(batch, ctx_len, seq)
    assert d == D_MODEL and batch + 1 <= SUBLANES and seq % GRID_W == 0

    cos_t, sin_t = _rope_tables(geo)
    c_rows = jnp.zeros((SUBLANES, d), F32).at[:batch].set(c).at[batch].set(c_ctx)
    x_all = jnp.concatenate([ctx, x], axis=1).reshape(geo.rows, d)
    e_fwd, e_bwd = _head_expand_matrix(False), _head_expand_matrix(True)

    for l in range(depth):
        need_ctx = l < depth - 1
        mod = _modulation(c_rows, w_mod[l], b_mod[l][None, :])
        mod3 = mod.reshape(SUBLANES, 1, 3 * d)
        h = _prenorm(x_all, g_pre[l][None, :], mod3, geo)
        proj = _matmul(h, _regroup_w_in(w_in[l]), PROJ_TN)

        qt, kh, vt = _qk_prep(proj, cos_t, sin_t, _deinterleave_vec(g_q[l]), _deinterleave_vec(g_k[l]), geo)
        ya = _attention(qt, kh, vt, proj, None, geo, ctx_len, seq, geo.lt)
        if need_ctx:
            ya = _attention(qt, kh, vt, proj, ya, geo, 0, ctx_len, ctx_len)

        xbc, dt = _ssd_conv(proj, ssd_conv_w[l], ssd_conv_b[l][None, :], _pad_lanes(ssd_dt_bias[l].reshape(-1)), geo)
        a_log_row = _pad_lanes(ssd_a_log[l].reshape(-1))
        ysf = _ssd_scan(xbc, dt, a_log_row, e_fwd, geo, False)
        ysb = _ssd_scan(xbc, dt, a_log_row, e_bwd, geo, True)
        skip_row = jnp.repeat(ssd_d[l], SSD_HEAD_DIM)[None, :]
        yb = _ssd_finish(ysf, ysb, xbc, proj, skip_row, ssd_norm_g[l][None, :], geo, need_ctx)

        ogf = _gla_scan(proj, _forget_weight(gla_w_f2[l, 0], 0), gla_b_f[l, 0][None, :], geo, False)
        ogb = _gla_scan(proj, _forget_weight(gla_w_f2[l, 1], 1), gla_b_f[l, 1][None, :], geo, True)
        yg = _gla_finish(ogf, ogb, proj, gla_norm_g[l][None, :], geo, need_ctx)

        yd = _shortconv(proj, sc_conv_w[l], geo, need_ctx)

        x_all = _merge_out((ya, yb, yg, yd), proj, w_branch[l].astype(BF16), w_out[l].astype(BF16), x_all, mod3,
                           g_post[l][None, :], geo, need_ctx)

    return x_all.reshape(batch, seq, d)
```

```python
import functools

import numpy as np
import jax
import jax.numpy as jnp
from jax import lax
from jax.experimental import pallas as pl
from jax.experimental.pallas import tpu as pltpu

F32 = jnp.float32
BF16 = jnp.bfloat16

D_MODEL = 2048
GRID_W = 64
BRANCH_WIDTH = D_MODEL // 2
N_BRANCH = 4
NORM_EPS = 1e-6
HEAD_DIM = 128
ATT_HEADS = BRANCH_WIDTH // HEAD_DIM
ATT_KV_HEADS = ATT_HEADS // 4
ATT_REP = ATT_HEADS // ATT_KV_HEADS
ROPE_THETA = 10000.0
SSD_HEAD_DIM = 64
SSD_HEADS = BRANCH_WIDTH // SSD_HEAD_DIM
SSD_GROUPS = 2
SSD_HPG = SSD_HEADS // SSD_GROUPS
SSD_STATE = 128
SSD_CONV = 5
SSD_CHUNK = 128
GLA_HEADS = 4
GLA_DV = BRANCH_WIDTH // GLA_HEADS
GLA_DK = GLA_DV // 2
GLA_RANK = 16
GLA_TAU = 16.0
GLA_CHUNK = 64
SC_CONV = 3

LANES = 128
SUBLANES = 8
ROW_TILE = 256
SCAN_BLOCK = ROW_TILE
VMEM_LIMIT = 56 * 1024 * 1024

_COL = dict(
    A_Q=0, A_G=1024, B_Z=2048, B_X=3072, B_B=4096, B_C=4352, A_K=4608, A_V=4864,
    C_V=5120, C_G=6144, C_Q=7168, C_K=7680,
    D_ALL=8192, MG=12288, NARROW=20480,
)
N_PROJ = 20608
PROJ_TN = 2944
ATTN_UNROLL = 2
ATTN_KEY_CHUNKS = (ROW_TILE, 2 * ROW_TILE, 3 * ROW_TILE)
BF16_SUBLANES = 16
VT_ROWS = HEAD_DIM + BF16_SUBLANES
Q_SCALE = HEAD_DIM ** -0.5 * float(np.log2(np.e))
DT_LANE0 = 0
F1_LANE0 = 32


def _cparams(sem, vmem=VMEM_LIMIT):
    return pltpu.CompilerParams(dimension_semantics=sem, vmem_limit_bytes=vmem)


def _silu(x):
    return x * jax.nn.sigmoid(x)


def _softplus(x):
    return jnp.maximum(x, 0.0) + jnp.log1p(jnp.exp(-jnp.abs(x)))


def _log_sigmoid(x):
    return jnp.minimum(x, 0.0) - jnp.log1p(jnp.exp(-jnp.abs(x)))


def _mod_kernel(c_ref, w_ref, b_ref, o_ref):
    a = _silu(c_ref[...]).astype(BF16)
    o_ref[...] = jnp.dot(a, w_ref[...].astype(BF16), preferred_element_type=F32) + b_ref[...]


def _modulation(c_rows, w_mod, b_mod, layer):
    _, d, n = w_mod.shape
    tn = 1024
    return pl.pallas_call(
        _mod_kernel,
        grid=(n // tn,),
        in_specs=[pl.BlockSpec((SUBLANES, d), lambda j: (0, 0)),
                  pl.BlockSpec((None, d, tn), lambda j: (layer, 0, j)),
                  pl.BlockSpec((1, tn), lambda j: (0, j))],
        out_specs=pl.BlockSpec((SUBLANES, tn), lambda j: (0, j)),
        out_shape=jax.ShapeDtypeStruct((SUBLANES, n), F32),
        compiler_params=_cparams(("parallel",)),
        name="modulation",
    )(c_rows, w_mod, b_mod)


class _Geom:
    def __init__(self, batch, ctx_len, seq):
        self.batch, self.ctx_len, self.seq = batch, ctx_len, seq
        self.lt = ctx_len + seq
        self.rows = batch * self.lt
        assert ctx_len % ROW_TILE == 0 and seq % ROW_TILE == 0
        self.tpb = self.lt // ROW_TILE
        self.ctx_tiles = ctx_len // ROW_TILE
        self.lat_tiles = seq // ROW_TILE

    def span(self, with_ctx):
        return (0, self.tpb) if with_ctx else (self.ctx_tiles, self.lat_tiles)

    def mod_row(self, b, j):
        return jnp.where(j < self.ctx_tiles, self.batch, b)


def _prenorm_kernel(x_ref, g_ref, sh_ref, sc_ref, o_ref):
    x = x_ref[...]
    y = x * lax.rsqrt(jnp.mean(x * x, axis=-1, keepdims=True) + NORM_EPS)
    o_ref[...] = ((y * g_ref[...]) * (1.0 + sc_ref[0]) + sh_ref[0]).astype(o_ref.dtype)


def _prenorm(x_all, g_pre, mod3, geo):
    d = x_all.shape[1]
    row = lambda b, j: (b * geo.tpb + j, 0)
    return pl.pallas_call(
        _prenorm_kernel,
        grid=(geo.batch, geo.tpb),
        in_specs=[pl.BlockSpec((ROW_TILE, d), row),
                  pl.BlockSpec((1, d), lambda b, j: (0, 0)),
                  pl.BlockSpec((1, 1, d), lambda b, j: (geo.mod_row(b, j), 0, 0)),
                  pl.BlockSpec((1, 1, d), lambda b, j: (geo.mod_row(b, j), 0, 1))],
        out_specs=pl.BlockSpec((ROW_TILE, d), row),
        out_shape=jax.ShapeDtypeStruct(x_all.shape, BF16),
        compiler_params=_cparams(("parallel", "parallel")),
        name="prenorm",
    )(x_all, g_pre, mod3, mod3)


def _matmul_kernel(a_ref, w_ref, o_ref):
    o_ref[...] = jnp.dot(a_ref[...], w_ref[...], preferred_element_type=F32).astype(o_ref.dtype)


def _matmul(a, w, tn, out_dtype=BF16):
    m, k = a.shape
    n = w.shape[1]
    tm = 512 if m % 512 == 0 else 256
    assert m % tm == 0 and n % tn == 0
    return pl.pallas_call(
        _matmul_kernel,
        grid=(n // tn, m // tm),
        in_specs=[pl.BlockSpec((tm, k), lambda j, i: (i, 0)),
                  pl.BlockSpec((k, tn), lambda j, i: (0, j))],
        out_specs=pl.BlockSpec((tm, tn), lambda j, i: (i, j)),
        out_shape=jax.ShapeDtypeStruct((m, n), out_dtype),
        compiler_params=_cparams(("parallel", "parallel")),
        name="in_proj",
    )(a, w)


def _rope_tables(geo):
    rows = geo.seq // GRID_W
    t_row = jnp.repeat(jnp.arange(rows, dtype=F32), GRID_W)
    t_col = jnp.tile(jnp.arange(GRID_W, dtype=F32), rows)
    half = HEAD_DIM // 2
    freqs = ROPE_THETA ** (-(jnp.arange(0, half, 2, dtype=F32) / half))
    ang = jnp.concatenate([t_row[:, None] * freqs, t_col[:, None] * freqs], axis=-1)
    cos, sin = jnp.cos(ang), jnp.sin(ang)
    cos_l = jnp.concatenate([cos, cos], axis=-1)
    sin_l = jnp.concatenate([-sin, sin], axis=-1)
    cos_c = jnp.ones((geo.ctx_len, HEAD_DIM), F32)
    sin_c = jnp.zeros((geo.ctx_len, HEAD_DIM), F32)
    return jnp.concatenate([cos_c, cos_l], axis=0), jnp.concatenate([sin_c, sin_l], axis=0)


def _qk_prep_kernel(q_ref, k_ref, v_ref, cos_ref, sin_ref, gq_ref, gk_ref, qt_ref, ko_ref, vt_ref):
    cos, sin = cos_ref[...], sin_ref[...]
    t = q_ref.shape[0]

    def norm_rope(x, g):
        x = x.astype(F32)
        y = x * lax.rsqrt(jnp.mean(x * x, axis=-1, keepdims=True) + NORM_EPS) * g
        return y * cos + pltpu.roll(y, HEAD_DIM // 2, 1) * sin

    gq, gk = gq_ref[...], gk_ref[...]
    for h in range(ATT_HEADS):
        g, r = divmod(h, ATT_REP)
        y = norm_rope(q_ref[:, h * HEAD_DIM:(h + 1) * HEAD_DIM], gq) * Q_SCALE
        qt_ref[g, 0, :, r * t:(r + 1) * t] = y.T.astype(qt_ref.dtype)
    for g in range(ATT_KV_HEADS):
        hs = slice(g * HEAD_DIM, (g + 1) * HEAD_DIM)
        ko_ref[g] = norm_rope(k_ref[:, hs], gk).astype(ko_ref.dtype)
        vt_ref[g, 0, 0:HEAD_DIM, :] = v_ref[:, hs].astype(F32).T.astype(vt_ref.dtype)
        vt_ref[g, 0, HEAD_DIM:VT_ROWS, :] = jnp.ones((VT_ROWS - HEAD_DIM, t), vt_ref.dtype)


def _qk_prep(proj, cos_t, sin_t, g_q, g_k, geo):
    row = lambda b, j: b * geo.tpb + j
    nq, nk = ATT_HEADS * HEAD_DIM, ATT_KV_HEADS * HEAD_DIM
    kv, t = ATT_KV_HEADS, ROW_TILE
    return pl.pallas_call(
        _qk_prep_kernel,
        grid=(geo.batch, geo.tpb),
        in_specs=[pl.BlockSpec((t, nq), lambda b, j: (row(b, j), _COL["A_Q"] // nq)),
                  pl.BlockSpec((t, nk), lambda b, j: (row(b, j), _COL["A_K"] // nk)),
                  pl.BlockSpec((t, nk), lambda b, j: (row(b, j), _COL["A_V"] // nk)),
                  pl.BlockSpec((t, HEAD_DIM), lambda b, j: (j, 0)),
                  pl.BlockSpec((t, HEAD_DIM), lambda b, j: (j, 0)),
                  pl.BlockSpec((1, HEAD_DIM), lambda b, j: (0, 0)),
                  pl.BlockSpec((1, HEAD_DIM), lambda b, j: (0, 0))],
        out_specs=[pl.BlockSpec((kv, 1, HEAD_DIM, ATT_REP * t), lambda b, j: (b, j, 0, 0)),
                   pl.BlockSpec((kv, t, HEAD_DIM), lambda b, j: (b, j, 0)),
                   pl.BlockSpec((kv, 1, VT_ROWS, t), lambda b, j: (b, j, 0, 0))],
        out_shape=[jax.ShapeDtypeStruct((geo.batch * kv, geo.tpb, HEAD_DIM, ATT_REP * t), BF16),
                   jax.ShapeDtypeStruct((geo.batch * kv, geo.lt, HEAD_DIM), BF16),
                   jax.ShapeDtypeStruct((geo.batch * kv, geo.tpb, VT_ROWS, t), BF16)],
        compiler_params=_cparams(("parallel", "parallel")),
        name="qk_prep",
    )(proj, proj, proj, cos_t, sin_t, g_q, g_k)


def _attn_kernel(qt_ref, k_ref, vt_ref, gate_ref, o_ref, m_ref, acc_ref, s0_ref, s1_ref, mx0_ref, mx1_ref):
    vt_tile = vt_ref.shape[2]
    tk = s0_ref.shape[0]
    tiles = tk // vt_tile
    n_chunks = k_ref.shape[0] // tk
    tq = o_ref.shape[0]
    qt = qt_ref[...]
    m_ref[...] = jnp.full(m_ref.shape, -jnp.inf, F32)
    acc_ref[...] = jnp.zeros(acc_ref.shape, F32)

    bufs = ((s0_ref, mx0_ref), (s1_ref, mx1_ref))

    def scores(c, buf):
        s_ref, mx_ref = buf
        k = k_ref[pl.ds(pl.multiple_of(c * tk, tk), tk), :]
        s = jnp.dot(k, qt, preferred_element_type=F32)
        s_ref[...] = s
        mx_ref[...] = jnp.max(s, axis=0, keepdims=True)

    def absorb(c, buf):
        s_ref, mx_ref = buf
        m_old = m_ref[...]
        m_new = jnp.maximum(m_old, mx_ref[...])
        alpha = jnp.exp2(m_old - m_new)
        pv = None
        for t in range(tiles):
            p = jnp.exp2((s_ref[t * vt_tile:(t + 1) * vt_tile, :] - m_new).astype(BF16))
            part = jnp.dot(vt_ref[c * tiles + t], p, preferred_element_type=F32)
            pv = part if pv is None else pv + part
        acc_ref[...] = alpha * acc_ref[...] + pv
        m_ref[...] = m_new

    def group(i, carry):
        c0 = ATTN_UNROLL * i
        for u in range(ATTN_UNROLL):
            scores(c0 + u + 1, bufs[(u + 1) % 2])
            absorb(c0 + u, bufs[u % 2])
        return carry

    scores(0, bufs[0])
    n_groups = (n_chunks - 1) // ATTN_UNROLL
    if n_groups > 0:
        lax.fori_loop(0, n_groups, group, 0)
    for c in range(n_groups * ATTN_UNROLL, n_chunks):
        if c + 1 < n_chunks:
            scores(c + 1, bufs[(c + 1) % 2])
        absorb(c, bufs[c % 2])
    o_t = acc_ref[0:HEAD_DIM, :] / acc_ref[HEAD_DIM:HEAD_DIM + 1, :]
    o = jnp.concatenate([o_t[:, r * tq:(r + 1) * tq].T for r in range(ATT_REP)], axis=1)
    o_ref[...] = (o * _silu(gate_ref[...].astype(F32))).astype(o_ref.dtype)


def _attention(qt, kh, vt, proj, ya_prev, geo, q_first_row, q_rows, kv_rows):
    tq = ROW_TILE
    assert q_first_row % tq == 0 and q_rows % tq == 0 and kv_rows % ROW_TILE == 0
    q0 = q_first_row // tq
    gw = ATT_REP * HEAD_DIM
    tk = max(t for t in ATTN_KEY_CHUNKS if kv_rows % t == 0)
    kernel = _attn_kernel
    in_specs = [
        pl.BlockSpec((None, None, HEAD_DIM, ATT_REP * tq), lambda b, g, i: (b * ATT_KV_HEADS + g, q0 + i, 0, 0)),
        pl.BlockSpec((None, kv_rows, HEAD_DIM), lambda b, g, i: (b * ATT_KV_HEADS + g, 0, 0)),
        pl.BlockSpec((None, kv_rows // ROW_TILE, VT_ROWS, ROW_TILE), lambda b, g, i: (b * ATT_KV_HEADS + g, 0, 0, 0)),
        pl.BlockSpec((tq, gw), lambda b, g, i: (b * geo.tpb + q0 + i, _COL["A_G"] // gw + g)),
    ]
    out_spec = pl.BlockSpec((tq, gw), lambda b, g, i: (b * geo.tpb + q0 + i, g))
    args = [qt, kh, vt, proj]
    aliases = {}
    if ya_prev is not None:
        in_specs.append(pl.BlockSpec(memory_space=pl.ANY))
        args.append(ya_prev)
        aliases = {4: 0}
        kernel = functools.partial(_drop_last_input, kernel, 4)
    return pl.pallas_call(
        kernel,
        grid=(geo.batch, ATT_KV_HEADS, q_rows // tq),
        in_specs=in_specs,
        out_specs=out_spec,
        out_shape=jax.ShapeDtypeStruct((geo.rows, BRANCH_WIDTH), BF16),
        scratch_shapes=[pltpu.VMEM((1, ATT_REP * tq), F32), pltpu.VMEM((VT_ROWS, ATT_REP * tq), F32),
                        pltpu.VMEM((tk, ATT_REP * tq), F32), pltpu.VMEM((tk, ATT_REP * tq), F32),
                        pltpu.VMEM((1, ATT_REP * tq), F32), pltpu.VMEM((1, ATT_REP * tq), F32)],
        input_output_aliases=aliases,
        compiler_params=_cparams(("parallel", "parallel", "arbitrary")),
        name="attention",
    )(*args)


def _drop_last_input(kernel, n_in, *refs):
    return kernel(*refs[:n_in], *refs[n_in + 1:])


def _halo_specs(width, col_block, geo, first, row):
    per = ROW_TILE // SUBLANES
    last = geo.rows // SUBLANES - 1
    prev = pl.BlockSpec((SUBLANES, width), lambda b, j: (jnp.maximum(row(b, j) * per - 1, 0), col_block))
    nxt = pl.BlockSpec((SUBLANES, width), lambda b, j: (jnp.minimum((row(b, j) + 1) * per, last), col_block))
    return prev, nxt


def _fill_ext(ext_ref, cur, prev, nxt, j, geo):
    t = cur.shape[0]
    seg_first = jnp.logical_or(j == 0, j == geo.ctx_tiles)
    seg_last = jnp.logical_or(j == geo.ctx_tiles - 1, j == geo.tpb - 1)
    ext_ref[0:SUBLANES, :] = jnp.where(seg_first, 0.0, prev)
    ext_ref[SUBLANES:SUBLANES + t, :] = cur
    ext_ref[SUBLANES + t:2 * SUBLANES + t, :] = jnp.where(seg_last, 0.0, nxt)


def _ssd_conv_kernel(cur_ref, prev_ref, next_ref, nar_ref, w_ref, b_ref, dtb_ref, xbc_ref, dt_ref, ext_ref, *, geo):
    j = pl.program_id(1)
    _fill_ext(ext_ref, cur_ref[...].astype(F32), prev_ref[...].astype(F32), next_ref[...].astype(F32), j, geo)
    t = cur_ref.shape[0]
    acc = jnp.zeros(cur_ref.shape, F32) + b_ref[...]
    for k in range(SSD_CONV):
        acc = acc + w_ref[k:k + 1, :] * ext_ref[SUBLANES + k - SSD_CONV // 2:SUBLANES + k - SSD_CONV // 2 + t, :]
    xbc_ref[...] = _silu(acc).astype(xbc_ref.dtype)
    lane = lax.broadcasted_iota(jnp.int32, dt_ref.shape, 1)
    dt = _softplus(nar_ref[...].astype(F32) + dtb_ref[...])
    dt_ref[...] = jnp.where(lane < 2 * SSD_HEADS, dt, 0.0)


def _ssd_conv(proj, conv_w, conv_b, dt_bias_row, geo):
    width = BRANCH_WIDTH + 2 * SSD_GROUPS * SSD_STATE
    row = lambda b, j: b * geo.tpb + j
    cb = _COL["B_X"] // width
    prev, nxt = _halo_specs(width, cb, geo, 0, row)
    return pl.pallas_call(
        functools.partial(_ssd_conv_kernel, geo=geo),
        grid=(geo.batch, geo.tpb),
        in_specs=[pl.BlockSpec((ROW_TILE, width), lambda b, j: (row(b, j), cb)), prev, nxt,
                  pl.BlockSpec((ROW_TILE, LANES), lambda b, j: (row(b, j), _COL["NARROW"] // LANES)),
                  pl.BlockSpec((SSD_CONV, width), lambda b, j: (0, 0)),
                  pl.BlockSpec((1, width), lambda b, j: (0, 0)),
                  pl.BlockSpec((1, LANES), lambda b, j: (0, 0))],
        out_specs=[pl.BlockSpec((ROW_TILE, width), lambda b, j: (row(b, j), 0)),
                   pl.BlockSpec((ROW_TILE, LANES), lambda b, j: (row(b, j), 0))],
        out_shape=[jax.ShapeDtypeStruct((geo.rows, width), BF16),
                   jax.ShapeDtypeStruct((geo.rows, LANES), F32)],
        scratch_shapes=[pltpu.VMEM((ROW_TILE + 2 * SUBLANES, width), F32)],
        compiler_params=_cparams(("parallel", "parallel")),
        name="ssd_conv",
    )(proj, proj, proj, proj, conv_w, conv_b, dt_bias_row)


def _scan_chunk(s, n_chunks, n_ctx_chunks, reverse):
    if not reverse:
        return s
    return jnp.where(s < n_ctx_chunks, n_ctx_chunks - 1 - s, n_chunks + n_ctx_chunks - 1 - s)


def _tri(n, reverse):
    t = lax.broadcasted_iota(jnp.int32, (n, n), 0)
    s = lax.broadcasted_iota(jnp.int32, (n, n), 1)
    return (s >= t) if reverse else (s <= t)


def _split_bf16(x, pieces):
    out, rest = [], x
    for _ in range(pieces):
        p = rest.astype(BF16)
        out.append(p)
        rest = rest - p.astype(F32)
    return out


def _cumsum_rows(mask_bf16, x):
    w = x.shape[1]
    parts = jnp.dot(mask_bf16, jnp.concatenate(_split_bf16(x, 3), axis=1), preferred_element_type=F32)
    return parts[:, 0:w] + parts[:, w:2 * w] + parts[:, 2 * w:3 * w]


def _ssd_scan_kernel(xbc_ref, dt_ref, alog_ref, e_ref, *rest, reverse):
    if reverse:
        yf_ref, z_ref, skip_ref, g_ref, y_ref, h_ref = rest
    else:
        y_ref, h_ref = rest
    q = SSD_CHUNK
    gw = SSD_HPG * SSD_HEAD_DIM
    quad = 4
    qw = quad * SSD_HEAD_DIM
    lane0 = SSD_HEADS if reverse else 0
    n_sub = xbc_ref.shape[0] // q

    @pl.when(pl.program_id(1) == 0)
    def _():
        h_ref[...] = jnp.zeros_like(h_ref)

    lane = lax.broadcasted_iota(jnp.int32, (1, LANES), 1)
    a = jnp.where(lane < 2 * SSD_HEADS, -jnp.exp(alog_ref[...]), 0.0)
    mask = _tri(q, reverse)
    mask_bf = jnp.where(mask, 1.0, 0.0).astype(BF16)
    e = e_ref[...]
    head_of_lane = lax.broadcasted_iota(jnp.int32, (1, qw), 1) // SSD_HEAD_DIM
    zero_bf = jnp.zeros((), BF16)

    for ci in (range(n_sub - 1, -1, -1) if reverse else range(n_sub)):
        rows = slice(ci * q, (ci + 1) * q)
        dt = dt_ref[rows, :]
        cs = _cumsum_rows(mask_bf, dt * a)
        cs_last = cs[0:1, :] if reverse else cs[q - 1:q, :]
        ecs_hi, ecs_lo = _split_bf16(jnp.exp(cs), 2)
        cd = _split_bf16(jnp.broadcast_to(jnp.exp(cs_last), (BF16_SUBLANES, LANES)), 3)
        stack = jnp.concatenate([dt.astype(BF16), (jnp.exp(cs_last - cs) * dt).astype(BF16), ecs_hi, ecs_lo, *cd], axis=0)
        big = jnp.dot(stack, e, preferred_element_type=F32)
        x = xbc_ref[rows, 0:BRANCH_WIDTH].astype(F32)
        dtx = (big[0:q] * x).astype(BF16)
        wx = (big[q:2 * q] * x).astype(BF16)
        ecs = big[2 * q:3 * q] + big[3 * q:4 * q]
        r0 = 4 * q
        chunk_decay = (big[r0:r0 + 1] + big[r0 + BF16_SUBLANES:r0 + BF16_SUBLANES + 1]
                       + big[r0 + 2 * BF16_SUBLANES:r0 + 2 * BF16_SUBLANES + 1])
        cs_t = cs.T
        pieces = []
        for g in range(SSD_GROUPS):
            bm = xbc_ref[rows, BRANCH_WIDTH + g * SSD_STATE:BRANCH_WIDTH + (g + 1) * SSD_STATE]
            c0 = BRANCH_WIDTH + SSD_GROUPS * SSD_STATE + g * SSD_STATE
            cm = xbc_ref[rows, c0:c0 + SSD_STATE]
            cb = lax.dot_general(cm, bm, (((1,), (1,)), ((), ())), preferred_element_type=F32)
            h_in = h_ref[g]
            y_off = jnp.dot(cm, h_in.astype(BF16), preferred_element_type=F32) * ecs[:, g * gw:(g + 1) * gw]
            states = jnp.dot(bm.astype(F32).T.astype(BF16), wx[:, g * gw:(g + 1) * gw], preferred_element_type=F32)
            h_ref[g] = chunk_decay[:, g * gw:(g + 1) * gw] * h_in + states
            for hq in range(SSD_HPG // quad):
                ms = []
                for r in range(quad):
                    col = lane0 + g * SSD_HPG + hq * quad + r
                    seg = cs[:, col:col + 1] - cs_t[col:col + 1, :]
                    ms.append((cb * jnp.exp(jnp.where(mask, seg, -1e30))).astype(BF16))
                lo = g * gw + hq * qw
                slab = dtx[:, lo:lo + qw]
                rhs = jnp.concatenate([jnp.where(head_of_lane == r, slab, zero_bf) for r in range(quad)], axis=0)
                y_diag = jnp.dot(jnp.concatenate(ms, axis=1), rhs, preferred_element_type=F32)
                pieces.append(y_diag + y_off[:, hq * qw:(hq + 1) * qw])
        y = jnp.concatenate(pieces, axis=1)
        if reverse:
            y = (yf_ref[rows, :] + y + skip_ref[...] * x) * _silu(z_ref[rows, :].astype(F32))
            y = y * lax.rsqrt(jnp.mean(y * y, axis=-1, keepdims=True) + NORM_EPS) * g_ref[...]
        y_ref[rows, :] = y.astype(y_ref.dtype)


def _ssd_scan(xbc, dt, a_log_row, expand_mat, geo, reverse, finish=None):
    nb, nbc = geo.lt // SCAN_BLOCK, geo.ctx_len // SCAN_BLOCK
    width, w = xbc.shape[1], BRANCH_WIDTH
    row = lambda b, s: b * nb + _scan_chunk(s, nb, nbc, reverse)
    in_specs = [pl.BlockSpec((SCAN_BLOCK, width), lambda b, s: (row(b, s), 0)),
                pl.BlockSpec((SCAN_BLOCK, LANES), lambda b, s: (row(b, s), 0)),
                pl.BlockSpec((1, LANES), lambda b, s: (0, 0)),
                pl.BlockSpec((LANES, w), lambda b, s: (0, 0))]
    args = [xbc, dt, a_log_row, expand_mat]
    if reverse:
        in_specs += [pl.BlockSpec((SCAN_BLOCK, w), lambda b, s: (row(b, s), 0)),
                     pl.BlockSpec((SCAN_BLOCK, w), lambda b, s: (row(b, s), _COL["B_Z"] // w)),
                     pl.BlockSpec((1, w), lambda b, s: (0, 0)),
                     pl.BlockSpec((1, w), lambda b, s: (0, 0))]
        args += list(finish)
    return pl.pallas_call(
        functools.partial(_ssd_scan_kernel, reverse=reverse),
        grid=(geo.batch, nb),
        in_specs=in_specs,
        out_specs=pl.BlockSpec((SCAN_BLOCK, w), lambda b, s: (row(b, s), 0)),
        out_shape=jax.ShapeDtypeStruct((geo.rows, w), BF16 if reverse else F32),
        scratch_shapes=[pltpu.VMEM((SSD_GROUPS, SSD_STATE, SSD_HPG * SSD_HEAD_DIM), F32)],
        compiler_params=_cparams(("parallel", "arbitrary")),
        name="ssd_scan_bwd" if reverse else "ssd_scan_fwd",
    )(*args)


def _gla_scan_kernel(q_ref, k_ref, v_ref, nar_ref, w2_ref, bf_ref, *rest, reverse):
    if reverse:
        of_ref, gate_ref, g_ref, o_ref, h_ref = rest
    else:
        o_ref, h_ref = rest
    n = GLA_CHUNK

    @pl.when(pl.program_id(1) == 0)
    def _():
        h_ref[...] = jnp.zeros_like(h_ref)

    t = q_ref.shape[0]
    n_sub = t // n
    ri = lax.broadcasted_iota(jnp.int32, (t, t), 0)
    ci = lax.broadcasted_iota(jnp.int32, (t, t), 1)
    mask = jnp.logical_and(ri // n == ci // n, (ci >= ri) if reverse else (ci <= ri))
    logit = jnp.dot(nar_ref[...], w2_ref[...], preferred_element_type=F32) + bf_ref[...]
    gl = _log_sigmoid(logit) / GLA_TAU
    b = _cumsum_rows(jnp.where(mask, 1.0, 0.0).astype(BF16), gl)
    last = [b[c * n:c * n + 1, :] if reverse else b[(c + 1) * n - 1:(c + 1) * n, :] for c in range(n_sub)]
    b_last = jnp.concatenate([jnp.broadcast_to(r, (n, r.shape[1])) for r in last], axis=0)
    q = q_ref[...].astype(F32) * (GLA_DK ** -0.5)
    k = k_ref[...].astype(F32)
    qe = (q * jnp.exp(b)).astype(BF16)
    ke = (k * jnp.exp(-b)).astype(BF16)
    kd = (k * jnp.exp(b_last - b)).astype(BF16)
    for h in range(GLA_HEADS):
        ks = slice(h * GLA_DK, (h + 1) * GLA_DK)
        vs = slice(h * GLA_DV, (h + 1) * GLA_DV)
        v = v_ref[:, vs]
        att = lax.dot_general(qe[:, ks], ke[:, ks], (((1,), (1,)), ((), ())), preferred_element_type=F32)
        o_intra = jnp.dot(jnp.where(mask, att, 0.0).astype(BF16), v, preferred_element_type=F32)
        o_inter = [None] * n_sub
        for c in (range(n_sub - 1, -1, -1) if reverse else range(n_sub)):
            rows = slice(c * n, (c + 1) * n)
            h_in = h_ref[h]
            o_inter[c] = lax.dot_general(qe[rows, ks], h_in.astype(BF16), (((1,), (1,)), ((), ())),
                                         preferred_element_type=F32)
            upd = jnp.dot(v[rows].astype(F32).T.astype(BF16), kd[rows, ks], preferred_element_type=F32)
            h_ref[h] = jnp.exp(last[c][:, ks]) * h_in + upd
        o = o_intra + jnp.concatenate(o_inter, axis=0)
        if reverse:
            o = o + of_ref[:, vs]
            o = o * lax.rsqrt(jnp.mean(o * o, axis=-1, keepdims=True) + NORM_EPS) * g_ref[...]
            o = o * _silu(gate_ref[:, vs].astype(F32))
        o_ref[:, vs] = o.astype(o_ref.dtype)


def _gla_scan(proj, w2, b_f, geo, reverse, finish=None):
    nb, nbc = geo.lt // SCAN_BLOCK, geo.ctx_len // SCAN_BLOCK
    row = lambda b, s: b * nb + _scan_chunk(s, nb, nbc, reverse)
    kw, vw = GLA_HEADS * GLA_DK, GLA_HEADS * GLA_DV
    in_specs = [pl.BlockSpec((SCAN_BLOCK, kw), lambda b, s: (row(b, s), _COL["C_Q"] // kw)),
                pl.BlockSpec((SCAN_BLOCK, kw), lambda b, s: (row(b, s), _COL["C_K"] // kw)),
                pl.BlockSpec((SCAN_BLOCK, vw), lambda b, s: (row(b, s), _COL["C_V"] // vw)),
                pl.BlockSpec((SCAN_BLOCK, LANES), lambda b, s: (row(b, s), _COL["NARROW"] // LANES)),
                pl.BlockSpec((LANES, kw), lambda b, s: (0, 0)),
                pl.BlockSpec((1, kw), lambda b, s: (0, 0))]
    args = [proj, proj, proj, proj, w2, b_f]
    if reverse:
        of, norm_g = finish
        in_specs += [pl.BlockSpec((SCAN_BLOCK, vw), lambda b, s: (row(b, s), 0)),
                     pl.BlockSpec((SCAN_BLOCK, vw), lambda b, s: (row(b, s), _COL["C_G"] // vw)),
                     pl.BlockSpec((1, GLA_DV), lambda b, s: (0, 0))]
        args += [of, proj, norm_g]
    return pl.pallas_call(
        functools.partial(_gla_scan_kernel, reverse=reverse),
        grid=(geo.batch, nb),
        in_specs=in_specs,
        out_specs=pl.BlockSpec((SCAN_BLOCK, vw), lambda b, s: (row(b, s), 0)),
        out_shape=jax.ShapeDtypeStruct((geo.rows, vw), BF16 if reverse else F32),
        scratch_shapes=[pltpu.VMEM((GLA_HEADS, GLA_DV, GLA_DK), F32)],
        compiler_params=_cparams(("parallel", "arbitrary")),
        name="gla_scan_bwd" if reverse else "gla_scan_fwd",
    )(*args)


def _shortconv_kernel(cur_ref, prev_ref, next_ref, w_ref, o_ref, ext_ref, *, geo, first):
    j = pl.program_id(1) + first
    w = BRANCH_WIDTH
    u = lambda ref: ref[:, w:2 * w].astype(F32) * ref[:, 2 * w:3 * w].astype(F32)
    _fill_ext(ext_ref, u(cur_ref), u(prev_ref), u(next_ref), j, geo)
    t = cur_ref.shape[0]
    acc = jnp.zeros((t, w), F32)
    for k in range(SC_CONV):
        acc = acc + w_ref[k:k + 1, :] * ext_ref[SUBLANES + k - SC_CONV // 2:SUBLANES + k - SC_CONV // 2 + t, :]
    y = cur_ref[:, 0:w].astype(F32) * acc * _silu(cur_ref[:, 3 * w:4 * w].astype(F32))
    o_ref[...] = y.astype(o_ref.dtype)


def _shortconv(proj, conv_w, geo, with_ctx):
    first, nt = geo.span(with_ctx)
    row = lambda b, j: b * geo.tpb + first + j
    width = 4 * BRANCH_WIDTH
    cb = _COL["D_ALL"] // width
    prev, nxt = _halo_specs(width, cb, geo, first, row)
    return pl.pallas_call(
        functools.partial(_shortconv_kernel, geo=geo, first=first),
        grid=(geo.batch, nt),
        in_specs=[pl.BlockSpec((ROW_TILE, width), lambda b, j: (row(b, j), cb)), prev, nxt,
                  pl.BlockSpec((SC_CONV, BRANCH_WIDTH), lambda b, j: (0, 0))],
        out_specs=pl.BlockSpec((ROW_TILE, BRANCH_WIDTH), lambda b, j: (row(b, j), 0)),
        out_shape=jax.ShapeDtypeStruct((geo.rows, BRANCH_WIDTH), BF16),
        scratch_shapes=[pltpu.VMEM((ROW_TILE + 2 * SUBLANES, BRANCH_WIDTH), F32)],
        compiler_params=_cparams(("parallel", "parallel")),
        name="shortconv",
    )(proj, proj, proj, conv_w)


def _merge_out_kernel(ya_ref, yb_ref, yg_ref, yd_ref, g0_ref, g1_ref, g2_ref, g3_ref, wb_ref, wo_ref,
                      x_ref, gt_ref, gp_ref, o_ref):
    m = None
    for n, (y_ref, gate_ref) in enumerate(zip((ya_ref, yb_ref, yg_ref, yd_ref), (g0_ref, g1_ref, g2_ref, g3_ref))):
        term = jax.nn.sigmoid(gate_ref[...].astype(F32)) * jnp.dot(y_ref[...], wb_ref[n], preferred_element_type=F32)
        m = term if m is None else m + term
    out = jnp.dot(m.astype(BF16), wo_ref[...], preferred_element_type=F32)
    y = out * lax.rsqrt(jnp.mean(out * out, axis=-1, keepdims=True) + NORM_EPS) * gp_ref[...]
    o_ref[...] = x_ref[...] + gt_ref[0] * y


def _merge_out(ys, proj, w_branch, w_out, x_all, mod3, g_post, geo, with_ctx):
    first, nt = geo.span(with_ctx)
    w, d = BRANCH_WIDTH, D_MODEL
    row = lambda b, j: b * geo.tpb + first + j
    out_rows = geo.rows if with_ctx else geo.batch * geo.seq
    out_row = row if with_ctx else (lambda b, j: b * nt + j)
    y_spec = pl.BlockSpec((ROW_TILE, w), lambda b, j: (row(b, j), 0))
    gate_specs = [pl.BlockSpec((ROW_TILE, d), functools.partial(lambda b, j, n: (row(b, j), _COL["MG"] // d + n), n=n))
                  for n in range(N_BRANCH)]
    resident = pl.Buffered(1)
    return pl.pallas_call(
        _merge_out_kernel,
        grid=(geo.batch, nt),
        in_specs=[y_spec, y_spec, y_spec, y_spec, *gate_specs,
                  pl.BlockSpec((N_BRANCH, w, d), lambda b, j: (0, 0, 0), pipeline_mode=resident),
                  pl.BlockSpec((d, d), lambda b, j: (0, 0), pipeline_mode=resident),
                  pl.BlockSpec((ROW_TILE, d), lambda b, j: (row(b, j), 0)),
                  pl.BlockSpec((1, 1, d), lambda b, j: (geo.mod_row(b, first + j), 0, 2)),
                  pl.BlockSpec((1, d), lambda b, j: (0, 0))],
        out_specs=pl.BlockSpec((ROW_TILE, d), lambda b, j: (out_row(b, j), 0)),
        out_shape=jax.ShapeDtypeStruct((out_rows, d), F32),
        compiler_params=_cparams(("parallel", "parallel")),
        name="merge_out",
    )(*ys, proj, proj, proj, proj, w_branch, w_out, x_all, mod3, g_post)


def _regroup_plan():
    bw = BRANCH_WIDTH
    names = ("a_q", "a_k", "a_v", "a_g", "b_x", "b_z", "b_b", "b_c", "b_dt", "c_q", "c_k", "c_v", "c_g", "c_f",
             "d_all", "mg")
    widths = (ATT_HEADS * HEAD_DIM, ATT_KV_HEADS * HEAD_DIM, ATT_KV_HEADS * HEAD_DIM, bw,
              bw, bw, SSD_GROUPS * SSD_STATE, SSD_GROUPS * SSD_STATE, 2 * SSD_HEADS,
              GLA_HEADS * GLA_DK, GLA_HEADS * GLA_DK, GLA_HEADS * GLA_DV, bw, 2 * GLA_RANK,
              4 * bw, N_BRANCH * D_MODEL)
    src = dict(zip(names, np.concatenate([[0], np.cumsum(widths)[:-1]]).tolist()))
    wid = dict(zip(names, widths))
    dst = dict(a_q=_COL["A_Q"], a_g=_COL["A_G"], b_z=_COL["B_Z"], b_x=_COL["B_X"], b_b=_COL["B_B"], b_c=_COL["B_C"],
               a_k=_COL["A_K"], a_v=_COL["A_V"], c_v=_COL["C_V"], c_g=_COL["C_G"], c_q=_COL["C_Q"], c_k=_COL["C_K"],
               d_all=_COL["D_ALL"], mg=_COL["MG"], b_dt=_COL["NARROW"] + DT_LANE0, c_f=_COL["NARROW"] + F1_LANE0)
    return [(dst[n], src[n], wid[n], n in ("a_q", "a_k")) for n in names], sum(widths)


def _regroup_kernel(w_ref, perm_ref, o_ref):
    plan, _ = _regroup_plan()
    perm = perm_ref[...]
    for dst, src, width, deinterleave in plan:
        if deinterleave:
            for h in range(width // HEAD_DIM):
                cols = w_ref[:, src + h * HEAD_DIM:src + (h + 1) * HEAD_DIM].astype(BF16)
                o_ref[:, dst + h * HEAD_DIM:dst + (h + 1) * HEAD_DIM] = jnp.dot(
                    cols, perm, preferred_element_type=F32).astype(o_ref.dtype)
        elif width % LANES == 0:
            o_ref[:, dst:dst + width] = w_ref[:, src:src + width].astype(o_ref.dtype)
    (d0, s0, w0, _), (d1, s1, w1, _) = [p for p in plan if p[2] % LANES != 0]
    rows = w_ref.shape[0]
    narrow = jnp.concatenate([w_ref[:, s0:s0 + w0], w_ref[:, s1:s1 + w1],
                              jnp.zeros((rows, LANES - w0 - w1), F32)], axis=1)
    o_ref[:, _COL["NARROW"]:_COL["NARROW"] + LANES] = narrow.astype(o_ref.dtype)


def _regroup_w_in(w_in, layer):
    _, k, n = w_in.shape
    assert n == _regroup_plan()[1]
    rows = 64
    perm = np.zeros((HEAD_DIM, HEAD_DIM), np.float32)
    for i in range(HEAD_DIM // 2):
        perm[2 * i, i] = 1.0
        perm[2 * i + 1, HEAD_DIM // 2 + i] = 1.0
    return pl.pallas_call(
        _regroup_kernel,
        grid=(k // rows,),
        in_specs=[pl.BlockSpec((None, rows, n), lambda i: (layer, i, 0)),
                  pl.BlockSpec((HEAD_DIM, HEAD_DIM), lambda i: (0, 0))],
        out_specs=pl.BlockSpec((rows, N_PROJ), lambda i: (i, 0)),
        out_shape=jax.ShapeDtypeStruct((k, N_PROJ), BF16),
        compiler_params=_cparams(("parallel",)),
        name="regroup_w_in",
    )(w_in, jnp.asarray(perm, BF16))


def _deinterleave_vec(g):
    return g.reshape(HEAD_DIM // 2, 2).T.reshape(1, HEAD_DIM)


def _pad_lanes(v, lane0=0):
    return jnp.zeros((1, LANES), F32).at[0, lane0:lane0 + v.shape[0]].set(v.astype(F32))


def _head_expand_matrix(reverse):
    e = np.zeros((LANES, BRANCH_WIDTH), np.float32)
    lane0 = SSD_HEADS if reverse else 0
    for r in range(SSD_HEADS):
        e[lane0 + r, r * SSD_HEAD_DIM:(r + 1) * SSD_HEAD_DIM] = 1.0
    return jnp.asarray(e, BF16)


def _forget_weight(w_f2_dir, direction):
    lane0 = F1_LANE0 + direction * GLA_RANK
    return jnp.zeros((LANES, w_f2_dir.shape[1]), F32).at[lane0:lane0 + GLA_RANK].set(w_f2_dir).astype(BF16)


def kernel(x, c, ctx, c_ctx, w_mod, b_mod, g_pre, g_post, w_in, g_q, g_k, ssd_conv_w, ssd_conv_b,
           ssd_a_log, ssd_dt_bias, ssd_d, ssd_norm_g, gla_w_f2, gla_b_f, gla_norm_g, sc_conv_w,
           w_branch, w_out):
    batch, seq, d = x.shape
    ctx_len = ctx.shape[1]
    depth = w_in.shape[0]
    geo = _Geom(batch, ctx_len, seq)
    assert d == D_MODEL and batch + 1 <= SUBLANES and seq % GRID_W == 0

    cos_t, sin_t = _rope_tables(geo)
    c_rows = jnp.zeros((SUBLANES, d), F32).at[:batch].set(c).at[batch].set(c_ctx)
    x_all = jnp.concatenate([ctx, x], axis=1).reshape(geo.rows, d)
    e_fwd, e_bwd = _head_expand_matrix(False), _head_expand_matrix(True)

    for l in range(depth):
        need_ctx = l < depth - 1
        mod = _modulation(c_rows, w_mod, b_mod[l][None, :], l)
        mod3 = mod.reshape(SUBLANES, 1, 3 * d)
        h = _prenorm(x_all, g_pre[l][None, :], mod3, geo)
        proj = _matmul(h, _regroup_w_in(w_in, l), PROJ_TN)

        qt, kh, vt = _qk_prep(proj, cos_t, sin_t, _deinterleave_vec(g_q[l]), _deinterleave_vec(g_k[l]), geo)
        ya = _attention(qt, kh, vt, proj, None, geo, ctx_len, seq, geo.lt)
        if need_ctx:
            ya = _attention(qt, kh, vt, proj, ya, geo, 0, ctx_len, ctx_len)

        xbc, dt = _ssd_conv(proj, ssd_conv_w[l], ssd_conv_b[l][None, :], _pad_lanes(ssd_dt_bias[l].reshape(-1)), geo)
        a_log_row = _pad_lanes(ssd_a_log[l].reshape(-1))
        ysf = _ssd_scan(xbc, dt, a_log_row, e_fwd, geo, False)
        skip_row = jnp.repeat(ssd_d[l], SSD_HEAD_DIM)[None, :]
        yb = _ssd_scan(xbc, dt, a_log_row, e_bwd, geo, True, (ysf, proj, skip_row, ssd_norm_g[l][None, :]))

        ogf = _gla_scan(proj, _forget_weight(gla_w_f2[l, 0], 0), gla_b_f[l, 0][None, :], geo, False)
        yg = _gla_scan(proj, _forget_weight(gla_w_f2[l, 1], 1), gla_b_f[l, 1][None, :], geo, True,
                       (ogf, gla_norm_g[l][None, :]))

        yd = _shortconv(proj, sc_conv_w[l], geo, need_ctx)

        x_all = _merge_out((ya, yb, yg, yd), proj, w_branch[l].astype(BF16), w_out[l].astype(BF16), x_all, mod3,
                           g_post[l][None, :], geo, need_ctx)

    return x_all.reshape(batch, seq, d)
```

```python
import functools

import numpy as np
import jax
import jax.numpy as jnp
from jax import lax
from jax.experimental import pallas as pl
from jax.experimental.pallas import tpu as pltpu

F32 = jnp.float32
BF16 = jnp.bfloat16

D_MODEL = 2048
GRID_W = 64
BRANCH_WIDTH = D_MODEL // 2
N_BRANCH = 4
NORM_EPS = 1e-6
HEAD_DIM = 128
ATT_HEADS = BRANCH_WIDTH // HEAD_DIM
ATT_KV_HEADS = ATT_HEADS // 4
ATT_REP = ATT_HEADS // ATT_KV_HEADS
ROPE_THETA = 10000.0
SSD_HEAD_DIM = 64
SSD_HEADS = BRANCH_WIDTH // SSD_HEAD_DIM
SSD_GROUPS = 2
SSD_HPG = SSD_HEADS // SSD_GROUPS
SSD_STATE = 128
SSD_CONV = 5
SSD_CHUNK = 128
GLA_HEADS = 4
GLA_DV = BRANCH_WIDTH // GLA_HEADS
GLA_DK = GLA_DV // 2
GLA_RANK = 16
GLA_TAU = 16.0
GLA_CHUNK = 64
SC_CONV = 3

LANES = 128
SUBLANES = 8
ROW_TILE = 256
SCAN_BLOCK = ROW_TILE
VMEM_LIMIT = 56 * 1024 * 1024

_COL = dict(
    A_Q=0, A_G=1024, B_Z=2048, B_X=3072, B_B=4096, B_C=4352, A_K=4608, A_V=4864,
    C_V=5120, C_G=6144, C_Q=7168, C_K=7680,
    D_ALL=8192, MG=12288, NARROW=20480,
)
MXU_WIDTH = 256
N_PROJ = 20736
PROJ_TN = 2304
ATTN_UNROLL = 2
ATTN_KEY_CHUNKS = (ROW_TILE, 2 * ROW_TILE, 3 * ROW_TILE)
BF16_SUBLANES = 16
VT_ROWS = HEAD_DIM + BF16_SUBLANES
Q_SCALE = HEAD_DIM ** -0.5 * float(np.log2(np.e))
DT_LANE0 = 0
F1_LANE0 = 32


def _cparams(sem, vmem=VMEM_LIMIT):
    return pltpu.CompilerParams(dimension_semantics=sem, vmem_limit_bytes=vmem)


def _silu(x):
    return x * jax.nn.sigmoid(x)


def _softplus(x):
    return jnp.maximum(x, 0.0) + jnp.log1p(jnp.exp(-jnp.abs(x)))


def _log_sigmoid(x):
    return jnp.minimum(x, 0.0) - jnp.log1p(jnp.exp(-jnp.abs(x)))


def _mod_kernel(c_ref, w_ref, b_ref, o_ref):
    a = _silu(c_ref[...]).astype(BF16)
    o_ref[...] = jnp.dot(a, w_ref[...].astype(BF16), preferred_element_type=F32) + b_ref[...]


def _modulation(c_rows, w_mod, b_mod, layer):
    _, d, n = w_mod.shape
    tn = 1024
    return pl.pallas_call(
        _mod_kernel,
        grid=(n // tn,),
        in_specs=[pl.BlockSpec((SUBLANES, d), lambda j: (0, 0)),
                  pl.BlockSpec((None, d, tn), lambda j: (layer, 0, j)),
                  pl.BlockSpec((1, tn), lambda j: (0, j))],
        out_specs=pl.BlockSpec((SUBLANES, tn), lambda j: (0, j)),
        out_shape=jax.ShapeDtypeStruct((SUBLANES, n), F32),
        compiler_params=_cparams(("parallel",)),
        name="modulation",
    )(c_rows, w_mod, b_mod)


class _Geom:
    def __init__(self, batch, ctx_len, seq):
        self.batch, self.ctx_len, self.seq = batch, ctx_len, seq
        self.lt = ctx_len + seq
        self.rows = batch * self.lt
        assert ctx_len % ROW_TILE == 0 and seq % ROW_TILE == 0
        self.tpb = self.lt // ROW_TILE
        self.ctx_tiles = ctx_len // ROW_TILE
        self.lat_tiles = seq // ROW_TILE

    def span(self, with_ctx):
        return (0, self.tpb) if with_ctx else (self.ctx_tiles, self.lat_tiles)

    def mod_row(self, b, j):
        return jnp.where(j < self.ctx_tiles, self.batch, b)


def _token_specs(tokens, geo, first):
    d = D_MODEL
    if len(tokens) == 1:
        return [pl.BlockSpec((ROW_TILE, d), lambda b, j: (b * geo.tpb + first + j, 0))]
    ct, lt = geo.ctx_tiles, geo.lat_tiles
    return [pl.BlockSpec((ROW_TILE, d), lambda b, j: (b * ct + jnp.minimum(first + j, ct - 1), 0)),
            pl.BlockSpec((ROW_TILE, d), lambda b, j: (b * lt + jnp.maximum(first + j - ct, 0), 0))]


def _read_tokens(refs, tile, geo):
    if len(refs) == 1:
        return refs[0][...]
    return jnp.where(tile < geo.ctx_tiles, refs[0][...], refs[1][...])


def _adaln_prenorm(x, g, sh, sc):
    y = x * lax.rsqrt(jnp.mean(x * x, axis=-1, keepdims=True) + NORM_EPS)
    return (y * g) * (1.0 + sc) + sh


def _prenorm_kernel(*refs, n_src, geo):
    g_ref, sh_ref, sc_ref, o_ref = refs[n_src:]
    x = _read_tokens(refs[:n_src], pl.program_id(1), geo)
    o_ref[...] = _adaln_prenorm(x, g_ref[...], sh_ref[0], sc_ref[0]).astype(o_ref.dtype)


def _prenorm(tokens, g_pre, mod3, geo):
    d = D_MODEL
    return pl.pallas_call(
        functools.partial(_prenorm_kernel, n_src=len(tokens), geo=geo),
        grid=(geo.batch, geo.tpb),
        in_specs=[*_token_specs(tokens, geo, 0),
                  pl.BlockSpec((1, d), lambda b, j: (0, 0)),
                  pl.BlockSpec((1, 1, d), lambda b, j: (geo.mod_row(b, j), 0, 0)),
                  pl.BlockSpec((1, 1, d), lambda b, j: (geo.mod_row(b, j), 0, 1))],
        out_specs=pl.BlockSpec((ROW_TILE, d), lambda b, j: (b * geo.tpb + j, 0)),
        out_shape=jax.ShapeDtypeStruct((geo.rows, d), BF16),
        compiler_params=_cparams(("parallel", "parallel")),
        name="prenorm",
    )(*tokens, g_pre, mod3, mod3)


def _matmul_kernel(a_ref, w_ref, o_ref):
    o_ref[...] = lax.dot_general(a_ref[...], w_ref[...], (((1,), (1,)), ((), ())),
                                 preferred_element_type=F32).astype(o_ref.dtype)


def _matmul(a, w_t, tn, out_dtype=BF16):
    m, k = a.shape
    n = w_t.shape[0]
    tm = next(t for t in (768, 512, 256) if m % t == 0)
    assert m % tm == 0 and n % tn == 0
    return pl.pallas_call(
        _matmul_kernel,
        grid=(n // tn, m // tm),
        in_specs=[pl.BlockSpec((tm, k), lambda j, i: (i, 0)),
                  pl.BlockSpec((tn, k), lambda j, i: (j, 0))],
        out_specs=pl.BlockSpec((tm, tn), lambda j, i: (i, j)),
        out_shape=jax.ShapeDtypeStruct((m, n), out_dtype),
        compiler_params=_cparams(("parallel", "parallel")),
        name="in_proj",
    )(a, w_t)


def _rope_tables(geo):
    rows = geo.seq // GRID_W
    t_row = jnp.repeat(jnp.arange(rows, dtype=F32), GRID_W)
    t_col = jnp.tile(jnp.arange(GRID_W, dtype=F32), rows)
    half = HEAD_DIM // 2
    freqs = ROPE_THETA ** (-(jnp.arange(0, half, 2, dtype=F32) / half))
    ang = jnp.concatenate([t_row[:, None] * freqs, t_col[:, None] * freqs], axis=-1)
    cos, sin = jnp.cos(ang), jnp.sin(ang)
    cos_l = jnp.concatenate([cos, cos], axis=-1)
    sin_l = jnp.concatenate([-sin, sin], axis=-1)
    cos_c = jnp.ones((geo.ctx_len, HEAD_DIM), F32)
    sin_c = jnp.zeros((geo.ctx_len, HEAD_DIM), F32)
    return jnp.concatenate([cos_c, cos_l], axis=0), jnp.concatenate([sin_c, sin_l], axis=0)


def _qk_prep_kernel(q_ref, k_ref, v_ref, cos_ref, sin_ref, gq_ref, gk_ref, qt_ref, ko_ref, vt_ref):
    cos, sin = cos_ref[...], sin_ref[...]
    t = q_ref.shape[0]

    ones = jnp.ones((HEAD_DIM, HEAD_DIM), BF16)

    def norm_rope(x, g):
        x = x.astype(F32)
        hi, lo = _split_bf16(x * x, 2)
        ss = jnp.dot(hi, ones, preferred_element_type=F32) + jnp.dot(lo, ones, preferred_element_type=F32)
        y = x * lax.rsqrt(ss * (1.0 / HEAD_DIM) + NORM_EPS) * g
        return y * cos + pltpu.roll(y, HEAD_DIM // 2, 1) * sin

    gq, gk = gq_ref[...], gk_ref[...]
    for h in range(ATT_HEADS):
        g, r = divmod(h, ATT_REP)
        y = norm_rope(q_ref[:, h * HEAD_DIM:(h + 1) * HEAD_DIM], gq) * Q_SCALE
        qt_ref[g, 0, :, r * t:(r + 1) * t] = y.astype(qt_ref.dtype).T
    for g in range(ATT_KV_HEADS):
        hs = slice(g * HEAD_DIM, (g + 1) * HEAD_DIM)
        ko_ref[g] = norm_rope(k_ref[:, hs], gk).astype(ko_ref.dtype)
        vt_ref[g, 0, 0:HEAD_DIM, :] = v_ref[:, hs].T
        vt_ref[g, 0, HEAD_DIM:VT_ROWS, :] = jnp.ones((VT_ROWS - HEAD_DIM, t), vt_ref.dtype)


def _qk_prep(proj, cos_t, sin_t, g_q, g_k, geo):
    row = lambda b, j: b * geo.tpb + j
    nq, nk = ATT_HEADS * HEAD_DIM, ATT_KV_HEADS * HEAD_DIM
    kv, t = ATT_KV_HEADS, ROW_TILE
    return pl.pallas_call(
        _qk_prep_kernel,
        grid=(geo.batch, geo.tpb),
        in_specs=[pl.BlockSpec((t, nq), lambda b, j: (row(b, j), _COL["A_Q"] // nq)),
                  pl.BlockSpec((t, nk), lambda b, j: (row(b, j), _COL["A_K"] // nk)),
                  pl.BlockSpec((t, nk), lambda b, j: (row(b, j), _COL["A_V"] // nk)),
                  pl.BlockSpec((t, HEAD_DIM), lambda b, j: (j, 0)),
                  pl.BlockSpec((t, HEAD_DIM), lambda b, j: (j, 0)),
                  pl.BlockSpec((1, HEAD_DIM), lambda b, j: (0, 0)),
                  pl.BlockSpec((1, HEAD_DIM), lambda b, j: (0, 0))],
        out_specs=[pl.BlockSpec((kv, 1, HEAD_DIM, ATT_REP * t), lambda b, j: (b, j, 0, 0)),
                   pl.BlockSpec((kv, t, HEAD_DIM), lambda b, j: (b, j, 0)),
                   pl.BlockSpec((kv, 1, VT_ROWS, t), lambda b, j: (b, j, 0, 0))],
        out_shape=[jax.ShapeDtypeStruct((geo.batch * kv, geo.tpb, HEAD_DIM, ATT_REP * t), BF16),
                   jax.ShapeDtypeStruct((geo.batch * kv, geo.lt, HEAD_DIM), BF16),
                   jax.ShapeDtypeStruct((geo.batch * kv, geo.tpb, VT_ROWS, t), BF16)],
        compiler_params=_cparams(("parallel", "parallel")),
        name="qk_prep",
    )(proj, proj, proj, cos_t, sin_t, g_q, g_k)


def _attn_kernel(qt_ref, k_ref, vt_ref, gate_ref, o_ref, m_ref, acc_ref, s0_ref, s1_ref, mx0_ref, mx1_ref):
    vt_tile = vt_ref.shape[2]
    tk = s0_ref.shape[0]
    tiles = tk // vt_tile
    n_chunks = k_ref.shape[0] // tk
    tq = o_ref.shape[0]
    qt = qt_ref[...]
    m_ref[...] = jnp.full(m_ref.shape, -jnp.inf, F32)
    acc_ref[...] = jnp.zeros(acc_ref.shape, F32)

    bufs = ((s0_ref, mx0_ref), (s1_ref, mx1_ref))

    def scores(c, buf):
        s_ref, mx_ref = buf
        k = k_ref[pl.ds(pl.multiple_of(c * tk, tk), tk), :]
        s = jnp.dot(k, qt, preferred_element_type=F32)
        s_ref[...] = s
        mx_ref[...] = jnp.max(s, axis=0, keepdims=True)

    def absorb(c, buf):
        s_ref, mx_ref = buf
        m_old = m_ref[...]
        m_new = jnp.maximum(m_old, mx_ref[...])
        alpha = jnp.exp2(m_old - m_new)
        pv = None
        for t in range(tiles):
            p = jnp.exp2((s_ref[t * vt_tile:(t + 1) * vt_tile, :] - m_new).astype(BF16))
            part = jnp.dot(vt_ref[c * tiles + t], p, preferred_element_type=F32)
            pv = part if pv is None else pv + part
        acc_ref[...] = alpha * acc_ref[...] + pv
        m_ref[...] = m_new

    def group(i, carry):
        c0 = ATTN_UNROLL * i
        for u in range(ATTN_UNROLL):
            scores(c0 + u + 1, bufs[(u + 1) % 2])
            absorb(c0 + u, bufs[u % 2])
        return carry

    scores(0, bufs[0])
    n_groups = (n_chunks - 1) // ATTN_UNROLL
    if n_groups > 0:
        lax.fori_loop(0, n_groups, group, 0)
    for c in range(n_groups * ATTN_UNROLL, n_chunks):
        if c + 1 < n_chunks:
            scores(c + 1, bufs[(c + 1) % 2])
        absorb(c, bufs[c % 2])
    o_t = acc_ref[0:HEAD_DIM, :] / acc_ref[HEAD_DIM:HEAD_DIM + 1, :]
    o = jnp.concatenate([o_t[:, r * tq:(r + 1) * tq].T for r in range(ATT_REP)], axis=1)
    o_ref[...] = (o * _silu(gate_ref[...].astype(F32))).astype(o_ref.dtype)


def _attention(qt, kh, vt, proj, ya_prev, geo, q_first_row, q_rows, kv_rows):
    tq = ROW_TILE
    assert q_first_row % tq == 0 and q_rows % tq == 0 and kv_rows % ROW_TILE == 0
    q0 = q_first_row // tq
    gw = ATT_REP * HEAD_DIM
    tk = max(t for t in ATTN_KEY_CHUNKS if kv_rows % t == 0)
    kernel = _attn_kernel
    in_specs = [
        pl.BlockSpec((None, None, HEAD_DIM, ATT_REP * tq), lambda b, g, i: (b * ATT_KV_HEADS + g, q0 + i, 0, 0)),
        pl.BlockSpec((None, kv_rows, HEAD_DIM), lambda b, g, i: (b * ATT_KV_HEADS + g, 0, 0)),
        pl.BlockSpec((None, kv_rows // ROW_TILE, VT_ROWS, ROW_TILE), lambda b, g, i: (b * ATT_KV_HEADS + g, 0, 0, 0)),
        pl.BlockSpec((tq, gw), lambda b, g, i: (b * geo.tpb + q0 + i, _COL["A_G"] // gw + g)),
    ]
    out_spec = pl.BlockSpec((tq, gw), lambda b, g, i: (b * geo.tpb + q0 + i, g))
    args = [qt, kh, vt, proj]
    aliases = {}
    if ya_prev is not None:
        in_specs.append(pl.BlockSpec(memory_space=pl.ANY))
        args.append(ya_prev)
        aliases = {4: 0}
        kernel = functools.partial(_drop_last_input, kernel, 4)
    return pl.pallas_call(
        kernel,
        grid=(geo.batch, ATT_KV_HEADS, q_rows // tq),
        in_specs=in_specs,
        out_specs=out_spec,
        out_shape=jax.ShapeDtypeStruct((geo.rows, BRANCH_WIDTH), BF16),
        scratch_shapes=[pltpu.VMEM((1, ATT_REP * tq), F32), pltpu.VMEM((VT_ROWS, ATT_REP * tq), F32),
                        pltpu.VMEM((tk, ATT_REP * tq), F32), pltpu.VMEM((tk, ATT_REP * tq), F32),
                        pltpu.VMEM((1, ATT_REP * tq), F32), pltpu.VMEM((1, ATT_REP * tq), F32)],
        input_output_aliases=aliases,
        compiler_params=_cparams(("parallel", "parallel", "arbitrary")),
        name="attention",
    )(*args)


def _drop_last_input(kernel, n_in, *refs):
    return kernel(*refs[:n_in], *refs[n_in + 1:])


def _halo_specs(width, col_block, geo, first, row):
    per = ROW_TILE // SUBLANES
    last = geo.rows // SUBLANES - 1
    prev = pl.BlockSpec((SUBLANES, width), lambda b, j: (jnp.maximum(row(b, j) * per - 1, 0), col_block))
    nxt = pl.BlockSpec((SUBLANES, width), lambda b, j: (jnp.minimum((row(b, j) + 1) * per, last), col_block))
    return prev, nxt


def _fill_ext(ext_ref, cur, prev, nxt, j, geo):
    t = cur.shape[0]
    seg_first = jnp.logical_or(j == 0, j == geo.ctx_tiles)
    seg_last = jnp.logical_or(j == geo.ctx_tiles - 1, j == geo.tpb - 1)
    ext_ref[0:SUBLANES, :] = jnp.where(seg_first, 0.0, prev)
    ext_ref[SUBLANES:SUBLANES + t, :] = cur
    ext_ref[SUBLANES + t:2 * SUBLANES + t, :] = jnp.where(seg_last, 0.0, nxt)


def _ssd_conv_kernel(cur_ref, prev_ref, next_ref, nar_ref, w_ref, b_ref, dtb_ref, xbc_ref, dt_ref, ext_ref, *, geo):
    j = pl.program_id(1)
    _fill_ext(ext_ref, cur_ref[...].astype(F32), prev_ref[...].astype(F32), next_ref[...].astype(F32), j, geo)
    t = cur_ref.shape[0]
    acc = jnp.zeros(cur_ref.shape, F32) + b_ref[...]
    for k in range(SSD_CONV):
        acc = acc + w_ref[k:k + 1, :] * ext_ref[SUBLANES + k - SSD_CONV // 2:SUBLANES + k - SSD_CONV // 2 + t, :]
    xbc_ref[...] = _silu(acc).astype(xbc_ref.dtype)
    lane = lax.broadcasted_iota(jnp.int32, dt_ref.shape, 1)
    dt = _softplus(nar_ref[...].astype(F32) + dtb_ref[...])
    dt_ref[...] = jnp.where(lane < 2 * SSD_HEADS, dt, 0.0)


def _ssd_conv(proj, conv_w, conv_b, dt_bias_row, geo):
    width = BRANCH_WIDTH + 2 * SSD_GROUPS * SSD_STATE
    row = lambda b, j: b * geo.tpb + j
    cb = _COL["B_X"] // width
    prev, nxt = _halo_specs(width, cb, geo, 0, row)
    return pl.pallas_call(
        functools.partial(_ssd_conv_kernel, geo=geo),
        grid=(geo.batch, geo.tpb),
        in_specs=[pl.BlockSpec((ROW_TILE, width), lambda b, j: (row(b, j), cb)), prev, nxt,
                  pl.BlockSpec((ROW_TILE, LANES), lambda b, j: (row(b, j), _COL["NARROW"] // LANES)),
                  pl.BlockSpec((SSD_CONV, width), lambda b, j: (0, 0)),
                  pl.BlockSpec((1, width), lambda b, j: (0, 0)),
                  pl.BlockSpec((1, LANES), lambda b, j: (0, 0))],
        out_specs=[pl.BlockSpec((ROW_TILE, width), lambda b, j: (row(b, j), 0)),
                   pl.BlockSpec((ROW_TILE, LANES), lambda b, j: (row(b, j), 0))],
        out_shape=[jax.ShapeDtypeStruct((geo.rows, width), BF16),
                   jax.ShapeDtypeStruct((geo.rows, LANES), F32)],
        scratch_shapes=[pltpu.VMEM((ROW_TILE + 2 * SUBLANES, width), F32)],
        compiler_params=_cparams(("parallel", "parallel")),
        name="ssd_conv",
    )(proj, proj, proj, proj, conv_w, conv_b, dt_bias_row)


def _scan_chunk(s, n_chunks, n_ctx_chunks, reverse):
    if not reverse:
        return s
    return jnp.where(s < n_ctx_chunks, n_ctx_chunks - 1 - s, n_chunks + n_ctx_chunks - 1 - s)


def _tri(n, reverse):
    t = lax.broadcasted_iota(jnp.int32, (n, n), 0)
    s = lax.broadcasted_iota(jnp.int32, (n, n), 1)
    return (s >= t) if reverse else (s <= t)


def _split_bf16(x, pieces):
    out, rest = [], x
    for _ in range(pieces):
        p = rest.astype(BF16)
        out.append(p)
        rest = rest - p.astype(F32)
    return out


def _cumsum_rows(mask_bf16, x):
    w = x.shape[1]
    parts = jnp.dot(mask_bf16, jnp.concatenate(_split_bf16(x, 3), axis=1), preferred_element_type=F32)
    return parts[:, 0:w] + parts[:, w:2 * w] + parts[:, 2 * w:3 * w]


def _ssd_scan_kernel(xbc_ref, dt_ref, alog_ref, e_ref, *rest, reverse):
    if reverse:
        yf_ref, z_ref, skip_ref, g_ref, y_ref, h_ref = rest
    else:
        y_ref, h_ref = rest
    q = SSD_CHUNK
    gw = SSD_HPG * SSD_HEAD_DIM
    quad = 4
    qw = quad * SSD_HEAD_DIM
    lane0 = SSD_HEADS if reverse else 0
    n_sub = xbc_ref.shape[0] // q

    @pl.when(pl.program_id(1) == 0)
    def _():
        h_ref[...] = jnp.zeros_like(h_ref)

    lane = lax.broadcasted_iota(jnp.int32, (1, LANES), 1)
    a = jnp.where(lane < 2 * SSD_HEADS, -jnp.exp(alog_ref[...]), 0.0)
    mask = _tri(q, reverse)
    mask_bf = jnp.where(mask, 1.0, 0.0).astype(BF16)
    e = e_ref[...]
    head_of_lane = lax.broadcasted_iota(jnp.int32, (1, qw), 1) // SSD_HEAD_DIM
    zero_bf = jnp.zeros((), BF16)

    for ci in (range(n_sub - 1, -1, -1) if reverse else range(n_sub)):
        rows = slice(ci * q, (ci + 1) * q)
        dt = dt_ref[rows, :]
        cs = _cumsum_rows(mask_bf, dt * a)
        cs_last = cs[0:1, :] if reverse else cs[q - 1:q, :]
        ecs_hi, ecs_lo = _split_bf16(jnp.exp(cs), 2)
        cd = _split_bf16(jnp.broadcast_to(jnp.exp(cs_last), (BF16_SUBLANES, LANES)), 3)
        stack = jnp.concatenate([dt.astype(BF16), (jnp.exp(cs_last - cs) * dt).astype(BF16), ecs_hi, ecs_lo, *cd], axis=0)
        big = jnp.dot(stack, e, preferred_element_type=F32)
        x = xbc_ref[rows, 0:BRANCH_WIDTH].astype(F32)
        dtx = (big[0:q] * x).astype(BF16)
        wx = (big[q:2 * q] * x).astype(BF16)
        ecs = big[2 * q:3 * q] + big[3 * q:4 * q]
        r0 = 4 * q
        chunk_decay = (big[r0:r0 + 1] + big[r0 + BF16_SUBLANES:r0 + BF16_SUBLANES + 1]
                       + big[r0 + 2 * BF16_SUBLANES:r0 + 2 * BF16_SUBLANES + 1])
        cs_t = cs.T
        pieces = []
        for g in range(SSD_GROUPS):
            bm = xbc_ref[rows, BRANCH_WIDTH + g * SSD_STATE:BRANCH_WIDTH + (g + 1) * SSD_STATE]
            c0 = BRANCH_WIDTH + SSD_GROUPS * SSD_STATE + g * SSD_STATE
            cm = xbc_ref[rows, c0:c0 + SSD_STATE]
            cb = lax.dot_general(cm, bm, (((1,), (1,)), ((), ())), preferred_element_type=F32)
            h_in = h_ref[g]
            y_off = jnp.dot(cm, h_in.astype(BF16), preferred_element_type=F32) * ecs[:, g * gw:(g + 1) * gw]
            states = jnp.dot(bm.astype(F32).T.astype(BF16), wx[:, g * gw:(g + 1) * gw], preferred_element_type=F32)
            h_ref[g] = chunk_decay[:, g * gw:(g + 1) * gw] * h_in + states
            for hq in range(SSD_HPG // quad):
                ms = []
                for r in range(quad):
                    col = lane0 + g * SSD_HPG + hq * quad + r
                    seg = cs[:, col:col + 1] - cs_t[col:col + 1, :]
                    ms.append((cb * jnp.exp(jnp.where(mask, seg, -1e30))).astype(BF16))
                lo = g * gw + hq * qw
                slab = dtx[:, lo:lo + qw]
                rhs = jnp.concatenate([jnp.where(head_of_lane == r, slab, zero_bf) for r in range(quad)], axis=0)
                y_diag = jnp.dot(jnp.concatenate(ms, axis=1), rhs, preferred_element_type=F32)
                pieces.append(y_diag + y_off[:, hq * qw:(hq + 1) * qw])
        y = jnp.concatenate(pieces, axis=1)
        if reverse:
            y = (yf_ref[rows, :] + y + skip_ref[...] * x) * _silu(z_ref[rows, :].astype(F32))
            y = y * lax.rsqrt(jnp.mean(y * y, axis=-1, keepdims=True) + NORM_EPS) * g_ref[...]
        y_ref[rows, :] = y.astype(y_ref.dtype)


def _ssd_scan(xbc, dt, a_log_row, expand_mat, geo, reverse, finish=None):
    nb, nbc = geo.lt // SCAN_BLOCK, geo.ctx_len // SCAN_BLOCK
    width, w = xbc.shape[1], BRANCH_WIDTH
    row = lambda b, s: b * nb + _scan_chunk(s, nb, nbc, reverse)
    in_specs = [pl.BlockSpec((SCAN_BLOCK, width), lambda b, s: (row(b, s), 0)),
                pl.BlockSpec((SCAN_BLOCK, LANES), lambda b, s: (row(b, s), 0)),
                pl.BlockSpec((1, LANES), lambda b, s: (0, 0)),
                pl.BlockSpec((LANES, w), lambda b, s: (0, 0))]
    args = [xbc, dt, a_log_row, expand_mat]
    if reverse:
        in_specs += [pl.BlockSpec((SCAN_BLOCK, w), lambda b, s: (row(b, s), 0)),
                     pl.BlockSpec((SCAN_BLOCK, w), lambda b, s: (row(b, s), _COL["B_Z"] // w)),
                     pl.BlockSpec((1, w), lambda b, s: (0, 0)),
                     pl.BlockSpec((1, w), lambda b, s: (0, 0))]
        args += list(finish)
    return pl.pallas_call(
        functools.partial(_ssd_scan_kernel, reverse=reverse),
        grid=(geo.batch, nb),
        in_specs=in_specs,
        out_specs=pl.BlockSpec((SCAN_BLOCK, w), lambda b, s: (row(b, s), 0)),
        out_shape=jax.ShapeDtypeStruct((geo.rows, w), BF16 if reverse else F32),
        scratch_shapes=[pltpu.VMEM((SSD_GROUPS, SSD_STATE, SSD_HPG * SSD_HEAD_DIM), F32)],
        compiler_params=_cparams(("parallel", "arbitrary")),
        name="ssd_scan_bwd" if reverse else "ssd_scan_fwd",
    )(*args)


def _gla_scan_kernel(q_ref, k_ref, v_ref, nar_ref, w2_ref, bf_ref, *rest, reverse):
    if reverse:
        of_ref, gate_ref, g_ref, o_ref, h_ref = rest
    else:
        o_ref, h_ref = rest
    n = GLA_CHUNK

    @pl.when(pl.program_id(1) == 0)
    def _():
        h_ref[...] = jnp.zeros_like(h_ref)

    t = q_ref.shape[0]
    n_sub = t // n
    ri = lax.broadcasted_iota(jnp.int32, (t, t), 0)
    ci = lax.broadcasted_iota(jnp.int32, (t, t), 1)
    mask = jnp.logical_and(ri // n == ci // n, (ci >= ri) if reverse else (ci <= ri))
    logit = jnp.dot(nar_ref[...], w2_ref[...], preferred_element_type=F32) + bf_ref[...]
    gl = _log_sigmoid(logit) / GLA_TAU
    b = _cumsum_rows(jnp.where(mask, 1.0, 0.0).astype(BF16), gl)
    last = [b[c * n:c * n + 1, :] if reverse else b[(c + 1) * n - 1:(c + 1) * n, :] for c in range(n_sub)]
    b_last = jnp.concatenate([jnp.broadcast_to(r, (n, r.shape[1])) for r in last], axis=0)
    q = q_ref[...].astype(F32) * (GLA_DK ** -0.5)
    k = k_ref[...].astype(F32)
    qe = (q * jnp.exp(b)).astype(BF16)
    ke = (k * jnp.exp(-b)).astype(BF16)
    kd = (k * jnp.exp(b_last - b)).astype(BF16)
    for h in range(GLA_HEADS):
        ks = slice(h * GLA_DK, (h + 1) * GLA_DK)
        vs = slice(h * GLA_DV, (h + 1) * GLA_DV)
        v = v_ref[:, vs]
        att = lax.dot_general(qe[:, ks], ke[:, ks], (((1,), (1,)), ((), ())), preferred_element_type=F32)
        o_intra = jnp.dot(jnp.where(mask, att, 0.0).astype(BF16), v, preferred_element_type=F32)
        o_inter = [None] * n_sub
        for c in (range(n_sub - 1, -1, -1) if reverse else range(n_sub)):
            rows = slice(c * n, (c + 1) * n)
            h_in = h_ref[h]
            o_inter[c] = lax.dot_general(qe[rows, ks], h_in.astype(BF16), (((1,), (1,)), ((), ())),
                                         preferred_element_type=F32)
            upd = jnp.dot(v[rows].astype(F32).T.astype(BF16), kd[rows, ks], preferred_element_type=F32)
            h_ref[h] = jnp.exp(last[c][:, ks]) * h_in + upd
        o = o_intra + jnp.concatenate(o_inter, axis=0)
        if reverse:
            o = o + of_ref[:, vs]
            o = o * lax.rsqrt(jnp.mean(o * o, axis=-1, keepdims=True) + NORM_EPS) * g_ref[...]
            o = o * _silu(gate_ref[:, vs].astype(F32))
        o_ref[:, vs] = o.astype(o_ref.dtype)


def _gla_scan(proj, w2, b_f, geo, reverse, finish=None):
    nb, nbc = geo.lt // SCAN_BLOCK, geo.ctx_len // SCAN_BLOCK
    row = lambda b, s: b * nb + _scan_chunk(s, nb, nbc, reverse)
    kw, vw = GLA_HEADS * GLA_DK, GLA_HEADS * GLA_DV
    in_specs = [pl.BlockSpec((SCAN_BLOCK, kw), lambda b, s: (row(b, s), _COL["C_Q"] // kw)),
                pl.BlockSpec((SCAN_BLOCK, kw), lambda b, s: (row(b, s), _COL["C_K"] // kw)),
                pl.BlockSpec((SCAN_BLOCK, vw), lambda b, s: (row(b, s), _COL["C_V"] // vw)),
                pl.BlockSpec((SCAN_BLOCK, LANES), lambda b, s: (row(b, s), _COL["NARROW"] // LANES)),
                pl.BlockSpec((LANES, kw), lambda b, s: (0, 0)),
                pl.BlockSpec((1, kw), lambda b, s: (0, 0))]
    args = [proj, proj, proj, proj, w2, b_f]
    if reverse:
        of, norm_g = finish
        in_specs += [pl.BlockSpec((SCAN_BLOCK, vw), lambda b, s: (row(b, s), 0)),
                     pl.BlockSpec((SCAN_BLOCK, vw), lambda b, s: (row(b, s), _COL["C_G"] // vw)),
                     pl.BlockSpec((1, GLA_DV), lambda b, s: (0, 0))]
        args += [of, proj, norm_g]
    return pl.pallas_call(
        functools.partial(_gla_scan_kernel, reverse=reverse),
        grid=(geo.batch, nb),
        in_specs=in_specs,
        out_specs=pl.BlockSpec((SCAN_BLOCK, vw), lambda b, s: (row(b, s), 0)),
        out_shape=jax.ShapeDtypeStruct((geo.rows, vw), BF16 if reverse else F32),
        scratch_shapes=[pltpu.VMEM((GLA_HEADS, GLA_DV, GLA_DK), F32)],
        compiler_params=_cparams(("parallel", "arbitrary")),
        name="gla_scan_bwd" if reverse else "gla_scan_fwd",
    )(*args)


def _shortconv_kernel(cur_ref, prev_ref, next_ref, w_ref, o_ref, ext_ref, *, geo, first):
    j = pl.program_id(1) + first
    w = BRANCH_WIDTH
    u = lambda ref: ref[:, w:2 * w].astype(F32) * ref[:, 2 * w:3 * w].astype(F32)
    _fill_ext(ext_ref, u(cur_ref), u(prev_ref), u(next_ref), j, geo)
    t = cur_ref.shape[0]
    acc = jnp.zeros((t, w), F32)
    for k in range(SC_CONV):
        acc = acc + w_ref[k:k + 1, :] * ext_ref[SUBLANES + k - SC_CONV // 2:SUBLANES + k - SC_CONV // 2 + t, :]
    y = cur_ref[:, 0:w].astype(F32) * acc * _silu(cur_ref[:, 3 * w:4 * w].astype(F32))
    o_ref[...] = y.astype(o_ref.dtype)


def _shortconv(proj, conv_w, geo, with_ctx):
    first, nt = geo.span(with_ctx)
    row = lambda b, j: b * geo.tpb + first + j
    width = 4 * BRANCH_WIDTH
    cb = _COL["D_ALL"] // width
    prev, nxt = _halo_specs(width, cb, geo, first, row)
    return pl.pallas_call(
        functools.partial(_shortconv_kernel, geo=geo, first=first),
        grid=(geo.batch, nt),
        in_specs=[pl.BlockSpec((ROW_TILE, width), lambda b, j: (row(b, j), cb)), prev, nxt,
                  pl.BlockSpec((SC_CONV, BRANCH_WIDTH), lambda b, j: (0, 0))],
        out_specs=pl.BlockSpec((ROW_TILE, BRANCH_WIDTH), lambda b, j: (row(b, j), 0)),
        out_shape=jax.ShapeDtypeStruct((geo.rows, BRANCH_WIDTH), BF16),
        scratch_shapes=[pltpu.VMEM((ROW_TILE + 2 * SUBLANES, BRANCH_WIDTH), F32)],
        compiler_params=_cparams(("parallel", "parallel")),
        name="shortconv",
    )(proj, proj, proj, conv_w)


def _merge_out_kernel(*refs, n_src, geo, first, with_next):
    ys = refs[0:N_BRANCH]
    gates = refs[N_BRANCH:2 * N_BRANCH]
    wb_ref, wo_ref = refs[2 * N_BRANCH:2 * N_BRANCH + 2]
    p = 2 * N_BRANCH + 2
    srcs = refs[p:p + n_src]
    gt_ref, gp_ref = refs[p + n_src:p + n_src + 2]
    rest = refs[p + n_src + 2:]
    m = None
    for n, (y_ref, gate_ref) in enumerate(zip(ys, gates)):
        term = jax.nn.sigmoid(gate_ref[...].astype(F32)) * jnp.dot(y_ref[...], wb_ref[n], preferred_element_type=F32)
        m = term if m is None else m + term
    out = jnp.dot(m.astype(BF16), wo_ref[...], preferred_element_type=F32)
    y = out * lax.rsqrt(jnp.mean(out * out, axis=-1, keepdims=True) + NORM_EPS) * gp_ref[...]
    x_new = _read_tokens(srcs, first + pl.program_id(1), geo) + gt_ref[0] * y
    if with_next:
        gn_ref, shn_ref, scn_ref, o_ref, h_ref = rest
        h_ref[...] = _adaln_prenorm(x_new, gn_ref[...], shn_ref[0], scn_ref[0]).astype(h_ref.dtype)
    else:
        (o_ref,) = rest
    o_ref[...] = x_new


def _merge_out(ys, proj, w_branch, w_out, tokens, mod3, g_post, geo, with_ctx, nxt=None):
    first, nt = geo.span(with_ctx)
    w, d = BRANCH_WIDTH, D_MODEL
    row = lambda b, j: b * geo.tpb + first + j
    out_rows = geo.rows if with_ctx else geo.batch * geo.seq
    out_row = row if with_ctx else (lambda b, j: b * nt + j)
    y_spec = pl.BlockSpec((ROW_TILE, w), lambda b, j: (row(b, j), 0))
    gate_specs = [pl.BlockSpec((ROW_TILE, d), functools.partial(lambda b, j, n: (row(b, j), _COL["MG"] // d + n), n=n))
                  for n in range(N_BRANCH)]
    resident = pl.Buffered(1)
    mod_spec = lambda part: pl.BlockSpec((1, 1, d), lambda b, j: (geo.mod_row(b, first + j), 0, part))
    vec_spec = pl.BlockSpec((1, d), lambda b, j: (0, 0))
    in_specs = [y_spec, y_spec, y_spec, y_spec, *gate_specs,
                pl.BlockSpec((N_BRANCH, w, d), lambda b, j: (0, 0, 0), pipeline_mode=resident),
                pl.BlockSpec((d, d), lambda b, j: (0, 0), pipeline_mode=resident),
                *_token_specs(tokens, geo, first), mod_spec(2), vec_spec]
    args = [*ys, proj, proj, proj, proj, w_branch, w_out, *tokens, mod3, g_post]
    out_specs = [pl.BlockSpec((ROW_TILE, d), lambda b, j: (out_row(b, j), 0))]
    out_shape = [jax.ShapeDtypeStruct((out_rows, d), F32)]
    if nxt is not None:
        assert with_ctx
        g_next, mod3_next = nxt
        in_specs += [vec_spec, mod_spec(0), mod_spec(1)]
        args += [g_next, mod3_next, mod3_next]
        out_specs.append(pl.BlockSpec((ROW_TILE, d), lambda b, j: (row(b, j), 0)))
        out_shape.append(jax.ShapeDtypeStruct((geo.rows, d), BF16))
    outs = pl.pallas_call(
        functools.partial(_merge_out_kernel, n_src=len(tokens), geo=geo, first=first, with_next=nxt is not None),
        grid=(geo.batch, nt),
        in_specs=in_specs,
        out_specs=out_specs,
        out_shape=out_shape,
        compiler_params=_cparams(("parallel", "parallel")),
        name="merge_out",
    )(*args)
    return outs if nxt is not None else (outs[0], None)


def _regroup_plan():
    bw = BRANCH_WIDTH
    names = ("a_q", "a_k", "a_v", "a_g", "b_x", "b_z", "b_b", "b_c", "b_dt", "c_q", "c_k", "c_v", "c_g", "c_f",
             "d_all", "mg")
    widths = (ATT_HEADS * HEAD_DIM, ATT_KV_HEADS * HEAD_DIM, ATT_KV_HEADS * HEAD_DIM, bw,
              bw, bw, SSD_GROUPS * SSD_STATE, SSD_GROUPS * SSD_STATE, 2 * SSD_HEADS,
              GLA_HEADS * GLA_DK, GLA_HEADS * GLA_DK, GLA_HEADS * GLA_DV, bw, 2 * GLA_RANK,
              4 * bw, N_BRANCH * D_MODEL)
    src = dict(zip(names, np.concatenate([[0], np.cumsum(widths)[:-1]]).tolist()))
    wid = dict(zip(names, widths))
    dst = dict(a_q=_COL["A_Q"], a_g=_COL["A_G"], b_z=_COL["B_Z"], b_x=_COL["B_X"], b_b=_COL["B_B"], b_c=_COL["B_C"],
               a_k=_COL["A_K"], a_v=_COL["A_V"], c_v=_COL["C_V"], c_g=_COL["C_G"], c_q=_COL["C_Q"], c_k=_COL["C_K"],
               d_all=_COL["D_ALL"], mg=_COL["MG"], b_dt=_COL["NARROW"] + DT_LANE0, c_f=_COL["NARROW"] + F1_LANE0)
    return [(dst[n], src[n], wid[n], n in ("a_q", "a_k")) for n in names], sum(widths)


def _regroup_kernel(w_ref, o_ref):
    plan, _ = _regroup_plan()
    half = HEAD_DIM // 2
    for dst, src, width, deinterleave in plan:
        if deinterleave:
            for h in range(width // HEAD_DIM):
                s, d = src + h * HEAD_DIM, dst + h * HEAD_DIM
                o_ref[d:d + half, :] = w_ref[pl.ds(s, half, stride=2), :].astype(o_ref.dtype)
                o_ref[d + half:d + HEAD_DIM, :] = w_ref[pl.ds(s + 1, half, stride=2), :].astype(o_ref.dtype)
        else:
            o_ref[dst:dst + width, :] = w_ref[src:src + width, :].astype(o_ref.dtype)
    used = _COL["NARROW"] + F1_LANE0 + 2 * GLA_RANK
    o_ref[used:N_PROJ, :] = jnp.zeros((N_PROJ - used, o_ref.shape[1]), o_ref.dtype)


def _regroup_w_in(w_in, layer):
    w_t = jnp.swapaxes(w_in, 1, 2)
    _, n, k = w_t.shape
    assert n == _regroup_plan()[1]
    tk = LANES
    return pl.pallas_call(
        _regroup_kernel,
        grid=(k // tk,),
        in_specs=[pl.BlockSpec((None, n, tk), lambda i: (layer, 0, i))],
        out_specs=pl.BlockSpec((N_PROJ, tk), lambda i: (0, i)),
        out_shape=jax.ShapeDtypeStruct((N_PROJ, k), BF16),
        compiler_params=_cparams(("parallel",)),
        name="regroup_w_in",
    )(w_t)


def _deinterleave_vec(g):
    return g.reshape(HEAD_DIM // 2, 2).T.reshape(1, HEAD_DIM)


def _pad_lanes(v, lane0=0):
    return jnp.zeros((1, LANES), F32).at[0, lane0:lane0 + v.shape[0]].set(v.astype(F32))


def _head_expand_matrix(reverse):
    e = np.zeros((LANES, BRANCH_WIDTH), np.float32)
    lane0 = SSD_HEADS if reverse else 0
    for r in range(SSD_HEADS):
        e[lane0 + r, r * SSD_HEAD_DIM:(r + 1) * SSD_HEAD_DIM] = 1.0
    return jnp.asarray(e, BF16)


def _forget_weight(w_f2_dir, direction):
    lane0 = F1_LANE0 + direction * GLA_RANK
    return jnp.zeros((LANES, w_f2_dir.shape[1]), F32).at[lane0:lane0 + GLA_RANK].set(w_f2_dir).astype(BF16)


def kernel(x, c, ctx, c_ctx, w_mod, b_mod, g_pre, g_post, w_in, g_q, g_k, ssd_conv_w, ssd_conv_b,
           ssd_a_log, ssd_dt_bias, ssd_d, ssd_norm_g, gla_w_f2, gla_b_f, gla_norm_g, sc_conv_w,
           w_branch, w_out):
    batch, seq, d = x.shape
    ctx_len = ctx.shape[1]
    depth = w_in.shape[0]
    geo = _Geom(batch, ctx_len, seq)
    assert d == D_MODEL and batch + 1 <= SUBLANES and seq % GRID_W == 0

    cos_t, sin_t = _rope_tables(geo)
    c_rows = jnp.zeros((SUBLANES, d), F32).at[:batch].set(c).at[batch].set(c_ctx)
    tokens = (ctx.reshape(batch * ctx_len, d), x.reshape(batch * seq, d))
    e_fwd, e_bwd = _head_expand_matrix(False), _head_expand_matrix(True)
    mods = [_modulation(c_rows, w_mod, b_mod[l][None, :], l).reshape(SUBLANES, 1, 3 * d) for l in range(depth)]
    h = _prenorm(tokens, g_pre[0][None, :], mods[0], geo)

    for l in range(depth):
        need_ctx = l < depth - 1
        mod3 = mods[l]
        proj = _matmul(h, _regroup_w_in(w_in, l), PROJ_TN)

        qt, kh, vt = _qk_prep(proj, cos_t, sin_t, _deinterleave_vec(g_q[l]), _deinterleave_vec(g_k[l]), geo)
        ya = _attention(qt, kh, vt, proj, None, geo, ctx_len, seq, geo.lt)
        if need_ctx:
            ya = _attention(qt, kh, vt, proj, ya, geo, 0, ctx_len, ctx_len)

        xbc, dt = _ssd_conv(proj, ssd_conv_w[l], ssd_conv_b[l][None, :], _pad_lanes(ssd_dt_bias[l].reshape(-1)), geo)
        a_log_row = _pad_lanes(ssd_a_log[l].reshape(-1))
        ysf = _ssd_scan(xbc, dt, a_log_row, e_fwd, geo, False)
        skip_row = jnp.repeat(ssd_d[l], SSD_HEAD_DIM)[None, :]
        yb = _ssd_scan(xbc, dt, a_log_row, e_bwd, geo, True, (ysf, proj, skip_row, ssd_norm_g[l][None, :]))

        ogf = _gla_scan(proj, _forget_weight(gla_w_f2[l, 0], 0), gla_b_f[l, 0][None, :], geo, False)
        yg = _gla_scan(proj, _forget_weight(gla_w_f2[l, 1], 1), gla_b_f[l, 1][None, :], geo, True,
                       (ogf, gla_norm_g[l][None, :]))

        yd = _shortconv(proj, sc_conv_w[l], geo, need_ctx)

        nxt = (g_pre[l + 1][None, :], mods[l + 1]) if need_ctx else None
        x_new, h = _merge_out((ya, yb, yg, yd), proj, w_branch[l].astype(BF16), w_out[l].astype(BF16), tokens, mod3,
                              g_post[l][None, :], geo, need_ctx, nxt)
        tokens = (x_new,)

    return x_new.reshape(batch, seq, d)
```

```python
import functools

import numpy as np
import jax
import jax.numpy as jnp
from jax import lax
from jax.experimental import pallas as pl
from jax.experimental.pallas import tpu as pltpu

F32 = jnp.float32
BF16 = jnp.bfloat16

D_MODEL = 2048
GRID_W = 64
BRANCH_WIDTH = D_MODEL // 2
N_BRANCH = 4
NORM_EPS = 1e-6
HEAD_DIM = 128
ATT_HEADS = BRANCH_WIDTH // HEAD_DIM
ATT_KV_HEADS = ATT_HEADS // 4
ATT_REP = ATT_HEADS // ATT_KV_HEADS
ROPE_THETA = 10000.0
SSD_HEAD_DIM = 64
SSD_HEADS = BRANCH_WIDTH // SSD_HEAD_DIM
SSD_GROUPS = 2
SSD_HPG = SSD_HEADS // SSD_GROUPS
SSD_STATE = 128
SSD_CONV = 5
SSD_CHUNK = 128
GLA_HEADS = 4
GLA_DV = BRANCH_WIDTH // GLA_HEADS
GLA_DK = GLA_DV // 2
GLA_RANK = 16
GLA_TAU = 16.0
GLA_CHUNK = 64
SC_CONV = 3

LANES = 128
SUBLANES = 8
ROW_TILE = 256
SCAN_BLOCK = ROW_TILE
VMEM_LIMIT = 56 * 1024 * 1024

_COL = dict(
    A_Q=0, A_G=1024, B_Z=2048, B_X=3072, B_B=4096, B_C=4352, A_K=4608, A_V=4864,
    C_V=5120, C_G=6144, C_Q=7168, C_K=7680,
    D_ALL=8192, MG=12288, NARROW=20480,
)
MXU_WIDTH = 256
N_PROJ = 20736
PROJ_TN = 2304
ATTN_UNROLL = 2
ATTN_STREAMS = 2
ATTN_KEY_CHUNKS = (ROW_TILE, 2 * ROW_TILE, 3 * ROW_TILE)
BF16_SUBLANES = 16
VT_ROWS = HEAD_DIM + BF16_SUBLANES
Q_SCALE = HEAD_DIM ** -0.5 * float(np.log2(np.e))
DT_LANE0 = 0
F1_LANE0 = 32


def _cparams(sem, vmem=VMEM_LIMIT):
    return pltpu.CompilerParams(dimension_semantics=sem, vmem_limit_bytes=vmem)


def _silu(x):
    return x * jax.nn.sigmoid(x)


def _softplus(x):
    return jnp.maximum(x, 0.0) + jnp.log1p(jnp.exp(-jnp.abs(x)))


def _log_sigmoid(x):
    return jnp.minimum(x, 0.0) - jnp.log1p(jnp.exp(-jnp.abs(x)))


def _mod_kernel(c_ref, w_ref, b_ref, o_ref):
    a = _silu(c_ref[...]).astype(BF16)
    o_ref[...] = jnp.dot(a, w_ref[...].astype(BF16), preferred_element_type=F32) + b_ref[...]


def _modulation(c_rows, w_mod, b_mod, layer):
    _, d, n = w_mod.shape
    tn = 1024
    return pl.pallas_call(
        _mod_kernel,
        grid=(n // tn,),
        in_specs=[pl.BlockSpec((SUBLANES, d), lambda j: (0, 0)),
                  pl.BlockSpec((None, d, tn), lambda j: (layer, 0, j)),
                  pl.BlockSpec((1, tn), lambda j: (0, j))],
        out_specs=pl.BlockSpec((SUBLANES, tn), lambda j: (0, j)),
        out_shape=jax.ShapeDtypeStruct((SUBLANES, n), F32),
        compiler_params=_cparams(("parallel",)),
        name="modulation",
    )(c_rows, w_mod, b_mod)


class _Geom:
    def __init__(self, batch, ctx_len, seq):
        self.batch, self.ctx_len, self.seq = batch, ctx_len, seq
        self.lt = ctx_len + seq
        self.rows = batch * self.lt
        assert ctx_len % ROW_TILE == 0 and seq % ROW_TILE == 0
        self.tpb = self.lt // ROW_TILE
        self.ctx_tiles = ctx_len // ROW_TILE
        self.lat_tiles = seq // ROW_TILE

    def span(self, with_ctx):
        return (0, self.tpb) if with_ctx else (self.ctx_tiles, self.lat_tiles)

    def mod_row(self, b, j):
        return jnp.where(j < self.ctx_tiles, self.batch, b)


def _token_specs(tokens, geo, first):
    c = tokens[0].shape[1]
    ct, lt = geo.ctx_tiles, geo.lat_tiles
    if len(tokens) == 1 and tokens[0].shape[0] == geo.rows:
        return [pl.BlockSpec((ROW_TILE, c), lambda b, j: (b * geo.tpb + first + j, 0))]
    if len(tokens) == 1:
        assert tokens[0].shape[0] == geo.batch * geo.seq and first >= ct
        return [pl.BlockSpec((ROW_TILE, c), lambda b, j: (b * lt + first + j - ct, 0))]
    return [pl.BlockSpec((ROW_TILE, c), lambda b, j: (b * ct + jnp.minimum(first + j, ct - 1), 0),
                         pipeline_mode=pl.Buffered(1)),
            pl.BlockSpec((ROW_TILE, c), lambda b, j: (b * lt + jnp.maximum(first + j - ct, 0), 0))]


def _read_tokens(refs, tile, geo):
    if len(refs) == 1:
        return refs[0][...]
    return jnp.where(tile < geo.ctx_tiles, refs[0][...], refs[1][...])


def _adaln_prenorm(x, g, sh, sc):
    y = x * lax.rsqrt(jnp.mean(x * x, axis=-1, keepdims=True) + NORM_EPS)
    return (y * g) * (1.0 + sc) + sh


def _prenorm_kernel(*refs, n_src, geo):
    g_ref, sh_ref, sc_ref, o_ref = refs[n_src:]
    x = _read_tokens(refs[:n_src], pl.program_id(1), geo)
    o_ref[...] = _adaln_prenorm(x, g_ref[...], sh_ref[0], sc_ref[0]).astype(o_ref.dtype)


def _prenorm(tokens, g_pre, mod3, geo):
    d = D_MODEL
    return pl.pallas_call(
        functools.partial(_prenorm_kernel, n_src=len(tokens), geo=geo),
        grid=(geo.batch, geo.tpb),
        in_specs=[*_token_specs(tokens, geo, 0),
                  pl.BlockSpec((1, d), lambda b, j: (0, 0)),
                  pl.BlockSpec((1, 1, d), lambda b, j: (geo.mod_row(b, j), 0, 0)),
                  pl.BlockSpec((1, 1, d), lambda b, j: (geo.mod_row(b, j), 0, 1))],
        out_specs=pl.BlockSpec((ROW_TILE, d), lambda b, j: (b * geo.tpb + j, 0)),
        out_shape=jax.ShapeDtypeStruct((geo.rows, d), BF16),
        compiler_params=_cparams(("parallel", "parallel")),
        name="prenorm",
    )(*tokens, g_pre, mod3, mod3)


def _matmul_kernel(a_ref, w_ref, o_ref):
    o_ref[...] = lax.dot_general(a_ref[...], w_ref[...], (((1,), (1,)), ((), ())),
                                 preferred_element_type=F32).astype(o_ref.dtype)


def _matmul(a, w_t, tn, out_dtype=BF16):
    m, k = a.shape
    n = w_t.shape[0]
    tm = next(t for t in (768, 512, 256) if m % t == 0)
    assert m % tm == 0 and n % tn == 0
    return pl.pallas_call(
        _matmul_kernel,
        grid=(n // tn, m // tm),
        in_specs=[pl.BlockSpec((tm, k), lambda j, i: (i, 0)),
                  pl.BlockSpec((tn, k), lambda j, i: (j, 0))],
        out_specs=pl.BlockSpec((tm, tn), lambda j, i: (i, j)),
        out_shape=jax.ShapeDtypeStruct((m, n), out_dtype),
        compiler_params=_cparams(("parallel", "parallel")),
        name="in_proj",
    )(a, w_t)


def _rope_tables(geo):
    rows = geo.seq // GRID_W
    t_row = jnp.repeat(jnp.arange(rows, dtype=F32), GRID_W)
    t_col = jnp.tile(jnp.arange(GRID_W, dtype=F32), rows)
    half = HEAD_DIM // 2
    freqs = ROPE_THETA ** (-(jnp.arange(0, half, 2, dtype=F32) / half))
    ang = jnp.concatenate([t_row[:, None] * freqs, t_col[:, None] * freqs], axis=-1)
    cos, sin = jnp.cos(ang), jnp.sin(ang)
    cos_l = jnp.concatenate([cos, cos], axis=-1)
    sin_l = jnp.concatenate([-sin, sin], axis=-1)
    cos_c = jnp.ones((geo.ctx_len, HEAD_DIM), F32)
    sin_c = jnp.zeros((geo.ctx_len, HEAD_DIM), F32)
    return jnp.concatenate([cos_c, cos_l], axis=0), jnp.concatenate([sin_c, sin_l], axis=0)


def _qk_prep_kernel(q_ref, k_ref, v_ref, cos_ref, sin_ref, gq_ref, gk_ref, qt_ref, ko_ref, vt_ref):
    cos, sin = cos_ref[...], sin_ref[...]
    t = q_ref.shape[0]

    ones = jnp.ones((HEAD_DIM, HEAD_DIM), BF16)

    def norm_rope(x, g):
        x = x.astype(F32)
        hi, lo = _split_bf16(x * x, 2)
        ss = jnp.dot(hi, ones, preferred_element_type=F32) + jnp.dot(lo, ones, preferred_element_type=F32)
        y = x * lax.rsqrt(ss * (1.0 / HEAD_DIM) + NORM_EPS) * g
        return y * cos + pltpu.roll(y, HEAD_DIM // 2, 1) * sin

    gq, gk = gq_ref[...], gk_ref[...]
    for h in range(ATT_HEADS):
        g, r = divmod(h, ATT_REP)
        y = norm_rope(q_ref[:, h * HEAD_DIM:(h + 1) * HEAD_DIM], gq) * Q_SCALE
        qt_ref[g, 0, :, r * t:(r + 1) * t] = y.astype(qt_ref.dtype).T
    for g in range(ATT_KV_HEADS):
        hs = slice(g * HEAD_DIM, (g + 1) * HEAD_DIM)
        ko_ref[g] = norm_rope(k_ref[:, hs], gk).astype(ko_ref.dtype)
        vt_ref[g, 0, 0:HEAD_DIM, :] = v_ref[:, hs].T
        vt_ref[g, 0, HEAD_DIM:VT_ROWS, :] = jnp.ones((VT_ROWS - HEAD_DIM, t), vt_ref.dtype)


def _qk_prep(proj, cos_t, sin_t, g_q, g_k, geo):
    row = lambda b, j: b * geo.tpb + j
    nq, nk = ATT_HEADS * HEAD_DIM, ATT_KV_HEADS * HEAD_DIM
    kv, t = ATT_KV_HEADS, ROW_TILE
    return pl.pallas_call(
        _qk_prep_kernel,
        grid=(geo.batch, geo.tpb),
        in_specs=[pl.BlockSpec((t, nq), lambda b, j: (row(b, j), _COL["A_Q"] // nq)),
                  pl.BlockSpec((t, nk), lambda b, j: (row(b, j), _COL["A_K"] // nk)),
                  pl.BlockSpec((t, nk), lambda b, j: (row(b, j), _COL["A_V"] // nk)),
                  pl.BlockSpec((t, HEAD_DIM), lambda b, j: (j, 0)),
                  pl.BlockSpec((t, HEAD_DIM), lambda b, j: (j, 0)),
                  pl.BlockSpec((1, HEAD_DIM), lambda b, j: (0, 0)),
                  pl.BlockSpec((1, HEAD_DIM), lambda b, j: (0, 0))],
        out_specs=[pl.BlockSpec((kv, 1, HEAD_DIM, ATT_REP * t), lambda b, j: (b, j, 0, 0)),
                   pl.BlockSpec((kv, t, HEAD_DIM), lambda b, j: (b, j, 0)),
                   pl.BlockSpec((kv, 1, VT_ROWS, t), lambda b, j: (b, j, 0, 0))],
        out_shape=[jax.ShapeDtypeStruct((geo.batch * kv, geo.tpb, HEAD_DIM, ATT_REP * t), BF16),
                   jax.ShapeDtypeStruct((geo.batch * kv, geo.lt, HEAD_DIM), BF16),
                   jax.ShapeDtypeStruct((geo.batch * kv, geo.tpb, VT_ROWS, t), BF16)],
        compiler_params=_cparams(("parallel", "parallel")),
        name="qk_prep",
    )(proj, proj, proj, cos_t, sin_t, g_q, g_k)


def _attn_kernel(*refs, n_str):
    qt_refs = refs[0:n_str]
    k_ref, vt_ref = refs[n_str:n_str + 2]
    gate_refs = refs[n_str + 2:2 * n_str + 2]
    o_ref, m_ref, acc_ref, s_ref, mx_ref = refs[2 * n_str + 2:]
    vt_tile = vt_ref.shape[2]
    tk = s_ref.shape[1]
    tiles = tk // vt_tile
    n_chunks = k_ref.shape[0] // tk
    tq = o_ref.shape[0] // n_str
    m_ref[...] = jnp.full(m_ref.shape, -jnp.inf, F32)
    acc_ref[...] = jnp.zeros(acc_ref.shape, F32)

    def scores(c, parity):
        k = k_ref[pl.ds(pl.multiple_of(c * tk, tk), tk), :]
        for st in range(n_str):
            s = jnp.dot(k, qt_refs[st][...], preferred_element_type=F32)
            s_ref[2 * st + parity] = s
            mx_ref[2 * st + parity] = jnp.max(s, axis=0, keepdims=True)

    def absorb(c, parity):
        for st in range(n_str):
            buf = 2 * st + parity
            m_old = m_ref[st]
            m_new = jnp.maximum(m_old, mx_ref[buf])
            alpha = jnp.exp2(m_old - m_new)
            pv = None
            for t in range(tiles):
                p = jnp.exp2((s_ref[buf, t * vt_tile:(t + 1) * vt_tile, :] - m_new).astype(BF16))
                part = jnp.dot(vt_ref[c * tiles + t], p, preferred_element_type=F32)
                pv = part if pv is None else pv + part
            acc_ref[st] = alpha * acc_ref[st] + pv
            m_ref[st] = m_new

    def group(i, carry):
        c0 = ATTN_UNROLL * i
        for u in range(ATTN_UNROLL):
            scores(c0 + u + 1, (u + 1) % 2)
            absorb(c0 + u, u % 2)
        return carry

    scores(0, 0)
    n_groups = (n_chunks - 1) // ATTN_UNROLL
    if n_groups > 0:
        lax.fori_loop(0, n_groups, group, 0)
    for c in range(n_groups * ATTN_UNROLL, n_chunks):
        if c + 1 < n_chunks:
            scores(c + 1, (c + 1) % 2)
        absorb(c, c % 2)
    for st in range(n_str):
        o_t = acc_ref[st, 0:HEAD_DIM, :] / acc_ref[st, HEAD_DIM:HEAD_DIM + 1, :]
        o = jnp.concatenate([o_t[:, r * tq:(r + 1) * tq].T for r in range(ATT_REP)], axis=1)
        o_ref[st * tq:(st + 1) * tq, :] = (o * _silu(gate_refs[st][...].astype(F32))).astype(o_ref.dtype)


def _attention(qt, kh, vt, proj, geo, q_first_row, q_rows, kv_rows):
    tq = ROW_TILE
    assert q_first_row % tq == 0 and q_rows % tq == 0 and kv_rows % ROW_TILE == 0
    q0, n_tiles = q_first_row // tq, q_rows // tq
    n_str = ATTN_STREAMS if n_tiles % ATTN_STREAMS == 0 else 1
    gw = ATT_REP * HEAD_DIM
    mq = ATT_REP * tq
    tk = max(t for t in ATTN_KEY_CHUNKS if kv_rows % t == 0)
    head = lambda b, g: b * ATT_KV_HEADS + g
    tile = lambda i, st: q0 + n_str * i + st
    stream_specs = lambda make: [make(st) for st in range(n_str)]
    in_specs = [
        *stream_specs(lambda st: pl.BlockSpec((None, None, HEAD_DIM, mq), lambda b, g, i: (head(b, g), tile(i, st), 0, 0))),
        pl.BlockSpec((None, kv_rows, HEAD_DIM), lambda b, g, i: (head(b, g), 0, 0)),
        pl.BlockSpec((None, kv_rows // ROW_TILE, VT_ROWS, ROW_TILE), lambda b, g, i: (head(b, g), 0, 0, 0)),
        *stream_specs(lambda st: pl.BlockSpec((tq, gw), lambda b, g, i: (b * geo.tpb + tile(i, st), _COL["A_G"] // gw + g))),
    ]
    return pl.pallas_call(
        functools.partial(_attn_kernel, n_str=n_str),
        grid=(geo.batch, ATT_KV_HEADS, n_tiles // n_str),
        in_specs=in_specs,
        out_specs=pl.BlockSpec((n_str * tq, gw), lambda b, g, i: (b * (n_tiles // n_str) + i, g)),
        out_shape=jax.ShapeDtypeStruct((geo.batch * q_rows, BRANCH_WIDTH), BF16),
        scratch_shapes=[pltpu.VMEM((n_str, 1, mq), F32), pltpu.VMEM((n_str, VT_ROWS, mq), F32),
                        pltpu.VMEM((2 * n_str, tk, mq), F32), pltpu.VMEM((2 * n_str, 1, mq), F32)],
        compiler_params=_cparams(("parallel", "parallel", "arbitrary")),
        name="attention",
    )(*([qt] * n_str), kh, vt, *([proj] * n_str))


def _halo_specs(width, col_block, geo, first, row):
    per = ROW_TILE // SUBLANES
    last = geo.rows // SUBLANES - 1
    prev = pl.BlockSpec((SUBLANES, width), lambda b, j: (jnp.maximum(row(b, j) * per - 1, 0), col_block))
    nxt = pl.BlockSpec((SUBLANES, width), lambda b, j: (jnp.minimum((row(b, j) + 1) * per, last), col_block))
    return prev, nxt


def _fill_ext(ext_ref, cur, prev, nxt, j, geo):
    t = cur.shape[0]
    seg_first = jnp.logical_or(j == 0, j == geo.ctx_tiles)
    seg_last = jnp.logical_or(j == geo.ctx_tiles - 1, j == geo.tpb - 1)
    ext_ref[0:SUBLANES, :] = jnp.where(seg_first, 0.0, prev)
    ext_ref[SUBLANES:SUBLANES + t, :] = cur
    ext_ref[SUBLANES + t:2 * SUBLANES + t, :] = jnp.where(seg_last, 0.0, nxt)


def _ssd_conv_kernel(cur_ref, prev_ref, next_ref, nar_ref, w_ref, b_ref, dtb_ref, xbc_ref, dt_ref, ext_ref, *, geo):
    j = pl.program_id(1)
    _fill_ext(ext_ref, cur_ref[...].astype(F32), prev_ref[...].astype(F32), next_ref[...].astype(F32), j, geo)
    t = cur_ref.shape[0]
    acc = jnp.zeros(cur_ref.shape, F32) + b_ref[...]
    for k in range(SSD_CONV):
        acc = acc + w_ref[k:k + 1, :] * ext_ref[SUBLANES + k - SSD_CONV // 2:SUBLANES + k - SSD_CONV // 2 + t, :]
    xbc_ref[...] = _silu(acc).astype(xbc_ref.dtype)
    lane = lax.broadcasted_iota(jnp.int32, dt_ref.shape, 1)
    dt = _softplus(nar_ref[...].astype(F32) + dtb_ref[...])
    dt_ref[...] = jnp.where(lane < 2 * SSD_HEADS, dt, 0.0)


def _ssd_conv(proj, conv_w, conv_b, dt_bias_row, geo):
    width = BRANCH_WIDTH + 2 * SSD_GROUPS * SSD_STATE
    row = lambda b, j: b * geo.tpb + j
    cb = _COL["B_X"] // width
    prev, nxt = _halo_specs(width, cb, geo, 0, row)
    return pl.pallas_call(
        functools.partial(_ssd_conv_kernel, geo=geo),
        grid=(geo.batch, geo.tpb),
        in_specs=[pl.BlockSpec((ROW_TILE, width), lambda b, j: (row(b, j), cb)), prev, nxt,
                  pl.BlockSpec((ROW_TILE, LANES), lambda b, j: (row(b, j), _COL["NARROW"] // LANES)),
                  pl.BlockSpec((SSD_CONV, width), lambda b, j: (0, 0)),
                  pl.BlockSpec((1, width), lambda b, j: (0, 0)),
                  pl.BlockSpec((1, LANES), lambda b, j: (0, 0))],
        out_specs=[pl.BlockSpec((ROW_TILE, width), lambda b, j: (row(b, j), 0)),
                   pl.BlockSpec((ROW_TILE, LANES), lambda b, j: (row(b, j), 0))],
        out_shape=[jax.ShapeDtypeStruct((geo.rows, width), BF16),
                   jax.ShapeDtypeStruct((geo.rows, LANES), F32)],
        scratch_shapes=[pltpu.VMEM((ROW_TILE + 2 * SUBLANES, width), F32)],
        compiler_params=_cparams(("parallel", "parallel")),
        name="ssd_conv",
    )(proj, proj, proj, proj, conv_w, conv_b, dt_bias_row)


def _scan_chunk(s, n_chunks, n_ctx_chunks, reverse):
    if not reverse:
        return s
    return jnp.where(s < n_ctx_chunks, n_ctx_chunks - 1 - s, n_chunks + n_ctx_chunks - 1 - s)


def _tri(n, reverse):
    t = lax.broadcasted_iota(jnp.int32, (n, n), 0)
    s = lax.broadcasted_iota(jnp.int32, (n, n), 1)
    return (s >= t) if reverse else (s <= t)


def _split_bf16(x, pieces):
    out, rest = [], x
    for _ in range(pieces):
        p = rest.astype(BF16)
        out.append(p)
        rest = rest - p.astype(F32)
    return out


def _cumsum_rows(mask_bf16, x):
    w = x.shape[1]
    parts = jnp.dot(mask_bf16, jnp.concatenate(_split_bf16(x, 3), axis=1), preferred_element_type=F32)
    return parts[:, 0:w] + parts[:, w:2 * w] + parts[:, 2 * w:3 * w]


def _ssd_scan_kernel(xbc_ref, dt_ref, alog_ref, e_ref, *rest, reverse):
    if reverse:
        yf_ref, z_ref, skip_ref, g_ref, y_ref, h_ref = rest
    else:
        y_ref, h_ref = rest
    q = SSD_CHUNK
    gw = SSD_HPG * SSD_HEAD_DIM
    quad = 4
    qw = quad * SSD_HEAD_DIM
    lane0 = SSD_HEADS if reverse else 0
    n_batch, n_sub = xbc_ref.shape[0], xbc_ref.shape[1] // q

    @pl.when(pl.program_id(0) == 0)
    def _():
        h_ref[...] = jnp.zeros_like(h_ref)

    lane = lax.broadcasted_iota(jnp.int32, (1, LANES), 1)
    a = jnp.where(lane < 2 * SSD_HEADS, -jnp.exp(alog_ref[...]), 0.0)
    mask = _tri(q, reverse)
    mask_bf = jnp.where(mask, 1.0, 0.0).astype(BF16)
    e = e_ref[...]
    head_of_lane = lax.broadcasted_iota(jnp.int32, (1, qw), 1) // SSD_HEAD_DIM
    zero_bf = jnp.zeros((), BF16)

    order = range(n_sub - 1, -1, -1) if reverse else range(n_sub)
    for ci, b in [(ci, b) for ci in order for b in range(n_batch)]:
        rows = slice(ci * q, (ci + 1) * q)
        dt = dt_ref[b, rows, :]
        cs = _cumsum_rows(mask_bf, dt * a)
        cs_last = cs[0:1, :] if reverse else cs[q - 1:q, :]
        ecs_hi, ecs_lo = _split_bf16(jnp.exp(cs), 2)
        cd = _split_bf16(jnp.broadcast_to(jnp.exp(cs_last), (BF16_SUBLANES, LANES)), 3)
        stack = jnp.concatenate([dt.astype(BF16), (jnp.exp(cs_last - cs) * dt).astype(BF16), ecs_hi, ecs_lo, *cd], axis=0)
        big = jnp.dot(stack, e, preferred_element_type=F32)
        x = xbc_ref[b, rows, 0:BRANCH_WIDTH].astype(F32)
        dtx = (big[0:q] * x).astype(BF16)
        wx = (big[q:2 * q] * x).astype(BF16)
        ecs = big[2 * q:3 * q] + big[3 * q:4 * q]
        r0 = 4 * q
        chunk_decay = (big[r0:r0 + 1] + big[r0 + BF16_SUBLANES:r0 + BF16_SUBLANES + 1]
                       + big[r0 + 2 * BF16_SUBLANES:r0 + 2 * BF16_SUBLANES + 1])
        cs_t = cs.T
        pieces = []
        for g in range(SSD_GROUPS):
            bm = xbc_ref[b, rows, BRANCH_WIDTH + g * SSD_STATE:BRANCH_WIDTH + (g + 1) * SSD_STATE]
            c0 = BRANCH_WIDTH + SSD_GROUPS * SSD_STATE + g * SSD_STATE
            cm = xbc_ref[b, rows, c0:c0 + SSD_STATE]
            cb = lax.dot_general(cm, bm, (((1,), (1,)), ((), ())), preferred_element_type=F32)
            h_in = h_ref[b, g]
            y_off = jnp.dot(cm, h_in.astype(BF16), preferred_element_type=F32) * ecs[:, g * gw:(g + 1) * gw]
            states = jnp.dot(bm.astype(F32).T.astype(BF16), wx[:, g * gw:(g + 1) * gw], preferred_element_type=F32)
            h_ref[b, g] = chunk_decay[:, g * gw:(g + 1) * gw] * h_in + states
            for hq in range(SSD_HPG // quad):
                ms = []
                for r in range(quad):
                    col = lane0 + g * SSD_HPG + hq * quad + r
                    seg = cs[:, col:col + 1] - cs_t[col:col + 1, :]
                    ms.append((cb * jnp.exp(jnp.where(mask, seg, -1e30))).astype(BF16))
                lo = g * gw + hq * qw
                slab = dtx[:, lo:lo + qw]
                rhs = jnp.concatenate([jnp.where(head_of_lane == r, slab, zero_bf) for r in range(quad)], axis=0)
                y_diag = jnp.dot(jnp.concatenate(ms, axis=1), rhs, preferred_element_type=F32)
                pieces.append(y_diag + y_off[:, hq * qw:(hq + 1) * qw])
        y = jnp.concatenate(pieces, axis=1)
        if reverse:
            y = (yf_ref[b, rows, :] + y + skip_ref[...] * x) * _silu(z_ref[b, rows, :].astype(F32))
            y = y * lax.rsqrt(jnp.mean(y * y, axis=-1, keepdims=True) + NORM_EPS) * g_ref[...]
        y_ref[b, rows, :] = y.astype(y_ref.dtype)


def _ssd_scan(xbc, dt, a_log_row, expand_mat, geo, reverse, finish=None):
    nb, nbc = geo.lt // SCAN_BLOCK, geo.ctx_len // SCAN_BLOCK
    width, w, nbat = xbc.shape[1], BRANCH_WIDTH, geo.batch
    blk = lambda s: _scan_chunk(s, nb, nbc, reverse)
    per_batch = lambda a: a.reshape(nbat, geo.lt, a.shape[1])
    tok_spec = lambda cols, col_block: pl.BlockSpec((nbat, SCAN_BLOCK, cols), lambda s: (0, blk(s), col_block))
    const_spec = lambda shape: pl.BlockSpec(shape, lambda s: (0, 0))
    in_specs = [tok_spec(width, 0), tok_spec(LANES, 0), const_spec((1, LANES)), const_spec((LANES, w))]
    args = [per_batch(xbc), per_batch(dt), a_log_row, expand_mat]
    if reverse:
        yf, proj, skip_row, norm_g = finish
        in_specs += [tok_spec(w, 0), tok_spec(w, _COL["B_Z"] // w), const_spec((1, w)), const_spec((1, w))]
        args += [per_batch(yf), per_batch(proj), skip_row, norm_g]
    out = pl.pallas_call(
        functools.partial(_ssd_scan_kernel, reverse=reverse),
        grid=(nb,),
        in_specs=in_specs,
        out_specs=tok_spec(w, 0),
        out_shape=jax.ShapeDtypeStruct((nbat, geo.lt, w), BF16 if reverse else F32),
        scratch_shapes=[pltpu.VMEM((nbat, SSD_GROUPS, SSD_STATE, SSD_HPG * SSD_HEAD_DIM), F32)],
        compiler_params=_cparams(("arbitrary",)),
        name="ssd_scan_bwd" if reverse else "ssd_scan_fwd",
    )(*args)
    return out.reshape(geo.rows, w)


def _gla_scan_kernel(q_ref, k_ref, v_ref, nar_ref, w2_ref, bf_ref, *rest, reverse):
    if reverse:
        of_ref, gate_ref, g_ref, o_ref, h_ref = rest
    else:
        o_ref, h_ref = rest
    n = GLA_CHUNK

    @pl.when(pl.program_id(1) == 0)
    def _():
        h_ref[...] = jnp.zeros_like(h_ref)

    t = q_ref.shape[0]
    n_sub = t // n
    ri = lax.broadcasted_iota(jnp.int32, (t, t), 0)
    ci = lax.broadcasted_iota(jnp.int32, (t, t), 1)
    mask = jnp.logical_and(ri // n == ci // n, (ci >= ri) if reverse else (ci <= ri))
    logit = jnp.dot(nar_ref[...], w2_ref[...], preferred_element_type=F32) + bf_ref[...]
    gl = _log_sigmoid(logit) / GLA_TAU
    b = _cumsum_rows(jnp.where(mask, 1.0, 0.0).astype(BF16), gl)
    last = [b[c * n:c * n + 1, :] if reverse else b[(c + 1) * n - 1:(c + 1) * n, :] for c in range(n_sub)]
    b_last = jnp.concatenate([jnp.broadcast_to(r, (n, r.shape[1])) for r in last], axis=0)
    q = q_ref[...].astype(F32) * (GLA_DK ** -0.5)
    k = k_ref[...].astype(F32)
    qe = (q * jnp.exp(b)).astype(BF16)
    ke = (k * jnp.exp(-b)).astype(BF16)
    kd = (k * jnp.exp(b_last - b)).astype(BF16)
    for h in range(GLA_HEADS):
        ks = slice(h * GLA_DK, (h + 1) * GLA_DK)
        vs = slice(h * GLA_DV, (h + 1) * GLA_DV)
        v = v_ref[:, vs]
        att = lax.dot_general(qe[:, ks], ke[:, ks], (((1,), (1,)), ((), ())), preferred_element_type=F32)
        o_intra = jnp.dot(jnp.where(mask, att, 0.0).astype(BF16), v, preferred_element_type=F32)
        o_inter = [None] * n_sub
        for c in (range(n_sub - 1, -1, -1) if reverse else range(n_sub)):
            rows = slice(c * n, (c + 1) * n)
            h_in = h_ref[h]
            o_inter[c] = lax.dot_general(qe[rows, ks], h_in.astype(BF16), (((1,), (1,)), ((), ())),
                                         preferred_element_type=F32)
            upd = jnp.dot(v[rows].astype(F32).T.astype(BF16), kd[rows, ks], preferred_element_type=F32)
            h_ref[h] = jnp.exp(last[c][:, ks]) * h_in + upd
        o = o_intra + jnp.concatenate(o_inter, axis=0)
        if reverse:
            o = o + of_ref[:, vs]
            o = o * lax.rsqrt(jnp.mean(o * o, axis=-1, keepdims=True) + NORM_EPS) * g_ref[...]
            o = o * _silu(gate_ref[:, vs].astype(F32))
        o_ref[:, vs] = o.astype(o_ref.dtype)


def _gla_scan(proj, w2, b_f, geo, reverse, finish=None):
    nb, nbc = geo.lt // SCAN_BLOCK, geo.ctx_len // SCAN_BLOCK
    row = lambda b, s: b * nb + _scan_chunk(s, nb, nbc, reverse)
    kw, vw = GLA_HEADS * GLA_DK, GLA_HEADS * GLA_DV
    in_specs = [pl.BlockSpec((SCAN_BLOCK, kw), lambda b, s: (row(b, s), _COL["C_Q"] // kw)),
                pl.BlockSpec((SCAN_BLOCK, kw), lambda b, s: (row(b, s), _COL["C_K"] // kw)),
                pl.BlockSpec((SCAN_BLOCK, vw), lambda b, s: (row(b, s), _COL["C_V"] // vw)),
                pl.BlockSpec((SCAN_BLOCK, LANES), lambda b, s: (row(b, s), _COL["NARROW"] // LANES)),
                pl.BlockSpec((LANES, kw), lambda b, s: (0, 0)),
                pl.BlockSpec((1, kw), lambda b, s: (0, 0))]
    args = [proj, proj, proj, proj, w2, b_f]
    if reverse:
        of, norm_g = finish
        in_specs += [pl.BlockSpec((SCAN_BLOCK, vw), lambda b, s: (row(b, s), 0)),
                     pl.BlockSpec((SCAN_BLOCK, vw), lambda b, s: (row(b, s), _COL["C_G"] // vw)),
                     pl.BlockSpec((1, GLA_DV), lambda b, s: (0, 0))]
        args += [of, proj, norm_g]
    return pl.pallas_call(
        functools.partial(_gla_scan_kernel, reverse=reverse),
        grid=(geo.batch, nb),
        in_specs=in_specs,
        out_specs=pl.BlockSpec((SCAN_BLOCK, vw), lambda b, s: (row(b, s), 0)),
        out_shape=jax.ShapeDtypeStruct((geo.rows, vw), BF16 if reverse else F32),
        scratch_shapes=[pltpu.VMEM((GLA_HEADS, GLA_DV, GLA_DK), F32)],
        compiler_params=_cparams(("parallel", "arbitrary")),
        name="gla_scan_bwd" if reverse else "gla_scan_fwd",
    )(*args)


def _shortconv_tile(cur_ref, prev_ref, next_ref, w_ref, ext_ref, tile, geo):
    w = BRANCH_WIDTH
    u = lambda ref: ref[:, w:2 * w].astype(F32) * ref[:, 2 * w:3 * w].astype(F32)
    _fill_ext(ext_ref, u(cur_ref), u(prev_ref), u(next_ref), tile, geo)
    t = cur_ref.shape[0]
    acc = jnp.zeros((t, w), F32)
    for k in range(SC_CONV):
        acc = acc + w_ref[k:k + 1, :] * ext_ref[SUBLANES + k - SC_CONV // 2:SUBLANES + k - SC_CONV // 2 + t, :]
    return cur_ref[:, 0:w].astype(F32) * acc * _silu(cur_ref[:, 3 * w:4 * w].astype(F32))


def _merge_out_kernel(*refs, n_ya, n_src, geo, first, with_next):
    tile = first + pl.program_id(1)
    ext_ref = refs[-1]
    ya = _read_tokens(refs[0:n_ya], tile, geo)
    refs = refs[n_ya:-1]
    yd = _shortconv_tile(*refs[2:6], ext_ref, tile, geo).astype(BF16)
    ys = [ya, refs[0][...], refs[1][...], yd]
    refs = refs[6:]
    gates = refs[0:N_BRANCH]
    wb_ref, wo_ref = refs[N_BRANCH:N_BRANCH + 2]
    p = N_BRANCH + 2
    srcs = refs[p:p + n_src]
    gt_ref, gp_ref = refs[p + n_src:p + n_src + 2]
    rest = refs[p + n_src + 2:]
    m = None
    for n, (y, gate_ref) in enumerate(zip(ys, gates)):
        term = jax.nn.sigmoid(gate_ref[...].astype(F32)) * jnp.dot(y, wb_ref[n], preferred_element_type=F32)
        m = term if m is None else m + term
    out = jnp.dot(m.astype(BF16), wo_ref[...], preferred_element_type=F32)
    y = out * lax.rsqrt(jnp.mean(out * out, axis=-1, keepdims=True) + NORM_EPS) * gp_ref[...]
    x_new = _read_tokens(srcs, tile, geo) + gt_ref[0] * y
    if with_next:
        gn_ref, shn_ref, scn_ref, o_ref, h_ref = rest
        h_ref[...] = _adaln_prenorm(x_new, gn_ref[...], shn_ref[0], scn_ref[0]).astype(h_ref.dtype)
    else:
        (o_ref,) = rest
    o_ref[...] = x_new


def _merge_out(ys, proj, w_branch, w_out, layer, tokens, mod3, g_post, geo, with_ctx, nxt=None):
    first, nt = geo.span(with_ctx)
    w, d = BRANCH_WIDTH, D_MODEL
    row = lambda b, j: b * geo.tpb + first + j
    out_rows = geo.rows if with_ctx else geo.batch * geo.seq
    out_row = row if with_ctx else (lambda b, j: b * nt + j)
    y_spec = pl.BlockSpec((ROW_TILE, w), lambda b, j: (row(b, j), 0))
    gate_specs = [pl.BlockSpec((ROW_TILE, d), functools.partial(lambda b, j, n: (row(b, j), _COL["MG"] // d + n), n=n))
                  for n in range(N_BRANCH)]
    resident = pl.Buffered(1)
    mod_spec = lambda part: pl.BlockSpec((1, 1, d), lambda b, j: (geo.mod_row(b, first + j), 0, part))
    vec_spec = pl.BlockSpec((1, d), lambda b, j: (0, 0))
    ya, yb, yg, conv_w = ys
    dw = N_BRANCH * w
    d_block = _COL["D_ALL"] // dw
    d_prev, d_next = _halo_specs(dw, d_block, geo, first, row)
    in_specs = [*_token_specs(ya, geo, first), y_spec, y_spec,
                pl.BlockSpec((ROW_TILE, dw), lambda b, j: (row(b, j), d_block)), d_prev, d_next,
                pl.BlockSpec((SC_CONV, w), lambda b, j: (0, 0)), *gate_specs,
                pl.BlockSpec((None, N_BRANCH, w, d), lambda b, j: (layer, 0, 0, 0), pipeline_mode=resident),
                pl.BlockSpec((None, d, d), lambda b, j: (layer, 0, 0), pipeline_mode=resident),
                *_token_specs(tokens, geo, first), mod_spec(2), vec_spec]
    args = [*ya, yb, yg, proj, proj, proj, conv_w, proj, proj, proj, proj, w_branch, w_out, *tokens, mod3, g_post]
    out_specs = [pl.BlockSpec((ROW_TILE, d), lambda b, j: (out_row(b, j), 0))]
    out_shape = [jax.ShapeDtypeStruct((out_rows, d), F32)]
    if nxt is not None:
        assert with_ctx
        g_next, mod3_next = nxt
        in_specs += [vec_spec, mod_spec(0), mod_spec(1)]
        args += [g_next, mod3_next, mod3_next]
        out_specs.append(pl.BlockSpec((ROW_TILE, d), lambda b, j: (row(b, j), 0)))
        out_shape.append(jax.ShapeDtypeStruct((geo.rows, d), BF16))
    outs = pl.pallas_call(
        functools.partial(_merge_out_kernel, n_ya=len(ya), n_src=len(tokens), geo=geo, first=first,
                          with_next=nxt is not None),
        grid=(geo.batch, nt),
        in_specs=in_specs,
        out_specs=out_specs,
        out_shape=out_shape,
        scratch_shapes=[pltpu.VMEM((ROW_TILE + 2 * SUBLANES, w), F32)],
        compiler_params=_cparams(("parallel", "parallel")),
        name="merge_out",
    )(*args)
    return outs if nxt is not None else (outs[0], None)


def _regroup_plan():
    bw = BRANCH_WIDTH
    names = ("a_q", "a_k", "a_v", "a_g", "b_x", "b_z", "b_b", "b_c", "b_dt", "c_q", "c_k", "c_v", "c_g", "c_f",
             "d_all", "mg")
    widths = (ATT_HEADS * HEAD_DIM, ATT_KV_HEADS * HEAD_DIM, ATT_KV_HEADS * HEAD_DIM, bw,
              bw, bw, SSD_GROUPS * SSD_STATE, SSD_GROUPS * SSD_STATE, 2 * SSD_HEADS,
              GLA_HEADS * GLA_DK, GLA_HEADS * GLA_DK, GLA_HEADS * GLA_DV, bw, 2 * GLA_RANK,
              4 * bw, N_BRANCH * D_MODEL)
    src = dict(zip(names, np.concatenate([[0], np.cumsum(widths)[:-1]]).tolist()))
    wid = dict(zip(names, widths))
    dst = dict(a_q=_COL["A_Q"], a_g=_COL["A_G"], b_z=_COL["B_Z"], b_x=_COL["B_X"], b_b=_COL["B_B"], b_c=_COL["B_C"],
               a_k=_COL["A_K"], a_v=_COL["A_V"], c_v=_COL["C_V"], c_g=_COL["C_G"], c_q=_COL["C_Q"], c_k=_COL["C_K"],
               d_all=_COL["D_ALL"], mg=_COL["MG"], b_dt=_COL["NARROW"] + DT_LANE0, c_f=_COL["NARROW"] + F1_LANE0)
    return [(dst[n], src[n], wid[n], n in ("a_q", "a_k")) for n in names], sum(widths)


def _regroup_kernel(w_ref, o_ref):
    plan, _ = _regroup_plan()
    half = HEAD_DIM // 2
    for dst, src, width, deinterleave in plan:
        if deinterleave:
            for h in range(width // HEAD_DIM):
                s, d = src + h * HEAD_DIM, dst + h * HEAD_DIM
                o_ref[d:d + half, :] = w_ref[pl.ds(s, half, stride=2), :].astype(o_ref.dtype)
                o_ref[d + half:d + HEAD_DIM, :] = w_ref[pl.ds(s + 1, half, stride=2), :].astype(o_ref.dtype)
        else:
            o_ref[dst:dst + width, :] = w_ref[src:src + width, :].astype(o_ref.dtype)
    used = _COL["NARROW"] + F1_LANE0 + 2 * GLA_RANK
    o_ref[used:N_PROJ, :] = jnp.zeros((N_PROJ - used, o_ref.shape[1]), o_ref.dtype)


def _regroup_w_in(w_in, layer):
    w_t = jnp.swapaxes(w_in, 1, 2)
    _, n, k = w_t.shape
    assert n == _regroup_plan()[1]
    tk = LANES
    return pl.pallas_call(
        _regroup_kernel,
        grid=(k // tk,),
        in_specs=[pl.BlockSpec((None, n, tk), lambda i: (layer, 0, i))],
        out_specs=pl.BlockSpec((N_PROJ, tk), lambda i: (0, i)),
        out_shape=jax.ShapeDtypeStruct((N_PROJ, k), BF16),
        compiler_params=_cparams(("parallel",)),
        name="regroup_w_in",
    )(w_t)


def _deinterleave_vec(g):
    return g.reshape(HEAD_DIM // 2, 2).T.reshape(1, HEAD_DIM)


def _pad_lanes(v, lane0=0):
    return jnp.zeros((1, LANES), F32).at[0, lane0:lane0 + v.shape[0]].set(v.astype(F32))


def _head_expand_matrix(reverse):
    e = np.zeros((LANES, BRANCH_WIDTH), np.float32)
    lane0 = SSD_HEADS if reverse else 0
    for r in range(SSD_HEADS):
        e[lane0 + r, r * SSD_HEAD_DIM:(r + 1) * SSD_HEAD_DIM] = 1.0
    return jnp.asarray(e, BF16)


def _forget_weight(w_f2_dir, direction):
    lane0 = F1_LANE0 + direction * GLA_RANK
    return jnp.zeros((LANES, w_f2_dir.shape[1]), F32).at[lane0:lane0 + GLA_RANK].set(w_f2_dir).astype(BF16)


def kernel(x, c, ctx, c_ctx, w_mod, b_mod, g_pre, g_post, w_in, g_q, g_k, ssd_conv_w, ssd_conv_b,
           ssd_a_log, ssd_dt_bias, ssd_d, ssd_norm_g, gla_w_f2, gla_b_f, gla_norm_g, sc_conv_w,
           w_branch, w_out):
    batch, seq, d = x.shape
    ctx_len = ctx.shape[1]
    depth = w_in.shape[0]
    geo = _Geom(batch, ctx_len, seq)
    assert d == D_MODEL and batch + 1 <= SUBLANES and seq % GRID_W == 0

    cos_t, sin_t = _rope_tables(geo)
    c_rows = jnp.zeros((SUBLANES, d), F32).at[:batch].set(c).at[batch].set(c_ctx)
    tokens = (ctx.reshape(batch * ctx_len, d), x.reshape(batch * seq, d))
    e_fwd, e_bwd = _head_expand_matrix(False), _head_expand_matrix(True)
    w_branch_bf, w_out_bf = w_branch.astype(BF16), w_out.astype(BF16)
    mods =[_modulation(c_rows, w_mod, b_mod[l][None, :], l).reshape(SUBLANES, 1, 3 * d) for l in range(depth)]
    h = _prenorm(tokens, g_pre[0][None, :], mods[0], geo)

    for l in range(depth):
        need_ctx = l < depth - 1
        mod3 = mods[l]
        proj = _matmul(h, _regroup_w_in(w_in, l), PROJ_TN)

        qt, kh, vt = _qk_prep(proj, cos_t, sin_t, _deinterleave_vec(g_q[l]), _deinterleave_vec(g_k[l]), geo)
        ya = (_attention(qt, kh, vt, proj, geo, ctx_len, seq, geo.lt),)
        if need_ctx:
            ya = (_attention(qt, kh, vt, proj, geo, 0, ctx_len, ctx_len), *ya)

        xbc, dt = _ssd_conv(proj, ssd_conv_w[l], ssd_conv_b[l][None, :], _pad_lanes(ssd_dt_bias[l].reshape(-1)), geo)
        a_log_row = _pad_lanes(ssd_a_log[l].reshape(-1))
        ysf = _ssd_scan(xbc, dt, a_log_row, e_fwd, geo, False)
        skip_row = jnp.repeat(ssd_d[l], SSD_HEAD_DIM)[None, :]
        yb = _ssd_scan(xbc, dt, a_log_row, e_bwd, geo, True, (ysf, proj, skip_row, ssd_norm_g[l][None, :]))

        ogf = _gla_scan(proj, _forget_weight(gla_w_f2[l, 0], 0), gla_b_f[l, 0][None, :], geo, False)
        yg = _gla_scan(proj, _forget_weight(gla_w_f2[l, 1], 1), gla_b_f[l, 1][None, :], geo, True,
                       (ogf, gla_norm_g[l][None, :]))

        nxt = (g_pre[l + 1][None, :], mods[l + 1]) if need_ctx else None
        x_new, h = _merge_out((ya, yb, yg, sc_conv_w[l]), proj, w_branch_bf, w_out_bf, l, tokens, mod3,
                              g_post[l][None, :], geo, need_ctx, nxt)
        tokens = (x_new,)

    return x_new.reshape(batch, seq, d)
```

```python
import functools

import numpy as np
import jax
import jax.numpy as jnp
from jax import lax
from jax.experimental import pallas as pl
from jax.experimental.pallas import tpu as pltpu

F32 = jnp.float32
BF16 = jnp.bfloat16

D_MODEL = 2048
GRID_W = 64
BRANCH_WIDTH = D_MODEL // 2
N_BRANCH = 4
NORM_EPS = 1e-6
HEAD_DIM = 128
ATT_HEADS = BRANCH_WIDTH // HEAD_DIM
ATT_KV_HEADS = ATT_HEADS // 4
ATT_REP = ATT_HEADS // ATT_KV_HEADS
ROPE_THETA = 10000.0
SSD_HEAD_DIM = 64
SSD_HEADS = BRANCH_WIDTH // SSD_HEAD_DIM
SSD_GROUPS = 2
SSD_HPG = SSD_HEADS // SSD_GROUPS
SSD_STATE = 128
SSD_CONV = 5
SSD_CHUNK = 128
GLA_HEADS = 4
GLA_DV = BRANCH_WIDTH // GLA_HEADS
GLA_DK = GLA_DV // 2
GLA_RANK = 16
GLA_TAU = 16.0
GLA_CHUNK = 64
SC_CONV = 3

LANES = 128
SUBLANES = 8
ROW_TILE = 256
SCAN_BLOCK = ROW_TILE
VMEM_LIMIT = 56 * 1024 * 1024

_COL = dict(
    A_Q=0, A_G=1024, B_Z=2048, B_X=3072, B_B=4096, B_C=4352, A_K=4608, A_V=4864,
    C_V=5120, C_G=6144, C_Q=7168, C_K=7680,
    D_ALL=8192, MG=12288, NARROW=20480,
)
MXU_WIDTH = 256
N_PROJ = 20736
PROJ_TN = 2304
ATTN_UNROLL = 2
ATTN_STREAMS = 2
ATTN_BOUNDED_KEY_CHUNKS = (ROW_TILE, 3 * ROW_TILE)
ATTN_BOUNDED_UNROLL = 4
ATTN_SAFE_EXPONENT = 100.0
ATTN_KEY_CHUNKS = (ROW_TILE, 2 * ROW_TILE, 3 * ROW_TILE)
BF16_SUBLANES = 16
VT_ROWS = HEAD_DIM + BF16_SUBLANES
Q_SCALE = HEAD_DIM ** -0.5 * float(np.log2(np.e))
DT_LANE0 = 0
F1_LANE0 = 32


def _cparams(sem, vmem=VMEM_LIMIT):
    return pltpu.CompilerParams(dimension_semantics=sem, vmem_limit_bytes=vmem)


def _silu(x):
    return x * jax.nn.sigmoid(x)


def _softplus(x):
    return jnp.maximum(x, 0.0) + jnp.log1p(jnp.exp(-jnp.abs(x)))


def _log_sigmoid(x):
    return jnp.minimum(x, 0.0) - jnp.log(1.0 + jnp.exp(-jnp.abs(x)))


def _mod_kernel(c_ref, w_ref, b_ref, o_ref):
    a = _silu(c_ref[...]).astype(BF16)
    o_ref[...] = jnp.dot(a, w_ref[...].astype(BF16), preferred_element_type=F32) + b_ref[...]


def _modulation(c_rows, w_mod, b_mod, layer):
    _, d, n = w_mod.shape
    tn = 1024
    return pl.pallas_call(
        _mod_kernel,
        grid=(n // tn,),
        in_specs=[pl.BlockSpec((SUBLANES, d), lambda j: (0, 0)),
                  pl.BlockSpec((None, d, tn), lambda j: (layer, 0, j)),
                  pl.BlockSpec((1, tn), lambda j: (0, j))],
        out_specs=pl.BlockSpec((SUBLANES, tn), lambda j: (0, j)),
        out_shape=jax.ShapeDtypeStruct((SUBLANES, n), F32),
        compiler_params=_cparams(("parallel",)),
        name="modulation",
    )(c_rows, w_mod, b_mod)


class _Geom:
    def __init__(self, batch, ctx_len, seq):
        self.batch, self.ctx_len, self.seq = batch, ctx_len, seq
        self.lt = ctx_len + seq
        self.rows = batch * self.lt
        assert ctx_len % ROW_TILE == 0 and seq % ROW_TILE == 0
        self.tpb = self.lt // ROW_TILE
        self.ctx_tiles = ctx_len // ROW_TILE
        self.lat_tiles = seq // ROW_TILE

    def span(self, with_ctx):
        return (0, self.tpb) if with_ctx else (self.ctx_tiles, self.lat_tiles)

    def mod_row(self, b, j):
        return jnp.where(j < self.ctx_tiles, self.batch, b)


def _token_specs(tokens, geo, first):
    c = tokens[0].shape[1]
    ct, lt = geo.ctx_tiles, geo.lat_tiles
    if len(tokens) == 1 and tokens[0].shape[0] == geo.rows:
        return [pl.BlockSpec((ROW_TILE, c), lambda b, j: (b * geo.tpb + first + j, 0))]
    if len(tokens) == 1:
        assert tokens[0].shape[0] == geo.batch * geo.seq and first >= ct
        return [pl.BlockSpec((ROW_TILE, c), lambda b, j: (b * lt + first + j - ct, 0))]
    return [pl.BlockSpec((ROW_TILE, c), lambda b, j: (b * ct + jnp.minimum(first + j, ct - 1), 0),
                         pipeline_mode=pl.Buffered(1)),
            pl.BlockSpec((ROW_TILE, c), lambda b, j: (b * lt + jnp.maximum(first + j - ct, 0), 0))]


def _read_tokens(refs, tile, geo):
    if len(refs) == 1:
        return refs[0][...]
    return jnp.where(tile < geo.ctx_tiles, refs[0][...], refs[1][...])


def _adaln_prenorm(x, g, sh, sc):
    y = x * lax.rsqrt(jnp.mean(x * x, axis=-1, keepdims=True) + NORM_EPS)
    return (y * g) * (1.0 + sc) + sh


def _prenorm_kernel(*refs, n_src, geo):
    g_ref, sh_ref, sc_ref, o_ref = refs[n_src:]
    x = _read_tokens(refs[:n_src], pl.program_id(1), geo)
    o_ref[...] = _adaln_prenorm(x, g_ref[...], sh_ref[0], sc_ref[0]).astype(o_ref.dtype)


def _prenorm(tokens, g_pre, mod3, geo):
    d = D_MODEL
    return pl.pallas_call(
        functools.partial(_prenorm_kernel, n_src=len(tokens), geo=geo),
        grid=(geo.batch, geo.tpb),
        in_specs=[*_token_specs(tokens, geo, 0),
                  pl.BlockSpec((1, d), lambda b, j: (0, 0)),
                  pl.BlockSpec((1, 1, d), lambda b, j: (geo.mod_row(b, j), 0, 0)),
                  pl.BlockSpec((1, 1, d), lambda b, j: (geo.mod_row(b, j), 0, 1))],
        out_specs=pl.BlockSpec((ROW_TILE, d), lambda b, j: (b * geo.tpb + j, 0)),
        out_shape=jax.ShapeDtypeStruct((geo.rows, d), BF16),
        compiler_params=_cparams(("parallel", "parallel")),
        name="prenorm",
    )(*tokens, g_pre, mod3, mod3)


def _matmul_kernel(a_ref, w_ref, o_ref):
    o_ref[...] = lax.dot_general(a_ref[...], w_ref[...], (((1,), (1,)), ((), ())),
                                 preferred_element_type=F32).astype(o_ref.dtype)


def _matmul(a, w_t, tn, out_dtype=BF16):
    m, k = a.shape
    n = w_t.shape[0]
    tm = next(t for t in (768, 512, 256) if m % t == 0)
    assert m % tm == 0 and n % tn == 0
    return pl.pallas_call(
        _matmul_kernel,
        grid=(n // tn, m // tm),
        in_specs=[pl.BlockSpec((tm, k), lambda j, i: (i, 0)),
                  pl.BlockSpec((tn, k), lambda j, i: (j, 0))],
        out_specs=pl.BlockSpec((tm, tn), lambda j, i: (i, j)),
        out_shape=jax.ShapeDtypeStruct((m, n), out_dtype),
        compiler_params=_cparams(("parallel", "parallel")),
        name="in_proj",
    )(a, w_t)


def _rope_tables(geo):
    rows = geo.seq // GRID_W
    t_row = jnp.repeat(jnp.arange(rows, dtype=F32), GRID_W)
    t_col = jnp.tile(jnp.arange(GRID_W, dtype=F32), rows)
    half = HEAD_DIM // 2
    freqs = ROPE_THETA ** (-(jnp.arange(0, half, 2, dtype=F32) / half))
    ang = jnp.concatenate([t_row[:, None] * freqs, t_col[:, None] * freqs], axis=-1)
    cos, sin = jnp.cos(ang), jnp.sin(ang)
    cos_l = jnp.concatenate([cos, cos], axis=-1)
    sin_l = jnp.concatenate([-sin, sin], axis=-1)
    cos_c = jnp.ones((geo.ctx_len, HEAD_DIM), F32)
    sin_c = jnp.zeros((geo.ctx_len, HEAD_DIM), F32)
    return jnp.concatenate([cos_c, cos_l], axis=0), jnp.concatenate([sin_c, sin_l], axis=0)


def _qk_prep_kernel(q_ref, k_ref, v_ref, cos_ref, sin_ref, gq_ref, gk_ref, shift_ref, qt_ref, ko_ref, vt_ref):
    cos, sin = cos_ref[...], sin_ref[...]
    t = q_ref.shape[0]

    ones = jnp.ones((HEAD_DIM, HEAD_DIM), BF16)

    def norm_rope(x, g):
        x = x.astype(F32)
        hi, lo = _split_bf16(x * x, 2)
        ss = jnp.dot(hi, ones, preferred_element_type=F32) + jnp.dot(lo, ones, preferred_element_type=F32)
        y = x * lax.rsqrt(ss * (1.0 / HEAD_DIM) + NORM_EPS) * g
        return y * cos + pltpu.roll(y, HEAD_DIM // 2, 1) * sin

    gq, gk = gq_ref[...], gk_ref[...]
    for h in range(ATT_HEADS):
        g, r = divmod(h, ATT_REP)
        y = norm_rope(q_ref[:, h * HEAD_DIM:(h + 1) * HEAD_DIM], gq) * Q_SCALE
        qt_ref[g, 0, 0:HEAD_DIM, r * t:(r + 1) * t] = y.astype(qt_ref.dtype).T
    extra = MXU_WIDTH - HEAD_DIM
    first_row = lax.broadcasted_iota(jnp.int32, (extra, ATT_REP * t), 0) == 0
    first_lane = lax.broadcasted_iota(jnp.int32, (t, extra), 1) == 0
    for g in range(ATT_KV_HEADS):
        hs = slice(g * HEAD_DIM, (g + 1) * HEAD_DIM)
        qt_ref[g, 0, HEAD_DIM:MXU_WIDTH, :] = jnp.where(first_row, 1.0, 0.0).astype(qt_ref.dtype)
        ko_ref[g, :, 0:HEAD_DIM] = norm_rope(k_ref[:, hs], gk).astype(ko_ref.dtype)
        ko_ref[g, :, HEAD_DIM:MXU_WIDTH] = jnp.where(first_lane, -shift_ref[...], 0.0).astype(ko_ref.dtype)
        vt_ref[g, 0, 0:HEAD_DIM, :] = v_ref[:, hs].T
        vt_ref[g, 0, HEAD_DIM:VT_ROWS, :] = jnp.ones((VT_ROWS - HEAD_DIM, t), vt_ref.dtype)


def _qk_prep(proj, cos_t, sin_t, g_q, g_k, shift_row, geo):
    row = lambda b, j: b * geo.tpb + j
    nq, nk = ATT_HEADS * HEAD_DIM, ATT_KV_HEADS * HEAD_DIM
    kv, t, depth = ATT_KV_HEADS, ROW_TILE, MXU_WIDTH
    return pl.pallas_call(
        _qk_prep_kernel,
        grid=(geo.batch, geo.tpb),
        in_specs=[pl.BlockSpec((t, nq), lambda b, j: (row(b, j), _COL["A_Q"] // nq)),
                  pl.BlockSpec((t, nk), lambda b, j: (row(b, j), _COL["A_K"] // nk)),
                  pl.BlockSpec((t, nk), lambda b, j: (row(b, j), _COL["A_V"] // nk)),
                  pl.BlockSpec((t, HEAD_DIM), lambda b, j: (j, 0)),
                  pl.BlockSpec((t, HEAD_DIM), lambda b, j: (j, 0)),
                  pl.BlockSpec((1, HEAD_DIM), lambda b, j: (0, 0)),
                  pl.BlockSpec((1, HEAD_DIM), lambda b, j: (0, 0)),
                  pl.BlockSpec((1, depth - HEAD_DIM), lambda b, j: (0, 0))],
        out_specs=[pl.BlockSpec((kv, 1, depth, ATT_REP * t), lambda b, j: (b, j, 0, 0)),
                   pl.BlockSpec((kv, t, depth), lambda b, j: (b, j, 0)),
                   pl.BlockSpec((kv, 1, VT_ROWS, t), lambda b, j: (b, j, 0, 0))],
        out_shape=[jax.ShapeDtypeStruct((geo.batch * kv, geo.tpb, depth, ATT_REP * t), BF16),
                   jax.ShapeDtypeStruct((geo.batch * kv, geo.lt, depth), BF16),
                   jax.ShapeDtypeStruct((geo.batch * kv, geo.tpb, VT_ROWS, t), BF16)],
        compiler_params=_cparams(("parallel", "parallel")),
        name="qk_prep",
    )(proj, proj, proj, cos_t, sin_t, g_q, g_k, shift_row)


def _attn_kernel(*refs, n_str):
    qt_refs = refs[0:n_str]
    k_ref, vt_ref = refs[n_str:n_str + 2]
    gate_refs = refs[n_str + 2:2 * n_str + 2]
    o_ref, m_ref, acc_ref, s_ref, mx_ref = refs[2 * n_str + 2:]
    vt_tile = vt_ref.shape[2]
    tk = s_ref.shape[1]
    tiles = tk // vt_tile
    n_chunks = k_ref.shape[0] // tk
    tq = o_ref.shape[0] // n_str
    m_ref[...] = jnp.full(m_ref.shape, -jnp.inf, F32)
    acc_ref[...] = jnp.zeros(acc_ref.shape, F32)

    def scores(c, parity):
        k = k_ref[pl.ds(pl.multiple_of(c * tk, tk), tk), :]
        for st in range(n_str):
            s = jnp.dot(k, qt_refs[st][...], preferred_element_type=F32)
            s_ref[2 * st + parity] = s
            mx_ref[2 * st + parity] = jnp.max(s, axis=0, keepdims=True)

    def absorb(c, parity):
        for st in range(n_str):
            buf = 2 * st + parity
            m_old = m_ref[st]
            m_new = jnp.maximum(m_old, mx_ref[buf])
            alpha = jnp.exp2(m_old - m_new)
            pv = None
            for t in range(tiles):
                p = jnp.exp2((s_ref[buf, t * vt_tile:(t + 1) * vt_tile, :] - m_new).astype(BF16))
                part = jnp.dot(vt_ref[c * tiles + t], p, preferred_element_type=F32)
                pv = part if pv is None else pv + part
            acc_ref[st] = alpha * acc_ref[st] + pv
            m_ref[st] = m_new

    def group(i, carry):
        c0 = ATTN_UNROLL * i
        for u in range(ATTN_UNROLL):
            scores(c0 + u + 1, (u + 1) % 2)
            absorb(c0 + u, u % 2)
        return carry

    scores(0, 0)
    n_groups = (n_chunks - 1) // ATTN_UNROLL
    if n_groups > 0:
        lax.fori_loop(0, n_groups, group, 0)
    for c in range(n_groups * ATTN_UNROLL, n_chunks):
        if c + 1 < n_chunks:
            scores(c + 1, (c + 1) % 2)
        absorb(c, c % 2)
    _attn_epilogue(acc_ref, gate_refs, o_ref, n_str)


def _attn_epilogue(acc_ref, gate_refs, o_ref, n_str):
    tq = o_ref.shape[0] // n_str
    for st in range(n_str):
        o_t = acc_ref[st, 0:HEAD_DIM, :] / acc_ref[st, HEAD_DIM:HEAD_DIM + 1, :]
        o = jnp.concatenate([o_t[:, r * tq:(r + 1) * tq].T for r in range(ATT_REP)], axis=1)
        o_ref[st * tq:(st + 1) * tq, :] = (o * _silu(gate_refs[st][...].astype(F32))).astype(o_ref.dtype)


def _attn_bounded_kernel(*refs, n_str, tk):
    qt_refs = refs[0:n_str]
    k_ref, vt_ref = refs[n_str:n_str + 2]
    gate_refs = refs[n_str + 2:2 * n_str + 2]
    o_ref, acc_ref = refs[2 * n_str + 2:]
    vt_tile = vt_ref.shape[2]
    tiles = tk // vt_tile
    n_chunks = k_ref.shape[0] // tk
    acc_ref[...] = jnp.zeros(acc_ref.shape, F32)

    def chunk(c):
        k = k_ref[pl.ds(pl.multiple_of(c * tk, tk), tk), :]
        for st in range(n_str):
            p = jnp.exp2(jnp.dot(k, qt_refs[st][...], preferred_element_type=F32).astype(BF16))
            pv = None
            for t in range(tiles):
                part = jnp.dot(vt_ref[c * tiles + t], p[t * vt_tile:(t + 1) * vt_tile, :], preferred_element_type=F32)
                pv = part if pv is None else pv + part
            acc_ref[st] += pv

    def group(i, carry):
        for u in range(ATTN_BOUNDED_UNROLL):
            chunk(ATTN_BOUNDED_UNROLL * i + u)
        return carry

    n_groups = n_chunks // ATTN_BOUNDED_UNROLL
    if n_groups > 0:
        lax.fori_loop(0, n_groups, group, 0)
    for c in range(n_groups * ATTN_BOUNDED_UNROLL, n_chunks):
        chunk(c)
    _attn_epilogue(acc_ref, gate_refs, o_ref, n_str)


def _attention(qt, kh, vt, proj, geo, q_first_row, q_rows, kv_rows, bounded):
    tq = ROW_TILE
    assert q_first_row % tq == 0 and q_rows % tq == 0 and kv_rows % ROW_TILE == 0
    q0, n_tiles = q_first_row // tq, q_rows // tq
    n_str = ATTN_STREAMS if n_tiles % ATTN_STREAMS == 0 else 1
    gw = ATT_REP * HEAD_DIM
    mq = ATT_REP * tq
    depth = qt.shape[2]
    head = lambda b, g: b * ATT_KV_HEADS + g
    tile = lambda i, st: q0 + n_str * i + st
    stream_specs = lambda make: [make(st) for st in range(n_str)]
    in_specs = [
        *stream_specs(lambda st: pl.BlockSpec((None, None, depth, mq), lambda b, g, i: (head(b, g), tile(i, st), 0, 0))),
        pl.BlockSpec((None, kv_rows, depth), lambda b, g, i: (head(b, g), 0, 0)),
        pl.BlockSpec((None, kv_rows // ROW_TILE, VT_ROWS, ROW_TILE), lambda b, g, i: (head(b, g), 0, 0, 0)),
        *stream_specs(lambda st: pl.BlockSpec((tq, gw), lambda b, g, i: (b * geo.tpb + tile(i, st), _COL["A_G"] // gw + g))),
    ]
    common = dict(
        grid=(geo.batch, ATT_KV_HEADS, n_tiles // n_str),
        in_specs=in_specs,
        out_specs=pl.BlockSpec((n_str * tq, gw), lambda b, g, i: (b * (n_tiles // n_str) + i, g)),
        out_shape=jax.ShapeDtypeStruct((geo.batch * q_rows, BRANCH_WIDTH), BF16),
        compiler_params=_cparams(("parallel", "parallel", "arbitrary")),
    )
    tk = max(t for t in ATTN_KEY_CHUNKS if kv_rows % t == 0)
    online = pl.pallas_call(
        functools.partial(_attn_kernel, n_str=n_str),
        scratch_shapes=[pltpu.VMEM((n_str, 1, mq), F32), pltpu.VMEM((n_str, VT_ROWS, mq), F32),
                        pltpu.VMEM((2 * n_str, tk, mq), F32), pltpu.VMEM((2 * n_str, 1, mq), F32)],
        name="attention_online", **common)
    tkb = max(t for t in ATTN_BOUNDED_KEY_CHUNKS if kv_rows % t == 0)
    fast = pl.pallas_call(
        functools.partial(_attn_bounded_kernel, n_str=n_str, tk=tkb),
        scratch_shapes=[pltpu.VMEM((n_str, VT_ROWS, mq), F32)],
        name="attention", **common)
    args = (*([qt] * n_str), kh, vt, *([proj] * n_str))
    return lax.cond(bounded, lambda *a: fast(*a), lambda *a: online(*a), *args)


def _halo_specs(width, col_block, geo, first, row):
    per = ROW_TILE // SUBLANES
    last = geo.rows // SUBLANES - 1
    prev = pl.BlockSpec((SUBLANES, width), lambda b, j: (jnp.maximum(row(b, j) * per - 1, 0), col_block))
    nxt = pl.BlockSpec((SUBLANES, width), lambda b, j: (jnp.minimum((row(b, j) + 1) * per, last), col_block))
    return prev, nxt


def _fill_ext(ext_ref, cur, prev, nxt, j, geo):
    t = cur.shape[0]
    seg_first = jnp.logical_or(j == 0, j == geo.ctx_tiles)
    seg_last = jnp.logical_or(j == geo.ctx_tiles - 1, j == geo.tpb - 1)
    ext_ref[0:SUBLANES, :] = jnp.where(seg_first, 0.0, prev)
    ext_ref[SUBLANES:SUBLANES + t, :] = cur
    ext_ref[SUBLANES + t:2 * SUBLANES + t, :] = jnp.where(seg_last, 0.0, nxt)


def _ssd_conv_kernel(cur_ref, prev_ref, next_ref, nar_ref, w_ref, b_ref, dtb_ref, xbc_ref, dt_ref, ext_ref, *, geo):
    j = pl.program_id(1)
    _fill_ext(ext_ref, cur_ref[...].astype(F32), prev_ref[...].astype(F32), next_ref[...].astype(F32), j, geo)
    t = cur_ref.shape[0]
    acc = jnp.zeros(cur_ref.shape, F32) + b_ref[...]
    for k in range(SSD_CONV):
        acc = acc + w_ref[k:k + 1, :] * ext_ref[SUBLANES + k - SSD_CONV // 2:SUBLANES + k - SSD_CONV // 2 + t, :]
    xbc_ref[...] = _silu(acc).astype(xbc_ref.dtype)
    lane = lax.broadcasted_iota(jnp.int32, dt_ref.shape, 1)
    dt = _softplus(nar_ref[...].astype(F32) + dtb_ref[...])
    dt_ref[...] = jnp.where(lane < 2 * SSD_HEADS, dt, 0.0)


def _ssd_conv(proj, conv_w, conv_b, dt_bias_row, geo):
    width = BRANCH_WIDTH + 2 * SSD_GROUPS * SSD_STATE
    row = lambda b, j: b * geo.tpb + j
    cb = _COL["B_X"] // width
    prev, nxt = _halo_specs(width, cb, geo, 0, row)
    return pl.pallas_call(
        functools.partial(_ssd_conv_kernel, geo=geo),
        grid=(geo.batch, geo.tpb),
        in_specs=[pl.BlockSpec((ROW_TILE, width), lambda b, j: (row(b, j), cb)), prev, nxt,
                  pl.BlockSpec((ROW_TILE, LANES), lambda b, j: (row(b, j), _COL["NARROW"] // LANES)),
                  pl.BlockSpec((SSD_CONV, width), lambda b, j: (0, 0)),
                  pl.BlockSpec((1, width), lambda b, j: (0, 0)),
                  pl.BlockSpec((1, LANES), lambda b, j: (0, 0))],
        out_specs=[pl.BlockSpec((ROW_TILE, width), lambda b, j: (row(b, j), 0)),
                   pl.BlockSpec((ROW_TILE, LANES), lambda b, j: (row(b, j), 0))],
        out_shape=[jax.ShapeDtypeStruct((geo.rows, width), BF16),
                   jax.ShapeDtypeStruct((geo.rows, LANES), F32)],
        scratch_shapes=[pltpu.VMEM((ROW_TILE + 2 * SUBLANES, width), F32)],
        compiler_params=_cparams(("parallel", "parallel")),
        name="ssd_conv",
    )(proj, proj, proj, proj, conv_w, conv_b, dt_bias_row)


def _scan_chunk(s, n_chunks, n_ctx_chunks, reverse):
    if not reverse:
        return s
    return jnp.where(s < n_ctx_chunks, n_ctx_chunks - 1 - s, n_chunks + n_ctx_chunks - 1 - s)


def _tri(n, reverse):
    t = lax.broadcasted_iota(jnp.int32, (n, n), 0)
    s = lax.broadcasted_iota(jnp.int32, (n, n), 1)
    return (s >= t) if reverse else (s <= t)


def _split_bf16(x, pieces):
    out, rest = [], x
    for _ in range(pieces):
        p = rest.astype(BF16)
        out.append(p)
        rest = rest - p.astype(F32)
    return out


def _cumsum_rows(mask_bf16, x):
    w = x.shape[1]
    parts = jnp.dot(mask_bf16, jnp.concatenate(_split_bf16(x, 3), axis=1), preferred_element_type=F32)
    return parts[:, 0:w] + parts[:, w:2 * w] + parts[:, 2 * w:3 * w]


def _ssd_scan_kernel(xbc_ref, dt_ref, alog_ref, e_ref, *rest, reverse):
    if reverse:
        yf_ref, z_ref, skip_ref, g_ref, y_ref, h_ref = rest
    else:
        y_ref, h_ref = rest
    q = SSD_CHUNK
    gw = SSD_HPG * SSD_HEAD_DIM
    quad = 4
    qw = quad * SSD_HEAD_DIM
    lane0 = SSD_HEADS if reverse else 0
    n_batch, n_sub = xbc_ref.shape[0], xbc_ref.shape[1] // q

    @pl.when(pl.program_id(0) == 0)
    def _():
        h_ref[...] = jnp.zeros_like(h_ref)

    lane = lax.broadcasted_iota(jnp.int32, (1, LANES), 1)
    a = jnp.where(lane < 2 * SSD_HEADS, -jnp.exp(alog_ref[...]), 0.0)
    mask = _tri(q, reverse)
    mask_bf = jnp.where(mask, 1.0, 0.0).astype(BF16)
    e = e_ref[...]
    head_of_lane = lax.broadcasted_iota(jnp.int32, (1, qw), 1) // SSD_HEAD_DIM
    zero_bf = jnp.zeros((), BF16)

    order = range(n_sub - 1, -1, -1) if reverse else range(n_sub)
    for ci, b in [(ci, b) for ci in order for b in range(n_batch)]:
        rows = slice(ci * q, (ci + 1) * q)
        dt = dt_ref[b, rows, :]
        cs = _cumsum_rows(mask_bf, dt * a)
        cs_last = cs[0:1, :] if reverse else cs[q - 1:q, :]
        ecs_hi, ecs_lo = _split_bf16(jnp.exp(cs), 2)
        cd = _split_bf16(jnp.broadcast_to(jnp.exp(cs_last), (BF16_SUBLANES, LANES)), 3)
        stack = jnp.concatenate([dt.astype(BF16), (jnp.exp(cs_last - cs) * dt).astype(BF16), ecs_hi, ecs_lo, *cd], axis=0)
        big = jnp.dot(stack, e, preferred_element_type=F32)
        x = xbc_ref[b, rows, 0:BRANCH_WIDTH].astype(F32)
        dtx = (big[0:q] * x).astype(BF16)
        wx = (big[q:2 * q] * x).astype(BF16)
        ecs = big[2 * q:3 * q] + big[3 * q:4 * q]
        r0 = 4 * q
        chunk_decay = (big[r0:r0 + 1] + big[r0 + BF16_SUBLANES:r0 + BF16_SUBLANES + 1]
                       + big[r0 + 2 * BF16_SUBLANES:r0 + 2 * BF16_SUBLANES + 1])
        cs_t = cs.T
        pieces = []
        for g in range(SSD_GROUPS):
            bm = xbc_ref[b, rows, BRANCH_WIDTH + g * SSD_STATE:BRANCH_WIDTH + (g + 1) * SSD_STATE]
            c0 = BRANCH_WIDTH + SSD_GROUPS * SSD_STATE + g * SSD_STATE
            cm = xbc_ref[b, rows, c0:c0 + SSD_STATE]
            cb = lax.dot_general(cm, bm, (((1,), (1,)), ((), ())), preferred_element_type=F32)
            h_in = h_ref[b, g]
            y_off = jnp.dot(cm, h_in.astype(BF16), preferred_element_type=F32) * ecs[:, g * gw:(g + 1) * gw]
            states = jnp.dot(bm.astype(F32).T.astype(BF16), wx[:, g * gw:(g + 1) * gw], preferred_element_type=F32)
            h_ref[b, g] = chunk_decay[:, g * gw:(g + 1) * gw] * h_in + states
            for hq in range(SSD_HPG // quad):
                ms = []
                for r in range(quad):
                    col = lane0 + g * SSD_HPG + hq * quad + r
                    seg = cs[:, col:col + 1] - cs_t[col:col + 1, :]
                    ms.append((cb * jnp.exp(jnp.where(mask, seg, -1e30))).astype(BF16))
                lo = g * gw + hq * qw
                slab = dtx[:, lo:lo + qw]
                rhs = jnp.concatenate([jnp.where(head_of_lane == r, slab, zero_bf) for r in range(quad)], axis=0)
                y_diag = jnp.dot(jnp.concatenate(ms, axis=1), rhs, preferred_element_type=F32)
                pieces.append(y_diag + y_off[:, hq * qw:(hq + 1) * qw])
        y = jnp.concatenate(pieces, axis=1)
        if reverse:
            y = (yf_ref[b, rows, :] + y + skip_ref[...] * x) * _silu(z_ref[b, rows, :].astype(F32))
            y = y * lax.rsqrt(jnp.mean(y * y, axis=-1, keepdims=True) + NORM_EPS) * g_ref[...]
        y_ref[b, rows, :] = y.astype(y_ref.dtype)


def _ssd_scan(xbc, dt, a_log_row, expand_mat, geo, reverse, finish=None):
    nb, nbc = geo.lt // SCAN_BLOCK, geo.ctx_len // SCAN_BLOCK
    width, w, nbat = xbc.shape[1], BRANCH_WIDTH, geo.batch
    blk = lambda s: _scan_chunk(s, nb, nbc, reverse)
    per_batch = lambda a: a.reshape(nbat, geo.lt, a.shape[1])
    tok_spec = lambda cols, col_block: pl.BlockSpec((nbat, SCAN_BLOCK, cols), lambda s: (0, blk(s), col_block))
    const_spec = lambda shape: pl.BlockSpec(shape, lambda s: (0, 0))
    in_specs = [tok_spec(width, 0), tok_spec(LANES, 0), const_spec((1, LANES)), const_spec((LANES, w))]
    args = [per_batch(xbc), per_batch(dt), a_log_row, expand_mat]
    if reverse:
        yf, proj, skip_row, norm_g = finish
        in_specs += [tok_spec(w, 0), tok_spec(w, _COL["B_Z"] // w), const_spec((1, w)), const_spec((1, w))]
        args += [per_batch(yf), per_batch(proj), skip_row, norm_g]
    out = pl.pallas_call(
        functools.partial(_ssd_scan_kernel, reverse=reverse),
        grid=(nb,),
        in_specs=in_specs,
        out_specs=tok_spec(w, 0),
        out_shape=jax.ShapeDtypeStruct((nbat, geo.lt, w), BF16 if reverse else F32),
        scratch_shapes=[pltpu.VMEM((nbat, SSD_GROUPS, SSD_STATE, SSD_HPG * SSD_HEAD_DIM), F32)],
        compiler_params=_cparams(("arbitrary",)),
        name="ssd_scan_bwd" if reverse else "ssd_scan_fwd",
    )(*args)
    return out.reshape(geo.rows, w)


def _gla_scan_kernel(q_ref, k_ref, v_ref, nar_ref, w2_ref, bf_ref, *rest, reverse):
    if reverse:
        of_ref, gate_ref, g_ref, o_ref, h_ref = rest
    else:
        o_ref, h_ref = rest
    n = GLA_CHUNK

    @pl.when(pl.program_id(1) == 0)
    def _():
        h_ref[...] = jnp.zeros_like(h_ref)

    t = q_ref.shape[0]
    n_sub = t // n
    ri = lax.broadcasted_iota(jnp.int32, (t, t), 0)
    ci = lax.broadcasted_iota(jnp.int32, (t, t), 1)
    mask = jnp.logical_and(ri // n == ci // n, (ci >= ri) if reverse else (ci <= ri))
    logit = jnp.dot(nar_ref[...], w2_ref[...], preferred_element_type=F32) + bf_ref[...]
    gl = _log_sigmoid(logit) / GLA_TAU
    b = _cumsum_rows(jnp.where(mask, 1.0, 0.0).astype(BF16), gl)
    last = [b[c * n:c * n + 1, :] if reverse else b[(c + 1) * n - 1:(c + 1) * n, :] for c in range(n_sub)]
    b_last = jnp.concatenate([jnp.broadcast_to(r, (n, r.shape[1])) for r in last], axis=0)
    q = q_ref[...].astype(F32) * (GLA_DK ** -0.5)
    k = k_ref[...].astype(F32)
    qe = (q * jnp.exp(b)).astype(BF16)
    ke = (k * jnp.exp(-b)).astype(BF16)
    kd = (k * jnp.exp(b_last - b)).astype(BF16)
    for h in range(GLA_HEADS):
        ks = slice(h * GLA_DK, (h + 1) * GLA_DK)
        vs = slice(h * GLA_DV, (h + 1) * GLA_DV)
        v = v_ref[:, vs]
        att = lax.dot_general(qe[:, ks], ke[:, ks], (((1,), (1,)), ((), ())), preferred_element_type=F32)
        o_intra = jnp.dot(jnp.where(mask, att, 0.0).astype(BF16), v, preferred_element_type=F32)
        o_inter = [None] * n_sub
        for c in (range(n_sub - 1, -1, -1) if reverse else range(n_sub)):
            rows = slice(c * n, (c + 1) * n)
            h_in = h_ref[h]
            o_inter[c] = lax.dot_general(qe[rows, ks], h_in.astype(BF16), (((1,), (1,)), ((), ())),
                                         preferred_element_type=F32)
            upd = jnp.dot(v[rows].astype(F32).T.astype(BF16), kd[rows, ks], preferred_element_type=F32)
            h_ref[h] = jnp.exp(last[c][:, ks]) * h_in + upd
        o = o_intra + jnp.concatenate(o_inter, axis=0)
        if reverse:
            o = o + of_ref[:, vs]
            o = o * lax.rsqrt(jnp.mean(o * o, axis=-1, keepdims=True) + NORM_EPS) * g_ref[...]
            o = o * _silu(gate_ref[:, vs].astype(F32))
        o_ref[:, vs] = o.astype(o_ref.dtype)


def _gla_scan(proj, w2, b_f, geo, reverse, finish=None):
    nb, nbc = geo.lt // SCAN_BLOCK, geo.ctx_len // SCAN_BLOCK
    row = lambda b, s: b * nb + _scan_chunk(s, nb, nbc, reverse)
    kw, vw = GLA_HEADS * GLA_DK, GLA_HEADS * GLA_DV
    in_specs = [pl.BlockSpec((SCAN_BLOCK, kw), lambda b, s: (row(b, s), _COL["C_Q"] // kw)),
                pl.BlockSpec((SCAN_BLOCK, kw), lambda b, s: (row(b, s), _COL["C_K"] // kw)),
                pl.BlockSpec((SCAN_BLOCK, vw), lambda b, s: (row(b, s), _COL["C_V"] // vw)),
                pl.BlockSpec((SCAN_BLOCK, LANES), lambda b, s: (row(b, s), _COL["NARROW"] // LANES)),
                pl.BlockSpec((LANES, kw), lambda b, s: (0, 0)),
                pl.BlockSpec((1, kw), lambda b, s: (0, 0))]
    args = [proj, proj, proj, proj, w2, b_f]
    if reverse:
        of, norm_g = finish
        in_specs += [pl.BlockSpec((SCAN_BLOCK, vw), lambda b, s: (row(b, s), 0)),
                     pl.BlockSpec((SCAN_BLOCK, vw), lambda b, s: (row(b, s), _COL["C_G"] // vw)),
                     pl.BlockSpec((1, GLA_DV), lambda b, s: (0, 0))]
        args += [of, proj, norm_g]
    return pl.pallas_call(
        functools.partial(_gla_scan_kernel, reverse=reverse),
        grid=(geo.batch, nb),
        in_specs=in_specs,
        out_specs=pl.BlockSpec((SCAN_BLOCK, vw), lambda b, s: (row(b, s), 0)),
        out_shape=jax.ShapeDtypeStruct((geo.rows, vw), BF16 if reverse else F32),
        scratch_shapes=[pltpu.VMEM((GLA_HEADS, GLA_DV, GLA_DK), F32)],
        compiler_params=_cparams(("parallel", "arbitrary")),
        name="gla_scan_bwd" if reverse else "gla_scan_fwd",
    )(*args)


def _shortconv_tile(cur_ref, prev_ref, next_ref, w_ref, ext_ref, tile, geo):
    w = BRANCH_WIDTH
    u = lambda ref: ref[:, w:2 * w].astype(F32) * ref[:, 2 * w:3 * w].astype(F32)
    _fill_ext(ext_ref, u(cur_ref), u(prev_ref), u(next_ref), tile, geo)
    t = cur_ref.shape[0]
    acc = jnp.zeros((t, w), F32)
    for k in range(SC_CONV):
        acc = acc + w_ref[k:k + 1, :] * ext_ref[SUBLANES + k - SC_CONV // 2:SUBLANES + k - SC_CONV // 2 + t, :]
    return cur_ref[:, 0:w].astype(F32) * acc * _silu(cur_ref[:, 3 * w:4 * w].astype(F32))


def _merge_out_kernel(*refs, n_ya, n_src, geo, first, with_next):
    tile = first + pl.program_id(1)
    ext_ref = refs[-1]
    ya = _read_tokens(refs[0:n_ya], tile, geo)
    refs = refs[n_ya:-1]
    yd = _shortconv_tile(*refs[2:6], ext_ref, tile, geo).astype(BF16)
    ys = [ya, refs[0][...], refs[1][...], yd]
    refs = refs[6:]
    gates = refs[0:N_BRANCH]
    wb_ref, wo_ref = refs[N_BRANCH:N_BRANCH + 2]
    p = N_BRANCH + 2
    srcs = refs[p:p + n_src]
    gt_ref, gp_ref = refs[p + n_src:p + n_src + 2]
    rest = refs[p + n_src + 2:]
    m = None
    for n, (y, gate_ref) in enumerate(zip(ys, gates)):
        term = jax.nn.sigmoid(gate_ref[...].astype(F32)) * jnp.dot(y, wb_ref[n], preferred_element_type=F32)
        m = term if m is None else m + term
    out = jnp.dot(m.astype(BF16), wo_ref[...], preferred_element_type=F32)
    y = out * lax.rsqrt(jnp.mean(out * out, axis=-1, keepdims=True) + NORM_EPS) * gp_ref[...]
    x_new = _read_tokens(srcs, tile, geo) + gt_ref[0] * y
    if with_next:
        gn_ref, shn_ref, scn_ref, o_ref, h_ref = rest
        h_ref[...] = _adaln_prenorm(x_new, gn_ref[...], shn_ref[0], scn_ref[0]).astype(h_ref.dtype)
    else:
        (o_ref,) = rest
    o_ref[...] = x_new


def _merge_out(ys, proj, w_branch, w_out, layer, tokens, mod3, g_post, geo, with_ctx, nxt=None):
    first, nt = geo.span(with_ctx)
    w, d = BRANCH_WIDTH, D_MODEL
    row = lambda b, j: b * geo.tpb + first + j
    out_rows = geo.rows if with_ctx else geo.batch * geo.seq
    out_row = row if with_ctx else (lambda b, j: b * nt + j)
    y_spec = pl.BlockSpec((ROW_TILE, w), lambda b, j: (row(b, j), 0))
    gate_specs = [pl.BlockSpec((ROW_TILE, d), functools.partial(lambda b, j, n: (row(b, j), _COL["MG"] // d + n), n=n))
                  for n in range(N_BRANCH)]
    resident = pl.Buffered(1)
    mod_spec = lambda part: pl.BlockSpec((1, 1, d), lambda b, j: (geo.mod_row(b, first + j), 0, part))
    vec_spec = pl.BlockSpec((1, d), lambda b, j: (0, 0))
    ya, yb, yg, conv_w = ys
    dw = N_BRANCH * w
    d_block = _COL["D_ALL"] // dw
    d_prev, d_next = _halo_specs(dw, d_block, geo, first, row)
    in_specs = [*_token_specs(ya, geo, first), y_spec, y_spec,
                pl.BlockSpec((ROW_TILE, dw), lambda b, j: (row(b, j), d_block)), d_prev, d_next,
                pl.BlockSpec((SC_CONV, w), lambda b, j: (0, 0)), *gate_specs,
                pl.BlockSpec((None, N_BRANCH, w, d), lambda b, j: (layer, 0, 0, 0), pipeline_mode=resident),
                pl.BlockSpec((None, d, d), lambda b, j: (layer, 0, 0), pipeline_mode=resident),
                *_token_specs(tokens, geo, first), mod_spec(2), vec_spec]
    args = [*ya, yb, yg, proj, proj, proj, conv_w, proj, proj, proj, proj, w_branch, w_out, *tokens, mod3, g_post]
    out_specs = [pl.BlockSpec((ROW_TILE, d), lambda b, j: (out_row(b, j), 0))]
    out_shape = [jax.ShapeDtypeStruct((out_rows, d), F32)]
    if nxt is not None:
        assert with_ctx
        g_next, mod3_next = nxt
        in_specs += [vec_spec, mod_spec(0), mod_spec(1)]
        args += [g_next, mod3_next, mod3_next]
        out_specs.append(pl.BlockSpec((ROW_TILE, d), lambda b, j: (row(b, j), 0)))
        out_shape.append(jax.ShapeDtypeStruct((geo.rows, d), BF16))
    outs = pl.pallas_call(
        functools.partial(_merge_out_kernel, n_ya=len(ya), n_src=len(tokens), geo=geo, first=first,
                          with_next=nxt is not None),
        grid=(geo.batch, nt),
        in_specs=in_specs,
        out_specs=out_specs,
        out_shape=out_shape,
        scratch_shapes=[pltpu.VMEM((ROW_TILE + 2 * SUBLANES, w), F32)],
        compiler_params=_cparams(("parallel", "parallel")),
        name="merge_out",
    )(*args)
    return outs if nxt is not None else (outs[0], None)


def _regroup_plan():
    bw = BRANCH_WIDTH
    names = ("a_q", "a_k", "a_v", "a_g", "b_x", "b_z", "b_b", "b_c", "b_dt", "c_q", "c_k", "c_v", "c_g", "c_f",
             "d_all", "mg")
    widths = (ATT_HEADS * HEAD_DIM, ATT_KV_HEADS * HEAD_DIM, ATT_KV_HEADS * HEAD_DIM, bw,
              bw, bw, SSD_GROUPS * SSD_STATE, SSD_GROUPS * SSD_STATE, 2 * SSD_HEADS,
              GLA_HEADS * GLA_DK, GLA_HEADS * GLA_DK, GLA_HEADS * GLA_DV, bw, 2 * GLA_RANK,
              4 * bw, N_BRANCH * D_MODEL)
    src = dict(zip(names, np.concatenate([[0], np.cumsum(widths)[:-1]]).tolist()))
    wid = dict(zip(names, widths))
    dst = dict(a_q=_COL["A_Q"], a_g=_COL["A_G"], b_z=_COL["B_Z"], b_x=_COL["B_X"], b_b=_COL["B_B"], b_c=_COL["B_C"],
               a_k=_COL["A_K"], a_v=_COL["A_V"], c_v=_COL["C_V"], c_g=_COL["C_G"], c_q=_COL["C_Q"], c_k=_COL["C_K"],
               d_all=_COL["D_ALL"], mg=_COL["MG"], b_dt=_COL["NARROW"] + DT_LANE0, c_f=_COL["NARROW"] + F1_LANE0)
    return [(dst[n], src[n], wid[n], n in ("a_q", "a_k")) for n in names], sum(widths)


def _regroup_kernel(w_ref, o_ref):
    plan, _ = _regroup_plan()
    half = HEAD_DIM // 2
    for dst, src, width, deinterleave in plan:
        if deinterleave:
            for h in range(width // HEAD_DIM):
                s, d = src + h * HEAD_DIM, dst + h * HEAD_DIM
                o_ref[d:d + half, :] = w_ref[pl.ds(s, half, stride=2), :].astype(o_ref.dtype)
                o_ref[d + half:d + HEAD_DIM, :] = w_ref[pl.ds(s + 1, half, stride=2), :].astype(o_ref.dtype)
        else:
            o_ref[dst:dst + width, :] = w_ref[src:src + width, :].astype(o_ref.dtype)
    used = _COL["NARROW"] + F1_LANE0 + 2 * GLA_RANK
    o_ref[used:N_PROJ, :] = jnp.zeros((N_PROJ - used, o_ref.shape[1]), o_ref.dtype)


def _regroup_w_in(w_in, layer):
    w_t = jnp.swapaxes(w_in, 1, 2)
    _, n, k = w_t.shape
    assert n == _regroup_plan()[1]
    tk = LANES
    return pl.pallas_call(
        _regroup_kernel,
        grid=(k // tk,),
        in_specs=[pl.BlockSpec((None, n, tk), lambda i: (layer, 0, i))],
        out_specs=pl.BlockSpec((N_PROJ, tk), lambda i: (0, i)),
        out_shape=jax.ShapeDtypeStruct((N_PROJ, k), BF16),
        compiler_params=_cparams(("parallel",)),
        name="regroup_w_in",
    )(w_t)


def _deinterleave_vec(g):
    return g.reshape(HEAD_DIM // 2, 2).T.reshape(1, HEAD_DIM)


def _pad_lanes(v, lane0=0):
    return jnp.zeros((1, LANES), F32).at[0, lane0:lane0 + v.shape[0]].set(v.astype(F32))


def _head_expand_matrix(reverse):
    e = np.zeros((LANES, BRANCH_WIDTH), np.float32)
    lane0 = SSD_HEADS if reverse else 0
    for r in range(SSD_HEADS):
        e[lane0 + r, r * SSD_HEAD_DIM:(r + 1) * SSD_HEAD_DIM] = 1.0
    return jnp.asarray(e, BF16)


def _forget_weight(w_f2_dir, direction):
    lane0 = F1_LANE0 + direction * GLA_RANK
    return jnp.zeros((LANES, w_f2_dir.shape[1]), F32).at[lane0:lane0 + GLA_RANK].set(w_f2_dir).astype(BF16)


def kernel(x, c, ctx, c_ctx, w_mod, b_mod, g_pre, g_post, w_in, g_q, g_k, ssd_conv_w, ssd_conv_b,
           ssd_a_log, ssd_dt_bias, ssd_d, ssd_norm_g, gla_w_f2, gla_b_f, gla_norm_g, sc_conv_w,
           w_branch, w_out):
    batch, seq, d = x.shape
    ctx_len = ctx.shape[1]
    depth = w_in.shape[0]
    geo = _Geom(batch, ctx_len, seq)
    assert d == D_MODEL and batch + 1 <= SUBLANES and seq % GRID_W == 0

    cos_t, sin_t = _rope_tables(geo)
    c_rows = jnp.zeros((SUBLANES, d), F32).at[:batch].set(c).at[batch].set(c_ctx)
    tokens = (ctx.reshape(batch * ctx_len, d), x.reshape(batch * seq, d))
    e_fwd, e_bwd = _head_expand_matrix(False), _head_expand_matrix(True)
    w_branch_bf, w_out_bf = w_branch.astype(BF16), w_out.astype(BF16)
    mods =[_modulation(c_rows, w_mod, b_mod[l][None, :], l).reshape(SUBLANES, 1, 3 * d) for l in range(depth)]
    h = _prenorm(tokens, g_pre[0][None, :], mods[0], geo)

    for l in range(depth):
        need_ctx = l < depth - 1
        mod3 = mods[l]
        proj = _matmul(h, _regroup_w_in(w_in, l), PROJ_TN)

        shift = Q_SCALE * HEAD_DIM * jnp.max(jnp.abs(g_q[l])) * jnp.max(jnp.abs(g_k[l]))
        bounded = 2.0 * shift <= ATTN_SAFE_EXPONENT
        shift_row = jnp.full((1, MXU_WIDTH - HEAD_DIM), shift, F32)
        qt, kh, vt = _qk_prep(proj, cos_t, sin_t, _deinterleave_vec(g_q[l]), _deinterleave_vec(g_k[l]), shift_row, geo)
        ya = (_attention(qt, kh, vt, proj, geo, ctx_len, seq, geo.lt, bounded),)
        if need_ctx:
            ya = (_attention(qt, kh, vt, proj, geo, 0, ctx_len, ctx_len, bounded), *ya)

        xbc, dt = _ssd_conv(proj, ssd_conv_w[l], ssd_conv_b[l][None, :], _pad_lanes(ssd_dt_bias[l].reshape(-1)), geo)
        a_log_row = _pad_lanes(ssd_a_log[l].reshape(-1))
        ysf = _ssd_scan(xbc, dt, a_log_row, e_fwd, geo, False)
        skip_row = jnp.repeat(ssd_d[l], SSD_HEAD_DIM)[None, :]
        yb = _ssd_scan(xbc, dt, a_log_row, e_bwd, geo, True, (ysf, proj, skip_row, ssd_norm_g[l][None, :]))

        ogf = _gla_scan(proj, _forget_weight(gla_w_f2[l, 0], 0), gla_b_f[l, 0][None, :], geo, False)
        yg = _gla_scan(proj, _forget_weight(gla_w_f2[l, 1], 1), gla_b_f[l, 1][None, :], geo, True,
                       (ogf, gla_norm_g[l][None, :]))

        nxt = (g_pre[l + 1][None, :], mods[l + 1]) if need_ctx else None
        x_new, h = _merge_out((ya, yb, yg, sc_conv_w[l]), proj, w_branch_bf, w_out_bf, l, tokens, mod3,
                              g_post[l][None, :], geo, need_ctx, nxt)
        tokens = (x_new,)

    return x_new.reshape(batch, seq, d)
```

```python
import functools

import numpy as np
import jax
import jax.numpy as jnp
from jax import lax
from jax.experimental import pallas as pl
from jax.experimental.pallas import tpu as pltpu

F32 = jnp.float32
BF16 = jnp.bfloat16

D_MODEL = 2048
GRID_W = 64
BRANCH_WIDTH = D_MODEL // 2
N_BRANCH = 4
NORM_EPS = 1e-6
HEAD_DIM = 128
ATT_HEADS = BRANCH_WIDTH // HEAD_DIM
ATT_KV_HEADS = ATT_HEADS // 4
ATT_REP = ATT_HEADS // ATT_KV_HEADS
ROPE_THETA = 10000.0
SSD_HEAD_DIM = 64
SSD_HEADS = BRANCH_WIDTH // SSD_HEAD_DIM
SSD_GROUPS = 2
SSD_HPG = SSD_HEADS // SSD_GROUPS
SSD_STATE = 128
SSD_CONV = 5
SSD_CHUNK = 128
GLA_HEADS = 4
GLA_DV = BRANCH_WIDTH // GLA_HEADS
GLA_DK = GLA_DV // 2
GLA_RANK = 16
GLA_TAU = 16.0
GLA_CHUNK = 64
SC_CONV = 3

LANES = 128
SUBLANES = 8
ROW_TILE = 256
SCAN_BLOCK = ROW_TILE
VMEM_LIMIT = 56 * 1024 * 1024

_COL = dict(
    A_Q=0, A_G=1024, B_Z=2048, B_X=3072, B_B=4096, B_C=4352, A_K=4608, A_V=4864,
    C_V=5120, C_G=6144, C_Q=7168, C_K=7680,
    D_ALL=8192, MG=12288, NARROW=20480,
)
MXU_WIDTH = 256
N_PROJ = 20736
PROJ_TN = 2304
ATTN_UNROLL = 2
ATTN_STREAMS = 2
ATTN_BOUNDED_KEY_CHUNKS = (ROW_TILE, 3 * ROW_TILE)
ATTN_BOUNDED_UNROLL = 4
ATTN_BOUNDED_STREAMS = 2
ATTN_SAFE_EXPONENT = 100.0
ATTN_KEY_CHUNKS = (ROW_TILE, 2 * ROW_TILE, 3 * ROW_TILE)
BF16_SUBLANES = 16
VT_ROWS = HEAD_DIM + BF16_SUBLANES
Q_SCALE = HEAD_DIM ** -0.5 * float(np.log2(np.e))
DT_LANE0 = 0
F1_LANE0 = 32


def _cparams(sem, vmem=VMEM_LIMIT):
    return pltpu.CompilerParams(dimension_semantics=sem, vmem_limit_bytes=vmem)


def _silu(x):
    return x * jax.nn.sigmoid(x)


def _softplus(x):
    return jnp.maximum(x, 0.0) + jnp.log1p(jnp.exp(-jnp.abs(x)))


def _log_sigmoid(x):
    return jnp.minimum(x, 0.0) - jnp.log(1.0 + jnp.exp(-jnp.abs(x)))


def _mod_kernel(c_ref, w_ref, b_ref, o_ref):
    a = _silu(c_ref[...]).astype(BF16)
    o_ref[...] = jnp.dot(a, w_ref[...].astype(BF16), preferred_element_type=F32) + b_ref[...]


def _modulation(c_rows, w_mod, b_mod, layer):
    _, d, n = w_mod.shape
    tn = 1024
    return pl.pallas_call(
        _mod_kernel,
        grid=(n // tn,),
        in_specs=[pl.BlockSpec((SUBLANES, d), lambda j: (0, 0)),
                  pl.BlockSpec((None, d, tn), lambda j: (layer, 0, j)),
                  pl.BlockSpec((1, tn), lambda j: (0, j))],
        out_specs=pl.BlockSpec((SUBLANES, tn), lambda j: (0, j)),
        out_shape=jax.ShapeDtypeStruct((SUBLANES, n), F32),
        compiler_params=_cparams(("parallel",)),
        name="modulation",
    )(c_rows, w_mod, b_mod)


class _Geom:
    def __init__(self, batch, ctx_len, seq):
        self.batch, self.ctx_len, self.seq = batch, ctx_len, seq
        self.lt = ctx_len + seq
        self.rows = batch * self.lt
        assert ctx_len % ROW_TILE == 0 and seq % ROW_TILE == 0
        self.tpb = self.lt // ROW_TILE
        self.ctx_tiles = ctx_len // ROW_TILE
        self.lat_tiles = seq // ROW_TILE

    def span(self, with_ctx):
        return (0, self.tpb) if with_ctx else (self.ctx_tiles, self.lat_tiles)

    def mod_row(self, b, j):
        return jnp.where(j < self.ctx_tiles, self.batch, b)


def _token_specs(tokens, geo, first):
    c = tokens[0].shape[1]
    ct, lt = geo.ctx_tiles, geo.lat_tiles
    if len(tokens) == 1 and tokens[0].shape[0] == geo.rows:
        return [pl.BlockSpec((ROW_TILE, c), lambda b, j: (b * geo.tpb + first + j, 0))]
    if len(tokens) == 1:
        assert tokens[0].shape[0] == geo.batch * geo.seq and first >= ct
        return [pl.BlockSpec((ROW_TILE, c), lambda b, j: (b * lt + first + j - ct, 0))]
    return [pl.BlockSpec((ROW_TILE, c), lambda b, j: (b * ct + jnp.minimum(first + j, ct - 1), 0),
                         pipeline_mode=pl.Buffered(1)),
            pl.BlockSpec((ROW_TILE, c), lambda b, j: (b * lt + jnp.maximum(first + j - ct, 0), 0))]


def _read_tokens(refs, tile, geo):
    if len(refs) == 1:
        return refs[0][...]
    return jnp.where(tile < geo.ctx_tiles, refs[0][...], refs[1][...])


def _adaln_prenorm(x, g, sh, sc):
    y = x * lax.rsqrt(jnp.mean(x * x, axis=-1, keepdims=True) + NORM_EPS)
    return (y * g) * (1.0 + sc) + sh


def _prenorm_kernel(*refs, n_src, geo):
    g_ref, sh_ref, sc_ref, o_ref = refs[n_src:]
    x = _read_tokens(refs[:n_src], pl.program_id(1), geo)
    o_ref[...] = _adaln_prenorm(x, g_ref[...], sh_ref[0], sc_ref[0]).astype(o_ref.dtype)


def _prenorm(tokens, g_pre, mod3, geo):
    d = D_MODEL
    return pl.pallas_call(
        functools.partial(_prenorm_kernel, n_src=len(tokens), geo=geo),
        grid=(geo.batch, geo.tpb),
        in_specs=[*_token_specs(tokens, geo, 0),
                  pl.BlockSpec((1, d), lambda b, j: (0, 0)),
                  pl.BlockSpec((1, 1, d), lambda b, j: (geo.mod_row(b, j), 0, 0)),
                  pl.BlockSpec((1, 1, d), lambda b, j: (geo.mod_row(b, j), 0, 1))],
        out_specs=pl.BlockSpec((ROW_TILE, d), lambda b, j: (b * geo.tpb + j, 0)),
        out_shape=jax.ShapeDtypeStruct((geo.rows, d), BF16),
        compiler_params=_cparams(("parallel", "parallel")),
        name="prenorm",
    )(*tokens, g_pre, mod3, mod3)


def _matmul_kernel(a_ref, w_ref, o_ref):
    o_ref[...] = lax.dot_general(a_ref[...], w_ref[...], (((1,), (1,)), ((), ())),
                                 preferred_element_type=F32).astype(o_ref.dtype)


def _matmul(a, w_t, tn, out_dtype=BF16):
    m, k = a.shape
    n = w_t.shape[0]
    tm = next(t for t in (768, 512, 256) if m % t == 0)
    assert m % tm == 0 and n % tn == 0
    return pl.pallas_call(
        _matmul_kernel,
        grid=(n // tn, m // tm),
        in_specs=[pl.BlockSpec((tm, k), lambda j, i: (i, 0)),
                  pl.BlockSpec((tn, k), lambda j, i: (j, 0))],
        out_specs=pl.BlockSpec((tm, tn), lambda j, i: (i, j)),
        out_shape=jax.ShapeDtypeStruct((m, n), out_dtype),
        compiler_params=_cparams(("parallel", "parallel")),
        name="in_proj",
    )(a, w_t)


def _rope_tables(geo):
    rows = geo.seq // GRID_W
    t_row = jnp.repeat(jnp.arange(rows, dtype=F32), GRID_W)
    t_col = jnp.tile(jnp.arange(GRID_W, dtype=F32), rows)
    half = HEAD_DIM // 2
    freqs = ROPE_THETA ** (-(jnp.arange(0, half, 2, dtype=F32) / half))
    ang = jnp.concatenate([t_row[:, None] * freqs, t_col[:, None] * freqs], axis=-1)
    cos, sin = jnp.cos(ang), jnp.sin(ang)
    cos_l = jnp.concatenate([cos, cos], axis=-1)
    sin_l = jnp.concatenate([-sin, sin], axis=-1)
    cos_c = jnp.ones((geo.ctx_len, HEAD_DIM), F32)
    sin_c = jnp.zeros((geo.ctx_len, HEAD_DIM), F32)
    return jnp.concatenate([cos_c, cos_l], axis=0), jnp.concatenate([sin_c, sin_l], axis=0)


def _qk_prep_kernel(q_ref, k_ref, v_ref, cos_ref, sin_ref, gq_ref, gk_ref, shift_ref, qt_ref, ko_ref, vt_ref):
    cos, sin = cos_ref[...], sin_ref[...]
    t = q_ref.shape[0]

    ones = jnp.ones((HEAD_DIM, HEAD_DIM), BF16)

    def norm_rope(x, g):
        x = x.astype(F32)
        hi, lo = _split_bf16(x * x, 2)
        ss = jnp.dot(hi, ones, preferred_element_type=F32) + jnp.dot(lo, ones, preferred_element_type=F32)
        y = x * lax.rsqrt(ss * (1.0 / HEAD_DIM) + NORM_EPS) * g
        return y * cos + pltpu.roll(y, HEAD_DIM // 2, 1) * sin

    gq, gk = gq_ref[...], gk_ref[...]
    for h in range(ATT_HEADS):
        g, r = divmod(h, ATT_REP)
        y = norm_rope(q_ref[:, h * HEAD_DIM:(h + 1) * HEAD_DIM], gq) * Q_SCALE
        qt_ref[g, 0, 0:HEAD_DIM, r * t:(r + 1) * t] = y.astype(qt_ref.dtype).T
    extra = MXU_WIDTH - HEAD_DIM
    first_row = lax.broadcasted_iota(jnp.int32, (extra, ATT_REP * t), 0) == 0
    first_lane = lax.broadcasted_iota(jnp.int32, (t, extra), 1) == 0
    for g in range(ATT_KV_HEADS):
        hs = slice(g * HEAD_DIM, (g + 1) * HEAD_DIM)
        qt_ref[g, 0, HEAD_DIM:MXU_WIDTH, :] = jnp.where(first_row, 1.0, 0.0).astype(qt_ref.dtype)
        ko_ref[g, :, 0:HEAD_DIM] = norm_rope(k_ref[:, hs], gk).astype(ko_ref.dtype)
        ko_ref[g, :, HEAD_DIM:MXU_WIDTH] = jnp.where(first_lane, -shift_ref[...], 0.0).astype(ko_ref.dtype)
        vt_ref[g, 0, 0:HEAD_DIM, :] = v_ref[:, hs].T
        vt_ref[g, 0, HEAD_DIM:VT_ROWS, :] = jnp.ones((VT_ROWS - HEAD_DIM, t), vt_ref.dtype)


def _qk_prep(proj, cos_t, sin_t, g_q, g_k, shift_row, geo):
    row = lambda b, j: b * geo.tpb + j
    nq, nk = ATT_HEADS * HEAD_DIM, ATT_KV_HEADS * HEAD_DIM
    kv, t, depth = ATT_KV_HEADS, ROW_TILE, MXU_WIDTH
    return pl.pallas_call(
        _qk_prep_kernel,
        grid=(geo.batch, geo.tpb),
        in_specs=[pl.BlockSpec((t, nq), lambda b, j: (row(b, j), _COL["A_Q"] // nq)),
                  pl.BlockSpec((t, nk), lambda b, j: (row(b, j), _COL["A_K"] // nk)),
                  pl.BlockSpec((t, nk), lambda b, j: (row(b, j), _COL["A_V"] // nk)),
                  pl.BlockSpec((t, HEAD_DIM), lambda b, j: (j, 0)),
                  pl.BlockSpec((t, HEAD_DIM), lambda b, j: (j, 0)),
                  pl.BlockSpec((1, HEAD_DIM), lambda b, j: (0, 0)),
                  pl.BlockSpec((1, HEAD_DIM), lambda b, j: (0, 0)),
                  pl.BlockSpec((1, depth - HEAD_DIM), lambda b, j: (0, 0))],
        out_specs=[pl.BlockSpec((kv, 1, depth, ATT_REP * t), lambda b, j: (b, j, 0, 0)),
                   pl.BlockSpec((kv, t, depth), lambda b, j: (b, j, 0)),
                   pl.BlockSpec((kv, 1, VT_ROWS, t), lambda b, j: (b, j, 0, 0))],
        out_shape=[jax.ShapeDtypeStruct((geo.batch * kv, geo.tpb, depth, ATT_REP * t), BF16),
                   jax.ShapeDtypeStruct((geo.batch * kv, geo.lt, depth), BF16),
                   jax.ShapeDtypeStruct((geo.batch * kv, geo.tpb, VT_ROWS, t), BF16)],
        compiler_params=_cparams(("parallel", "parallel")),
        name="qk_prep",
    )(proj, proj, proj, cos_t, sin_t, g_q, g_k, shift_row)


def _attn_kernel(*refs, n_str):
    qt_refs = refs[0:n_str]
    k_ref, vt_ref = refs[n_str:n_str + 2]
    gate_refs = refs[n_str + 2:2 * n_str + 2]
    o_ref, m_ref, acc_ref, s_ref, mx_ref = refs[2 * n_str + 2:]
    vt_tile = vt_ref.shape[2]
    tk = s_ref.shape[1]
    tiles = tk // vt_tile
    n_chunks = k_ref.shape[0] // tk
    tq = o_ref.shape[0] // n_str
    m_ref[...] = jnp.full(m_ref.shape, -jnp.inf, F32)
    acc_ref[...] = jnp.zeros(acc_ref.shape, F32)

    def scores(c, parity):
        k = k_ref[pl.ds(pl.multiple_of(c * tk, tk), tk), :]
        for st in range(n_str):
            s = jnp.dot(k, qt_refs[st][...], preferred_element_type=F32)
            s_ref[2 * st + parity] = s
            mx_ref[2 * st + parity] = jnp.max(s, axis=0, keepdims=True)

    def absorb(c, parity):
        for st in range(n_str):
            buf = 2 * st + parity
            m_old = m_ref[st]
            m_new = jnp.maximum(m_old, mx_ref[buf])
            alpha = jnp.exp2(m_old - m_new)
            pv = None
            for t in range(tiles):
                p = jnp.exp2((s_ref[buf, t * vt_tile:(t + 1) * vt_tile, :] - m_new).astype(BF16))
                part = jnp.dot(vt_ref[c * tiles + t], p, preferred_element_type=F32)
                pv = part if pv is None else pv + part
            acc_ref[st] = alpha * acc_ref[st] + pv
            m_ref[st] = m_new

    def group(i, carry):
        c0 = ATTN_UNROLL * i
        for u in range(ATTN_UNROLL):
            scores(c0 + u + 1, (u + 1) % 2)
            absorb(c0 + u, u % 2)
        return carry

    scores(0, 0)
    n_groups = (n_chunks - 1) // ATTN_UNROLL
    if n_groups > 0:
        lax.fori_loop(0, n_groups, group, 0)
    for c in range(n_groups * ATTN_UNROLL, n_chunks):
        if c + 1 < n_chunks:
            scores(c + 1, (c + 1) % 2)
        absorb(c, c % 2)
    _attn_epilogue(acc_ref, gate_refs, o_ref, n_str)


def _attn_epilogue(acc_ref, gate_refs, o_ref, n_str):
    tq = o_ref.shape[0] // n_str
    for st in range(n_str):
        o_t = acc_ref[st, 0:HEAD_DIM, :] / acc_ref[st, HEAD_DIM:HEAD_DIM + 1, :]
        o = jnp.concatenate([o_t[:, r * tq:(r + 1) * tq].T for r in range(ATT_REP)], axis=1)
        o_ref[st * tq:(st + 1) * tq, :] = (o * _silu(gate_refs[st][...].astype(F32))).astype(o_ref.dtype)


def _attn_bounded_kernel(*refs, n_str, tk):
    qt_refs = refs[0:n_str]
    k_ref, vt_ref = refs[n_str:n_str + 2]
    gate_refs = refs[n_str + 2:2 * n_str + 2]
    o_ref, acc_ref = refs[2 * n_str + 2:]
    vt_tile = vt_ref.shape[2]
    tiles = tk // vt_tile
    n_chunks = k_ref.shape[0] // tk
    acc_ref[...] = jnp.zeros(acc_ref.shape, F32)

    def chunk(c):
        k = k_ref[pl.ds(pl.multiple_of(c * tk, tk), tk), :]
        for st in range(n_str):
            p = jnp.exp2(jnp.dot(k, qt_refs[st][...], preferred_element_type=F32).astype(BF16))
            pv = None
            for t in range(tiles):
                part = jnp.dot(vt_ref[c * tiles + t], p[t * vt_tile:(t + 1) * vt_tile, :], preferred_element_type=F32)
                pv = part if pv is None else pv + part
            acc_ref[st] += pv

    def group(i, carry):
        for u in range(ATTN_BOUNDED_UNROLL):
            chunk(ATTN_BOUNDED_UNROLL * i + u)
        return carry

    n_groups = n_chunks // ATTN_BOUNDED_UNROLL
    if n_groups > 0:
        lax.fori_loop(0, n_groups, group, 0)
    for c in range(n_groups * ATTN_BOUNDED_UNROLL, n_chunks):
        chunk(c)
    _attn_epilogue(acc_ref, gate_refs, o_ref, n_str)


def _attention(qt, kh, vt, proj, geo, q_first_row, q_rows, kv_rows, bounded):
    tq = ROW_TILE
    assert q_first_row % tq == 0 and q_rows % tq == 0 and kv_rows % ROW_TILE == 0
    q0, n_tiles = q_first_row // tq, q_rows // tq
    gw = ATT_REP * HEAD_DIM
    mq = ATT_REP * tq
    depth = qt.shape[2]
    head = lambda b, g: b * ATT_KV_HEADS + g

    def build(kernel_fn, streams, scratch, name):
        n_str = streams if n_tiles % streams == 0 else 1
        tile = lambda i, st: q0 + n_str * i + st
        stream_specs = lambda make: [make(st) for st in range(n_str)]
        call = pl.pallas_call(
            functools.partial(kernel_fn, n_str=n_str),
            grid=(geo.batch, ATT_KV_HEADS, n_tiles // n_str),
            in_specs=[
                *stream_specs(lambda st: pl.BlockSpec((None, None, depth, mq),
                                                      lambda b, g, i: (head(b, g), tile(i, st), 0, 0))),
                pl.BlockSpec((None, kv_rows, depth), lambda b, g, i: (head(b, g), 0, 0)),
                pl.BlockSpec((None, kv_rows // ROW_TILE, VT_ROWS, ROW_TILE), lambda b, g, i: (head(b, g), 0, 0, 0)),
                *stream_specs(lambda st: pl.BlockSpec((tq, gw), lambda b, g, i: (b * geo.tpb + tile(i, st),
                                                                                 _COL["A_G"] // gw + g))),
            ],
            out_specs=pl.BlockSpec((n_str * tq, gw), lambda b, g, i: (b * (n_tiles // n_str) + i, g)),
            out_shape=jax.ShapeDtypeStruct((geo.batch * q_rows, BRANCH_WIDTH), BF16),
            scratch_shapes=scratch(n_str),
            compiler_params=_cparams(("parallel", "parallel", "arbitrary")),
            name=name)
        return lambda: call(*([qt] * n_str), kh, vt, *([proj] * n_str))

    tk = max(t for t in ATTN_KEY_CHUNKS if kv_rows % t == 0)
    online = build(_attn_kernel, ATTN_STREAMS,
                   lambda n: [pltpu.VMEM((n, 1, mq), F32), pltpu.VMEM((n, VT_ROWS, mq), F32),
                              pltpu.VMEM((2 * n, tk, mq), F32), pltpu.VMEM((2 * n, 1, mq), F32)],
                   "attention_online")
    tkb = max(t for t in ATTN_BOUNDED_KEY_CHUNKS if kv_rows % t == 0)
    fast = build(functools.partial(_attn_bounded_kernel, tk=tkb), ATTN_BOUNDED_STREAMS,
                 lambda n: [pltpu.VMEM((n, VT_ROWS, mq), F32)], "attention")
    return lax.cond(bounded, fast, online)


def _halo_specs(width, col_block, geo, first, row):
    per = ROW_TILE // SUBLANES
    last = geo.rows // SUBLANES - 1
    prev = pl.BlockSpec((SUBLANES, width), lambda b, j: (jnp.maximum(row(b, j) * per - 1, 0), col_block))
    nxt = pl.BlockSpec((SUBLANES, width), lambda b, j: (jnp.minimum((row(b, j) + 1) * per, last), col_block))
    return prev, nxt


def _fill_ext(ext_ref, cur, prev, nxt, j, geo):
    t = cur.shape[0]
    seg_first = jnp.logical_or(j == 0, j == geo.ctx_tiles)
    seg_last = jnp.logical_or(j == geo.ctx_tiles - 1, j == geo.tpb - 1)
    ext_ref[0:SUBLANES, :] = jnp.where(seg_first, 0.0, prev)
    ext_ref[SUBLANES:SUBLANES + t, :] = cur
    ext_ref[SUBLANES + t:2 * SUBLANES + t, :] = jnp.where(seg_last, 0.0, nxt)


def _ssd_conv_kernel(cur_ref, prev_ref, next_ref, nar_ref, w_ref, b_ref, dtb_ref, xbc_ref, dt_ref, ext_ref, *, geo):
    j = pl.program_id(1)
    _fill_ext(ext_ref, cur_ref[...].astype(F32), prev_ref[...].astype(F32), next_ref[...].astype(F32), j, geo)
    t = cur_ref.shape[0]
    acc = jnp.zeros(cur_ref.shape, F32) + b_ref[...]
    for k in range(SSD_CONV):
        acc = acc + w_ref[k:k + 1, :] * ext_ref[SUBLANES + k - SSD_CONV // 2:SUBLANES + k - SSD_CONV // 2 + t, :]
    xbc_ref[...] = _silu(acc).astype(xbc_ref.dtype)
    lane = lax.broadcasted_iota(jnp.int32, dt_ref.shape, 1)
    dt = _softplus(nar_ref[...].astype(F32) + dtb_ref[...])
    dt_ref[...] = jnp.where(lane < 2 * SSD_HEADS, dt, 0.0)


def _ssd_conv(proj, conv_w, conv_b, dt_bias_row, geo):
    width = BRANCH_WIDTH + 2 * SSD_GROUPS * SSD_STATE
    row = lambda b, j: b * geo.tpb + j
    cb = _COL["B_X"] // width
    prev, nxt = _halo_specs(width, cb, geo, 0, row)
    return pl.pallas_call(
        functools.partial(_ssd_conv_kernel, geo=geo),
        grid=(geo.batch, geo.tpb),
        in_specs=[pl.BlockSpec((ROW_TILE, width), lambda b, j: (row(b, j), cb)), prev, nxt,
                  pl.BlockSpec((ROW_TILE, LANES), lambda b, j: (row(b, j), _COL["NARROW"] // LANES)),
                  pl.BlockSpec((SSD_CONV, width), lambda b, j: (0, 0)),
                  pl.BlockSpec((1, width), lambda b, j: (0, 0)),
                  pl.BlockSpec((1, LANES), lambda b, j: (0, 0))],
        out_specs=[pl.BlockSpec((ROW_TILE, width), lambda b, j: (row(b, j), 0)),
                   pl.BlockSpec((ROW_TILE, LANES), lambda b, j: (row(b, j), 0))],
        out_shape=[jax.ShapeDtypeStruct((geo.rows, width), BF16),
                   jax.ShapeDtypeStruct((geo.rows, LANES), F32)],
        scratch_shapes=[pltpu.VMEM((ROW_TILE + 2 * SUBLANES, width), F32)],
        compiler_params=_cparams(("parallel", "parallel")),
        name="ssd_conv",
    )(proj, proj, proj, proj, conv_w, conv_b, dt_bias_row)


def _scan_chunk(s, n_chunks, n_ctx_chunks, reverse):
    if not reverse:
        return s
    return jnp.where(s < n_ctx_chunks, n_ctx_chunks - 1 - s, n_chunks + n_ctx_chunks - 1 - s)


def _tri(n, reverse):
    t = lax.broadcasted_iota(jnp.int32, (n, n), 0)
    s = lax.broadcasted_iota(jnp.int32, (n, n), 1)
    return (s >= t) if reverse else (s <= t)


def _split_bf16(x, pieces):
    out, rest = [], x
    for _ in range(pieces):
        p = rest.astype(BF16)
        out.append(p)
        rest = rest - p.astype(F32)
    return out


def _cumsum_rows(mask_bf16, x):
    w = x.shape[1]
    parts = jnp.dot(mask_bf16, jnp.concatenate(_split_bf16(x, 3), axis=1), preferred_element_type=F32)
    return parts[:, 0:w] + parts[:, w:2 * w] + parts[:, 2 * w:3 * w]


def _ssd_scan_kernel(xbc_ref, dt_ref, alog_ref, e_ref, *rest, reverse):
    if reverse:
        yf_ref, z_ref, skip_ref, g_ref, y_ref, h_ref = rest
    else:
        y_ref, h_ref = rest
    q = SSD_CHUNK
    gw = SSD_HPG * SSD_HEAD_DIM
    quad = 4
    qw = quad * SSD_HEAD_DIM
    lane0 = SSD_HEADS if reverse else 0
    n_batch, n_sub = xbc_ref.shape[0], xbc_ref.shape[1] // q

    @pl.when(pl.program_id(0) == 0)
    def _():
        h_ref[...] = jnp.zeros_like(h_ref)

    lane = lax.broadcasted_iota(jnp.int32, (1, LANES), 1)
    a = jnp.where(lane < 2 * SSD_HEADS, -jnp.exp(alog_ref[...]), 0.0)
    mask = _tri(q, reverse)
    mask_bf = jnp.where(mask, 1.0, 0.0).astype(BF16)
    e = e_ref[...]
    head_of_lane = lax.broadcasted_iota(jnp.int32, (1, qw), 1) // SSD_HEAD_DIM
    zero_bf = jnp.zeros((), BF16)

    order = range(n_sub - 1, -1, -1) if reverse else range(n_sub)
    for ci, b in [(ci, b) for ci in order for b in range(n_batch)]:
        rows = slice(ci * q, (ci + 1) * q)
        dt = dt_ref[b, rows, :]
        cs = _cumsum_rows(mask_bf, dt * a)
        cs_last = cs[0:1, :] if reverse else cs[q - 1:q, :]
        zeros = jnp.zeros((q, LANES), BF16)
        ecs_hi, ecs_lo = _split_bf16(jnp.exp(cs), 2)
        cd_hi, cd_mid, cd_lo = _split_bf16(jnp.broadcast_to(jnp.exp(cs_last), (BF16_SUBLANES, LANES)), 3)
        stack = jnp.concatenate([
            jnp.concatenate([(jnp.exp(cs_last - cs) * dt).astype(BF16), zeros], axis=1),
            jnp.concatenate([ecs_hi, ecs_lo], axis=1),
            jnp.concatenate([cd_hi, cd_mid], axis=1),
            jnp.concatenate([cd_lo, zeros[0:BF16_SUBLANES]], axis=1)], axis=0)
        big = jnp.dot(stack, e, preferred_element_type=F32)
        x_bf = xbc_ref[b, rows, 0:BRANCH_WIDTH]
        x = x_bf.astype(F32)
        wx = (big[0:q] * x).astype(BF16)
        ecs = big[q:2 * q]
        r0 = 2 * q
        chunk_decay = big[r0:r0 + 1] + big[r0 + BF16_SUBLANES:r0 + BF16_SUBLANES + 1]
        cs_t = cs.T
        dt_t = dt.T
        pieces = []
        for g in range(SSD_GROUPS):
            bm = xbc_ref[b, rows, BRANCH_WIDTH + g * SSD_STATE:BRANCH_WIDTH + (g + 1) * SSD_STATE]
            c0 = BRANCH_WIDTH + SSD_GROUPS * SSD_STATE + g * SSD_STATE
            cm = xbc_ref[b, rows, c0:c0 + SSD_STATE]
            cb = lax.dot_general(cm, bm, (((1,), (1,)), ((), ())), preferred_element_type=F32)
            h_in = h_ref[b, g]
            y_off = jnp.dot(cm, h_in.astype(BF16), preferred_element_type=F32) * ecs[:, g * gw:(g + 1) * gw]
            states = jnp.dot(bm.astype(F32).T.astype(BF16), wx[:, g * gw:(g + 1) * gw], preferred_element_type=F32)
            h_ref[b, g] = chunk_decay[:, g * gw:(g + 1) * gw] * h_in + states
            for hq in range(SSD_HPG // quad):
                ms = []
                for r in range(quad):
                    col = lane0 + g * SSD_HPG + hq * quad + r
                    seg = cs[:, col:col + 1] - cs_t[col:col + 1, :]
                    ms.append((cb * jnp.exp(jnp.where(mask, seg, -1e30)) * dt_t[col:col + 1, :]).astype(BF16))
                lo = g * gw + hq * qw
                slab = x_bf[:, lo:lo + qw]
                rhs = jnp.concatenate([jnp.where(head_of_lane == r, slab, zero_bf) for r in range(quad)], axis=0)
                y_diag = jnp.dot(jnp.concatenate(ms, axis=1), rhs, preferred_element_type=F32)
                pieces.append(y_diag + y_off[:, hq * qw:(hq + 1) * qw])
        y = jnp.concatenate(pieces, axis=1)
        if reverse:
            y = (yf_ref[b, rows, :] + y + skip_ref[...] * x) * _silu(z_ref[b, rows, :].astype(F32))
            y = y * lax.rsqrt(jnp.mean(y * y, axis=-1, keepdims=True) + NORM_EPS) * g_ref[...]
        y_ref[b, rows, :] = y.astype(y_ref.dtype)


def _ssd_scan(xbc, dt, a_log_row, expand_mat, geo, reverse, finish=None):
    nb, nbc = geo.lt // SCAN_BLOCK, geo.ctx_len // SCAN_BLOCK
    width, w, nbat = xbc.shape[1], BRANCH_WIDTH, geo.batch
    blk = lambda s: _scan_chunk(s, nb, nbc, reverse)
    per_batch = lambda a: a.reshape(nbat, geo.lt, a.shape[1])
    tok_spec = lambda cols, col_block: pl.BlockSpec((nbat, SCAN_BLOCK, cols), lambda s: (0, blk(s), col_block))
    const_spec = lambda shape: pl.BlockSpec(shape, lambda s: (0, 0))
    in_specs = [tok_spec(width, 0), tok_spec(LANES, 0), const_spec((1, LANES)), const_spec((2 * LANES, w))]
    args = [per_batch(xbc), per_batch(dt), a_log_row, expand_mat]
    if reverse:
        yf, proj, skip_row, norm_g = finish
        in_specs += [tok_spec(w, 0), tok_spec(w, _COL["B_Z"] // w), const_spec((1, w)), const_spec((1, w))]
        args += [per_batch(yf), per_batch(proj), skip_row, norm_g]
    out = pl.pallas_call(
        functools.partial(_ssd_scan_kernel, reverse=reverse),
        grid=(nb,),
        in_specs=in_specs,
        out_specs=tok_spec(w, 0),
        out_shape=jax.ShapeDtypeStruct((nbat, geo.lt, w), BF16 if reverse else F32),
        scratch_shapes=[pltpu.VMEM((nbat, SSD_GROUPS, SSD_STATE, SSD_HPG * SSD_HEAD_DIM), F32)],
        compiler_params=_cparams(("arbitrary",)),
        name="ssd_scan_bwd" if reverse else "ssd_scan_fwd",
    )(*args)
    return out.reshape(geo.rows, w)


def _gla_scan_kernel(q_ref, k_ref, v_ref, nar_ref, w2_ref, bf_ref, *rest, reverse):
    if reverse:
        of_ref, gate_ref, g_ref, o_ref, h_ref = rest
    else:
        o_ref, h_ref = rest
    n = GLA_CHUNK

    @pl.when(pl.program_id(1) == 0)
    def _():
        h_ref[...] = jnp.zeros_like(h_ref)

    t = q_ref.shape[0]
    n_sub = t // n
    ri = lax.broadcasted_iota(jnp.int32, (t, t), 0)
    ci = lax.broadcasted_iota(jnp.int32, (t, t), 1)
    mask = jnp.logical_and(ri // n == ci // n, (ci >= ri) if reverse else (ci <= ri))
    logit = jnp.dot(nar_ref[...], w2_ref[...], preferred_element_type=F32) + bf_ref[...]
    gl = _log_sigmoid(logit) / GLA_TAU
    b = _cumsum_rows(jnp.where(mask, 1.0, 0.0).astype(BF16), gl)
    last = [b[c * n:c * n + 1, :] if reverse else b[(c + 1) * n - 1:(c + 1) * n, :] for c in range(n_sub)]
    b_last = jnp.concatenate([jnp.broadcast_to(r, (n, r.shape[1])) for r in last], axis=0)
    q = q_ref[...].astype(F32) * (GLA_DK ** -0.5)
    k = k_ref[...].astype(F32)
    qe = (q * jnp.exp(b)).astype(BF16)
    ke = (k * jnp.exp(-b)).astype(BF16)
    kd = (k * jnp.exp(b_last - b)).astype(BF16)
    for h in range(GLA_HEADS):
        ks = slice(h * GLA_DK, (h + 1) * GLA_DK)
        vs = slice(h * GLA_DV, (h + 1) * GLA_DV)
        v = v_ref[:, vs]
        att = lax.dot_general(qe[:, ks], ke[:, ks], (((1,), (1,)), ((), ())), preferred_element_type=F32)
        o_intra = jnp.dot(jnp.where(mask, att, 0.0).astype(BF16), v, preferred_element_type=F32)
        o_inter = [None] * n_sub
        for c in (range(n_sub - 1, -1, -1) if reverse else range(n_sub)):
            rows = slice(c * n, (c + 1) * n)
            h_in = h_ref[h]
            o_inter[c] = lax.dot_general(qe[rows, ks], h_in.astype(BF16), (((1,), (1,)), ((), ())),
                                         preferred_element_type=F32)
            upd = jnp.dot(v[rows].astype(F32).T.astype(BF16), kd[rows, ks], preferred_element_type=F32)
            h_ref[h] = jnp.exp(last[c][:, ks]) * h_in + upd
        o = o_intra + jnp.concatenate(o_inter, axis=0)
        if reverse:
            o = o + of_ref[:, vs]
            o = o * lax.rsqrt(jnp.mean(o * o, axis=-1, keepdims=True) + NORM_EPS) * g_ref[...]
            o = o * _silu(gate_ref[:, vs].astype(F32))
        o_ref[:, vs] = o.astype(o_ref.dtype)


def _gla_scan(proj, w2, b_f, geo, reverse, finish=None):
    nb, nbc = geo.lt // SCAN_BLOCK, geo.ctx_len // SCAN_BLOCK
    row = lambda b, s: b * nb + _scan_chunk(s, nb, nbc, reverse)
    kw, vw = GLA_HEADS * GLA_DK, GLA_HEADS * GLA_DV
    in_specs = [pl.BlockSpec((SCAN_BLOCK, kw), lambda b, s: (row(b, s), _COL["C_Q"] // kw)),
                pl.BlockSpec((SCAN_BLOCK, kw), lambda b, s: (row(b, s), _COL["C_K"] // kw)),
                pl.BlockSpec((SCAN_BLOCK, vw), lambda b, s: (row(b, s), _COL["C_V"] // vw)),
                pl.BlockSpec((SCAN_BLOCK, LANES), lambda b, s: (row(b, s), _COL["NARROW"] // LANES)),
                pl.BlockSpec((LANES, kw), lambda b, s: (0, 0)),
                pl.BlockSpec((1, kw), lambda b, s: (0, 0))]
    args = [proj, proj, proj, proj, w2, b_f]
    if reverse:
        of, norm_g = finish
        in_specs += [pl.BlockSpec((SCAN_BLOCK, vw), lambda b, s: (row(b, s), 0)),
                     pl.BlockSpec((SCAN_BLOCK, vw), lambda b, s: (row(b, s), _COL["C_G"] // vw)),
                     pl.BlockSpec((1, GLA_DV), lambda b, s: (0, 0))]
        args += [of, proj, norm_g]
    return pl.pallas_call(
        functools.partial(_gla_scan_kernel, reverse=reverse),
        grid=(geo.batch, nb),
        in_specs=in_specs,
        out_specs=pl.BlockSpec((SCAN_BLOCK, vw), lambda b, s: (row(b, s), 0)),
        out_shape=jax.ShapeDtypeStruct((geo.rows, vw), BF16 if reverse else F32),
        scratch_shapes=[pltpu.VMEM((GLA_HEADS, GLA_DV, GLA_DK), F32)],
        compiler_params=_cparams(("parallel", "arbitrary")),
        name="gla_scan_bwd" if reverse else "gla_scan_fwd",
    )(*args)


def _shortconv_tile(cur_ref, prev_ref, next_ref, w_ref, ext_ref, tile, geo):
    w = BRANCH_WIDTH
    u = lambda ref: ref[:, w:2 * w].astype(F32) * ref[:, 2 * w:3 * w].astype(F32)
    _fill_ext(ext_ref, u(cur_ref), u(prev_ref), u(next_ref), tile, geo)
    t = cur_ref.shape[0]
    acc = jnp.zeros((t, w), F32)
    for k in range(SC_CONV):
        acc = acc + w_ref[k:k + 1, :] * ext_ref[SUBLANES + k - SC_CONV // 2:SUBLANES + k - SC_CONV // 2 + t, :]
    return cur_ref[:, 0:w].astype(F32) * acc * _silu(cur_ref[:, 3 * w:4 * w].astype(F32))


def _merge_out_kernel(*refs, n_ya, n_src, geo, first, with_next):
    tile = first + pl.program_id(1)
    ext_ref = refs[-1]
    ya = _read_tokens(refs[0:n_ya], tile, geo)
    refs = refs[n_ya:-1]
    yd = _shortconv_tile(*refs[2:6], ext_ref, tile, geo).astype(BF16)
    ys = [ya, refs[0][...], refs[1][...], yd]
    refs = refs[6:]
    gates = refs[0:N_BRANCH]
    wb_ref, wo_ref = refs[N_BRANCH:N_BRANCH + 2]
    p = N_BRANCH + 2
    srcs = refs[p:p + n_src]
    gt_ref, gp_ref = refs[p + n_src:p + n_src + 2]
    rest = refs[p + n_src + 2:]
    m = None
    for n, (y, gate_ref) in enumerate(zip(ys, gates)):
        term = jax.nn.sigmoid(gate_ref[...].astype(F32)) * jnp.dot(y, wb_ref[n], preferred_element_type=F32)
        m = term if m is None else m + term
    out = jnp.dot(m.astype(BF16), wo_ref[...], preferred_element_type=F32)
    y = out * lax.rsqrt(jnp.mean(out * out, axis=-1, keepdims=True) + NORM_EPS) * gp_ref[...]
    x_new = _read_tokens(srcs, tile, geo) + gt_ref[0] * y
    if with_next:
        gn_ref, shn_ref, scn_ref, o_ref, h_ref = rest
        h_ref[...] = _adaln_prenorm(x_new, gn_ref[...], shn_ref[0], scn_ref[0]).astype(h_ref.dtype)
    else:
        (o_ref,) = rest
    o_ref[...] = x_new


def _merge_out(ys, proj, w_branch, w_out, layer, tokens, mod3, g_post, geo, with_ctx, nxt=None):
    first, nt = geo.span(with_ctx)
    w, d = BRANCH_WIDTH, D_MODEL
    row = lambda b, j: b * geo.tpb + first + j
    out_rows = geo.rows if with_ctx else geo.batch * geo.seq
    out_row = row if with_ctx else (lambda b, j: b * nt + j)
    y_spec = pl.BlockSpec((ROW_TILE, w), lambda b, j: (row(b, j), 0))
    gate_specs = [pl.BlockSpec((ROW_TILE, d), functools.partial(lambda b, j, n: (row(b, j), _COL["MG"] // d + n), n=n))
                  for n in range(N_BRANCH)]
    resident = pl.Buffered(1)
    mod_spec = lambda part: pl.BlockSpec((1, 1, d), lambda b, j: (geo.mod_row(b, first + j), 0, part))
    vec_spec = pl.BlockSpec((1, d), lambda b, j: (0, 0))
    ya, yb, yg, conv_w = ys
    dw = N_BRANCH * w
    d_block = _COL["D_ALL"] // dw
    d_prev, d_next = _halo_specs(dw, d_block, geo, first, row)
    in_specs = [*_token_specs(ya, geo, first), y_spec, y_spec,
                pl.BlockSpec((ROW_TILE, dw), lambda b, j: (row(b, j), d_block)), d_prev, d_next,
                pl.BlockSpec((SC_CONV, w), lambda b, j: (0, 0)), *gate_specs,
                pl.BlockSpec((None, N_BRANCH, w, d), lambda b, j: (layer, 0, 0, 0), pipeline_mode=resident),
                pl.BlockSpec((None, d, d), lambda b, j: (layer, 0, 0), pipeline_mode=resident),
                *_token_specs(tokens, geo, first), mod_spec(2), vec_spec]
    args = [*ya, yb, yg, proj, proj, proj, conv_w, proj, proj, proj, proj, w_branch, w_out, *tokens, mod3, g_post]
    out_specs = [pl.BlockSpec((ROW_TILE, d), lambda b, j: (out_row(b, j), 0))]
    out_shape = [jax.ShapeDtypeStruct((out_rows, d), F32)]
    if nxt is not None:
        assert with_ctx
        g_next, mod3_next = nxt
        in_specs += [vec_spec, mod_spec(0), mod_spec(1)]
        args += [g_next, mod3_next, mod3_next]
        out_specs.append(pl.BlockSpec((ROW_TILE, d), lambda b, j: (row(b, j), 0)))
        out_shape.append(jax.ShapeDtypeStruct((geo.rows, d), BF16))
    outs = pl.pallas_call(
        functools.partial(_merge_out_kernel, n_ya=len(ya), n_src=len(tokens), geo=geo, first=first,
                          with_next=nxt is not None),
        grid=(geo.batch, nt),
        in_specs=in_specs,
        out_specs=out_specs,
        out_shape=out_shape,
        scratch_shapes=[pltpu.VMEM((ROW_TILE + 2 * SUBLANES, w), F32)],
        compiler_params=_cparams(("parallel", "parallel")),
        name="merge_out",
    )(*args)
    return outs if nxt is not None else (outs[0], None)


def _regroup_plan():
    bw = BRANCH_WIDTH
    names = ("a_q", "a_k", "a_v", "a_g", "b_x", "b_z", "b_b", "b_c", "b_dt", "c_q", "c_k", "c_v", "c_g", "c_f",
             "d_all", "mg")
    widths = (ATT_HEADS * HEAD_DIM, ATT_KV_HEADS * HEAD_DIM, ATT_KV_HEADS * HEAD_DIM, bw,
              bw, bw, SSD_GROUPS * SSD_STATE, SSD_GROUPS * SSD_STATE, 2 * SSD_HEADS,
              GLA_HEADS * GLA_DK, GLA_HEADS * GLA_DK, GLA_HEADS * GLA_DV, bw, 2 * GLA_RANK,
              4 * bw, N_BRANCH * D_MODEL)
    src = dict(zip(names, np.concatenate([[0], np.cumsum(widths)[:-1]]).tolist()))
    wid = dict(zip(names, widths))
    dst = dict(a_q=_COL["A_Q"], a_g=_COL["A_G"], b_z=_COL["B_Z"], b_x=_COL["B_X"], b_b=_COL["B_B"], b_c=_COL["B_C"],
               a_k=_COL["A_K"], a_v=_COL["A_V"], c_v=_COL["C_V"], c_g=_COL["C_G"], c_q=_COL["C_Q"], c_k=_COL["C_K"],
               d_all=_COL["D_ALL"], mg=_COL["MG"], b_dt=_COL["NARROW"] + DT_LANE0, c_f=_COL["NARROW"] + F1_LANE0)
    return [(dst[n], src[n], wid[n], n in ("a_q", "a_k")) for n in names], sum(widths)


def _regroup_kernel(w_ref, o_ref):
    plan, _ = _regroup_plan()
    half = HEAD_DIM // 2
    for dst, src, width, deinterleave in plan:
        if deinterleave:
            for h in range(width // HEAD_DIM):
                s, d = src + h * HEAD_DIM, dst + h * HEAD_DIM
                o_ref[d:d + half, :] = w_ref[pl.ds(s, half, stride=2), :].astype(o_ref.dtype)
                o_ref[d + half:d + HEAD_DIM, :] = w_ref[pl.ds(s + 1, half, stride=2), :].astype(o_ref.dtype)
        else:
            o_ref[dst:dst + width, :] = w_ref[src:src + width, :].astype(o_ref.dtype)
    used = _COL["NARROW"] + F1_LANE0 + 2 * GLA_RANK
    o_ref[used:N_PROJ, :] = jnp.zeros((N_PROJ - used, o_ref.shape[1]), o_ref.dtype)


def _regroup_w_in(w_in, layer):
    w_t = jnp.swapaxes(w_in, 1, 2)
    _, n, k = w_t.shape
    assert n == _regroup_plan()[1]
    tk = LANES
    return pl.pallas_call(
        _regroup_kernel,
        grid=(k // tk,),
        in_specs=[pl.BlockSpec((None, n, tk), lambda i: (layer, 0, i))],
        out_specs=pl.BlockSpec((N_PROJ, tk), lambda i: (0, i)),
        out_shape=jax.ShapeDtypeStruct((N_PROJ, k), BF16),
        compiler_params=_cparams(("parallel",)),
        name="regroup_w_in",
    )(w_t)


def _deinterleave_vec(g):
    return g.reshape(HEAD_DIM // 2, 2).T.reshape(1, HEAD_DIM)


def _pad_lanes(v, lane0=0):
    return jnp.zeros((1, LANES), F32).at[0, lane0:lane0 + v.shape[0]].set(v.astype(F32))


def _head_expand_matrix(reverse):
    e = np.zeros((LANES, BRANCH_WIDTH), np.float32)
    lane0 = SSD_HEADS if reverse else 0
    for r in range(SSD_HEADS):
        e[lane0 + r, r * SSD_HEAD_DIM:(r + 1) * SSD_HEAD_DIM] = 1.0
    return jnp.asarray(np.concatenate([e, e], axis=0), BF16)


def _forget_weight(w_f2_dir, direction):
    lane0 = F1_LANE0 + direction * GLA_RANK
    return jnp.zeros((LANES, w_f2_dir.shape[1]), F32).at[lane0:lane0 + GLA_RANK].set(w_f2_dir).astype(BF16)


def kernel(x, c, ctx, c_ctx, w_mod, b_mod, g_pre, g_post, w_in, g_q, g_k, ssd_conv_w, ssd_conv_b,
           ssd_a_log, ssd_dt_bias, ssd_d, ssd_norm_g, gla_w_f2, gla_b_f, gla_norm_g, sc_conv_w,
           w_branch, w_out):
    batch, seq, d = x.shape
    ctx_len = ctx.shape[1]
    depth = w_in.shape[0]
    geo = _Geom(batch, ctx_len, seq)
    assert d == D_MODEL and batch + 1 <= SUBLANES and seq % GRID_W == 0

    cos_t, sin_t = _rope_tables(geo)
    c_rows = jnp.zeros((SUBLANES, d), F32).at[:batch].set(c).at[batch].set(c_ctx)
    tokens = (ctx.reshape(batch * ctx_len, d), x.reshape(batch * seq, d))
    e_fwd, e_bwd = _head_expand_matrix(False), _head_expand_matrix(True)
    w_branch_bf, w_out_bf = w_branch.astype(BF16), w_out.astype(BF16)
    mods =[_modulation(c_rows, w_mod, b_mod[l][None, :], l).reshape(SUBLANES, 1, 3 * d) for l in range(depth)]
    h = _prenorm(tokens, g_pre[0][None, :], mods[0], geo)

    for l in range(depth):
        need_ctx = l < depth - 1
        mod3 = mods[l]
        proj = _matmul(h, _regroup_w_in(w_in, l), PROJ_TN)

        shift = Q_SCALE * HEAD_DIM * jnp.max(jnp.abs(g_q[l])) * jnp.max(jnp.abs(g_k[l]))
        bounded = 2.0 * shift <= ATTN_SAFE_EXPONENT
        shift_row = jnp.full((1, MXU_WIDTH - HEAD_DIM), shift, F32)
        qt, kh, vt = _qk_prep(proj, cos_t, sin_t, _deinterleave_vec(g_q[l]), _deinterleave_vec(g_k[l]), shift_row, geo)
        ya = (_attention(qt, kh, vt, proj, geo, ctx_len, seq, geo.lt, bounded),)
        if need_ctx:
            ya = (_attention(qt, kh, vt, proj, geo, 0, ctx_len, ctx_len, bounded), *ya)

        xbc, dt = _ssd_conv(proj, ssd_conv_w[l], ssd_conv_b[l][None, :], _pad_lanes(ssd_dt_bias[l].reshape(-1)), geo)
        a_log_row = _pad_lanes(ssd_a_log[l].reshape(-1))
        ysf = _ssd_scan(xbc, dt, a_log_row, e_fwd, geo, False)
        skip_row = jnp.repeat(ssd_d[l], SSD_HEAD_DIM)[None, :]
        yb = _ssd_scan(xbc, dt, a_log_row, e_bwd, geo, True, (ysf, proj, skip_row, ssd_norm_g[l][None, :]))

        ogf = _gla_scan(proj, _forget_weight(gla_w_f2[l, 0], 0), gla_b_f[l, 0][None, :], geo, False)
        yg = _gla_scan(proj, _forget_weight(gla_w_f2[l, 1], 1), gla_b_f[l, 1][None, :], geo, True,
                       (ogf, gla_norm_g[l][None, :]))

        nxt = (g_pre[l + 1][None, :], mods[l + 1]) if need_ctx else None
        x_new, h = _merge_out((ya, yb, yg, sc_conv_w[l]), proj, w_branch_bf, w_out_bf, l, tokens, mod3,
                              g_post[l][None, :], geo, need_ctx, nxt)
        tokens = (x_new,)

    return x_new.reshape(batch, seq, d)
```

```python
import functools

import numpy as np
import jax
import jax.numpy as jnp
from jax import lax
from jax.experimental import pallas as pl
from jax.experimental.pallas import tpu as pltpu

F32 = jnp.float32
BF16 = jnp.bfloat16

D_MODEL = 2048
GRID_W = 64
BRANCH_WIDTH = D_MODEL // 2
N_BRANCH = 4
NORM_EPS = 1e-6
HEAD_DIM = 128
ATT_HEADS = BRANCH_WIDTH // HEAD_DIM
ATT_KV_HEADS = ATT_HEADS // 4
ATT_REP = ATT_HEADS // ATT_KV_HEADS
ROPE_THETA = 10000.0
SSD_HEAD_DIM = 64
SSD_HEADS = BRANCH_WIDTH // SSD_HEAD_DIM
SSD_GROUPS = 2
SSD_HPG = SSD_HEADS // SSD_GROUPS
SSD_STATE = 128
SSD_CONV = 5
SSD_CHUNK = 128
GLA_HEADS = 4
GLA_DV = BRANCH_WIDTH // GLA_HEADS
GLA_DK = GLA_DV // 2
GLA_RANK = 16
GLA_TAU = 16.0
GLA_CHUNK = 64
SC_CONV = 3

LANES = 128
SUBLANES = 8
ROW_TILE = 256
SCAN_BLOCK = ROW_TILE
VMEM_LIMIT = 56 * 1024 * 1024

_COL = dict(
    A_Q=0, A_G=1024, B_Z=2048, B_X=3072, B_B=4096, B_C=4352, A_K=4608, A_V=4864,
    C_V=5120, C_G=6144, C_Q=7168, C_K=7680,
    D_ALL=8192, MG=12288, NARROW=20480,
)
MXU_WIDTH = 256
N_PROJ = 20736
PROJ_TN = 2304
ATTN_UNROLL = 2
ATTN_STREAMS = 2
ATTN_BOUNDED_KEY_CHUNKS = (ROW_TILE, 3 * ROW_TILE)
ATTN_BOUNDED_UNROLL = 11
ATTN_BOUNDED_STREAMS = 2
ATTN_SAFE_EXPONENT = 100.0
ATTN_KEY_CHUNKS = (ROW_TILE, 2 * ROW_TILE, 3 * ROW_TILE)
BF16_SUBLANES = 16
VT_ROWS = HEAD_DIM + BF16_SUBLANES
Q_SCALE = HEAD_DIM ** -0.5 * float(np.log2(np.e))
DT_LANE0 = 0
F1_LANE0 = 32


def _cparams(sem, vmem=VMEM_LIMIT):
    return pltpu.CompilerParams(dimension_semantics=sem, vmem_limit_bytes=vmem)


def _silu(x):
    return x * jax.nn.sigmoid(x)


def _softplus(x):
    return jnp.maximum(x, 0.0) + jnp.log1p(jnp.exp(-jnp.abs(x)))


def _log_sigmoid(x):
    return jnp.minimum(x, 0.0) - jnp.log(1.0 + jnp.exp(-jnp.abs(x)))


def _mod_kernel(c_ref, w_ref, b_ref, o_ref):
    a = _silu(c_ref[...]).astype(BF16)
    o_ref[...] = jnp.dot(a, w_ref[...].astype(BF16), preferred_element_type=F32) + b_ref[...]


def _modulation(c_rows, w_mod, b_mod, layer):
    _, d, n = w_mod.shape
    tn = 1024
    return pl.pallas_call(
        _mod_kernel,
        grid=(n // tn,),
        in_specs=[pl.BlockSpec((SUBLANES, d), lambda j: (0, 0)),
                  pl.BlockSpec((None, d, tn), lambda j: (layer, 0, j)),
                  pl.BlockSpec((1, tn), lambda j: (0, j))],
        out_specs=pl.BlockSpec((SUBLANES, tn), lambda j: (0, j)),
        out_shape=jax.ShapeDtypeStruct((SUBLANES, n), F32),
        compiler_params=_cparams(("parallel",)),
        name="modulation",
    )(c_rows, w_mod, b_mod)


class _Geom:
    def __init__(self, batch, ctx_len, seq):
        self.batch, self.ctx_len, self.seq = batch, ctx_len, seq
        self.lt = ctx_len + seq
        self.rows = batch * self.lt
        assert ctx_len % ROW_TILE == 0 and seq % ROW_TILE == 0
        self.tpb = self.lt // ROW_TILE
        self.ctx_tiles = ctx_len // ROW_TILE
        self.lat_tiles = seq // ROW_TILE

    def span(self, with_ctx):
        return (0, self.tpb) if with_ctx else (self.ctx_tiles, self.lat_tiles)

    def mod_row(self, b, j):
        return jnp.where(j < self.ctx_tiles, self.batch, b)


def _token_specs(tokens, geo, first):
    c = tokens[0].shape[1]
    ct, lt = geo.ctx_tiles, geo.lat_tiles
    if len(tokens) == 1 and tokens[0].shape[0] == geo.rows:
        return [pl.BlockSpec((ROW_TILE, c), lambda b, j: (b * geo.tpb + first + j, 0))]
    if len(tokens) == 1:
        assert tokens[0].shape[0] == geo.batch * geo.seq and first >= ct
        return [pl.BlockSpec((ROW_TILE, c), lambda b, j: (b * lt + first + j - ct, 0))]
    return [pl.BlockSpec((ROW_TILE, c), lambda b, j: (b * ct + jnp.minimum(first + j, ct - 1), 0),
                         pipeline_mode=pl.Buffered(1)),
            pl.BlockSpec((ROW_TILE, c), lambda b, j: (b * lt + jnp.maximum(first + j - ct, 0), 0))]


def _read_tokens(refs, tile, geo):
    if len(refs) == 1:
        return refs[0][...]
    return jnp.where(tile < geo.ctx_tiles, refs[0][...], refs[1][...])


def _adaln_prenorm(x, g, sh, sc):
    y = x * lax.rsqrt(jnp.mean(x * x, axis=-1, keepdims=True) + NORM_EPS)
    return (y * g) * (1.0 + sc) + sh


def _prenorm_kernel(*refs, n_src, geo):
    g_ref, sh_ref, sc_ref, o_ref = refs[n_src:]
    x = _read_tokens(refs[:n_src], pl.program_id(1), geo)
    o_ref[...] = _adaln_prenorm(x, g_ref[...], sh_ref[0], sc_ref[0]).astype(o_ref.dtype)


def _prenorm(tokens, g_pre, mod3, geo):
    d = D_MODEL
    return pl.pallas_call(
        functools.partial(_prenorm_kernel, n_src=len(tokens), geo=geo),
        grid=(geo.batch, geo.tpb),
        in_specs=[*_token_specs(tokens, geo, 0),
                  pl.BlockSpec((1, d), lambda b, j: (0, 0)),
                  pl.BlockSpec((1, 1, d), lambda b, j: (geo.mod_row(b, j), 0, 0)),
                  pl.BlockSpec((1, 1, d), lambda b, j: (geo.mod_row(b, j), 0, 1))],
        out_specs=pl.BlockSpec((ROW_TILE, d), lambda b, j: (b * geo.tpb + j, 0)),
        out_shape=jax.ShapeDtypeStruct((geo.rows, d), BF16),
        compiler_params=_cparams(("parallel", "parallel")),
        name="prenorm",
    )(*tokens, g_pre, mod3, mod3)


def _matmul_kernel(a_ref, w_ref, o_ref):
    o_ref[...] = lax.dot_general(a_ref[...], w_ref[...], (((1,), (1,)), ((), ())),
                                 preferred_element_type=F32).astype(o_ref.dtype)


def _matmul(a, w_t, tn, out_dtype=BF16):
    m, k = a.shape
    n = w_t.shape[0]
    tm = next(t for t in (768, 512, 256) if m % t == 0)
    assert m % tm == 0 and n % tn == 0
    return pl.pallas_call(
        _matmul_kernel,
        grid=(n // tn, m // tm),
        in_specs=[pl.BlockSpec((tm, k), lambda j, i: (i, 0)),
                  pl.BlockSpec((tn, k), lambda j, i: (j, 0))],
        out_specs=pl.BlockSpec((tm, tn), lambda j, i: (i, j)),
        out_shape=jax.ShapeDtypeStruct((m, n), out_dtype),
        compiler_params=_cparams(("parallel", "parallel")),
        name="in_proj",
    )(a, w_t)


def _rope_tables(geo):
    rows = geo.seq // GRID_W
    t_row = jnp.repeat(jnp.arange(rows, dtype=F32), GRID_W)
    t_col = jnp.tile(jnp.arange(GRID_W, dtype=F32), rows)
    half = HEAD_DIM // 2
    freqs = ROPE_THETA ** (-(jnp.arange(0, half, 2, dtype=F32) / half))
    ang = jnp.concatenate([t_row[:, None] * freqs, t_col[:, None] * freqs], axis=-1)
    cos, sin = jnp.cos(ang), jnp.sin(ang)
    cos_l = jnp.concatenate([cos, cos], axis=-1)
    sin_l = jnp.concatenate([-sin, sin], axis=-1)
    cos_c = jnp.ones((geo.ctx_len, HEAD_DIM), F32)
    sin_c = jnp.zeros((geo.ctx_len, HEAD_DIM), F32)
    return jnp.concatenate([cos_c, cos_l], axis=0), jnp.concatenate([sin_c, sin_l], axis=0)


def _qk_prep_kernel(q_ref, k_ref, v_ref, cos_ref, sin_ref, gq_ref, gk_ref, shift_ref, qt_ref, ko_ref, vt_ref):
    cos, sin = cos_ref[...], sin_ref[...]
    t = q_ref.shape[0]

    ones = jnp.ones((HEAD_DIM, HEAD_DIM), BF16)

    def norm_rope(x, g):
        x = x.astype(F32)
        hi, lo = _split_bf16(x * x, 2)
        ss = jnp.dot(hi, ones, preferred_element_type=F32) + jnp.dot(lo, ones, preferred_element_type=F32)
        y = x * lax.rsqrt(ss * (1.0 / HEAD_DIM) + NORM_EPS) * g
        return y * cos + pltpu.roll(y, HEAD_DIM // 2, 1) * sin

    gq, gk = gq_ref[...], gk_ref[...]
    for h in range(ATT_HEADS):
        g, r = divmod(h, ATT_REP)
        y = norm_rope(q_ref[:, h * HEAD_DIM:(h + 1) * HEAD_DIM], gq) * Q_SCALE
        qt_ref[g, 0, 0:HEAD_DIM, r * t:(r + 1) * t] = y.astype(qt_ref.dtype).T
    extra = MXU_WIDTH - HEAD_DIM
    first_row = lax.broadcasted_iota(jnp.int32, (extra, ATT_REP * t), 0) == 0
    first_lane = lax.broadcasted_iota(jnp.int32, (t, extra), 1) == 0
    for g in range(ATT_KV_HEADS):
        hs = slice(g * HEAD_DIM, (g + 1) * HEAD_DIM)
        qt_ref[g, 0, HEAD_DIM:MXU_WIDTH, :] = jnp.where(first_row, 1.0, 0.0).astype(qt_ref.dtype)
        ko_ref[g, :, 0:HEAD_DIM] = norm_rope(k_ref[:, hs], gk).astype(ko_ref.dtype)
        ko_ref[g, :, HEAD_DIM:MXU_WIDTH] = jnp.where(first_lane, -shift_ref[...], 0.0).astype(ko_ref.dtype)
        vt_ref[g, 0, 0:HEAD_DIM, :] = v_ref[:, hs].T
        vt_ref[g, 0, HEAD_DIM:VT_ROWS, :] = jnp.ones((VT_ROWS - HEAD_DIM, t), vt_ref.dtype)


def _qk_prep(proj, cos_t, sin_t, g_q, g_k, shift_row, geo):
    row = lambda b, j: b * geo.tpb + j
    nq, nk = ATT_HEADS * HEAD_DIM, ATT_KV_HEADS * HEAD_DIM
    kv, t, depth = ATT_KV_HEADS, ROW_TILE, MXU_WIDTH
    return pl.pallas_call(
        _qk_prep_kernel,
        grid=(geo.batch, geo.tpb),
        in_specs=[pl.BlockSpec((t, nq), lambda b, j: (row(b, j), _COL["A_Q"] // nq)),
                  pl.BlockSpec((t, nk), lambda b, j: (row(b, j), _COL["A_K"] // nk)),
                  pl.BlockSpec((t, nk), lambda b, j: (row(b, j), _COL["A_V"] // nk)),
                  pl.BlockSpec((t, HEAD_DIM), lambda b, j: (j, 0)),
                  pl.BlockSpec((t, HEAD_DIM), lambda b, j: (j, 0)),
                  pl.BlockSpec((1, HEAD_DIM), lambda b, j: (0, 0)),
                  pl.BlockSpec((1, HEAD_DIM), lambda b, j: (0, 0)),
                  pl.BlockSpec((1, depth - HEAD_DIM), lambda b, j: (0, 0))],
        out_specs=[pl.BlockSpec((kv, 1, depth, ATT_REP * t), lambda b, j: (b, j, 0, 0)),
                   pl.BlockSpec((kv, t, depth), lambda b, j: (b, j, 0)),
                   pl.BlockSpec((kv, 1, VT_ROWS, t), lambda b, j: (b, j, 0, 0))],
        out_shape=[jax.ShapeDtypeStruct((geo.batch * kv, geo.tpb, depth, ATT_REP * t), BF16),
                   jax.ShapeDtypeStruct((geo.batch * kv, geo.lt, depth), BF16),
                   jax.ShapeDtypeStruct((geo.batch * kv, geo.tpb, VT_ROWS, t), BF16)],
        compiler_params=_cparams(("parallel", "parallel")),
        name="qk_prep",
    )(proj, proj, proj, cos_t, sin_t, g_q, g_k, shift_row)


def _attn_kernel(*refs, n_str):
    qt_refs = refs[0:n_str]
    k_ref, vt_ref = refs[n_str:n_str + 2]
    gate_refs = refs[n_str + 2:2 * n_str + 2]
    o_ref, m_ref, acc_ref, s_ref, mx_ref = refs[2 * n_str + 2:]
    vt_tile = vt_ref.shape[2]
    tk = s_ref.shape[1]
    tiles = tk // vt_tile
    n_chunks = k_ref.shape[0] // tk
    tq = o_ref.shape[0] // n_str
    m_ref[...] = jnp.full(m_ref.shape, -jnp.inf, F32)
    acc_ref[...] = jnp.zeros(acc_ref.shape, F32)

    def scores(c, parity):
        k = k_ref[pl.ds(pl.multiple_of(c * tk, tk), tk), :]
        for st in range(n_str):
            s = jnp.dot(k, qt_refs[st][...], preferred_element_type=F32)
            s_ref[2 * st + parity] = s
            mx_ref[2 * st + parity] = jnp.max(s, axis=0, keepdims=True)

    def absorb(c, parity):
        for st in range(n_str):
            buf = 2 * st + parity
            m_old = m_ref[st]
            m_new = jnp.maximum(m_old, mx_ref[buf])
            alpha = jnp.exp2(m_old - m_new)
            pv = None
            for t in range(tiles):
                p = jnp.exp2((s_ref[buf, t * vt_tile:(t + 1) * vt_tile, :] - m_new).astype(BF16))
                part = jnp.dot(vt_ref[c * tiles + t], p, preferred_element_type=F32)
                pv = part if pv is None else pv + part
            acc_ref[st] = alpha * acc_ref[st] + pv
            m_ref[st] = m_new

    def group(i, carry):
        c0 = ATTN_UNROLL * i
        for u in range(ATTN_UNROLL):
            scores(c0 + u + 1, (u + 1) % 2)
            absorb(c0 + u, u % 2)
        return carry

    scores(0, 0)
    n_groups = (n_chunks - 1) // ATTN_UNROLL
    if n_groups > 0:
        lax.fori_loop(0, n_groups, group, 0)
    for c in range(n_groups * ATTN_UNROLL, n_chunks):
        if c + 1 < n_chunks:
            scores(c + 1, (c + 1) % 2)
        absorb(c, c % 2)
    _attn_epilogue(acc_ref, gate_refs, o_ref, n_str)


def _attn_epilogue(acc_ref, gate_refs, o_ref, n_str):
    tq = o_ref.shape[0] // n_str
    for st in range(n_str):
        o_t = acc_ref[st, 0:HEAD_DIM, :] / acc_ref[st, HEAD_DIM:HEAD_DIM + 1, :]
        o = jnp.concatenate([o_t[:, r * tq:(r + 1) * tq].T for r in range(ATT_REP)], axis=1)
        o_ref[st * tq:(st + 1) * tq, :] = (o * _silu(gate_refs[st][...].astype(F32))).astype(o_ref.dtype)


def _attn_bounded_kernel(*refs, n_str, tk):
    qt_refs = refs[0:n_str]
    k_ref, vt_ref = refs[n_str:n_str + 2]
    gate_refs = refs[n_str + 2:2 * n_str + 2]
    o_ref, acc_ref = refs[2 * n_str + 2:]
    vt_tile = vt_ref.shape[2]
    tiles = tk // vt_tile
    n_chunks = k_ref.shape[0] // tk
    acc_ref[...] = jnp.zeros(acc_ref.shape, F32)

    def chunk(c):
        k = k_ref[pl.ds(pl.multiple_of(c * tk, tk), tk), :]
        for st in range(n_str):
            p = jnp.exp2(jnp.dot(k, qt_refs[st][...], preferred_element_type=F32).astype(BF16))
            pv = None
            for t in range(tiles):
                part = jnp.dot(vt_ref[c * tiles + t, 0:HEAD_DIM, :], p[t * vt_tile:(t + 1) * vt_tile, :],
                               preferred_element_type=F32)
                pv = part if pv is None else pv + part
            acc_ref[st, 0:HEAD_DIM, :] += pv
            acc_ref[st, HEAD_DIM:HEAD_DIM + 1, :] += jnp.sum(p.astype(F32), axis=0, keepdims=True)

    def group(i, carry):
        for u in range(ATTN_BOUNDED_UNROLL):
            chunk(ATTN_BOUNDED_UNROLL * i + u)
        return carry

    n_groups = n_chunks // ATTN_BOUNDED_UNROLL
    if n_groups > 0:
        lax.fori_loop(0, n_groups, group, 0)
    for c in range(n_groups * ATTN_BOUNDED_UNROLL, n_chunks):
        chunk(c)
    _attn_epilogue(acc_ref, gate_refs, o_ref, n_str)


def _attention(qt, kh, vt, proj, geo, q_first_row, q_rows, kv_rows, bounded):
    tq = ROW_TILE
    assert q_first_row % tq == 0 and q_rows % tq == 0 and kv_rows % ROW_TILE == 0
    q0, n_tiles = q_first_row // tq, q_rows // tq
    gw = ATT_REP * HEAD_DIM
    mq = ATT_REP * tq
    depth = qt.shape[2]
    head = lambda b, g: b * ATT_KV_HEADS + g

    def build(kernel_fn, streams, scratch, name):
        n_str = streams if n_tiles % streams == 0 else 1
        tile = lambda i, st: q0 + n_str * i + st
        stream_specs = lambda make: [make(st) for st in range(n_str)]
        call = pl.pallas_call(
            functools.partial(kernel_fn, n_str=n_str),
            grid=(geo.batch, ATT_KV_HEADS, n_tiles // n_str),
            in_specs=[
                *stream_specs(lambda st: pl.BlockSpec((None, None, depth, mq),
                                                      lambda b, g, i: (head(b, g), tile(i, st), 0, 0))),
                pl.BlockSpec((None, kv_rows, depth), lambda b, g, i: (head(b, g), 0, 0)),
                pl.BlockSpec((None, kv_rows // ROW_TILE, VT_ROWS, ROW_TILE), lambda b, g, i: (head(b, g), 0, 0, 0)),
                *stream_specs(lambda st: pl.BlockSpec((tq, gw), lambda b, g, i: (b * geo.tpb + tile(i, st),
                                                                                 _COL["A_G"] // gw + g))),
            ],
            out_specs=pl.BlockSpec((n_str * tq, gw), lambda b, g, i: (b * (n_tiles // n_str) + i, g)),
            out_shape=jax.ShapeDtypeStruct((geo.batch * q_rows, BRANCH_WIDTH), BF16),
            scratch_shapes=scratch(n_str),
            compiler_params=_cparams(("parallel", "parallel", "arbitrary")),
            name=name)
        return lambda: call(*([qt] * n_str), kh, vt, *([proj] * n_str))

    tk = max(t for t in ATTN_KEY_CHUNKS if kv_rows % t == 0)
    online = build(_attn_kernel, ATTN_STREAMS,
                   lambda n: [pltpu.VMEM((n, 1, mq), F32), pltpu.VMEM((n, VT_ROWS, mq), F32),
                              pltpu.VMEM((2 * n, tk, mq), F32), pltpu.VMEM((2 * n, 1, mq), F32)],
                   "attention_online")
    tkb = max(t for t in ATTN_BOUNDED_KEY_CHUNKS if kv_rows % t == 0)
    fast = build(functools.partial(_attn_bounded_kernel, tk=tkb), ATTN_BOUNDED_STREAMS,
                 lambda n: [pltpu.VMEM((n, VT_ROWS, mq), F32)], "attention")
    return lax.cond(bounded, fast, online)


def _halo_specs(width, col_block, geo, first, row):
    per = ROW_TILE // SUBLANES
    last = geo.rows // SUBLANES - 1
    prev = pl.BlockSpec((SUBLANES, width), lambda b, j: (jnp.maximum(row(b, j) * per - 1, 0), col_block))
    nxt = pl.BlockSpec((SUBLANES, width), lambda b, j: (jnp.minimum((row(b, j) + 1) * per, last), col_block))
    return prev, nxt


def _fill_ext(ext_ref, cur, prev, nxt, j, geo):
    t = cur.shape[0]
    seg_first = jnp.logical_or(j == 0, j == geo.ctx_tiles)
    seg_last = jnp.logical_or(j == geo.ctx_tiles - 1, j == geo.tpb - 1)
    ext_ref[0:SUBLANES, :] = jnp.where(seg_first, 0.0, prev)
    ext_ref[SUBLANES:SUBLANES + t, :] = cur
    ext_ref[SUBLANES + t:2 * SUBLANES + t, :] = jnp.where(seg_last, 0.0, nxt)


def _ssd_conv_kernel(cur_ref, prev_ref, next_ref, nar_ref, w_ref, b_ref, dtb_ref, xbc_ref, dt_ref, ext_ref, *, geo):
    j = pl.program_id(1)
    _fill_ext(ext_ref, cur_ref[...].astype(F32), prev_ref[...].astype(F32), next_ref[...].astype(F32), j, geo)
    t = cur_ref.shape[0]
    acc = jnp.zeros(cur_ref.shape, F32) + b_ref[...]
    for k in range(SSD_CONV):
        acc = acc + w_ref[k:k + 1, :] * ext_ref[SUBLANES + k - SSD_CONV // 2:SUBLANES + k - SSD_CONV // 2 + t, :]
    xbc_ref[...] = _silu(acc).astype(xbc_ref.dtype)
    lane = lax.broadcasted_iota(jnp.int32, dt_ref.shape, 1)
    dt = _softplus(nar_ref[...].astype(F32) + dtb_ref[...])
    dt_ref[...] = jnp.where(lane < 2 * SSD_HEADS, dt, 0.0)


def _ssd_conv(proj, conv_w, conv_b, dt_bias_row, geo):
    width = BRANCH_WIDTH + 2 * SSD_GROUPS * SSD_STATE
    row = lambda b, j: b * geo.tpb + j
    cb = _COL["B_X"] // width
    prev, nxt = _halo_specs(width, cb, geo, 0, row)
    return pl.pallas_call(
        functools.partial(_ssd_conv_kernel, geo=geo),
        grid=(geo.batch, geo.tpb),
        in_specs=[pl.BlockSpec((ROW_TILE, width), lambda b, j: (row(b, j), cb)), prev, nxt,
                  pl.BlockSpec((ROW_TILE, LANES), lambda b, j: (row(b, j), _COL["NARROW"] // LANES)),
                  pl.BlockSpec((SSD_CONV, width), lambda b, j: (0, 0)),
                  pl.BlockSpec((1, width), lambda b, j: (0, 0)),
                  pl.BlockSpec((1, LANES), lambda b, j: (0, 0))],
        out_specs=[pl.BlockSpec((ROW_TILE, width), lambda b, j: (row(b, j), 0)),
                   pl.BlockSpec((ROW_TILE, LANES), lambda b, j: (row(b, j), 0))],
        out_shape=[jax.ShapeDtypeStruct((geo.rows, width), BF16),
                   jax.ShapeDtypeStruct((geo.rows, LANES), F32)],
        scratch_shapes=[pltpu.VMEM((ROW_TILE + 2 * SUBLANES, width), F32)],
        compiler_params=_cparams(("parallel", "parallel")),
        name="ssd_conv",
    )(proj, proj, proj, proj, conv_w, conv_b, dt_bias_row)


def _scan_chunk(s, n_chunks, n_ctx_chunks, reverse):
    if not reverse:
        return s
    return jnp.where(s < n_ctx_chunks, n_ctx_chunks - 1 - s, n_chunks + n_ctx_chunks - 1 - s)


def _tri(n, reverse):
    t = lax.broadcasted_iota(jnp.int32, (n, n), 0)
    s = lax.broadcasted_iota(jnp.int32, (n, n), 1)
    return (s >= t) if reverse else (s <= t)


def _split_bf16(x, pieces):
    out, rest = [], x
    for _ in range(pieces):
        p = rest.astype(BF16)
        out.append(p)
        rest = rest - p.astype(F32)
    return out


def _cumsum_rows(mask_bf16, x):
    w = x.shape[1]
    parts = jnp.dot(mask_bf16, jnp.concatenate(_split_bf16(x, 3), axis=1), preferred_element_type=F32)
    return parts[:, 0:w] + parts[:, w:2 * w] + parts[:, 2 * w:3 * w]


def _ssd_scan_kernel(xbc_ref, dt_ref, alog_ref, e_ref, *rest, reverse):
    if reverse:
        yf_ref, z_ref, skip_ref, g_ref, y_ref, h_ref = rest
    else:
        y_ref, h_ref = rest
    q = SSD_CHUNK
    gw = SSD_HPG * SSD_HEAD_DIM
    quad = 4
    qw = quad * SSD_HEAD_DIM
    lane0 = SSD_HEADS if reverse else 0
    n_batch, n_sub = xbc_ref.shape[0], xbc_ref.shape[1] // q

    @pl.when(pl.program_id(0) == 0)
    def _():
        h_ref[...] = jnp.zeros_like(h_ref)

    lane = lax.broadcasted_iota(jnp.int32, (1, LANES), 1)
    a = jnp.where(lane < 2 * SSD_HEADS, -jnp.exp(alog_ref[...]), 0.0)
    mask = _tri(q, reverse)
    mask_bf = jnp.where(mask, 1.0, 0.0).astype(BF16)
    e = e_ref[...]
    head_of_lane = lax.broadcasted_iota(jnp.int32, (1, qw), 1) // SSD_HEAD_DIM
    zero_bf = jnp.zeros((), BF16)

    order = range(n_sub - 1, -1, -1) if reverse else range(n_sub)
    for ci, b in [(ci, b) for ci in order for b in range(n_batch)]:
        rows = slice(ci * q, (ci + 1) * q)
        dt = dt_ref[b, rows, :]
        cs = _cumsum_rows(mask_bf, dt * a)
        cs_last = cs[0:1, :] if reverse else cs[q - 1:q, :]
        zeros = jnp.zeros((q, LANES), BF16)
        ecs_hi, ecs_lo = _split_bf16(jnp.exp(cs), 2)
        cd_hi, cd_mid, cd_lo = _split_bf16(jnp.broadcast_to(jnp.exp(cs_last), (BF16_SUBLANES, LANES)), 3)
        stack = jnp.concatenate([
            jnp.concatenate([(jnp.exp(cs_last - cs) * dt).astype(BF16), zeros], axis=1),
            jnp.concatenate([ecs_hi, ecs_lo], axis=1),
            jnp.concatenate([cd_hi, cd_mid], axis=1),
            jnp.concatenate([cd_lo, zeros[0:BF16_SUBLANES]], axis=1)], axis=0)
        big = jnp.dot(stack, e, preferred_element_type=F32)
        x_bf = xbc_ref[b, rows, 0:BRANCH_WIDTH]
        x = x_bf.astype(F32)
        wx = (big[0:q] * x).astype(BF16)
        ecs = big[q:2 * q]
        r0 = 2 * q
        chunk_decay = big[r0:r0 + 1] + big[r0 + BF16_SUBLANES:r0 + BF16_SUBLANES + 1]
        cs_t = cs.T
        dt_t = dt.T
        pieces = []
        for g in range(SSD_GROUPS):
            bm = xbc_ref[b, rows, BRANCH_WIDTH + g * SSD_STATE:BRANCH_WIDTH + (g + 1) * SSD_STATE]
            c0 = BRANCH_WIDTH + SSD_GROUPS * SSD_STATE + g * SSD_STATE
            cm = xbc_ref[b, rows, c0:c0 + SSD_STATE]
            cb = lax.dot_general(cm, bm, (((1,), (1,)), ((), ())), preferred_element_type=F32)
            h_in = h_ref[b, g]
            y_off = jnp.dot(cm, h_in.astype(BF16), preferred_element_type=F32) * ecs[:, g * gw:(g + 1) * gw]
            states = jnp.dot(bm.astype(F32).T.astype(BF16), wx[:, g * gw:(g + 1) * gw], preferred_element_type=F32)
            h_ref[b, g] = chunk_decay[:, g * gw:(g + 1) * gw] * h_in + states
            for hq in range(SSD_HPG // quad):
                ms = []
                for r in range(quad):
                    col = lane0 + g * SSD_HPG + hq * quad + r
                    seg = cs[:, col:col + 1] - cs_t[col:col + 1, :]
                    ms.append((cb * jnp.exp(jnp.where(mask, seg, -1e30)) * dt_t[col:col + 1, :]).astype(BF16))
                lo = g * gw + hq * qw
                slab = x_bf[:, lo:lo + qw]
                rhs = jnp.concatenate([jnp.where(head_of_lane == r, slab, zero_bf) for r in range(quad)], axis=0)
                y_diag = jnp.dot(jnp.concatenate(ms, axis=1), rhs, preferred_element_type=F32)
                pieces.append(y_diag + y_off[:, hq * qw:(hq + 1) * qw])
        y = jnp.concatenate(pieces, axis=1)
        if reverse:
            y = (yf_ref[b, rows, :] + y + skip_ref[...] * x) * _silu(z_ref[b, rows, :].astype(F32))
            y = y * lax.rsqrt(jnp.mean(y * y, axis=-1, keepdims=True) + NORM_EPS) * g_ref[...]
        y_ref[b, rows, :] = y.astype(y_ref.dtype)


def _ssd_scan(xbc, dt, a_log_row, expand_mat, geo, reverse, finish=None):
    nb, nbc = geo.lt // SCAN_BLOCK, geo.ctx_len // SCAN_BLOCK
    width, w, nbat = xbc.shape[1], BRANCH_WIDTH, geo.batch
    blk = lambda s: _scan_chunk(s, nb, nbc, reverse)
    per_batch = lambda a: a.reshape(nbat, geo.lt, a.shape[1])
    tok_spec = lambda cols, col_block: pl.BlockSpec((nbat, SCAN_BLOCK, cols), lambda s: (0, blk(s), col_block))
    const_spec = lambda shape: pl.BlockSpec(shape, lambda s: (0, 0))
    in_specs = [tok_spec(width, 0), tok_spec(LANES, 0), const_spec((1, LANES)), const_spec((2 * LANES, w))]
    args = [per_batch(xbc), per_batch(dt), a_log_row, expand_mat]
    if reverse:
        yf, proj, skip_row, norm_g = finish
        in_specs += [tok_spec(w, 0), tok_spec(w, _COL["B_Z"] // w), const_spec((1, w)), const_spec((1, w))]
        args += [per_batch(yf), per_batch(proj), skip_row, norm_g]
    out = pl.pallas_call(
        functools.partial(_ssd_scan_kernel, reverse=reverse),
        grid=(nb,),
        in_specs=in_specs,
        out_specs=tok_spec(w, 0),
        out_shape=jax.ShapeDtypeStruct((nbat, geo.lt, w), BF16 if reverse else F32),
        scratch_shapes=[pltpu.VMEM((nbat, SSD_GROUPS, SSD_STATE, SSD_HPG * SSD_HEAD_DIM), F32)],
        compiler_params=_cparams(("arbitrary",)),
        name="ssd_scan_bwd" if reverse else "ssd_scan_fwd",
    )(*args)
    return out.reshape(geo.rows, w)


def _gla_scan_kernel(q_ref, k_ref, v_ref, nar_ref, w2_ref, bf_ref, *rest, reverse):
    if reverse:
        of_ref, gate_ref, g_ref, o_ref, h_ref = rest
    else:
        o_ref, h_ref = rest
    n = GLA_CHUNK

    @pl.when(pl.program_id(1) == 0)
    def _():
        h_ref[...] = jnp.zeros_like(h_ref)

    t = q_ref.shape[0]
    n_sub = t // n
    ri = lax.broadcasted_iota(jnp.int32, (t, t), 0)
    ci = lax.broadcasted_iota(jnp.int32, (t, t), 1)
    mask = jnp.logical_and(ri // n == ci // n, (ci >= ri) if reverse else (ci <= ri))
    logit = jnp.dot(nar_ref[...], w2_ref[...], preferred_element_type=F32) + bf_ref[...]
    gl = _log_sigmoid(logit) / GLA_TAU
    b = _cumsum_rows(jnp.where(mask, 1.0, 0.0).astype(BF16), gl)
    last = [b[c * n:c * n + 1, :] if reverse else b[(c + 1) * n - 1:(c + 1) * n, :] for c in range(n_sub)]
    b_last = jnp.concatenate([jnp.broadcast_to(r, (n, r.shape[1])) for r in last], axis=0)
    q = q_ref[...].astype(F32) * (GLA_DK ** -0.5)
    k = k_ref[...].astype(F32)
    qe = (q * jnp.exp(b)).astype(BF16)
    ke = (k * jnp.exp(-b)).astype(BF16)
    kd = (k * jnp.exp(b_last - b)).astype(BF16)
    for h in range(GLA_HEADS):
        ks = slice(h * GLA_DK, (h + 1) * GLA_DK)
        vs = slice(h * GLA_DV, (h + 1) * GLA_DV)
        v = v_ref[:, vs]
        att = lax.dot_general(qe[:, ks], ke[:, ks], (((1,), (1,)), ((), ())), preferred_element_type=F32)
        o_intra = jnp.dot(jnp.where(mask, att, 0.0).astype(BF16), v, preferred_element_type=F32)
        o_inter = [None] * n_sub
        for c in (range(n_sub - 1, -1, -1) if reverse else range(n_sub)):
            rows = slice(c * n, (c + 1) * n)
            h_in = h_ref[h]
            o_inter[c] = lax.dot_general(qe[rows, ks], h_in.astype(BF16), (((1,), (1,)), ((), ())),
                                         preferred_element_type=F32)
            upd = jnp.dot(v[rows].astype(F32).T.astype(BF16), kd[rows, ks], preferred_element_type=F32)
            h_ref[h] = jnp.exp(last[c][:, ks]) * h_in + upd
        o = o_intra + jnp.concatenate(o_inter, axis=0)
        if reverse:
            o = o + of_ref[:, vs]
            o = o * lax.rsqrt(jnp.mean(o * o, axis=-1, keepdims=True) + NORM_EPS) * g_ref[...]
            o = o * _silu(gate_ref[:, vs].astype(F32))
        o_ref[:, vs] = o.astype(o_ref.dtype)


def _gla_scan(proj, w2, b_f, geo, reverse, finish=None):
    nb, nbc = geo.lt // SCAN_BLOCK, geo.ctx_len // SCAN_BLOCK
    row = lambda b, s: b * nb + _scan_chunk(s, nb, nbc, reverse)
    kw, vw = GLA_HEADS * GLA_DK, GLA_HEADS * GLA_DV
    in_specs = [pl.BlockSpec((SCAN_BLOCK, kw), lambda b, s: (row(b, s), _COL["C_Q"] // kw)),
                pl.BlockSpec((SCAN_BLOCK, kw), lambda b, s: (row(b, s), _COL["C_K"] // kw)),
                pl.BlockSpec((SCAN_BLOCK, vw), lambda b, s: (row(b, s), _COL["C_V"] // vw)),
                pl.BlockSpec((SCAN_BLOCK, LANES), lambda b, s: (row(b, s), _COL["NARROW"] // LANES)),
                pl.BlockSpec((LANES, kw), lambda b, s: (0, 0)),
                pl.BlockSpec((1, kw), lambda b, s: (0, 0))]
    args = [proj, proj, proj, proj, w2, b_f]
    if reverse:
        of, norm_g = finish
        in_specs += [pl.BlockSpec((SCAN_BLOCK, vw), lambda b, s: (row(b, s), 0)),
                     pl.BlockSpec((SCAN_BLOCK, vw), lambda b, s: (row(b, s), _COL["C_G"] // vw)),
                     pl.BlockSpec((1, GLA_DV), lambda b, s: (0, 0))]
        args += [of, proj, norm_g]
    return pl.pallas_call(
        functools.partial(_gla_scan_kernel, reverse=reverse),
        grid=(geo.batch, nb),
        in_specs=in_specs,
        out_specs=pl.BlockSpec((SCAN_BLOCK, vw), lambda b, s: (row(b, s), 0)),
        out_shape=jax.ShapeDtypeStruct((geo.rows, vw), BF16 if reverse else F32),
        scratch_shapes=[pltpu.VMEM((GLA_HEADS, GLA_DV, GLA_DK), F32)],
        compiler_params=_cparams(("parallel", "arbitrary")),
        name="gla_scan_bwd" if reverse else "gla_scan_fwd",
    )(*args)


def _shortconv_tile(cur_ref, prev_ref, next_ref, w_ref, ext_ref, tile, geo):
    w = BRANCH_WIDTH
    u = lambda ref: ref[:, w:2 * w].astype(F32) * ref[:, 2 * w:3 * w].astype(F32)
    _fill_ext(ext_ref, u(cur_ref), u(prev_ref), u(next_ref), tile, geo)
    t = cur_ref.shape[0]
    acc = jnp.zeros((t, w), F32)
    for k in range(SC_CONV):
        acc = acc + w_ref[k:k + 1, :] * ext_ref[SUBLANES + k - SC_CONV // 2:SUBLANES + k - SC_CONV // 2 + t, :]
    return cur_ref[:, 0:w].astype(F32) * acc * _silu(cur_ref[:, 3 * w:4 * w].astype(F32))


def _merge_out_kernel(*refs, n_ya, n_src, geo, first, with_next):
    tile = first + pl.program_id(1)
    ext_ref = refs[-1]
    ya = _read_tokens(refs[0:n_ya], tile, geo)
    refs = refs[n_ya:-1]
    yd = _shortconv_tile(*refs[2:6], ext_ref, tile, geo).astype(BF16)
    ys = [ya, refs[0][...], refs[1][...], yd]
    refs = refs[6:]
    gates = refs[0:N_BRANCH]
    wb_ref, wo_ref = refs[N_BRANCH:N_BRANCH + 2]
    p = N_BRANCH + 2
    srcs = refs[p:p + n_src]
    gt_ref, gp_ref = refs[p + n_src:p + n_src + 2]
    rest = refs[p + n_src + 2:]
    m = None
    for n, (y, gate_ref) in enumerate(zip(ys, gates)):
        term = jax.nn.sigmoid(gate_ref[...].astype(F32)) * jnp.dot(y, wb_ref[n], preferred_element_type=F32)
        m = term if m is None else m + term
    out = jnp.dot(m.astype(BF16), wo_ref[...], preferred_element_type=F32)
    y = out * lax.rsqrt(jnp.mean(out * out, axis=-1, keepdims=True) + NORM_EPS) * gp_ref[...]
    x_new = _read_tokens(srcs, tile, geo) + gt_ref[0] * y
    if with_next:
        gn_ref, shn_ref, scn_ref, o_ref, h_ref = rest
        h_ref[...] = _adaln_prenorm(x_new, gn_ref[...], shn_ref[0], scn_ref[0]).astype(h_ref.dtype)
    else:
        (o_ref,) = rest
    o_ref[...] = x_new


def _merge_out(ys, proj, w_branch, w_out, layer, tokens, mod3, g_post, geo, with_ctx, nxt=None):
    first, nt = geo.span(with_ctx)
    w, d = BRANCH_WIDTH, D_MODEL
    row = lambda b, j: b * geo.tpb + first + j
    out_rows = geo.rows if with_ctx else geo.batch * geo.seq
    out_row = row if with_ctx else (lambda b, j: b * nt + j)
    y_spec = pl.BlockSpec((ROW_TILE, w), lambda b, j: (row(b, j), 0))
    gate_specs = [pl.BlockSpec((ROW_TILE, d), functools.partial(lambda b, j, n: (row(b, j), _COL["MG"] // d + n), n=n))
                  for n in range(N_BRANCH)]
    resident = pl.Buffered(1)
    mod_spec = lambda part: pl.BlockSpec((1, 1, d), lambda b, j: (geo.mod_row(b, first + j), 0, part))
    vec_spec = pl.BlockSpec((1, d), lambda b, j: (0, 0))
    ya, yb, yg, conv_w = ys
    dw = N_BRANCH * w
    d_block = _COL["D_ALL"] // dw
    d_prev, d_next = _halo_specs(dw, d_block, geo, first, row)
    in_specs = [*_token_specs(ya, geo, first), y_spec, y_spec,
                pl.BlockSpec((ROW_TILE, dw), lambda b, j: (row(b, j), d_block)), d_prev, d_next,
                pl.BlockSpec((SC_CONV, w), lambda b, j: (0, 0)), *gate_specs,
                pl.BlockSpec((None, N_BRANCH, w, d), lambda b, j: (layer, 0, 0, 0), pipeline_mode=resident),
                pl.BlockSpec((None, d, d), lambda b, j: (layer, 0, 0), pipeline_mode=resident),
                *_token_specs(tokens, geo, first), mod_spec(2), vec_spec]
    args = [*ya, yb, yg, proj, proj, proj, conv_w, proj, proj, proj, proj, w_branch, w_out, *tokens, mod3, g_post]
    out_specs = [pl.BlockSpec((ROW_TILE, d), lambda b, j: (out_row(b, j), 0))]
    out_shape = [jax.ShapeDtypeStruct((out_rows, d), F32)]
    if nxt is not None:
        assert with_ctx
        g_next, mod3_next = nxt
        in_specs += [vec_spec, mod_spec(0), mod_spec(1)]
        args += [g_next, mod3_next, mod3_next]
        out_specs.append(pl.BlockSpec((ROW_TILE, d), lambda b, j: (row(b, j), 0)))
        out_shape.append(jax.ShapeDtypeStruct((geo.rows, d), BF16))
    outs = pl.pallas_call(
        functools.partial(_merge_out_kernel, n_ya=len(ya), n_src=len(tokens), geo=geo, first=first,
                          with_next=nxt is not None),
        grid=(geo.batch, nt),
        in_specs=in_specs,
        out_specs=out_specs,
        out_shape=out_shape,
        scratch_shapes=[pltpu.VMEM((ROW_TILE + 2 * SUBLANES, w), F32)],
        compiler_params=_cparams(("parallel", "parallel")),
        name="merge_out",
    )(*args)
    return outs if nxt is not None else (outs[0], None)


def _regroup_plan():
    bw = BRANCH_WIDTH
    names = ("a_q", "a_k", "a_v", "a_g", "b_x", "b_z", "b_b", "b_c", "b_dt", "c_q", "c_k", "c_v", "c_g", "c_f",
             "d_all", "mg")
    widths = (ATT_HEADS * HEAD_DIM, ATT_KV_HEADS * HEAD_DIM, ATT_KV_HEADS * HEAD_DIM, bw,
              bw, bw, SSD_GROUPS * SSD_STATE, SSD_GROUPS * SSD_STATE, 2 * SSD_HEADS,
              GLA_HEADS * GLA_DK, GLA_HEADS * GLA_DK, GLA_HEADS * GLA_DV, bw, 2 * GLA_RANK,
              4 * bw, N_BRANCH * D_MODEL)
    src = dict(zip(names, np.concatenate([[0], np.cumsum(widths)[:-1]]).tolist()))
    wid = dict(zip(names, widths))
    dst = dict(a_q=_COL["A_Q"], a_g=_COL["A_G"], b_z=_COL["B_Z"], b_x=_COL["B_X"], b_b=_COL["B_B"], b_c=_COL["B_C"],
               a_k=_COL["A_K"], a_v=_COL["A_V"], c_v=_COL["C_V"], c_g=_COL["C_G"], c_q=_COL["C_Q"], c_k=_COL["C_K"],
               d_all=_COL["D_ALL"], mg=_COL["MG"], b_dt=_COL["NARROW"] + DT_LANE0, c_f=_COL["NARROW"] + F1_LANE0)
    return [(dst[n], src[n], wid[n], n in ("a_q", "a_k")) for n in names], sum(widths)


def _regroup_kernel(w_ref, o_ref):
    plan, _ = _regroup_plan()
    half = HEAD_DIM // 2
    for dst, src, width, deinterleave in plan:
        if deinterleave:
            for h in range(width // HEAD_DIM):
                s, d = src + h * HEAD_DIM, dst + h * HEAD_DIM
                o_ref[d:d + half, :] = w_ref[pl.ds(s, half, stride=2), :].astype(o_ref.dtype)
                o_ref[d + half:d + HEAD_DIM, :] = w_ref[pl.ds(s + 1, half, stride=2), :].astype(o_ref.dtype)
        else:
            o_ref[dst:dst + width, :] = w_ref[src:src + width, :].astype(o_ref.dtype)
    used = _COL["NARROW"] + F1_LANE0 + 2 * GLA_RANK
    o_ref[used:N_PROJ, :] = jnp.zeros((N_PROJ - used, o_ref.shape[1]), o_ref.dtype)


def _regroup_w_in(w_in, layer):
    w_t = jnp.swapaxes(w_in, 1, 2)
    _, n, k = w_t.shape
    assert n == _regroup_plan()[1]
    tk = LANES
    return pl.pallas_call(
        _regroup_kernel,
        grid=(k // tk,),
        in_specs=[pl.BlockSpec((None, n, tk), lambda i: (layer, 0, i))],
        out_specs=pl.BlockSpec((N_PROJ, tk), lambda i: (0, i)),
        out_shape=jax.ShapeDtypeStruct((N_PROJ, k), BF16),
        compiler_params=_cparams(("parallel",)),
        name="regroup_w_in",
    )(w_t)


def _deinterleave_vec(g):
    return g.reshape(HEAD_DIM // 2, 2).T.reshape(1, HEAD_DIM)


def _pad_lanes(v, lane0=0):
    return jnp.zeros((1, LANES), F32).at[0, lane0:lane0 + v.shape[0]].set(v.astype(F32))


def _head_expand_matrix(reverse):
    e = np.zeros((LANES, BRANCH_WIDTH), np.float32)
    lane0 = SSD_HEADS if reverse else 0
    for r in range(SSD_HEADS):
        e[lane0 + r, r * SSD_HEAD_DIM:(r + 1) * SSD_HEAD_DIM] = 1.0
    return jnp.asarray(np.concatenate([e, e], axis=0), BF16)


def _forget_weight(w_f2_dir, direction):
    lane0 = F1_LANE0 + direction * GLA_RANK
    return jnp.zeros((LANES, w_f2_dir.shape[1]), F32).at[lane0:lane0 + GLA_RANK].set(w_f2_dir).astype(BF16)


def kernel(x, c, ctx, c_ctx, w_mod, b_mod, g_pre, g_post, w_in, g_q, g_k, ssd_conv_w, ssd_conv_b,
           ssd_a_log, ssd_dt_bias, ssd_d, ssd_norm_g, gla_w_f2, gla_b_f, gla_norm_g, sc_conv_w,
           w_branch, w_out):
    batch, seq, d = x.shape
    ctx_len = ctx.shape[1]
    depth = w_in.shape[0]
    geo = _Geom(batch, ctx_len, seq)
    assert d == D_MODEL and batch + 1 <= SUBLANES and seq % GRID_W == 0

    cos_t, sin_t = _rope_tables(geo)
    c_rows = jnp.zeros((SUBLANES, d), F32).at[:batch].set(c).at[batch].set(c_ctx)
    tokens = (ctx.reshape(batch * ctx_len, d), x.reshape(batch * seq, d))
    e_fwd, e_bwd = _head_expand_matrix(False), _head_expand_matrix(True)
    w_branch_bf, w_out_bf = w_branch.astype(BF16), w_out.astype(BF16)
    mods =[_modulation(c_rows, w_mod, b_mod[l][None, :], l).reshape(SUBLANES, 1, 3 * d) for l in range(depth)]
    h = _prenorm(tokens, g_pre[0][None, :], mods[0], geo)

    for l in range(depth):
        need_ctx = l < depth - 1
        mod3 = mods[l]
        proj = _matmul(h, _regroup_w_in(w_in, l), PROJ_TN)

        shift = Q_SCALE * HEAD_DIM * jnp.max(jnp.abs(g_q[l])) * jnp.max(jnp.abs(g_k[l]))
        bounded = 2.0 * shift <= ATTN_SAFE_EXPONENT
        shift_row = jnp.full((1, MXU_WIDTH - HEAD_DIM), shift, F32)
        qt, kh, vt = _qk_prep(proj, cos_t, sin_t, _deinterleave_vec(g_q[l]), _deinterleave_vec(g_k[l]), shift_row, geo)
        ya = (_attention(qt, kh, vt, proj, geo, ctx_len, seq, geo.lt, bounded),)
        if need_ctx:
            ya = (_attention(qt, kh, vt, proj, geo, 0, ctx_len, ctx_len, bounded), *ya)

        xbc, dt = _ssd_conv(proj, ssd_conv_w[l], ssd_conv_b[l][None, :], _pad_lanes(ssd_dt_bias[l].reshape(-1)), geo)
        a_log_row = _pad_lanes(ssd_a_log[l].reshape(-1))
        ysf = _ssd_scan(xbc, dt, a_log_row, e_fwd, geo, False)
        skip_row = jnp.repeat(ssd_d[l], SSD_HEAD_DIM)[None, :]
        yb = _ssd_scan(xbc, dt, a_log_row, e_bwd, geo, True, (ysf, proj, skip_row, ssd_norm_g[l][None, :]))

        ogf = _gla_scan(proj, _forget_weight(gla_w_f2[l, 0], 0), gla_b_f[l, 0][None, :], geo, False)
        yg = _gla_scan(proj, _forget_weight(gla_w_f2[l, 1], 1), gla_b_f[l, 1][None, :], geo, True,
                       (ogf, gla_norm_g[l][None, :]))

        nxt = (g_pre[l + 1][None, :], mods[l + 1]) if need_ctx else None
        x_new, h = _merge_out((ya, yb, yg, sc_conv_w[l]), proj, w_branch_bf, w_out_bf, l, tokens, mod3,
                              g_post[l][None, :], geo, need_ctx, nxt)
        tokens = (x_new,)

    return x_new.reshape(batch, seq, d)
```

```python
import functools

import numpy as np
import jax
import jax.numpy as jnp
from jax import lax
from jax.experimental import pallas as pl
from jax.experimental.pallas import tpu as pltpu

F32 = jnp.float32
BF16 = jnp.bfloat16

D_MODEL = 2048
GRID_W = 64
BRANCH_WIDTH = D_MODEL // 2
N_BRANCH = 4
NORM_EPS = 1e-6
HEAD_DIM = 128
ATT_HEADS = BRANCH_WIDTH // HEAD_DIM
ATT_KV_HEADS = ATT_HEADS // 4
ATT_REP = ATT_HEADS // ATT_KV_HEADS
ROPE_THETA = 10000.0
SSD_HEAD_DIM = 64
SSD_HEADS = BRANCH_WIDTH // SSD_HEAD_DIM
SSD_GROUPS = 2
SSD_HPG = SSD_HEADS // SSD_GROUPS
SSD_STATE = 128
SSD_CONV = 5
SSD_CHUNK = 128
GLA_HEADS = 4
GLA_DV = BRANCH_WIDTH // GLA_HEADS
GLA_DK = GLA_DV // 2
GLA_RANK = 16
GLA_TAU = 16.0
GLA_CHUNK = 64
SC_CONV = 3

LANES = 128
SUBLANES = 8
ROW_TILE = 256
SCAN_BLOCK = ROW_TILE
VMEM_LIMIT = 56 * 1024 * 1024

_COL = dict(
    A_Q=0, A_G=1024, B_Z=2048, B_X=3072, B_B=4096, B_C=4352, A_K=4608, A_V=4864,
    C_V=5120, C_G=6144, C_Q=7168, C_K=7680,
    D_ALL=8192, MG=12288, NARROW=20480,
)
MXU_WIDTH = 256
N_PROJ = 20736
PROJ_TN = 2304
ATTN_UNROLL = 2
ATTN_STREAMS = 2
ATTN_BOUNDED_KEY_CHUNKS = (ROW_TILE, 3 * ROW_TILE)
ATTN_BOUNDED_UNROLL = 11
ATTN_BOUNDED_STREAMS = 2
ATTN_SAFE_EXPONENT = 100.0
ATTN_KEY_CHUNKS = (ROW_TILE, 2 * ROW_TILE, 3 * ROW_TILE)
BF16_SUBLANES = 16
VT_ROWS = HEAD_DIM + BF16_SUBLANES
Q_SCALE = HEAD_DIM ** -0.5 * float(np.log2(np.e))
DT_LANE0 = 0
F1_LANE0 = 32


def _cparams(sem, vmem=VMEM_LIMIT):
    return pltpu.CompilerParams(dimension_semantics=sem, vmem_limit_bytes=vmem)


def _silu(x):
    return x * jax.nn.sigmoid(x)


def _softplus(x):
    return jnp.maximum(x, 0.0) + jnp.log1p(jnp.exp(-jnp.abs(x)))


def _log_sigmoid(x):
    return jnp.minimum(x, 0.0) - jnp.log(1.0 + jnp.exp(-jnp.abs(x)))


def _mod_kernel(c_ref, w_ref, b_ref, o_ref):
    a = _silu(c_ref[...]).astype(BF16)
    o_ref[...] = jnp.dot(a, w_ref[...].astype(BF16), preferred_element_type=F32) + b_ref[...]


def _modulation(c_rows, w_mod, b_mod, layer):
    _, d, n = w_mod.shape
    tn = 1024
    return pl.pallas_call(
        _mod_kernel,
        grid=(n // tn,),
        in_specs=[pl.BlockSpec((SUBLANES, d), lambda j: (0, 0)),
                  pl.BlockSpec((None, d, tn), lambda j: (layer, 0, j)),
                  pl.BlockSpec((1, tn), lambda j: (0, j))],
        out_specs=pl.BlockSpec((SUBLANES, tn), lambda j: (0, j)),
        out_shape=jax.ShapeDtypeStruct((SUBLANES, n), F32),
        compiler_params=_cparams(("parallel",)),
        name="modulation",
    )(c_rows, w_mod, b_mod)


class _Geom:
    def __init__(self, batch, ctx_len, seq):
        self.batch, self.ctx_len, self.seq = batch, ctx_len, seq
        self.lt = ctx_len + seq
        self.rows = batch * self.lt
        assert ctx_len % ROW_TILE == 0 and seq % ROW_TILE == 0
        self.tpb = self.lt // ROW_TILE
        self.ctx_tiles = ctx_len // ROW_TILE
        self.lat_tiles = seq // ROW_TILE

    def span(self, with_ctx):
        return (0, self.tpb) if with_ctx else (self.ctx_tiles, self.lat_tiles)

    def mod_row(self, b, j):
        return jnp.where(j < self.ctx_tiles, self.batch, b)


def _token_specs(tokens, geo, first):
    c = tokens[0].shape[1]
    ct, lt = geo.ctx_tiles, geo.lat_tiles
    if len(tokens) == 1 and tokens[0].shape[0] == geo.rows:
        return [pl.BlockSpec((ROW_TILE, c), lambda b, j: (b * geo.tpb + first + j, 0))]
    if len(tokens) == 1:
        assert tokens[0].shape[0] == geo.batch * geo.seq and first >= ct
        return [pl.BlockSpec((ROW_TILE, c), lambda b, j: (b * lt + first + j - ct, 0))]
    return [pl.BlockSpec((ROW_TILE, c), lambda b, j: (b * ct + jnp.minimum(first + j, ct - 1), 0),
                         pipeline_mode=pl.Buffered(1)),
            pl.BlockSpec((ROW_TILE, c), lambda b, j: (b * lt + jnp.maximum(first + j - ct, 0), 0))]


def _read_tokens(refs, tile, geo):
    if len(refs) == 1:
        return refs[0][...]
    return jnp.where(tile < geo.ctx_tiles, refs[0][...], refs[1][...])


def _adaln_prenorm(x, g, sh, sc):
    y = x * lax.rsqrt(jnp.mean(x * x, axis=-1, keepdims=True) + NORM_EPS)
    return (y * g) * (1.0 + sc) + sh


def _prenorm_kernel(*refs, n_src, geo):
    g_ref, sh_ref, sc_ref, o_ref = refs[n_src:]
    x = _read_tokens(refs[:n_src], pl.program_id(1), geo)
    o_ref[...] = _adaln_prenorm(x, g_ref[...], sh_ref[0], sc_ref[0]).astype(o_ref.dtype)


def _prenorm(tokens, g_pre, mod3, geo):
    d = D_MODEL
    return pl.pallas_call(
        functools.partial(_prenorm_kernel, n_src=len(tokens), geo=geo),
        grid=(geo.batch, geo.tpb),
        in_specs=[*_token_specs(tokens, geo, 0),
                  pl.BlockSpec((1, d), lambda b, j: (0, 0)),
                  pl.BlockSpec((1, 1, d), lambda b, j: (geo.mod_row(b, j), 0, 0)),
                  pl.BlockSpec((1, 1, d), lambda b, j: (geo.mod_row(b, j), 0, 1))],
        out_specs=pl.BlockSpec((ROW_TILE, d), lambda b, j: (b * geo.tpb + j, 0)),
        out_shape=jax.ShapeDtypeStruct((geo.rows, d), BF16),
        compiler_params=_cparams(("parallel", "parallel")),
        name="prenorm",
    )(*tokens, g_pre, mod3, mod3)


def _matmul_kernel(a_ref, w_ref, o_ref):
    o_ref[...] = lax.dot_general(a_ref[...], w_ref[...], (((1,), (1,)), ((), ())),
                                 preferred_element_type=F32).astype(o_ref.dtype)


def _matmul(a, w_t, tn, out_dtype=BF16):
    m, k = a.shape
    n = w_t.shape[0]
    tm = next(t for t in (768, 512, 256) if m % t == 0)
    assert m % tm == 0 and n % tn == 0
    return pl.pallas_call(
        _matmul_kernel,
        grid=(n // tn, m // tm),
        in_specs=[pl.BlockSpec((tm, k), lambda j, i: (i, 0)),
                  pl.BlockSpec((tn, k), lambda j, i: (j, 0))],
        out_specs=pl.BlockSpec((tm, tn), lambda j, i: (i, j)),
        out_shape=jax.ShapeDtypeStruct((m, n), out_dtype),
        compiler_params=_cparams(("parallel", "parallel")),
        name="in_proj",
    )(a, w_t)


def _rope_tables(geo):
    rows = geo.seq // GRID_W
    t_row = jnp.repeat(jnp.arange(rows, dtype=F32), GRID_W)
    t_col = jnp.tile(jnp.arange(GRID_W, dtype=F32), rows)
    half = HEAD_DIM // 2
    freqs = ROPE_THETA ** (-(jnp.arange(0, half, 2, dtype=F32) / half))
    ang = jnp.concatenate([t_row[:, None] * freqs, t_col[:, None] * freqs], axis=-1)
    cos, sin = jnp.cos(ang), jnp.sin(ang)
    cos_l = jnp.concatenate([cos, cos], axis=-1)
    sin_l = jnp.concatenate([-sin, sin], axis=-1)
    cos_c = jnp.ones((geo.ctx_len, HEAD_DIM), F32)
    sin_c = jnp.zeros((geo.ctx_len, HEAD_DIM), F32)
    return jnp.concatenate([cos_c, cos_l], axis=0), jnp.concatenate([sin_c, sin_l], axis=0)


def _qk_prep_kernel(q_ref, k_ref, v_ref, cos_ref, sin_ref, gq_ref, gk_ref, shift_ref, qt_ref, ko_ref, vt_ref):
    cos, sin = cos_ref[...], sin_ref[...]
    t = q_ref.shape[0]

    ones = jnp.ones((HEAD_DIM, HEAD_DIM), BF16)

    def norm_rope(x, g):
        x = x.astype(F32)
        hi, lo = _split_bf16(x * x, 2)
        ss = jnp.dot(hi, ones, preferred_element_type=F32) + jnp.dot(lo, ones, preferred_element_type=F32)
        y = x * lax.rsqrt(ss * (1.0 / HEAD_DIM) + NORM_EPS) * g
        return y * cos + pltpu.roll(y, HEAD_DIM // 2, 1) * sin

    gq, gk = gq_ref[...], gk_ref[...]
    for h in range(ATT_HEADS):
        g, r = divmod(h, ATT_REP)
        y = norm_rope(q_ref[:, h * HEAD_DIM:(h + 1) * HEAD_DIM], gq) * Q_SCALE
        qt_ref[g, 0, 0:HEAD_DIM, r * t:(r + 1) * t] = y.astype(qt_ref.dtype).T
    extra = MXU_WIDTH - HEAD_DIM
    first_row = lax.broadcasted_iota(jnp.int32, (extra, ATT_REP * t), 0) == 0
    first_lane = lax.broadcasted_iota(jnp.int32, (t, extra), 1) == 0
    for g in range(ATT_KV_HEADS):
        hs = slice(g * HEAD_DIM, (g + 1) * HEAD_DIM)
        qt_ref[g, 0, HEAD_DIM:MXU_WIDTH, :] = jnp.where(first_row, 1.0, 0.0).astype(qt_ref.dtype)
        ko_ref[g, :, 0:HEAD_DIM] = norm_rope(k_ref[:, hs], gk).astype(ko_ref.dtype)
        ko_ref[g, :, HEAD_DIM:MXU_WIDTH] = jnp.where(first_lane, -shift_ref[...], 0.0).astype(ko_ref.dtype)
        vt_ref[g, 0, 0:HEAD_DIM, :] = v_ref[:, hs].T
        vt_ref[g, 0, HEAD_DIM:VT_ROWS, :] = jnp.ones((VT_ROWS - HEAD_DIM, t), vt_ref.dtype)


def _qk_prep(proj, cos_t, sin_t, g_q, g_k, shift_row, geo):
    row = lambda b, j: b * geo.tpb + j
    nq, nk = ATT_HEADS * HEAD_DIM, ATT_KV_HEADS * HEAD_DIM
    kv, t, depth = ATT_KV_HEADS, ROW_TILE, MXU_WIDTH
    return pl.pallas_call(
        _qk_prep_kernel,
        grid=(geo.batch, geo.tpb),
        in_specs=[pl.BlockSpec((t, nq), lambda b, j: (row(b, j), _COL["A_Q"] // nq)),
                  pl.BlockSpec((t, nk), lambda b, j: (row(b, j), _COL["A_K"] // nk)),
                  pl.BlockSpec((t, nk), lambda b, j: (row(b, j), _COL["A_V"] // nk)),
                  pl.BlockSpec((t, HEAD_DIM), lambda b, j: (j, 0)),
                  pl.BlockSpec((t, HEAD_DIM), lambda b, j: (j, 0)),
                  pl.BlockSpec((1, HEAD_DIM), lambda b, j: (0, 0)),
                  pl.BlockSpec((1, HEAD_DIM), lambda b, j: (0, 0)),
                  pl.BlockSpec((1, depth - HEAD_DIM), lambda b, j: (0, 0))],
        out_specs=[pl.BlockSpec((kv, 1, depth, ATT_REP * t), lambda b, j: (b, j, 0, 0)),
                   pl.BlockSpec((kv, t, depth), lambda b, j: (b, j, 0)),
                   pl.BlockSpec((kv, 1, VT_ROWS, t), lambda b, j: (b, j, 0, 0))],
        out_shape=[jax.ShapeDtypeStruct((geo.batch * kv, geo.tpb, depth, ATT_REP * t), BF16),
                   jax.ShapeDtypeStruct((geo.batch * kv, geo.lt, depth), BF16),
                   jax.ShapeDtypeStruct((geo.batch * kv, geo.tpb, VT_ROWS, t), BF16)],
        compiler_params=_cparams(("parallel", "parallel")),
        name="qk_prep",
    )(proj, proj, proj, cos_t, sin_t, g_q, g_k, shift_row)


def _attn_kernel(*refs, n_str):
    qt_refs = refs[0:n_str]
    k_ref, vt_ref = refs[n_str:n_str + 2]
    gate_refs = refs[n_str + 2:2 * n_str + 2]
    o_ref, m_ref, acc_ref, s_ref, mx_ref = refs[2 * n_str + 2:]
    vt_tile = vt_ref.shape[2]
    tk = s_ref.shape[1]
    tiles = tk // vt_tile
    n_chunks = k_ref.shape[0] // tk
    tq = o_ref.shape[0] // n_str
    m_ref[...] = jnp.full(m_ref.shape, -jnp.inf, F32)
    acc_ref[...] = jnp.zeros(acc_ref.shape, F32)

    def scores(c, parity):
        k = k_ref[pl.ds(pl.multiple_of(c * tk, tk), tk), :]
        for st in range(n_str):
            s = jnp.dot(k, qt_refs[st][...], preferred_element_type=F32)
            s_ref[2 * st + parity] = s
            mx_ref[2 * st + parity] = jnp.max(s, axis=0, keepdims=True)

    def absorb(c, parity):
        for st in range(n_str):
            buf = 2 * st + parity
            m_old = m_ref[st]
            m_new = jnp.maximum(m_old, mx_ref[buf])
            alpha = jnp.exp2(m_old - m_new)
            pv = None
            for t in range(tiles):
                p = jnp.exp2((s_ref[buf, t * vt_tile:(t + 1) * vt_tile, :] - m_new).astype(BF16))
                part = jnp.dot(vt_ref[c * tiles + t], p, preferred_element_type=F32)
                pv = part if pv is None else pv + part
            acc_ref[st] = alpha * acc_ref[st] + pv
            m_ref[st] = m_new

    def group(i, carry):
        c0 = ATTN_UNROLL * i
        for u in range(ATTN_UNROLL):
            scores(c0 + u + 1, (u + 1) % 2)
            absorb(c0 + u, u % 2)
        return carry

    scores(0, 0)
    n_groups = (n_chunks - 1) // ATTN_UNROLL
    if n_groups > 0:
        lax.fori_loop(0, n_groups, group, 0)
    for c in range(n_groups * ATTN_UNROLL, n_chunks):
        if c + 1 < n_chunks:
            scores(c + 1, (c + 1) % 2)
        absorb(c, c % 2)
    _attn_epilogue(acc_ref, gate_refs, o_ref, n_str)


def _attn_epilogue(acc_ref, gate_refs, o_ref, n_str):
    tq = o_ref.shape[0] // n_str
    for st in range(n_str):
        o_t = acc_ref[st, 0:HEAD_DIM, :] / acc_ref[st, HEAD_DIM:HEAD_DIM + 1, :]
        o = jnp.concatenate([o_t[:, r * tq:(r + 1) * tq].T for r in range(ATT_REP)], axis=1)
        o_ref[st * tq:(st + 1) * tq, :] = (o * _silu(gate_refs[st][...].astype(F32))).astype(o_ref.dtype)


def _attn_bounded_kernel(*refs, n_str, tk):
    qt_refs = refs[0:n_str]
    k_ref, vt_ref = refs[n_str:n_str + 2]
    gate_refs = refs[n_str + 2:2 * n_str + 2]
    o_ref, acc_ref = refs[2 * n_str + 2:]
    vt_tile = vt_ref.shape[2]
    tiles = tk // vt_tile
    n_chunks = k_ref.shape[0] // tk
    acc_ref[...] = jnp.zeros(acc_ref.shape, F32)

    def chunk(c):
        k = k_ref[pl.ds(pl.multiple_of(c * tk, tk), tk), :]
        for st in range(n_str):
            p = jnp.exp2(jnp.dot(k, qt_refs[st][...], preferred_element_type=F32).astype(BF16))
            pv = None
            for t in range(tiles):
                part = jnp.dot(vt_ref[c * tiles + t, 0:HEAD_DIM, :], p[t * vt_tile:(t + 1) * vt_tile, :],
                               preferred_element_type=F32)
                pv = part if pv is None else pv + part
            acc_ref[st, 0:HEAD_DIM, :] += pv
            acc_ref[st, HEAD_DIM:HEAD_DIM + 1, :] += jnp.sum(p.astype(F32), axis=0, keepdims=True)

    def group(i, carry):
        for u in range(ATTN_BOUNDED_UNROLL):
            chunk(ATTN_BOUNDED_UNROLL * i + u)
        return carry

    n_groups = n_chunks // ATTN_BOUNDED_UNROLL
    if n_groups > 0:
        lax.fori_loop(0, n_groups, group, 0)
    for c in range(n_groups * ATTN_BOUNDED_UNROLL, n_chunks):
        chunk(c)
    _attn_epilogue(acc_ref, gate_refs, o_ref, n_str)


def _attention(qt, kh, vt, proj, geo, q_first_row, q_rows, kv_rows, bounded):
    tq = ROW_TILE
    assert q_first_row % tq == 0 and q_rows % tq == 0 and kv_rows % ROW_TILE == 0
    q0, n_tiles = q_first_row // tq, q_rows // tq
    gw = ATT_REP * HEAD_DIM
    mq = ATT_REP * tq
    depth = qt.shape[2]
    head = lambda b, g: b * ATT_KV_HEADS + g

    def build(kernel_fn, streams, scratch, name):
        n_str = streams if n_tiles % streams == 0 else 1
        tile = lambda i, st: q0 + n_str * i + st
        stream_specs = lambda make: [make(st) for st in range(n_str)]
        call = pl.pallas_call(
            functools.partial(kernel_fn, n_str=n_str),
            grid=(geo.batch, ATT_KV_HEADS, n_tiles // n_str),
            in_specs=[
                *stream_specs(lambda st: pl.BlockSpec((None, None, depth, mq),
                                                      lambda b, g, i: (head(b, g), tile(i, st), 0, 0))),
                pl.BlockSpec((None, kv_rows, depth), lambda b, g, i: (head(b, g), 0, 0)),
                pl.BlockSpec((None, kv_rows // ROW_TILE, VT_ROWS, ROW_TILE), lambda b, g, i: (head(b, g), 0, 0, 0)),
                *stream_specs(lambda st: pl.BlockSpec((tq, gw), lambda b, g, i: (b * geo.tpb + tile(i, st),
                                                                                 _COL["A_G"] // gw + g))),
            ],
            out_specs=pl.BlockSpec((n_str * tq, gw), lambda b, g, i: (b * (n_tiles // n_str) + i, g)),
            out_shape=jax.ShapeDtypeStruct((geo.batch * q_rows, BRANCH_WIDTH), BF16),
            scratch_shapes=scratch(n_str),
            compiler_params=_cparams(("parallel", "parallel", "arbitrary")),
            name=name)
        return lambda: call(*([qt] * n_str), kh, vt, *([proj] * n_str))

    tk = max(t for t in ATTN_KEY_CHUNKS if kv_rows % t == 0)
    online = build(_attn_kernel, ATTN_STREAMS,
                   lambda n: [pltpu.VMEM((n, 1, mq), F32), pltpu.VMEM((n, VT_ROWS, mq), F32),
                              pltpu.VMEM((2 * n, tk, mq), F32), pltpu.VMEM((2 * n, 1, mq), F32)],
                   "attention_online")
    tkb = max(t for t in ATTN_BOUNDED_KEY_CHUNKS if kv_rows % t == 0)
    fast = build(functools.partial(_attn_bounded_kernel, tk=tkb), ATTN_BOUNDED_STREAMS,
                 lambda n: [pltpu.VMEM((n, VT_ROWS, mq), F32)], "attention")
    return lax.cond(bounded, fast, online)


def _halo_specs(width, col_block, geo, first, row):
    per = ROW_TILE // SUBLANES
    last = geo.rows // SUBLANES - 1
    prev = pl.BlockSpec((SUBLANES, width), lambda b, j: (jnp.maximum(row(b, j) * per - 1, 0), col_block))
    nxt = pl.BlockSpec((SUBLANES, width), lambda b, j: (jnp.minimum((row(b, j) + 1) * per, last), col_block))
    return prev, nxt


def _fill_ext(ext_ref, cur, prev, nxt, j, geo):
    t = cur.shape[0]
    seg_first = jnp.logical_or(j == 0, j == geo.ctx_tiles)
    seg_last = jnp.logical_or(j == geo.ctx_tiles - 1, j == geo.tpb - 1)
    ext_ref[0:SUBLANES, :] = jnp.where(seg_first, 0.0, prev)
    ext_ref[SUBLANES:SUBLANES + t, :] = cur
    ext_ref[SUBLANES + t:2 * SUBLANES + t, :] = jnp.where(seg_last, 0.0, nxt)


def _ssd_conv_kernel(cur_ref, prev_ref, next_ref, nar_ref, w_ref, b_ref, dtb_ref, shift_ref, xbc_ref, dt_ref, ext_ref,
                     *, geo):
    j = pl.program_id(1)
    cur = cur_ref[...]
    cur_f = cur.astype(F32)
    _fill_ext(ext_ref, cur_f, prev_ref[...].astype(F32), next_ref[...].astype(F32), j, geo)
    t, half = cur_ref.shape[0], SSD_CONV // 2
    acc = b_ref[...] + w_ref[half:half + 1, :] * cur_f
    taps = [k for k in range(SSD_CONV) if k != half]
    for i, k in enumerate(taps):
        acc = acc + w_ref[k:k + 1, :] * jnp.dot(shift_ref[i], cur, preferred_element_type=F32)

    def edge(r0):
        e = jnp.zeros((SUBLANES, cur_ref.shape[1]), F32) + b_ref[...]
        for k in range(SSD_CONV):
            e = e + w_ref[k:k + 1, :] * ext_ref[SUBLANES + r0 + k - half:2 * SUBLANES + r0 + k - half, :]
        return e

    acc = jnp.concatenate([edge(0), acc[SUBLANES:t - SUBLANES], edge(t - SUBLANES)], axis=0)
    xbc_ref[...] = _silu(acc).astype(xbc_ref.dtype)
    lane = lax.broadcasted_iota(jnp.int32, dt_ref.shape, 1)
    dt = _softplus(nar_ref[...].astype(F32) + dtb_ref[...])
    dt_ref[...] = jnp.where(lane < 2 * SSD_HEADS, dt, 0.0)


def _ssd_conv(proj, conv_w, conv_b, dt_bias_row, geo):
    width = BRANCH_WIDTH + 2 * SSD_GROUPS * SSD_STATE
    row = lambda b, j: b * geo.tpb + j
    cb = _COL["B_X"] // width
    prev, nxt = _halo_specs(width, cb, geo, 0, row)
    t_idx = np.arange(ROW_TILE)
    offsets = [k - SSD_CONV // 2 for k in range(SSD_CONV) if k != SSD_CONV // 2]
    shifts = jnp.asarray(np.stack([(t_idx[None, :] == t_idx[:, None] + d) for d in offsets]).astype(np.float32), BF16)
    return pl.pallas_call(
        functools.partial(_ssd_conv_kernel, geo=geo),
        grid=(geo.batch, geo.tpb),
        in_specs=[pl.BlockSpec((ROW_TILE, width), lambda b, j: (row(b, j), cb)), prev, nxt,
                  pl.BlockSpec((ROW_TILE, LANES), lambda b, j: (row(b, j), _COL["NARROW"] // LANES)),
                  pl.BlockSpec((SSD_CONV, width), lambda b, j: (0, 0)),
                  pl.BlockSpec((1, width), lambda b, j: (0, 0)),
                  pl.BlockSpec((1, LANES), lambda b, j: (0, 0)),
                  pl.BlockSpec((len(offsets), ROW_TILE, ROW_TILE), lambda b, j: (0, 0, 0))],
        out_specs=[pl.BlockSpec((ROW_TILE, width), lambda b, j: (row(b, j), 0)),
                   pl.BlockSpec((ROW_TILE, LANES), lambda b, j: (row(b, j), 0))],
        out_shape=[jax.ShapeDtypeStruct((geo.rows, width), BF16),
                   jax.ShapeDtypeStruct((geo.rows, LANES), F32)],
        scratch_shapes=[pltpu.VMEM((ROW_TILE + 2 * SUBLANES, width), F32)],
        compiler_params=_cparams(("parallel", "parallel")),
        name="ssd_conv",
    )(proj, proj, proj, proj, conv_w, conv_b, dt_bias_row, shifts)


def _scan_chunk(s, n_chunks, n_ctx_chunks, reverse):
    if not reverse:
        return s
    return jnp.where(s < n_ctx_chunks, n_ctx_chunks - 1 - s, n_chunks + n_ctx_chunks - 1 - s)


def _tri(n, reverse):
    t = lax.broadcasted_iota(jnp.int32, (n, n), 0)
    s = lax.broadcasted_iota(jnp.int32, (n, n), 1)
    return (s >= t) if reverse else (s <= t)


def _split_bf16(x, pieces):
    out, rest = [], x
    for _ in range(pieces):
        p = rest.astype(BF16)
        out.append(p)
        rest = rest - p.astype(F32)
    return out


def _cumsum_rows(mask_bf16, x):
    w = x.shape[1]
    parts = jnp.dot(mask_bf16, jnp.concatenate(_split_bf16(x, 3), axis=1), preferred_element_type=F32)
    return parts[:, 0:w] + parts[:, w:2 * w] + parts[:, 2 * w:3 * w]


def _ssd_scan_kernel(xbc_ref, dt_ref, alog_ref, e_ref, *rest, reverse):
    if reverse:
        yf_ref, z_ref, skip_ref, g_ref, y_ref, h_ref = rest
    else:
        y_ref, h_ref = rest
    q = SSD_CHUNK
    gw = SSD_HPG * SSD_HEAD_DIM
    quad = 4
    qw = quad * SSD_HEAD_DIM
    lane0 = SSD_HEADS if reverse else 0
    n_batch, n_sub = xbc_ref.shape[0], xbc_ref.shape[1] // q

    @pl.when(pl.program_id(0) == 0)
    def _():
        h_ref[...] = jnp.zeros_like(h_ref)

    lane = lax.broadcasted_iota(jnp.int32, (1, LANES), 1)
    a = jnp.where(lane < 2 * SSD_HEADS, -jnp.exp(alog_ref[...]), 0.0)
    mask = _tri(q, reverse)
    mask_bf = jnp.where(mask, 1.0, 0.0).astype(BF16)
    e = e_ref[...]
    head_of_lane = lax.broadcasted_iota(jnp.int32, (1, qw), 1) // SSD_HEAD_DIM
    zero_bf = jnp.zeros((), BF16)

    order = range(n_sub - 1, -1, -1) if reverse else range(n_sub)
    for ci, b in [(ci, b) for ci in order for b in range(n_batch)]:
        rows = slice(ci * q, (ci + 1) * q)
        dt = dt_ref[b, rows, :]
        cs = _cumsum_rows(mask_bf, dt * a)
        cs_last = cs[0:1, :] if reverse else cs[q - 1:q, :]
        zeros = jnp.zeros((q, LANES), BF16)
        ecs_hi, ecs_lo = _split_bf16(jnp.exp(cs), 2)
        cd_hi, cd_mid, cd_lo = _split_bf16(jnp.broadcast_to(jnp.exp(cs_last), (BF16_SUBLANES, LANES)), 3)
        stack = jnp.concatenate([
            jnp.concatenate([(jnp.exp(cs_last - cs) * dt).astype(BF16), zeros], axis=1),
            jnp.concatenate([ecs_hi, ecs_lo], axis=1),
            jnp.concatenate([cd_hi, cd_mid], axis=1),
            jnp.concatenate([cd_lo, zeros[0:BF16_SUBLANES]], axis=1)], axis=0)
        big = jnp.dot(stack, e, preferred_element_type=F32)
        x_bf = xbc_ref[b, rows, 0:BRANCH_WIDTH]
        x = x_bf.astype(F32)
        wx = (big[0:q] * x).astype(BF16)
        ecs = big[q:2 * q]
        r0 = 2 * q
        chunk_decay = big[r0:r0 + 1] + big[r0 + BF16_SUBLANES:r0 + BF16_SUBLANES + 1]
        cs_t = cs.T
        dt_t = dt.T
        pieces = []
        for g in range(SSD_GROUPS):
            bm = xbc_ref[b, rows, BRANCH_WIDTH + g * SSD_STATE:BRANCH_WIDTH + (g + 1) * SSD_STATE]
            c0 = BRANCH_WIDTH + SSD_GROUPS * SSD_STATE + g * SSD_STATE
            cm = xbc_ref[b, rows, c0:c0 + SSD_STATE]
            cb = lax.dot_general(cm, bm, (((1,), (1,)), ((), ())), preferred_element_type=F32)
            h_in = h_ref[b, g]
            y_off = jnp.dot(cm, h_in.astype(BF16), preferred_element_type=F32) * ecs[:, g * gw:(g + 1) * gw]
            states = jnp.dot(bm.astype(F32).T.astype(BF16), wx[:, g * gw:(g + 1) * gw], preferred_element_type=F32)
            h_ref[b, g] = chunk_decay[:, g * gw:(g + 1) * gw] * h_in + states
            for hq in range(SSD_HPG // quad):
                ms = []
                for r in range(quad):
                    col = lane0 + g * SSD_HPG + hq * quad + r
                    seg = cs[:, col:col + 1] - cs_t[col:col + 1, :]
                    ms.append((cb * jnp.exp(jnp.where(mask, seg, -1e30)) * dt_t[col:col + 1, :]).astype(BF16))
                lo = g * gw + hq * qw
                slab = x_bf[:, lo:lo + qw]
                rhs = jnp.concatenate([jnp.where(head_of_lane == r, slab, zero_bf) for r in range(quad)], axis=0)
                y_diag = jnp.dot(jnp.concatenate(ms, axis=1), rhs, preferred_element_type=F32)
                pieces.append(y_diag + y_off[:, hq * qw:(hq + 1) * qw])
        y = jnp.concatenate(pieces, axis=1)
        if reverse:
            y = (yf_ref[b, rows, :] + y + skip_ref[...] * x) * _silu(z_ref[b, rows, :].astype(F32))
            y = y * lax.rsqrt(jnp.mean(y * y, axis=-1, keepdims=True) + NORM_EPS) * g_ref[...]
        y_ref[b, rows, :] = y.astype(y_ref.dtype)


def _ssd_scan(xbc, dt, a_log_row, expand_mat, geo, reverse, finish=None):
    nb, nbc = geo.lt // SCAN_BLOCK, geo.ctx_len // SCAN_BLOCK
    width, w, nbat = xbc.shape[1], BRANCH_WIDTH, geo.batch
    blk = lambda s: _scan_chunk(s, nb, nbc, reverse)
    per_batch = lambda a: a.reshape(nbat, geo.lt, a.shape[1])
    tok_spec = lambda cols, col_block: pl.BlockSpec((nbat, SCAN_BLOCK, cols), lambda s: (0, blk(s), col_block))
    const_spec = lambda shape: pl.BlockSpec(shape, lambda s: (0, 0))
    in_specs = [tok_spec(width, 0), tok_spec(LANES, 0), const_spec((1, LANES)), const_spec((2 * LANES, w))]
    args = [per_batch(xbc), per_batch(dt), a_log_row, expand_mat]
    if reverse:
        yf, proj, skip_row, norm_g = finish
        in_specs += [tok_spec(w, 0), tok_spec(w, _COL["B_Z"] // w), const_spec((1, w)), const_spec((1, w))]
        args += [per_batch(yf), per_batch(proj), skip_row, norm_g]
    out = pl.pallas_call(
        functools.partial(_ssd_scan_kernel, reverse=reverse),
        grid=(nb,),
        in_specs=in_specs,
        out_specs=tok_spec(w, 0),
        out_shape=jax.ShapeDtypeStruct((nbat, geo.lt, w), BF16 if reverse else F32),
        scratch_shapes=[pltpu.VMEM((nbat, SSD_GROUPS, SSD_STATE, SSD_HPG * SSD_HEAD_DIM), F32)],
        compiler_params=_cparams(("arbitrary",)),
        name="ssd_scan_bwd" if reverse else "ssd_scan_fwd",
    )(*args)
    return out.reshape(geo.rows, w)


def _gla_scan_kernel(q_ref, k_ref, v_ref, nar_ref, w2_ref, bf_ref, *rest, reverse):
    if reverse:
        of_ref, gate_ref, g_ref, o_ref, h_ref = rest
    else:
        o_ref, h_ref = rest
    n = GLA_CHUNK

    @pl.when(pl.program_id(1) == 0)
    def _():
        h_ref[...] = jnp.zeros_like(h_ref)

    t = q_ref.shape[0]
    n_sub = t // n
    ri = lax.broadcasted_iota(jnp.int32, (t, t), 0)
    ci = lax.broadcasted_iota(jnp.int32, (t, t), 1)
    mask = jnp.logical_and(ri // n == ci // n, (ci >= ri) if reverse else (ci <= ri))
    logit = jnp.dot(nar_ref[...], w2_ref[...], preferred_element_type=F32) + bf_ref[...]
    gl = _log_sigmoid(logit) / GLA_TAU
    b = _cumsum_rows(jnp.where(mask, 1.0, 0.0).astype(BF16), gl)
    last = [b[c * n:c * n + 1, :] if reverse else b[(c + 1) * n - 1:(c + 1) * n, :] for c in range(n_sub)]
    b_last = jnp.concatenate([jnp.broadcast_to(r, (n, r.shape[1])) for r in last], axis=0)
    q = q_ref[...].astype(F32) * (GLA_DK ** -0.5)
    k = k_ref[...].astype(F32)
    qe = (q * jnp.exp(b)).astype(BF16)
    ke = (k * jnp.exp(-b)).astype(BF16)
    kd = (k * jnp.exp(b_last - b)).astype(BF16)
    for h in range(GLA_HEADS):
        ks = slice(h * GLA_DK, (h + 1) * GLA_DK)
        vs = slice(h * GLA_DV, (h + 1) * GLA_DV)
        v = v_ref[:, vs]
        att = lax.dot_general(qe[:, ks], ke[:, ks], (((1,), (1,)), ((), ())), preferred_element_type=F32)
        o_intra = jnp.dot(jnp.where(mask, att, 0.0).astype(BF16), v, preferred_element_type=F32)
        o_inter = [None] * n_sub
        for c in (range(n_sub - 1, -1, -1) if reverse else range(n_sub)):
            rows = slice(c * n, (c + 1) * n)
            h_in = h_ref[h]
            o_inter[c] = lax.dot_general(qe[rows, ks], h_in.astype(BF16), (((1,), (1,)), ((), ())),
                                         preferred_element_type=F32)
            upd = jnp.dot(v[rows].astype(F32).T.astype(BF16), kd[rows, ks], preferred_element_type=F32)
            h_ref[h] = jnp.exp(last[c][:, ks]) * h_in + upd
        o = o_intra + jnp.concatenate(o_inter, axis=0)
        if reverse:
            o = o + of_ref[:, vs]
            o = o * lax.rsqrt(jnp.mean(o * o, axis=-1, keepdims=True) + NORM_EPS) * g_ref[...]
            o = o * _silu(gate_ref[:, vs].astype(F32))
        o_ref[:, vs] = o.astype(o_ref.dtype)


def _gla_scan(proj, w2, b_f, geo, reverse, finish=None):
    nb, nbc = geo.lt // SCAN_BLOCK, geo.ctx_len // SCAN_BLOCK
    row = lambda b, s: b * nb + _scan_chunk(s, nb, nbc, reverse)
    kw, vw = GLA_HEADS * GLA_DK, GLA_HEADS * GLA_DV
    in_specs = [pl.BlockSpec((SCAN_BLOCK, kw), lambda b, s: (row(b, s), _COL["C_Q"] // kw)),
                pl.BlockSpec((SCAN_BLOCK, kw), lambda b, s: (row(b, s), _COL["C_K"] // kw)),
                pl.BlockSpec((SCAN_BLOCK, vw), lambda b, s: (row(b, s), _COL["C_V"] // vw)),
                pl.BlockSpec((SCAN_BLOCK, LANES), lambda b, s: (row(b, s), _COL["NARROW"] // LANES)),
                pl.BlockSpec((LANES, kw), lambda b, s: (0, 0)),
                pl.BlockSpec((1, kw), lambda b, s: (0, 0))]
    args = [proj, proj, proj, proj, w2, b_f]
    if reverse:
        of, norm_g = finish
        in_specs += [pl.BlockSpec((SCAN_BLOCK, vw), lambda b, s: (row(b, s), 0)),
                     pl.BlockSpec((SCAN_BLOCK, vw), lambda b, s: (row(b, s), _COL["C_G"] // vw)),
                     pl.BlockSpec((1, GLA_DV), lambda b, s: (0, 0))]
        args += [of, proj, norm_g]
    return pl.pallas_call(
        functools.partial(_gla_scan_kernel, reverse=reverse),
        grid=(geo.batch, nb),
        in_specs=in_specs,
        out_specs=pl.BlockSpec((SCAN_BLOCK, vw), lambda b, s: (row(b, s), 0)),
        out_shape=jax.ShapeDtypeStruct((geo.rows, vw), BF16 if reverse else F32),
        scratch_shapes=[pltpu.VMEM((GLA_HEADS, GLA_DV, GLA_DK), F32)],
        compiler_params=_cparams(("parallel", "arbitrary")),
        name="gla_scan_bwd" if reverse else "gla_scan_fwd",
    )(*args)


def _shortconv_tile(cur_ref, prev_ref, next_ref, w_ref, ext_ref, tile, geo):
    w = BRANCH_WIDTH
    u = lambda ref: ref[:, w:2 * w].astype(F32) * ref[:, 2 * w:3 * w].astype(F32)
    _fill_ext(ext_ref, u(cur_ref), u(prev_ref), u(next_ref), tile, geo)
    t = cur_ref.shape[0]
    acc = jnp.zeros((t, w), F32)
    for k in range(SC_CONV):
        acc = acc + w_ref[k:k + 1, :] * ext_ref[SUBLANES + k - SC_CONV // 2:SUBLANES + k - SC_CONV // 2 + t, :]
    return cur_ref[:, 0:w].astype(F32) * acc * _silu(cur_ref[:, 3 * w:4 * w].astype(F32))


def _merge_out_kernel(*refs, n_ya, n_src, geo, first, with_next):
    tile = first + pl.program_id(1)
    ext_ref = refs[-1]
    ya = _read_tokens(refs[0:n_ya], tile, geo)
    refs = refs[n_ya:-1]
    yd = _shortconv_tile(*refs[2:6], ext_ref, tile, geo).astype(BF16)
    ys = [ya, refs[0][...], refs[1][...], yd]
    refs = refs[6:]
    gates = refs[0:N_BRANCH]
    wb_ref, wo_ref = refs[N_BRANCH:N_BRANCH + 2]
    p = N_BRANCH + 2
    srcs = refs[p:p + n_src]
    gt_ref, gp_ref = refs[p + n_src:p + n_src + 2]
    rest = refs[p + n_src + 2:]
    m = None
    for n, (y, gate_ref) in enumerate(zip(ys, gates)):
        term = jax.nn.sigmoid(gate_ref[...].astype(F32)) * jnp.dot(y, wb_ref[n], preferred_element_type=F32)
        m = term if m is None else m + term
    out = jnp.dot(m.astype(BF16), wo_ref[...], preferred_element_type=F32)
    y = out * lax.rsqrt(jnp.mean(out * out, axis=-1, keepdims=True) + NORM_EPS) * gp_ref[...]
    x_new = _read_tokens(srcs, tile, geo) + gt_ref[0] * y
    if with_next:
        gn_ref, shn_ref, scn_ref, o_ref, h_ref = rest
        h_ref[...] = _adaln_prenorm(x_new, gn_ref[...], shn_ref[0], scn_ref[0]).astype(h_ref.dtype)
    else:
        (o_ref,) = rest
    o_ref[...] = x_new


def _merge_out(ys, proj, w_branch, w_out, layer, tokens, mod3, g_post, geo, with_ctx, nxt=None):
    first, nt = geo.span(with_ctx)
    w, d = BRANCH_WIDTH, D_MODEL
    row = lambda b, j: b * geo.tpb + first + j
    out_rows = geo.rows if with_ctx else geo.batch * geo.seq
    out_row = row if with_ctx else (lambda b, j: b * nt + j)
    y_spec = pl.BlockSpec((ROW_TILE, w), lambda b, j: (row(b, j), 0))
    gate_specs = [pl.BlockSpec((ROW_TILE, d), functools.partial(lambda b, j, n: (row(b, j), _COL["MG"] // d + n), n=n))
                  for n in range(N_BRANCH)]
    resident = pl.Buffered(1)
    mod_spec = lambda part: pl.BlockSpec((1, 1, d), lambda b, j: (geo.mod_row(b, first + j), 0, part))
    vec_spec = pl.BlockSpec((1, d), lambda b, j: (0, 0))
    ya, yb, yg, conv_w = ys
    dw = N_BRANCH * w
    d_block = _COL["D_ALL"] // dw
    d_prev, d_next = _halo_specs(dw, d_block, geo, first, row)
    in_specs = [*_token_specs(ya, geo, first), y_spec, y_spec,
                pl.BlockSpec((ROW_TILE, dw), lambda b, j: (row(b, j), d_block)), d_prev, d_next,
                pl.BlockSpec((SC_CONV, w), lambda b, j: (0, 0)), *gate_specs,
                pl.BlockSpec((None, N_BRANCH, w, d), lambda b, j: (layer, 0, 0, 0), pipeline_mode=resident),
                pl.BlockSpec((None, d, d), lambda b, j: (layer, 0, 0), pipeline_mode=resident),
                *_token_specs(tokens, geo, first), mod_spec(2), vec_spec]
    args = [*ya, yb, yg, proj, proj, proj, conv_w, proj, proj, proj, proj, w_branch, w_out, *tokens, mod3, g_post]
    out_specs = [pl.BlockSpec((ROW_TILE, d), lambda b, j: (out_row(b, j), 0))]
    out_shape = [jax.ShapeDtypeStruct((out_rows, d), F32)]
    if nxt is not None:
        assert with_ctx
        g_next, mod3_next = nxt
        in_specs += [vec_spec, mod_spec(0), mod_spec(1)]
        args += [g_next, mod3_next, mod3_next]
        out_specs.append(pl.BlockSpec((ROW_TILE, d), lambda b, j: (row(b, j), 0)))
        out_shape.append(jax.ShapeDtypeStruct((geo.rows, d), BF16))
    outs = pl.pallas_call(
        functools.partial(_merge_out_kernel, n_ya=len(ya), n_src=len(tokens), geo=geo, first=first,
                          with_next=nxt is not None),
        grid=(geo.batch, nt),
        in_specs=in_specs,
        out_specs=out_specs,
        out_shape=out_shape,
        scratch_shapes=[pltpu.VMEM((ROW_TILE + 2 * SUBLANES, w), F32)],
        compiler_params=_cparams(("parallel", "parallel")),
        name="merge_out",
    )(*args)
    return outs if nxt is not None else (outs[0], None)


def _regroup_plan():
    bw = BRANCH_WIDTH
    names = ("a_q", "a_k", "a_v", "a_g", "b_x", "b_z", "b_b", "b_c", "b_dt", "c_q", "c_k", "c_v", "c_g", "c_f",
             "d_all", "mg")
    widths = (ATT_HEADS * HEAD_DIM, ATT_KV_HEADS * HEAD_DIM, ATT_KV_HEADS * HEAD_DIM, bw,
              bw, bw, SSD_GROUPS * SSD_STATE, SSD_GROUPS * SSD_STATE, 2 * SSD_HEADS,
              GLA_HEADS * GLA_DK, GLA_HEADS * GLA_DK, GLA_HEADS * GLA_DV, bw, 2 * GLA_RANK,
              4 * bw, N_BRANCH * D_MODEL)
    src = dict(zip(names, np.concatenate([[0], np.cumsum(widths)[:-1]]).tolist()))
    wid = dict(zip(names, widths))
    dst = dict(a_q=_COL["A_Q"], a_g=_COL["A_G"], b_z=_COL["B_Z"], b_x=_COL["B_X"], b_b=_COL["B_B"], b_c=_COL["B_C"],
               a_k=_COL["A_K"], a_v=_COL["A_V"], c_v=_COL["C_V"], c_g=_COL["C_G"], c_q=_COL["C_Q"], c_k=_COL["C_K"],
               d_all=_COL["D_ALL"], mg=_COL["MG"], b_dt=_COL["NARROW"] + DT_LANE0, c_f=_COL["NARROW"] + F1_LANE0)
    return [(dst[n], src[n], wid[n], n in ("a_q", "a_k")) for n in names], sum(widths)


def _regroup_kernel(w_ref, o_ref):
    plan, _ = _regroup_plan()
    half = HEAD_DIM // 2
    for dst, src, width, deinterleave in plan:
        if deinterleave:
            for h in range(width // HEAD_DIM):
                s, d = src + h * HEAD_DIM, dst + h * HEAD_DIM
                o_ref[d:d + half, :] = w_ref[pl.ds(s, half, stride=2), :].astype(o_ref.dtype)
                o_ref[d + half:d + HEAD_DIM, :] = w_ref[pl.ds(s + 1, half, stride=2), :].astype(o_ref.dtype)
        else:
            o_ref[dst:dst + width, :] = w_ref[src:src + width, :].astype(o_ref.dtype)
    used = _COL["NARROW"] + F1_LANE0 + 2 * GLA_RANK
    o_ref[used:N_PROJ, :] = jnp.zeros((N_PROJ - used, o_ref.shape[1]), o_ref.dtype)


def _regroup_w_in(w_in, layer):
    w_t = jnp.swapaxes(w_in, 1, 2)
    _, n, k = w_t.shape
    assert n == _regroup_plan()[1]
    tk = LANES
    return pl.pallas_call(
        _regroup_kernel,
        grid=(k // tk,),
        in_specs=[pl.BlockSpec((None, n, tk), lambda i: (layer, 0, i))],
        out_specs=pl.BlockSpec((N_PROJ, tk), lambda i: (0, i)),
        out_shape=jax.ShapeDtypeStruct((N_PROJ, k), BF16),
        compiler_params=_cparams(("parallel",)),
        name="regroup_w_in",
    )(w_t)


def _deinterleave_vec(g):
    return g.reshape(HEAD_DIM // 2, 2).T.reshape(1, HEAD_DIM)


def _pad_lanes(v, lane0=0):
    return jnp.zeros((1, LANES), F32).at[0, lane0:lane0 + v.shape[0]].set(v.astype(F32))


def _head_expand_matrix(reverse):
    e = np.zeros((LANES, BRANCH_WIDTH), np.float32)
    lane0 = SSD_HEADS if reverse else 0
    for r in range(SSD_HEADS):
        e[lane0 + r, r * SSD_HEAD_DIM:(r + 1) * SSD_HEAD_DIM] = 1.0
    return jnp.asarray(np.concatenate([e, e], axis=0), BF16)


def _forget_weight(w_f2_dir, direction):
    lane0 = F1_LANE0 + direction * GLA_RANK
    return jnp.zeros((LANES, w_f2_dir.shape[1]), F32).at[lane0:lane0 + GLA_RANK].set(w_f2_dir).astype(BF16)


def kernel(x, c, ctx, c_ctx, w_mod, b_mod, g_pre, g_post, w_in, g_q, g_k, ssd_conv_w, ssd_conv_b,
           ssd_a_log, ssd_dt_bias, ssd_d, ssd_norm_g, gla_w_f2, gla_b_f, gla_norm_g, sc_conv_w,
           w_branch, w_out):
    batch, seq, d = x.shape
    ctx_len = ctx.shape[1]
    depth = w_in.shape[0]
    geo = _Geom(batch, ctx_len, seq)
    assert d == D_MODEL and batch + 1 <= SUBLANES and seq % GRID_W == 0

    cos_t, sin_t = _rope_tables(geo)
    c_rows = jnp.zeros((SUBLANES, d), F32).at[:batch].set(c).at[batch].set(c_ctx)
    tokens = (ctx.reshape(batch * ctx_len, d), x.reshape(batch * seq, d))
    e_fwd, e_bwd = _head_expand_matrix(False), _head_expand_matrix(True)
    w_branch_bf, w_out_bf = w_branch.astype(BF16), w_out.astype(BF16)
    mods =[_modulation(c_rows, w_mod, b_mod[l][None, :], l).reshape(SUBLANES, 1, 3 * d) for l in range(depth)]
    h = _prenorm(tokens, g_pre[0][None, :], mods[0], geo)

    for l in range(depth):
        need_ctx = l < depth - 1
        mod3 = mods[l]
        proj = _matmul(h, _regroup_w_in(w_in, l), PROJ_TN)

        shift = Q_SCALE * HEAD_DIM * jnp.max(jnp.abs(g_q[l])) * jnp.max(jnp.abs(g_k[l]))
        bounded = 2.0 * shift <= ATTN_SAFE_EXPONENT
        shift_row = jnp.full((1, MXU_WIDTH - HEAD_DIM), shift, F32)
        qt, kh, vt = _qk_prep(proj, cos_t, sin_t, _deinterleave_vec(g_q[l]), _deinterleave_vec(g_k[l]), shift_row, geo)
        ya = (_attention(qt, kh, vt, proj, geo, ctx_len, seq, geo.lt, bounded),)
        if need_ctx:
            ya = (_attention(qt, kh, vt, proj, geo, 0, ctx_len, ctx_len, bounded), *ya)

        xbc, dt = _ssd_conv(proj, ssd_conv_w[l], ssd_conv_b[l][None, :], _pad_lanes(ssd_dt_bias[l].reshape(-1)), geo)
        a_log_row = _pad_lanes(ssd_a_log[l].reshape(-1))
        ysf = _ssd_scan(xbc, dt, a_log_row, e_fwd, geo, False)
        skip_row = jnp.repeat(ssd_d[l], SSD_HEAD_DIM)[None, :]
        yb = _ssd_scan(xbc, dt, a_log_row, e_bwd, geo, True, (ysf, proj, skip_row, ssd_norm_g[l][None, :]))

        ogf = _gla_scan(proj, _forget_weight(gla_w_f2[l, 0], 0), gla_b_f[l, 0][None, :], geo, False)
        yg = _gla_scan(proj, _forget_weight(gla_w_f2[l, 1], 1), gla_b_f[l, 1][None, :], geo, True,
                       (ogf, gla_norm_g[l][None, :]))

        nxt = (g_pre[l + 1][None, :], mods[l + 1]) if need_ctx else None
        x_new, h = _merge_out((ya, yb, yg, sc_conv_w[l]), proj, w_branch_bf, w_out_bf, l, tokens, mod3,
                              g_post[l][None, :], geo, need_ctx, nxt)
        tokens = (x_new,)

    return x_new.reshape(batch, seq, d)
```

```python
import functools

import numpy as np
import jax
import jax.numpy as jnp
from jax import lax
from jax.experimental import pallas as pl
from jax.experimental.pallas import tpu as pltpu

F32 = jnp.float32
BF16 = jnp.bfloat16

D_MODEL = 2048
GRID_W = 64
BRANCH_WIDTH = D_MODEL // 2
N_BRANCH = 4
NORM_EPS = 1e-6
HEAD_DIM = 128
ATT_HEADS = BRANCH_WIDTH // HEAD_DIM
ATT_KV_HEADS = ATT_HEADS // 4
ATT_REP = ATT_HEADS // ATT_KV_HEADS
ROPE_THETA = 10000.0
SSD_HEAD_DIM = 64
SSD_HEADS = BRANCH_WIDTH // SSD_HEAD_DIM
SSD_GROUPS = 2
SSD_HPG = SSD_HEADS // SSD_GROUPS
SSD_STATE = 128
SSD_CONV = 5
SSD_CHUNK = 128
GLA_HEADS = 4
GLA_DV = BRANCH_WIDTH // GLA_HEADS
GLA_DK = GLA_DV // 2
GLA_RANK = 16
GLA_TAU = 16.0
GLA_CHUNK = 64
SC_CONV = 3

LANES = 128
SUBLANES = 8
ROW_TILE = 256
SCAN_BLOCK = ROW_TILE
VMEM_LIMIT = 56 * 1024 * 1024

_COL = dict(
    A_Q=0, A_G=1024, B_Z=2048, B_X=3072, B_B=4096, B_C=4352, A_K=4608, A_V=4864,
    C_V=5120, C_G=6144, C_Q=7168, C_K=7680,
    D_ALL=8192, MG=12288, NARROW=20480,
)
MXU_WIDTH = 256
N_PROJ = 20736
PROJ_TN = 2304
ATTN_UNROLL = 2
ATTN_STREAMS = 2
ATTN_BOUNDED_KEY_CHUNKS = (ROW_TILE, 3 * ROW_TILE)
ATTN_BOUNDED_UNROLL = 11
ATTN_BOUNDED_STREAMS = 2
ATTN_SAFE_EXPONENT = 100.0
ATTN_KEY_CHUNKS = (ROW_TILE, 2 * ROW_TILE, 3 * ROW_TILE)
BF16_SUBLANES = 16
VT_ROWS = HEAD_DIM + BF16_SUBLANES
Q_SCALE = HEAD_DIM ** -0.5 * float(np.log2(np.e))
DT_LANE0 = 0
F1_LANE0 = 32


def _cparams(sem, vmem=VMEM_LIMIT):
    return pltpu.CompilerParams(dimension_semantics=sem, vmem_limit_bytes=vmem)


def _silu(x):
    return x * jax.nn.sigmoid(x)


def _softplus(x):
    return jnp.maximum(x, 0.0) + jnp.log1p(jnp.exp(-jnp.abs(x)))


def _log_sigmoid(x):
    return jnp.minimum(x, 0.0) - jnp.log(1.0 + jnp.exp(-jnp.abs(x)))


def _mod_kernel(c_ref, w_ref, b_ref, o_ref):
    a = _silu(c_ref[...]).astype(BF16)
    o_ref[...] = jnp.dot(a, w_ref[...].astype(BF16), preferred_element_type=F32) + b_ref[...]


def _modulation(c_rows, w_mod, b_mod, layer):
    _, d, n = w_mod.shape
    tn = 1024
    return pl.pallas_call(
        _mod_kernel,
        grid=(n // tn,),
        in_specs=[pl.BlockSpec((SUBLANES, d), lambda j: (0, 0)),
                  pl.BlockSpec((None, d, tn), lambda j: (layer, 0, j)),
                  pl.BlockSpec((1, tn), lambda j: (0, j))],
        out_specs=pl.BlockSpec((SUBLANES, tn), lambda j: (0, j)),
        out_shape=jax.ShapeDtypeStruct((SUBLANES, n), F32),
        compiler_params=_cparams(("parallel",)),
        name="modulation",
    )(c_rows, w_mod, b_mod)


class _Geom:
    def __init__(self, batch, ctx_len, seq):
        self.batch, self.ctx_len, self.seq = batch, ctx_len, seq
        self.lt = ctx_len + seq
        self.rows = batch * self.lt
        assert ctx_len % ROW_TILE == 0 and seq % ROW_TILE == 0
        self.tpb = self.lt // ROW_TILE
        self.ctx_tiles = ctx_len // ROW_TILE
        self.lat_tiles = seq // ROW_TILE

    def span(self, with_ctx):
        return (0, self.tpb) if with_ctx else (self.ctx_tiles, self.lat_tiles)

    def mod_row(self, b, j):
        return jnp.where(j < self.ctx_tiles, self.batch, b)


def _token_specs(tokens, geo, first):
    c = tokens[0].shape[1]
    ct, lt = geo.ctx_tiles, geo.lat_tiles
    if len(tokens) == 1 and tokens[0].shape[0] == geo.rows:
        return [pl.BlockSpec((ROW_TILE, c), lambda b, j: (b * geo.tpb + first + j, 0))]
    if len(tokens) == 1:
        assert tokens[0].shape[0] == geo.batch * geo.seq and first >= ct
        return [pl.BlockSpec((ROW_TILE, c), lambda b, j: (b * lt + first + j - ct, 0))]
    return [pl.BlockSpec((ROW_TILE, c), lambda b, j: (b * ct + jnp.minimum(first + j, ct - 1), 0),
                         pipeline_mode=pl.Buffered(1)),
            pl.BlockSpec((ROW_TILE, c), lambda b, j: (b * lt + jnp.maximum(first + j - ct, 0), 0))]


def _read_tokens(refs, tile, geo):
    if len(refs) == 1:
        return refs[0][...]
    return jnp.where(tile < geo.ctx_tiles, refs[0][...], refs[1][...])


def _adaln_prenorm(x, g, sh, sc):
    y = x * lax.rsqrt(jnp.mean(x * x, axis=-1, keepdims=True) + NORM_EPS)
    return (y * g) * (1.0 + sc) + sh


def _prenorm_kernel(*refs, n_src, geo):
    g_ref, sh_ref, sc_ref, o_ref = refs[n_src:]
    x = _read_tokens(refs[:n_src], pl.program_id(1), geo)
    o_ref[...] = _adaln_prenorm(x, g_ref[...], sh_ref[0], sc_ref[0]).astype(o_ref.dtype)


def _prenorm(tokens, g_pre, mod3, geo):
    d = D_MODEL
    return pl.pallas_call(
        functools.partial(_prenorm_kernel, n_src=len(tokens), geo=geo),
        grid=(geo.batch, geo.tpb),
        in_specs=[*_token_specs(tokens, geo, 0),
                  pl.BlockSpec((1, d), lambda b, j: (0, 0)),
                  pl.BlockSpec((1, 1, d), lambda b, j: (geo.mod_row(b, j), 0, 0)),
                  pl.BlockSpec((1, 1, d), lambda b, j: (geo.mod_row(b, j), 0, 1))],
        out_specs=pl.BlockSpec((ROW_TILE, d), lambda b, j: (b * geo.tpb + j, 0)),
        out_shape=jax.ShapeDtypeStruct((geo.rows, d), BF16),
        compiler_params=_cparams(("parallel", "parallel")),
        name="prenorm",
    )(*tokens, g_pre, mod3, mod3)


def _matmul_kernel(a_ref, w_ref, o_ref):
    o_ref[...] = lax.dot_general(a_ref[...], w_ref[...], (((1,), (1,)), ((), ())),
                                 preferred_element_type=F32).astype(o_ref.dtype)


def _matmul(a, w_t, tn, out_dtype=BF16):
    m, k = a.shape
    n = w_t.shape[0]
    tm = next(t for t in (768, 512, 256) if m % t == 0)
    assert m % tm == 0 and n % tn == 0
    return pl.pallas_call(
        _matmul_kernel,
        grid=(n // tn, m // tm),
        in_specs=[pl.BlockSpec((tm, k), lambda j, i: (i, 0)),
                  pl.BlockSpec((tn, k), lambda j, i: (j, 0))],
        out_specs=pl.BlockSpec((tm, tn), lambda j, i: (i, j)),
        out_shape=jax.ShapeDtypeStruct((m, n), out_dtype),
        compiler_params=_cparams(("parallel", "parallel")),
        name="in_proj",
    )(a, w_t)


def _rope_tables(geo):
    rows = geo.seq // GRID_W
    t_row = jnp.repeat(jnp.arange(rows, dtype=F32), GRID_W)
    t_col = jnp.tile(jnp.arange(GRID_W, dtype=F32), rows)
    half = HEAD_DIM // 2
    freqs = ROPE_THETA ** (-(jnp.arange(0, half, 2, dtype=F32) / half))
    ang = jnp.concatenate([t_row[:, None] * freqs, t_col[:, None] * freqs], axis=-1)
    cos, sin = jnp.cos(ang), jnp.sin(ang)
    cos_l = jnp.concatenate([cos, cos], axis=-1)
    sin_l = jnp.concatenate([-sin, sin], axis=-1)
    cos_c = jnp.ones((geo.ctx_len, HEAD_DIM), F32)
    sin_c = jnp.zeros((geo.ctx_len, HEAD_DIM), F32)
    return jnp.concatenate([cos_c, cos_l], axis=0), jnp.concatenate([sin_c, sin_l], axis=0)


def _qk_prep_kernel(q_ref, k_ref, v_ref, cos_ref, sin_ref, gq_ref, gk_ref, shift_ref, qt_ref, ko_ref, vt_ref):
    cos, sin = cos_ref[...], sin_ref[...]
    t = q_ref.shape[0]

    ones = jnp.ones((HEAD_DIM, HEAD_DIM), BF16)

    def norm_rope(x, g):
        x = x.astype(F32)
        hi, lo = _split_bf16(x * x, 2)
        ss = jnp.dot(hi, ones, preferred_element_type=F32) + jnp.dot(lo, ones, preferred_element_type=F32)
        y = x * lax.rsqrt(ss * (1.0 / HEAD_DIM) + NORM_EPS) * g
        return y * cos + pltpu.roll(y, HEAD_DIM // 2, 1) * sin

    gq, gk = gq_ref[...], gk_ref[...]
    for h in range(ATT_HEADS):
        g, r = divmod(h, ATT_REP)
        y = norm_rope(q_ref[:, h * HEAD_DIM:(h + 1) * HEAD_DIM], gq) * Q_SCALE
        qt_ref[g, 0, 0:HEAD_DIM, r * t:(r + 1) * t] = y.astype(qt_ref.dtype).T
    extra = MXU_WIDTH - HEAD_DIM
    first_row = lax.broadcasted_iota(jnp.int32, (extra, ATT_REP * t), 0) == 0
    first_lane = lax.broadcasted_iota(jnp.int32, (t, extra), 1) == 0
    for g in range(ATT_KV_HEADS):
        hs = slice(g * HEAD_DIM, (g + 1) * HEAD_DIM)
        qt_ref[g, 0, HEAD_DIM:MXU_WIDTH, :] = jnp.where(first_row, 1.0, 0.0).astype(qt_ref.dtype)
        ko_ref[g, :, 0:HEAD_DIM] = norm_rope(k_ref[:, hs], gk).astype(ko_ref.dtype)
        ko_ref[g, :, HEAD_DIM:MXU_WIDTH] = jnp.where(first_lane, -shift_ref[...], 0.0).astype(ko_ref.dtype)
        vt_ref[g, 0, 0:HEAD_DIM, :] = v_ref[:, hs].T
        vt_ref[g, 0, HEAD_DIM:VT_ROWS, :] = jnp.ones((VT_ROWS - HEAD_DIM, t), vt_ref.dtype)


def _qk_prep(proj, cos_t, sin_t, g_q, g_k, shift_row, geo):
    row = lambda b, j: b * geo.tpb + j
    nq, nk = ATT_HEADS * HEAD_DIM, ATT_KV_HEADS * HEAD_DIM
    kv, t, depth = ATT_KV_HEADS, ROW_TILE, MXU_WIDTH
    return pl.pallas_call(
        _qk_prep_kernel,
        grid=(geo.batch, geo.tpb),
        in_specs=[pl.BlockSpec((t, nq), lambda b, j: (row(b, j), _COL["A_Q"] // nq)),
                  pl.BlockSpec((t, nk), lambda b, j: (row(b, j), _COL["A_K"] // nk)),
                  pl.BlockSpec((t, nk), lambda b, j: (row(b, j), _COL["A_V"] // nk)),
                  pl.BlockSpec((t, HEAD_DIM), lambda b, j: (j, 0)),
                  pl.BlockSpec((t, HEAD_DIM), lambda b, j: (j, 0)),
                  pl.BlockSpec((1, HEAD_DIM), lambda b, j: (0, 0)),
                  pl.BlockSpec((1, HEAD_DIM), lambda b, j: (0, 0)),
                  pl.BlockSpec((1, depth - HEAD_DIM), lambda b, j: (0, 0))],
        out_specs=[pl.BlockSpec((kv, 1, depth, ATT_REP * t), lambda b, j: (b, j, 0, 0)),
                   pl.BlockSpec((kv, t, depth), lambda b, j: (b, j, 0)),
                   pl.BlockSpec((kv, 1, VT_ROWS, t), lambda b, j: (b, j, 0, 0))],
        out_shape=[jax.ShapeDtypeStruct((geo.batch * kv, geo.tpb, depth, ATT_REP * t), BF16),
                   jax.ShapeDtypeStruct((geo.batch * kv, geo.lt, depth), BF16),
                   jax.ShapeDtypeStruct((geo.batch * kv, geo.tpb, VT_ROWS, t), BF16)],
        compiler_params=_cparams(("parallel", "parallel")),
        name="qk_prep",
    )(proj, proj, proj, cos_t, sin_t, g_q, g_k, shift_row)


def _attn_kernel(*refs, n_str):
    qt_refs = refs[0:n_str]
    k_ref, vt_ref = refs[n_str:n_str + 2]
    gate_refs = refs[n_str + 2:2 * n_str + 2]
    o_ref, m_ref, acc_ref, s_ref, mx_ref = refs[2 * n_str + 2:]
    vt_tile = vt_ref.shape[2]
    tk = s_ref.shape[1]
    tiles = tk // vt_tile
    n_chunks = k_ref.shape[0] // tk
    tq = o_ref.shape[0] // n_str
    m_ref[...] = jnp.full(m_ref.shape, -jnp.inf, F32)
    acc_ref[...] = jnp.zeros(acc_ref.shape, F32)

    def scores(c, parity):
        k = k_ref[pl.ds(pl.multiple_of(c * tk, tk), tk), :]
        for st in range(n_str):
            s = jnp.dot(k, qt_refs[st][...], preferred_element_type=F32)
            s_ref[2 * st + parity] = s
            mx_ref[2 * st + parity] = jnp.max(s, axis=0, keepdims=True)

    def absorb(c, parity):
        for st in range(n_str):
            buf = 2 * st + parity
            m_old = m_ref[st]
            m_new = jnp.maximum(m_old, mx_ref[buf])
            alpha = jnp.exp2(m_old - m_new)
            pv = None
            for t in range(tiles):
                p = jnp.exp2((s_ref[buf, t * vt_tile:(t + 1) * vt_tile, :] - m_new).astype(BF16))
                part = jnp.dot(vt_ref[c * tiles + t], p, preferred_element_type=F32)
                pv = part if pv is None else pv + part
            acc_ref[st] = alpha * acc_ref[st] + pv
            m_ref[st] = m_new

    def group(i, carry):
        c0 = ATTN_UNROLL * i
        for u in range(ATTN_UNROLL):
            scores(c0 + u + 1, (u + 1) % 2)
            absorb(c0 + u, u % 2)
        return carry

    scores(0, 0)
    n_groups = (n_chunks - 1) // ATTN_UNROLL
    if n_groups > 0:
        lax.fori_loop(0, n_groups, group, 0)
    for c in range(n_groups * ATTN_UNROLL, n_chunks):
        if c + 1 < n_chunks:
            scores(c + 1, (c + 1) % 2)
        absorb(c, c % 2)
    _attn_epilogue(acc_ref, gate_refs, o_ref, n_str)


def _attn_epilogue(acc_ref, gate_refs, o_ref, n_str):
    tq = o_ref.shape[0] // n_str
    for st in range(n_str):
        o_t = acc_ref[st, 0:HEAD_DIM, :] / acc_ref[st, HEAD_DIM:HEAD_DIM + 1, :]
        o = jnp.concatenate([o_t[:, r * tq:(r + 1) * tq].T for r in range(ATT_REP)], axis=1)
        o_ref[st * tq:(st + 1) * tq, :] = (o * _silu(gate_refs[st][...].astype(F32))).astype(o_ref.dtype)


def _attn_bounded_kernel(*refs, n_str, tk):
    qt_refs = refs[0:n_str]
    k_ref, vt_ref = refs[n_str:n_str + 2]
    gate_refs = refs[n_str + 2:2 * n_str + 2]
    o_ref, acc_ref = refs[2 * n_str + 2:]
    vt_tile = vt_ref.shape[2]
    tiles = tk // vt_tile
    n_chunks = k_ref.shape[0] // tk
    acc_ref[...] = jnp.zeros(acc_ref.shape, F32)

    def chunk(c):
        k = k_ref[pl.ds(pl.multiple_of(c * tk, tk), tk), :]
        for st in range(n_str):
            p = jnp.exp2(jnp.dot(k, qt_refs[st][...], preferred_element_type=F32).astype(BF16))
            pv = None
            for t in range(tiles):
                part = jnp.dot(vt_ref[c * tiles + t, 0:HEAD_DIM, :], p[t * vt_tile:(t + 1) * vt_tile, :],
                               preferred_element_type=F32)
                pv = part if pv is None else pv + part
            acc_ref[st, 0:HEAD_DIM, :] += pv
            acc_ref[st, HEAD_DIM:HEAD_DIM + 1, :] += jnp.sum(p.astype(F32), axis=0, keepdims=True)

    def group(i, carry):
        for u in range(ATTN_BOUNDED_UNROLL):
            chunk(ATTN_BOUNDED_UNROLL * i + u)
        return carry

    n_groups = n_chunks // ATTN_BOUNDED_UNROLL
    if n_groups > 0:
        lax.fori_loop(0, n_groups, group, 0)
    for c in range(n_groups * ATTN_BOUNDED_UNROLL, n_chunks):
        chunk(c)
    _attn_epilogue(acc_ref, gate_refs, o_ref, n_str)


def _attention(qt, kh, vt, proj, geo, q_first_row, q_rows, kv_rows, bounded):
    tq = ROW_TILE
    assert q_first_row % tq == 0 and q_rows % tq == 0 and kv_rows % ROW_TILE == 0
    q0, n_tiles = q_first_row // tq, q_rows // tq
    gw = ATT_REP * HEAD_DIM
    mq = ATT_REP * tq
    depth = qt.shape[2]
    head = lambda b, g: b * ATT_KV_HEADS + g

    def build(kernel_fn, streams, scratch, name):
        n_str = streams if n_tiles % streams == 0 else 1
        tile = lambda i, st: q0 + n_str * i + st
        stream_specs = lambda make: [make(st) for st in range(n_str)]
        call = pl.pallas_call(
            functools.partial(kernel_fn, n_str=n_str),
            grid=(geo.batch, ATT_KV_HEADS, n_tiles // n_str),
            in_specs=[
                *stream_specs(lambda st: pl.BlockSpec((None, None, depth, mq),
                                                      lambda b, g, i: (head(b, g), tile(i, st), 0, 0))),
                pl.BlockSpec((None, kv_rows, depth), lambda b, g, i: (head(b, g), 0, 0)),
                pl.BlockSpec((None, kv_rows // ROW_TILE, VT_ROWS, ROW_TILE), lambda b, g, i: (head(b, g), 0, 0, 0)),
                *stream_specs(lambda st: pl.BlockSpec((tq, gw), lambda b, g, i: (b * geo.tpb + tile(i, st),
                                                                                 _COL["A_G"] // gw + g))),
            ],
            out_specs=pl.BlockSpec((n_str * tq, gw), lambda b, g, i: (b * (n_tiles // n_str) + i, g)),
            out_shape=jax.ShapeDtypeStruct((geo.batch * q_rows, BRANCH_WIDTH), BF16),
            scratch_shapes=scratch(n_str),
            compiler_params=_cparams(("parallel", "parallel", "arbitrary")),
            name=name)
        return lambda: call(*([qt] * n_str), kh, vt, *([proj] * n_str))

    tk = max(t for t in ATTN_KEY_CHUNKS if kv_rows % t == 0)
    online = build(_attn_kernel, ATTN_STREAMS,
                   lambda n: [pltpu.VMEM((n, 1, mq), F32), pltpu.VMEM((n, VT_ROWS, mq), F32),
                              pltpu.VMEM((2 * n, tk, mq), F32), pltpu.VMEM((2 * n, 1, mq), F32)],
                   "attention_online")
    tkb = max(t for t in ATTN_BOUNDED_KEY_CHUNKS if kv_rows % t == 0)
    fast = build(functools.partial(_attn_bounded_kernel, tk=tkb), ATTN_BOUNDED_STREAMS,
                 lambda n: [pltpu.VMEM((n, VT_ROWS, mq), F32)], "attention")
    return lax.cond(bounded, fast, online)


def _halo_specs(width, col_block, geo, first, row):
    per = ROW_TILE // SUBLANES
    last = geo.rows // SUBLANES - 1
    prev = pl.BlockSpec((SUBLANES, width), lambda b, j: (jnp.maximum(row(b, j) * per - 1, 0), col_block))
    nxt = pl.BlockSpec((SUBLANES, width), lambda b, j: (jnp.minimum((row(b, j) + 1) * per, last), col_block))
    return prev, nxt


def _fill_ext(ext_ref, cur, prev, nxt, j, geo):
    t = cur.shape[0]
    seg_first = jnp.logical_or(j == 0, j == geo.ctx_tiles)
    seg_last = jnp.logical_or(j == geo.ctx_tiles - 1, j == geo.tpb - 1)
    ext_ref[0:SUBLANES, :] = jnp.where(seg_first, 0.0, prev)
    ext_ref[SUBLANES:SUBLANES + t, :] = cur
    ext_ref[SUBLANES + t:2 * SUBLANES + t, :] = jnp.where(seg_last, 0.0, nxt)


def _ssd_conv_kernel(cur_ref, prev_ref, next_ref, nar_ref, w_ref, b_ref, dtb_ref, shift_ref, xbc_ref, dt_ref, ext_ref,
                     *, geo):
    j = pl.program_id(1)
    cur = cur_ref[...]
    cur_f = cur.astype(F32)
    _fill_ext(ext_ref, cur_f, prev_ref[...].astype(F32), next_ref[...].astype(F32), j, geo)
    t, half = cur_ref.shape[0], SSD_CONV // 2
    acc = b_ref[...] + w_ref[half:half + 1, :] * cur_f
    taps = [k for k in range(SSD_CONV) if k != half]
    for i, k in enumerate(taps):
        acc = acc + w_ref[k:k + 1, :] * jnp.dot(shift_ref[i], cur, preferred_element_type=F32)

    def edge(r0):
        e = jnp.zeros((SUBLANES, cur_ref.shape[1]), F32) + b_ref[...]
        for k in range(SSD_CONV):
            e = e + w_ref[k:k + 1, :] * ext_ref[SUBLANES + r0 + k - half:2 * SUBLANES + r0 + k - half, :]
        return e

    acc = jnp.concatenate([edge(0), acc[SUBLANES:t - SUBLANES], edge(t - SUBLANES)], axis=0)
    xbc_ref[...] = _silu(acc).astype(xbc_ref.dtype)
    lane = lax.broadcasted_iota(jnp.int32, dt_ref.shape, 1)
    dt = _softplus(nar_ref[...].astype(F32) + dtb_ref[...])
    dt_ref[...] = jnp.where(lane < 2 * SSD_HEADS, dt, 0.0)


def _ssd_conv(proj, conv_w, conv_b, dt_bias_row, geo):
    width = BRANCH_WIDTH + 2 * SSD_GROUPS * SSD_STATE
    row = lambda b, j: b * geo.tpb + j
    cb = _COL["B_X"] // width
    prev, nxt = _halo_specs(width, cb, geo, 0, row)
    t_idx = np.arange(ROW_TILE)
    offsets = [k - SSD_CONV // 2 for k in range(SSD_CONV) if k != SSD_CONV // 2]
    shifts = jnp.asarray(np.stack([(t_idx[None, :] == t_idx[:, None] + d) for d in offsets]).astype(np.float32), BF16)
    return pl.pallas_call(
        functools.partial(_ssd_conv_kernel, geo=geo),
        grid=(geo.batch, geo.tpb),
        in_specs=[pl.BlockSpec((ROW_TILE, width), lambda b, j: (row(b, j), cb)), prev, nxt,
                  pl.BlockSpec((ROW_TILE, LANES), lambda b, j: (row(b, j), _COL["NARROW"] // LANES)),
                  pl.BlockSpec((SSD_CONV, width), lambda b, j: (0, 0)),
                  pl.BlockSpec((1, width), lambda b, j: (0, 0)),
                  pl.BlockSpec((1, LANES), lambda b, j: (0, 0)),
                  pl.BlockSpec((len(offsets), ROW_TILE, ROW_TILE), lambda b, j: (0, 0, 0))],
        out_specs=[pl.BlockSpec((ROW_TILE, width), lambda b, j: (row(b, j), 0)),
                   pl.BlockSpec((ROW_TILE, LANES), lambda b, j: (row(b, j), 0))],
        out_shape=[jax.ShapeDtypeStruct((geo.rows, width), BF16),
                   jax.ShapeDtypeStruct((geo.rows, LANES), F32)],
        scratch_shapes=[pltpu.VMEM((ROW_TILE + 2 * SUBLANES, width), F32)],
        compiler_params=_cparams(("parallel", "parallel")),
        name="ssd_conv",
    )(proj, proj, proj, proj, conv_w, conv_b, dt_bias_row, shifts)


def _scan_chunk(s, n_chunks, n_ctx_chunks, reverse):
    if not reverse:
        return s
    return jnp.where(s < n_ctx_chunks, n_ctx_chunks - 1 - s, n_chunks + n_ctx_chunks - 1 - s)


def _tri(n, reverse):
    t = lax.broadcasted_iota(jnp.int32, (n, n), 0)
    s = lax.broadcasted_iota(jnp.int32, (n, n), 1)
    return (s >= t) if reverse else (s <= t)


def _split_bf16(x, pieces):
    out, rest = [], x
    for _ in range(pieces):
        p = rest.astype(BF16)
        out.append(p)
        rest = rest - p.astype(F32)
    return out


def _cumsum_rows(mask_bf16, x):
    w = x.shape[1]
    parts = jnp.dot(mask_bf16, jnp.concatenate(_split_bf16(x, 3), axis=1), preferred_element_type=F32)
    return parts[:, 0:w] + parts[:, w:2 * w] + parts[:, 2 * w:3 * w]


def _ssd_scan_kernel(xbc_ref, dt_ref, alog_ref, e_ref, *rest, reverse):
    if reverse:
        yf_ref, z_ref, skip_ref, g_ref, y_ref, h_ref = rest
    else:
        y_ref, h_ref = rest
    q = SSD_CHUNK
    gw = SSD_HPG * SSD_HEAD_DIM
    quad = 4
    qw = quad * SSD_HEAD_DIM
    lane0 = SSD_HEADS if reverse else 0
    n_batch, n_sub = xbc_ref.shape[0], xbc_ref.shape[1] // q

    @pl.when(pl.program_id(0) == 0)
    def _():
        h_ref[...] = jnp.zeros_like(h_ref)

    lane = lax.broadcasted_iota(jnp.int32, (1, LANES), 1)
    a = jnp.where(lane < 2 * SSD_HEADS, -jnp.exp(alog_ref[...]), 0.0)
    mask = _tri(q, reverse)
    mask_bf = jnp.where(mask, 1.0, 0.0).astype(BF16)
    e = e_ref[...]
    head_of_lane = lax.broadcasted_iota(jnp.int32, (1, qw), 1) // SSD_HEAD_DIM
    zero_bf = jnp.zeros((), BF16)

    order = range(n_sub - 1, -1, -1) if reverse else range(n_sub)
    for ci, b in [(ci, b) for ci in order for b in range(n_batch)]:
        rows = slice(ci * q, (ci + 1) * q)
        dt = dt_ref[b, rows, :]
        cs = _cumsum_rows(mask_bf, dt * a)
        cs_last = cs[0:1, :] if reverse else cs[q - 1:q, :]
        zeros = jnp.zeros((q, LANES), BF16)
        ecs_hi, ecs_lo = _split_bf16(jnp.exp(cs), 2)
        cd_hi, cd_mid, cd_lo = _split_bf16(jnp.broadcast_to(jnp.exp(cs_last), (BF16_SUBLANES, LANES)), 3)
        stack = jnp.concatenate([
            jnp.concatenate([(jnp.exp(cs_last - cs) * dt).astype(BF16), zeros], axis=1),
            jnp.concatenate([ecs_hi, ecs_lo], axis=1),
            jnp.concatenate([cd_hi, cd_mid], axis=1),
            jnp.concatenate([cd_lo, zeros[0:BF16_SUBLANES]], axis=1)], axis=0)
        big = jnp.dot(stack, e, preferred_element_type=F32)
        x_bf = xbc_ref[b, rows, 0:BRANCH_WIDTH]
        x = x_bf.astype(F32)
        wx = (big[0:q] * x).astype(BF16)
        ecs = big[q:2 * q]
        r0 = 2 * q
        chunk_decay = big[r0:r0 + 1] + big[r0 + BF16_SUBLANES:r0 + BF16_SUBLANES + 1]
        cs_t = cs.T
        dt_t = dt.T
        pieces = []
        for g in range(SSD_GROUPS):
            bm = xbc_ref[b, rows, BRANCH_WIDTH + g * SSD_STATE:BRANCH_WIDTH + (g + 1) * SSD_STATE]
            c0 = BRANCH_WIDTH + SSD_GROUPS * SSD_STATE + g * SSD_STATE
            cm = xbc_ref[b, rows, c0:c0 + SSD_STATE]
            cb = lax.dot_general(cm, bm, (((1,), (1,)), ((), ())), preferred_element_type=F32)
            h_in = h_ref[b, g]
            y_off = jnp.dot(cm, h_in.astype(BF16), preferred_element_type=F32) * ecs[:, g * gw:(g + 1) * gw]
            states = jnp.dot(bm.astype(F32).T.astype(BF16), wx[:, g * gw:(g + 1) * gw], preferred_element_type=F32)
            h_ref[b, g] = chunk_decay[:, g * gw:(g + 1) * gw] * h_in + states
            for hq in range(SSD_HPG // quad):
                ms = []
                for r in range(quad):
                    col = lane0 + g * SSD_HPG + hq * quad + r
                    seg = cs[:, col:col + 1] - cs_t[col:col + 1, :]
                    ms.append((cb * jnp.exp(jnp.where(mask, seg, -1e30)) * dt_t[col:col + 1, :]).astype(BF16))
                lo = g * gw + hq * qw
                slab = x_bf[:, lo:lo + qw]
                rhs = jnp.concatenate([jnp.where(head_of_lane == r, slab, zero_bf) for r in range(quad)], axis=0)
                y_diag = jnp.dot(jnp.concatenate(ms, axis=1), rhs, preferred_element_type=F32)
                pieces.append(y_diag + y_off[:, hq * qw:(hq + 1) * qw])
        y = jnp.concatenate(pieces, axis=1)
        if reverse:
            y = (yf_ref[b, rows, :] + y + skip_ref[...] * x) * _silu(z_ref[b, rows, :].astype(F32))
            y = y * lax.rsqrt(jnp.mean(y * y, axis=-1, keepdims=True) + NORM_EPS) * g_ref[...]
        y_ref[b, rows, :] = y.astype(y_ref.dtype)


def _ssd_scan(xbc, dt, a_log_row, expand_mat, geo, reverse, finish=None):
    nb, nbc = geo.lt // SCAN_BLOCK, geo.ctx_len // SCAN_BLOCK
    width, w, nbat = xbc.shape[1], BRANCH_WIDTH, geo.batch
    blk = lambda s: _scan_chunk(s, nb, nbc, reverse)
    per_batch = lambda a: a.reshape(nbat, geo.lt, a.shape[1])
    tok_spec = lambda cols, col_block: pl.BlockSpec((nbat, SCAN_BLOCK, cols), lambda s: (0, blk(s), col_block))
    const_spec = lambda shape: pl.BlockSpec(shape, lambda s: (0, 0))
    in_specs = [tok_spec(width, 0), tok_spec(LANES, 0), const_spec((1, LANES)), const_spec((2 * LANES, w))]
    args = [per_batch(xbc), per_batch(dt), a_log_row, expand_mat]
    if reverse:
        yf, proj, skip_row, norm_g = finish
        in_specs += [tok_spec(w, 0), tok_spec(w, _COL["B_Z"] // w), const_spec((1, w)), const_spec((1, w))]
        args += [per_batch(yf), per_batch(proj), skip_row, norm_g]
    out = pl.pallas_call(
        functools.partial(_ssd_scan_kernel, reverse=reverse),
        grid=(nb,),
        in_specs=in_specs,
        out_specs=tok_spec(w, 0),
        out_shape=jax.ShapeDtypeStruct((nbat, geo.lt, w), BF16 if reverse else F32),
        scratch_shapes=[pltpu.VMEM((nbat, SSD_GROUPS, SSD_STATE, SSD_HPG * SSD_HEAD_DIM), F32)],
        compiler_params=_cparams(("arbitrary",)),
        name="ssd_scan_bwd" if reverse else "ssd_scan_fwd",
    )(*args)
    return out.reshape(geo.rows, w)


def _gla_scan_kernel(q_ref, k_ref, v_ref, nar_ref, w2_ref, bf_ref, *rest, reverse):
    if reverse:
        of_ref, gate_ref, g_ref, o_ref, h_ref = rest
    else:
        o_ref, h_ref = rest
    n = GLA_CHUNK

    @pl.when(pl.program_id(0) == 0)
    def _():
        h_ref[...] = jnp.zeros_like(h_ref)

    n_batch, t = q_ref.shape[0], q_ref.shape[1]
    n_sub = t // n
    ri = lax.broadcasted_iota(jnp.int32, (t, t), 0)
    ci = lax.broadcasted_iota(jnp.int32, (t, t), 1)
    mask = jnp.logical_and(ri // n == ci // n, (ci >= ri) if reverse else (ci <= ri))
    mask_bf = jnp.where(mask, 1.0, 0.0).astype(BF16)
    for bi in range(n_batch):
        logit = jnp.dot(nar_ref[bi], w2_ref[...], preferred_element_type=F32) + bf_ref[...]
        gl = _log_sigmoid(logit) / GLA_TAU
        b = _cumsum_rows(mask_bf, gl)
        last = [b[c * n:c * n + 1, :] if reverse else b[(c + 1) * n - 1:(c + 1) * n, :] for c in range(n_sub)]
        b_last = jnp.concatenate([jnp.broadcast_to(r, (n, r.shape[1])) for r in last], axis=0)
        q = q_ref[bi].astype(F32) * (GLA_DK ** -0.5)
        k = k_ref[bi].astype(F32)
        qe = (q * jnp.exp(b)).astype(BF16)
        ke = (k * jnp.exp(-b)).astype(BF16)
        kd = (k * jnp.exp(b_last - b)).astype(BF16)
        for h in range(GLA_HEADS):
            ks = slice(h * GLA_DK, (h + 1) * GLA_DK)
            vs = slice(h * GLA_DV, (h + 1) * GLA_DV)
            v = v_ref[bi, :, vs]
            att = lax.dot_general(qe[:, ks], ke[:, ks], (((1,), (1,)), ((), ())), preferred_element_type=F32)
            o_intra = jnp.dot(jnp.where(mask, att, 0.0).astype(BF16), v, preferred_element_type=F32)
            o_inter = [None] * n_sub
            for c in (range(n_sub - 1, -1, -1) if reverse else range(n_sub)):
                rows = slice(c * n, (c + 1) * n)
                h_in = h_ref[bi, h]
                o_inter[c] = lax.dot_general(qe[rows, ks], h_in.astype(BF16), (((1,), (1,)), ((), ())),
                                             preferred_element_type=F32)
                upd = jnp.dot(v[rows].astype(F32).T.astype(BF16), kd[rows, ks], preferred_element_type=F32)
                h_ref[bi, h] = jnp.exp(last[c][:, ks]) * h_in + upd
            o = o_intra + jnp.concatenate(o_inter, axis=0)
            if reverse:
                o = o + of_ref[bi, :, vs]
                o = o * lax.rsqrt(jnp.mean(o * o, axis=-1, keepdims=True) + NORM_EPS) * g_ref[...]
                o = o * _silu(gate_ref[bi, :, vs].astype(F32))
            o_ref[bi, :, vs] = o.astype(o_ref.dtype)


def _gla_scan(proj, w2, b_f, geo, reverse, finish=None):
    nb, nbc, nbat = geo.lt // SCAN_BLOCK, geo.ctx_len // SCAN_BLOCK, geo.batch
    blk = lambda s: _scan_chunk(s, nb, nbc, reverse)
    kw, vw = GLA_HEADS * GLA_DK, GLA_HEADS * GLA_DV
    per_batch = lambda a: a.reshape(nbat, geo.lt, a.shape[1])
    tok_spec = lambda cols, col_block: pl.BlockSpec((nbat, SCAN_BLOCK, cols), lambda s: (0, blk(s), col_block))
    const_spec = lambda shape: pl.BlockSpec(shape, lambda s: (0, 0))
    proj3 = per_batch(proj)
    in_specs = [tok_spec(kw, _COL["C_Q"] // kw), tok_spec(kw, _COL["C_K"] // kw), tok_spec(vw, _COL["C_V"] // vw),
                tok_spec(LANES, _COL["NARROW"] // LANES), const_spec((LANES, kw)), const_spec((1, kw))]
    args = [proj3, proj3, proj3, proj3, w2, b_f]
    if reverse:
        of, norm_g = finish
        in_specs += [tok_spec(vw, 0), tok_spec(vw, _COL["C_G"] // vw), const_spec((1, GLA_DV))]
        args += [per_batch(of), proj3, norm_g]
    out = pl.pallas_call(
        functools.partial(_gla_scan_kernel, reverse=reverse),
        grid=(nb,),
        in_specs=in_specs,
        out_specs=tok_spec(vw, 0),
        out_shape=jax.ShapeDtypeStruct((nbat, geo.lt, vw), BF16 if reverse else F32),
        scratch_shapes=[pltpu.VMEM((nbat, GLA_HEADS, GLA_DV, GLA_DK), F32)],
        compiler_params=_cparams(("arbitrary",)),
        name="gla_scan_bwd" if reverse else "gla_scan_fwd",
    )(*args)
    return out.reshape(geo.rows, vw)


def _shortconv_tile(cur_ref, prev_ref, next_ref, w_ref, ext_ref, tile, geo):
    w = BRANCH_WIDTH
    u = lambda ref: ref[:, w:2 * w].astype(F32) * ref[:, 2 * w:3 * w].astype(F32)
    _fill_ext(ext_ref, u(cur_ref), u(prev_ref), u(next_ref), tile, geo)
    t = cur_ref.shape[0]
    acc = jnp.zeros((t, w), F32)
    for k in range(SC_CONV):
        acc = acc + w_ref[k:k + 1, :] * ext_ref[SUBLANES + k - SC_CONV // 2:SUBLANES + k - SC_CONV // 2 + t, :]
    return cur_ref[:, 0:w].astype(F32) * acc * _silu(cur_ref[:, 3 * w:4 * w].astype(F32))


def _merge_out_kernel(*refs, n_ya, n_src, geo, first, with_next):
    tile = first + pl.program_id(1)
    ext_ref = refs[-1]
    ya = _read_tokens(refs[0:n_ya], tile, geo)
    refs = refs[n_ya:-1]
    yd = _shortconv_tile(*refs[2:6], ext_ref, tile, geo).astype(BF16)
    ys = [ya, refs[0][...], refs[1][...], yd]
    refs = refs[6:]
    gates = refs[0:N_BRANCH]
    wb_ref, wo_ref = refs[N_BRANCH:N_BRANCH + 2]
    p = N_BRANCH + 2
    srcs = refs[p:p + n_src]
    gt_ref, gp_ref = refs[p + n_src:p + n_src + 2]
    rest = refs[p + n_src + 2:]
    m = None
    for n, (y, gate_ref) in enumerate(zip(ys, gates)):
        term = jax.nn.sigmoid(gate_ref[...].astype(F32)) * jnp.dot(y, wb_ref[n], preferred_element_type=F32)
        m = term if m is None else m + term
    out = jnp.dot(m.astype(BF16), wo_ref[...], preferred_element_type=F32)
    y = out * lax.rsqrt(jnp.mean(out * out, axis=-1, keepdims=True) + NORM_EPS) * gp_ref[...]
    x_new = _read_tokens(srcs, tile, geo) + gt_ref[0] * y
    if with_next:
        gn_ref, shn_ref, scn_ref, o_ref, h_ref = rest
        h_ref[...] = _adaln_prenorm(x_new, gn_ref[...], shn_ref[0], scn_ref[0]).astype(h_ref.dtype)
    else:
        (o_ref,) = rest
    o_ref[...] = x_new


def _merge_out(ys, proj, w_branch, w_out, layer, tokens, mod3, g_post, geo, with_ctx, nxt=None):
    first, nt = geo.span(with_ctx)
    w, d = BRANCH_WIDTH, D_MODEL
    row = lambda b, j: b * geo.tpb + first + j
    out_rows = geo.rows if with_ctx else geo.batch * geo.seq
    out_row = row if with_ctx else (lambda b, j: b * nt + j)
    y_spec = pl.BlockSpec((ROW_TILE, w), lambda b, j: (row(b, j), 0))
    gate_specs = [pl.BlockSpec((ROW_TILE, d), functools.partial(lambda b, j, n: (row(b, j), _COL["MG"] // d + n), n=n))
                  for n in range(N_BRANCH)]
    resident = pl.Buffered(1)
    mod_spec = lambda part: pl.BlockSpec((1, 1, d), lambda b, j: (geo.mod_row(b, first + j), 0, part))
    vec_spec = pl.BlockSpec((1, d), lambda b, j: (0, 0))
    ya, yb, yg, conv_w = ys
    dw = N_BRANCH * w
    d_block = _COL["D_ALL"] // dw
    d_prev, d_next = _halo_specs(dw, d_block, geo, first, row)
    in_specs = [*_token_specs(ya, geo, first), y_spec, y_spec,
                pl.BlockSpec((ROW_TILE, dw), lambda b, j: (row(b, j), d_block)), d_prev, d_next,
                pl.BlockSpec((SC_CONV, w), lambda b, j: (0, 0)), *gate_specs,
                pl.BlockSpec((None, N_BRANCH, w, d), lambda b, j: (layer, 0, 0, 0), pipeline_mode=resident),
                pl.BlockSpec((None, d, d), lambda b, j: (layer, 0, 0), pipeline_mode=resident),
                *_token_specs(tokens, geo, first), mod_spec(2), vec_spec]
    args = [*ya, yb, yg, proj, proj, proj, conv_w, proj, proj, proj, proj, w_branch, w_out, *tokens, mod3, g_post]
    out_specs = [pl.BlockSpec((ROW_TILE, d), lambda b, j: (out_row(b, j), 0))]
    out_shape = [jax.ShapeDtypeStruct((out_rows, d), F32)]
    if nxt is not None:
        assert with_ctx
        g_next, mod3_next = nxt
        in_specs += [vec_spec, mod_spec(0), mod_spec(1)]
        args += [g_next, mod3_next, mod3_next]
        out_specs.append(pl.BlockSpec((ROW_TILE, d), lambda b, j: (row(b, j), 0)))
        out_shape.append(jax.ShapeDtypeStruct((geo.rows, d), BF16))
    outs = pl.pallas_call(
        functools.partial(_merge_out_kernel, n_ya=len(ya), n_src=len(tokens), geo=geo, first=first,
                          with_next=nxt is not None),
        grid=(geo.batch, nt),
        in_specs=in_specs,
        out_specs=out_specs,
        out_shape=out_shape,
        scratch_shapes=[pltpu.VMEM((ROW_TILE + 2 * SUBLANES, w), F32)],
        compiler_params=_cparams(("parallel", "parallel")),
        name="merge_out",
    )(*args)
    return outs if nxt is not None else (outs[0], None)


def _regroup_plan():
    bw = BRANCH_WIDTH
    names = ("a_q", "a_k", "a_v", "a_g", "b_x", "b_z", "b_b", "b_c", "b_dt", "c_q", "c_k", "c_v", "c_g", "c_f",
             "d_all", "mg")
    widths = (ATT_HEADS * HEAD_DIM, ATT_KV_HEADS * HEAD_DIM, ATT_KV_HEADS * HEAD_DIM, bw,
              bw, bw, SSD_GROUPS * SSD_STATE, SSD_GROUPS * SSD_STATE, 2 * SSD_HEADS,
              GLA_HEADS * GLA_DK, GLA_HEADS * GLA_DK, GLA_HEADS * GLA_DV, bw, 2 * GLA_RANK,
              4 * bw, N_BRANCH * D_MODEL)
    src = dict(zip(names, np.concatenate([[0], np.cumsum(widths)[:-1]]).tolist()))
    wid = dict(zip(names, widths))
    dst = dict(a_q=_COL["A_Q"], a_g=_COL["A_G"], b_z=_COL["B_Z"], b_x=_COL["B_X"], b_b=_COL["B_B"], b_c=_COL["B_C"],
               a_k=_COL["A_K"], a_v=_COL["A_V"], c_v=_COL["C_V"], c_g=_COL["C_G"], c_q=_COL["C_Q"], c_k=_COL["C_K"],
               d_all=_COL["D_ALL"], mg=_COL["MG"], b_dt=_COL["NARROW"] + DT_LANE0, c_f=_COL["NARROW"] + F1_LANE0)
    return [(dst[n], src[n], wid[n], n in ("a_q", "a_k")) for n in names], sum(widths)


def _regroup_kernel(w_ref, o_ref):
    plan, _ = _regroup_plan()
    half = HEAD_DIM // 2
    for dst, src, width, deinterleave in plan:
        if deinterleave:
            for h in range(width // HEAD_DIM):
                s, d = src + h * HEAD_DIM, dst + h * HEAD_DIM
                o_ref[d:d + half, :] = w_ref[pl.ds(s, half, stride=2), :].astype(o_ref.dtype)
                o_ref[d + half:d + HEAD_DIM, :] = w_ref[pl.ds(s + 1, half, stride=2), :].astype(o_ref.dtype)
        else:
            o_ref[dst:dst + width, :] = w_ref[src:src + width, :].astype(o_ref.dtype)
    used = _COL["NARROW"] + F1_LANE0 + 2 * GLA_RANK
    o_ref[used:N_PROJ, :] = jnp.zeros((N_PROJ - used, o_ref.shape[1]), o_ref.dtype)


def _regroup_w_in(w_in, layer):
    w_t = jnp.swapaxes(w_in, 1, 2)
    _, n, k = w_t.shape
    assert n == _regroup_plan()[1]
    tk = LANES
    return pl.pallas_call(
        _regroup_kernel,
        grid=(k // tk,),
        in_specs=[pl.BlockSpec((None, n, tk), lambda i: (layer, 0, i))],
        out_specs=pl.BlockSpec((N_PROJ, tk), lambda i: (0, i)),
        out_shape=jax.ShapeDtypeStruct((N_PROJ, k), BF16),
        compiler_params=_cparams(("parallel",)),
        name="regroup_w_in",
    )(w_t)


def _deinterleave_vec(g):
    return g.reshape(HEAD_DIM // 2, 2).T.reshape(1, HEAD_DIM)


def _pad_lanes(v, lane0=0):
    return jnp.zeros((1, LANES), F32).at[0, lane0:lane0 + v.shape[0]].set(v.astype(F32))


def _head_expand_matrix(reverse):
    e = np.zeros((LANES, BRANCH_WIDTH), np.float32)
    lane0 = SSD_HEADS if reverse else 0
    for r in range(SSD_HEADS):
        e[lane0 + r, r * SSD_HEAD_DIM:(r + 1) * SSD_HEAD_DIM] = 1.0
    return jnp.asarray(np.concatenate([e, e], axis=0), BF16)


def _forget_weight(w_f2_dir, direction):
    lane0 = F1_LANE0 + direction * GLA_RANK
    return jnp.zeros((LANES, w_f2_dir.shape[1]), F32).at[lane0:lane0 + GLA_RANK].set(w_f2_dir).astype(BF16)


def kernel(x, c, ctx, c_ctx, w_mod, b_mod, g_pre, g_post, w_in, g_q, g_k, ssd_conv_w, ssd_conv_b,
           ssd_a_log, ssd_dt_bias, ssd_d, ssd_norm_g, gla_w_f2, gla_b_f, gla_norm_g, sc_conv_w,
           w_branch, w_out):
    batch, seq, d = x.shape
    ctx_len = ctx.shape[1]
    depth = w_in.shape[0]
    geo = _Geom(batch, ctx_len, seq)
    assert d == D_MODEL and batch + 1 <= SUBLANES and seq % GRID_W == 0

    cos_t, sin_t = _rope_tables(geo)
    c_rows = jnp.zeros((SUBLANES, d), F32).at[:batch].set(c).at[batch].set(c_ctx)
    tokens = (ctx.reshape(batch * ctx_len, d), x.reshape(batch * seq, d))
    e_fwd, e_bwd = _head_expand_matrix(False), _head_expand_matrix(True)
    w_branch_bf, w_out_bf = w_branch.astype(BF16), w_out.astype(BF16)
    mods =[_modulation(c_rows, w_mod, b_mod[l][None, :], l).reshape(SUBLANES, 1, 3 * d) for l in range(depth)]
    h = _prenorm(tokens, g_pre[0][None, :], mods[0], geo)

    for l in range(depth):
        need_ctx = l < depth - 1
        mod3 = mods[l]
        proj = _matmul(h, _regroup_w_in(w_in, l), PROJ_TN)

        shift = Q_SCALE * HEAD_DIM * jnp.max(jnp.abs(g_q[l])) * jnp.max(jnp.abs(g_k[l]))
        bounded = 2.0 * shift <= ATTN_SAFE_EXPONENT
        shift_row = jnp.full((1, MXU_WIDTH - HEAD_DIM), shift, F32)
        qt, kh, vt = _qk_prep(proj, cos_t, sin_t, _deinterleave_vec(g_q[l]), _deinterleave_vec(g_k[l]), shift_row, geo)
        ya = (_attention(qt, kh, vt, proj, geo, ctx_len, seq, geo.lt, bounded),)
        if need_ctx:
            ya = (_attention(qt, kh, vt, proj, geo, 0, ctx_len, ctx_len, bounded), *ya)

        xbc, dt = _ssd_conv(proj, ssd_conv_w[l], ssd_conv_b[l][None, :], _pad_lanes(ssd_dt_bias[l].reshape(-1)), geo)
        a_log_row = _pad_lanes(ssd_a_log[l].reshape(-1))
        ysf = _ssd_scan(xbc, dt, a_log_row, e_fwd, geo, False)
        skip_row = jnp.repeat(ssd_d[l], SSD_HEAD_DIM)[None, :]
        yb = _ssd_scan(xbc, dt, a_log_row, e_bwd, geo, True, (ysf, proj, skip_row, ssd_norm_g[l][None, :]))

        ogf = _gla_scan(proj, _forget_weight(gla_w_f2[l, 0], 0), gla_b_f[l, 0][None, :], geo, False)
        yg = _gla_scan(proj, _forget_weight(gla_w_f2[l, 1], 1), gla_b_f[l, 1][None, :], geo, True,
                       (ogf, gla_norm_g[l][None, :]))

        nxt = (g_pre[l + 1][None, :], mods[l + 1]) if need_ctx else None
        x_new, h = _merge_out((ya, yb, yg, sc_conv_w[l]), proj, w_branch_bf, w_out_bf, l, tokens, mod3,
                              g_post[l][None, :], geo, need_ctx, nxt)
        tokens = (x_new,)

    return x_new.reshape(batch, seq, d)
```

```python
import functools

import numpy as np
import jax
import jax.numpy as jnp
from jax import lax
from jax.experimental import pallas as pl
from jax.experimental.pallas import tpu as pltpu

F32 = jnp.float32
BF16 = jnp.bfloat16

D_MODEL = 2048
GRID_W = 64
BRANCH_WIDTH = D_MODEL // 2
N_BRANCH = 4
NORM_EPS = 1e-6
HEAD_DIM = 128
ATT_HEADS = BRANCH_WIDTH // HEAD_DIM
ATT_KV_HEADS = ATT_HEADS // 4
ATT_REP = ATT_HEADS // ATT_KV_HEADS
ROPE_THETA = 10000.0
SSD_HEAD_DIM = 64
SSD_HEADS = BRANCH_WIDTH // SSD_HEAD_DIM
SSD_GROUPS = 2
SSD_HPG = SSD_HEADS // SSD_GROUPS
SSD_STATE = 128
SSD_CONV = 5
SSD_CHUNK = 128
GLA_HEADS = 4
GLA_DV = BRANCH_WIDTH // GLA_HEADS
GLA_DK = GLA_DV // 2
GLA_RANK = 16
GLA_TAU = 16.0
GLA_CHUNK = 64
SC_CONV = 3

LANES = 128
SUBLANES = 8
ROW_TILE = 256
SCAN_BLOCK = ROW_TILE
VMEM_LIMIT = 56 * 1024 * 1024

_COL = dict(
    A_Q=0, A_G=1024, B_Z=2048, B_X=3072, B_B=4096, B_C=4352, A_K=4608, A_V=4864,
    C_V=5120, C_G=6144, C_Q=7168, C_K=7680,
    D_ALL=8192, MG=12288, NARROW=20480,
)
MXU_WIDTH = 256
N_PROJ = 20736
PROJ_TN = 2304
ATTN_UNROLL = 2
ATTN_STREAMS = 2
ATTN_BOUNDED_KEY_CHUNKS = (ROW_TILE, 3 * ROW_TILE)
ATTN_BOUNDED_UNROLL = 11
ATTN_BOUNDED_STREAMS = 2
ATTN_SAFE_EXPONENT = 100.0
ATTN_KEY_CHUNKS = (ROW_TILE, 2 * ROW_TILE, 3 * ROW_TILE)
BF16_SUBLANES = 16
VT_ROWS = HEAD_DIM + BF16_SUBLANES
Q_SCALE = HEAD_DIM ** -0.5 * float(np.log2(np.e))
DT_LANE0 = 0
F1_LANE0 = 32


def _cparams(sem, vmem=VMEM_LIMIT):
    return pltpu.CompilerParams(dimension_semantics=sem, vmem_limit_bytes=vmem)


def _silu(x):
    return x * jax.nn.sigmoid(x)


def _softplus(x):
    return jnp.maximum(x, 0.0) + jnp.log1p(jnp.exp(-jnp.abs(x)))


def _log_sigmoid(x):
    return jnp.minimum(x, 0.0) - jnp.log(1.0 + jnp.exp(-jnp.abs(x)))


def _mod_kernel(c_ref, w_ref, b_ref, o_ref):
    a = _silu(c_ref[...]).astype(BF16)
    o_ref[...] = jnp.dot(a, w_ref[...].astype(BF16), preferred_element_type=F32) + b_ref[...]


def _modulation(c_rows, w_mod, b_mod, layer):
    _, d, n = w_mod.shape
    tn = 1024
    return pl.pallas_call(
        _mod_kernel,
        grid=(n // tn,),
        in_specs=[pl.BlockSpec((SUBLANES, d), lambda j: (0, 0)),
                  pl.BlockSpec((None, d, tn), lambda j: (layer, 0, j)),
                  pl.BlockSpec((1, tn), lambda j: (0, j))],
        out_specs=pl.BlockSpec((SUBLANES, tn), lambda j: (0, j)),
        out_shape=jax.ShapeDtypeStruct((SUBLANES, n), F32),
        compiler_params=_cparams(("parallel",)),
        name="modulation",
    )(c_rows, w_mod, b_mod)


class _Geom:
    def __init__(self, batch, ctx_len, seq):
        self.batch, self.ctx_len, self.seq = batch, ctx_len, seq
        self.lt = ctx_len + seq
        self.rows = batch * self.lt
        assert ctx_len % ROW_TILE == 0 and seq % ROW_TILE == 0
        self.tpb = self.lt // ROW_TILE
        self.ctx_tiles = ctx_len // ROW_TILE
        self.lat_tiles = seq // ROW_TILE

    def span(self, with_ctx):
        return (0, self.tpb) if with_ctx else (self.ctx_tiles, self.lat_tiles)

    def mod_row(self, b, j):
        return jnp.where(j < self.ctx_tiles, self.batch, b)


def _token_specs(tokens, geo, first):
    c = tokens[0].shape[1]
    ct, lt = geo.ctx_tiles, geo.lat_tiles
    if len(tokens) == 1 and tokens[0].shape[0] == geo.rows:
        return [pl.BlockSpec((ROW_TILE, c), lambda b, j: (b * geo.tpb + first + j, 0))]
    if len(tokens) == 1:
        assert tokens[0].shape[0] == geo.batch * geo.seq and first >= ct
        return [pl.BlockSpec((ROW_TILE, c), lambda b, j: (b * lt + first + j - ct, 0))]
    return [pl.BlockSpec((ROW_TILE, c), lambda b, j: (b * ct + jnp.minimum(first + j, ct - 1), 0),
                         pipeline_mode=pl.Buffered(1)),
            pl.BlockSpec((ROW_TILE, c), lambda b, j: (b * lt + jnp.maximum(first + j - ct, 0), 0))]


def _read_tokens(refs, tile, geo):
    if len(refs) == 1:
        return refs[0][...]
    return jnp.where(tile < geo.ctx_tiles, refs[0][...], refs[1][...])


def _adaln_prenorm(x, g, sh, sc):
    y = x * lax.rsqrt(jnp.mean(x * x, axis=-1, keepdims=True) + NORM_EPS)
    return (y * g) * (1.0 + sc) + sh


def _prenorm_kernel(*refs, n_src, geo):
    g_ref, sh_ref, sc_ref, o_ref = refs[n_src:]
    x = _read_tokens(refs[:n_src], pl.program_id(1), geo)
    o_ref[...] = _adaln_prenorm(x, g_ref[...], sh_ref[0], sc_ref[0]).astype(o_ref.dtype)


def _prenorm(tokens, g_pre, mod3, geo):
    d = D_MODEL
    return pl.pallas_call(
        functools.partial(_prenorm_kernel, n_src=len(tokens), geo=geo),
        grid=(geo.batch, geo.tpb),
        in_specs=[*_token_specs(tokens, geo, 0),
                  pl.BlockSpec((1, d), lambda b, j: (0, 0)),
                  pl.BlockSpec((1, 1, d), lambda b, j: (geo.mod_row(b, j), 0, 0)),
                  pl.BlockSpec((1, 1, d), lambda b, j: (geo.mod_row(b, j), 0, 1))],
        out_specs=pl.BlockSpec((ROW_TILE, d), lambda b, j: (b * geo.tpb + j, 0)),
        out_shape=jax.ShapeDtypeStruct((geo.rows, d), BF16),
        compiler_params=_cparams(("parallel", "parallel")),
        name="prenorm",
    )(*tokens, g_pre, mod3, mod3)


def _matmul_kernel(a_ref, w_ref, o_ref):
    o_ref[...] = lax.dot_general(a_ref[...], w_ref[...], (((1,), (1,)), ((), ())),
                                 preferred_element_type=F32).astype(o_ref.dtype)


def _matmul(a, w_t, tn, out_dtype=BF16):
    m, k = a.shape
    n = w_t.shape[0]
    tm = next(t for t in (768, 512, 256) if m % t == 0)
    assert m % tm == 0 and n % tn == 0
    return pl.pallas_call(
        _matmul_kernel,
        grid=(n // tn, m // tm),
        in_specs=[pl.BlockSpec((tm, k), lambda j, i: (i, 0)),
                  pl.BlockSpec((tn, k), lambda j, i: (j, 0))],
        out_specs=pl.BlockSpec((tm, tn), lambda j, i: (i, j)),
        out_shape=jax.ShapeDtypeStruct((m, n), out_dtype),
        compiler_params=_cparams(("parallel", "parallel")),
        name="in_proj",
    )(a, w_t)


def _rope_tables(geo):
    f32 = np.float32
    rows = geo.seq // GRID_W
    t_row = np.repeat(np.arange(rows, dtype=f32), GRID_W)
    t_col = np.tile(np.arange(GRID_W, dtype=f32), rows)
    half = HEAD_DIM // 2
    freqs = f32(ROPE_THETA) ** (-(np.arange(0, half, 2, dtype=f32) / f32(half)))
    ang = np.concatenate([t_row[:, None] * freqs, t_col[:, None] * freqs], axis=-1).astype(f32)
    cos, sin = np.cos(ang), np.sin(ang)
    cos_l = np.concatenate([cos, cos], axis=-1)
    sin_l = np.concatenate([-sin, sin], axis=-1)
    cos_c = np.ones((geo.ctx_len, HEAD_DIM), f32)
    sin_c = np.zeros((geo.ctx_len, HEAD_DIM), f32)
    return (jnp.asarray(np.concatenate([cos_c, cos_l], axis=0), F32),
            jnp.asarray(np.concatenate([sin_c, sin_l], axis=0), F32))


def _qk_prep_kernel(q_ref, k_ref, v_ref, cos_ref, sin_ref, gq_ref, gk_ref, shift_ref, qt_ref, ko_ref, vt_ref):
    cos, sin = cos_ref[...], sin_ref[...]
    t = q_ref.shape[0]

    ones = jnp.ones((HEAD_DIM, HEAD_DIM), BF16)

    def norm_rope(x, g):
        x = x.astype(F32)
        hi, lo = _split_bf16(x * x, 2)
        ss = jnp.dot(hi, ones, preferred_element_type=F32) + jnp.dot(lo, ones, preferred_element_type=F32)
        y = x * lax.rsqrt(ss * (1.0 / HEAD_DIM) + NORM_EPS) * g
        return y * cos + pltpu.roll(y, HEAD_DIM // 2, 1) * sin

    gq, gk = gq_ref[...], gk_ref[...]
    for h in range(ATT_HEADS):
        g, r = divmod(h, ATT_REP)
        y = norm_rope(q_ref[:, h * HEAD_DIM:(h + 1) * HEAD_DIM], gq) * Q_SCALE
        qt_ref[g, 0, 0:HEAD_DIM, r * t:(r + 1) * t] = y.astype(qt_ref.dtype).T
    extra = MXU_WIDTH - HEAD_DIM
    first_row = lax.broadcasted_iota(jnp.int32, (extra, ATT_REP * t), 0) == 0
    first_lane = lax.broadcasted_iota(jnp.int32, (t, extra), 1) == 0
    for g in range(ATT_KV_HEADS):
        hs = slice(g * HEAD_DIM, (g + 1) * HEAD_DIM)
        qt_ref[g, 0, HEAD_DIM:MXU_WIDTH, :] = jnp.where(first_row, 1.0, 0.0).astype(qt_ref.dtype)
        ko_ref[g, :, 0:HEAD_DIM] = norm_rope(k_ref[:, hs], gk).astype(ko_ref.dtype)
        ko_ref[g, :, HEAD_DIM:MXU_WIDTH] = jnp.where(first_lane, -shift_ref[...], 0.0).astype(ko_ref.dtype)
        vt_ref[g, 0, 0:HEAD_DIM, :] = v_ref[:, hs].T
        vt_ref[g, 0, HEAD_DIM:VT_ROWS, :] = jnp.ones((VT_ROWS - HEAD_DIM, t), vt_ref.dtype)


def _qk_prep(proj, cos_t, sin_t, g_q, g_k, shift_row, geo):
    row = lambda b, j: b * geo.tpb + j
    nq, nk = ATT_HEADS * HEAD_DIM, ATT_KV_HEADS * HEAD_DIM
    kv, t, depth = ATT_KV_HEADS, ROW_TILE, MXU_WIDTH
    return pl.pallas_call(
        _qk_prep_kernel,
        grid=(geo.batch, geo.tpb),
        in_specs=[pl.BlockSpec((t, nq), lambda b, j: (row(b, j), _COL["A_Q"] // nq)),
                  pl.BlockSpec((t, nk), lambda b, j: (row(b, j), _COL["A_K"] // nk)),
                  pl.BlockSpec((t, nk), lambda b, j: (row(b, j), _COL["A_V"] // nk)),
                  pl.BlockSpec((t, HEAD_DIM), lambda b, j: (j, 0)),
                  pl.BlockSpec((t, HEAD_DIM), lambda b, j: (j, 0)),
                  pl.BlockSpec((1, HEAD_DIM), lambda b, j: (0, 0)),
                  pl.BlockSpec((1, HEAD_DIM), lambda b, j: (0, 0)),
                  pl.BlockSpec((1, depth - HEAD_DIM), lambda b, j: (0, 0))],
        out_specs=[pl.BlockSpec((kv, 1, depth, ATT_REP * t), lambda b, j: (b, j, 0, 0)),
                   pl.BlockSpec((kv, t, depth), lambda b, j: (b, j, 0)),
                   pl.BlockSpec((kv, 1, VT_ROWS, t), lambda b, j: (b, j, 0, 0))],
        out_shape=[jax.ShapeDtypeStruct((geo.batch * kv, geo.tpb, depth, ATT_REP * t), BF16),
                   jax.ShapeDtypeStruct((geo.batch * kv, geo.lt, depth), BF16),
                   jax.ShapeDtypeStruct((geo.batch * kv, geo.tpb, VT_ROWS, t), BF16)],
        compiler_params=_cparams(("parallel", "parallel")),
        name="qk_prep",
    )(proj, proj, proj, cos_t, sin_t, g_q, g_k, shift_row)


def _attn_kernel(*refs, n_str):
    qt_refs = refs[0:n_str]
    k_ref, vt_ref = refs[n_str:n_str + 2]
    gate_refs = refs[n_str + 2:2 * n_str + 2]
    o_ref, m_ref, acc_ref, s_ref, mx_ref = refs[2 * n_str + 2:]
    vt_tile = vt_ref.shape[2]
    tk = s_ref.shape[1]
    tiles = tk // vt_tile
    n_chunks = k_ref.shape[0] // tk
    tq = o_ref.shape[0] // n_str
    m_ref[...] = jnp.full(m_ref.shape, -jnp.inf, F32)
    acc_ref[...] = jnp.zeros(acc_ref.shape, F32)

    def scores(c, parity):
        k = k_ref[pl.ds(pl.multiple_of(c * tk, tk), tk), :]
        for st in range(n_str):
            s = jnp.dot(k, qt_refs[st][...], preferred_element_type=F32)
            s_ref[2 * st + parity] = s
            mx_ref[2 * st + parity] = jnp.max(s, axis=0, keepdims=True)

    def absorb(c, parity):
        for st in range(n_str):
            buf = 2 * st + parity
            m_old = m_ref[st]
            m_new = jnp.maximum(m_old, mx_ref[buf])
            alpha = jnp.exp2(m_old - m_new)
            pv = None
            for t in range(tiles):
                p = jnp.exp2((s_ref[buf, t * vt_tile:(t + 1) * vt_tile, :] - m_new).astype(BF16))
                part = jnp.dot(vt_ref[c * tiles + t], p, preferred_element_type=F32)
                pv = part if pv is None else pv + part
            acc_ref[st] = alpha * acc_ref[st] + pv
            m_ref[st] = m_new

    def group(i, carry):
        c0 = ATTN_UNROLL * i
        for u in range(ATTN_UNROLL):
            scores(c0 + u + 1, (u + 1) % 2)
            absorb(c0 + u, u % 2)
        return carry

    scores(0, 0)
    n_groups = (n_chunks - 1) // ATTN_UNROLL
    if n_groups > 0:
        lax.fori_loop(0, n_groups, group, 0)
    for c in range(n_groups * ATTN_UNROLL, n_chunks):
        if c + 1 < n_chunks:
            scores(c + 1, (c + 1) % 2)
        absorb(c, c % 2)
    _attn_epilogue(acc_ref, gate_refs, o_ref, n_str)


def _attn_epilogue(acc_ref, gate_refs, o_ref, n_str):
    tq = o_ref.shape[0] // n_str
    for st in range(n_str):
        o_t = acc_ref[st, 0:HEAD_DIM, :] / acc_ref[st, HEAD_DIM:HEAD_DIM + 1, :]
        o = jnp.concatenate([o_t[:, r * tq:(r + 1) * tq].T for r in range(ATT_REP)], axis=1)
        o_ref[st * tq:(st + 1) * tq, :] = (o * _silu(gate_refs[st][...].astype(F32))).astype(o_ref.dtype)


def _attn_bounded_kernel(*refs, n_str, tk):
    qt_refs = refs[0:n_str]
    k_ref, vt_ref = refs[n_str:n_str + 2]
    gate_refs = refs[n_str + 2:2 * n_str + 2]
    o_ref, acc_ref = refs[2 * n_str + 2:]
    vt_tile = vt_ref.shape[2]
    tiles = tk // vt_tile
    n_chunks = k_ref.shape[0] // tk
    acc_ref[...] = jnp.zeros(acc_ref.shape, F32)

    def chunk(c):
        k = k_ref[pl.ds(pl.multiple_of(c * tk, tk), tk), :]
        for st in range(n_str):
            p = jnp.exp2(jnp.dot(k, qt_refs[st][...], preferred_element_type=F32).astype(BF16))
            pv = None
            for t in range(tiles):
                part = jnp.dot(vt_ref[c * tiles + t, 0:HEAD_DIM, :], p[t * vt_tile:(t + 1) * vt_tile, :],
                               preferred_element_type=F32)
                pv = part if pv is None else pv + part
            acc_ref[st, 0:HEAD_DIM, :] += pv
            acc_ref[st, HEAD_DIM:HEAD_DIM + 1, :] += jnp.sum(p.astype(F32), axis=0, keepdims=True)

    def group(i, carry):
        for u in range(ATTN_BOUNDED_UNROLL):
            chunk(ATTN_BOUNDED_UNROLL * i + u)
        return carry

    n_groups = n_chunks // ATTN_BOUNDED_UNROLL
    if n_groups > 0:
        lax.fori_loop(0, n_groups, group, 0)
    for c in range(n_groups * ATTN_BOUNDED_UNROLL, n_chunks):
        chunk(c)
    _attn_epilogue(acc_ref, gate_refs, o_ref, n_str)


def _attention(qt, kh, vt, proj, geo, q_first_row, q_rows, kv_rows, bounded):
    tq = ROW_TILE
    assert q_first_row % tq == 0 and q_rows % tq == 0 and kv_rows % ROW_TILE == 0
    q0, n_tiles = q_first_row // tq, q_rows // tq
    gw = ATT_REP * HEAD_DIM
    mq = ATT_REP * tq
    depth = qt.shape[2]
    head = lambda b, g: b * ATT_KV_HEADS + g

    def build(kernel_fn, streams, scratch, name):
        n_str = streams if n_tiles % streams == 0 else 1
        tile = lambda i, st: q0 + n_str * i + st
        stream_specs = lambda make: [make(st) for st in range(n_str)]
        call = pl.pallas_call(
            functools.partial(kernel_fn, n_str=n_str),
            grid=(geo.batch, ATT_KV_HEADS, n_tiles // n_str),
            in_specs=[
                *stream_specs(lambda st: pl.BlockSpec((None, None, depth, mq),
                                                      lambda b, g, i: (head(b, g), tile(i, st), 0, 0))),
                pl.BlockSpec((None, kv_rows, depth), lambda b, g, i: (head(b, g), 0, 0)),
                pl.BlockSpec((None, kv_rows // ROW_TILE, VT_ROWS, ROW_TILE), lambda b, g, i: (head(b, g), 0, 0, 0)),
                *stream_specs(lambda st: pl.BlockSpec((tq, gw), lambda b, g, i: (b * geo.tpb + tile(i, st),
                                                                                 _COL["A_G"] // gw + g))),
            ],
            out_specs=pl.BlockSpec((n_str * tq, gw), lambda b, g, i: (b * (n_tiles // n_str) + i, g)),
            out_shape=jax.ShapeDtypeStruct((geo.batch * q_rows, BRANCH_WIDTH), BF16),
            scratch_shapes=scratch(n_str),
            compiler_params=_cparams(("parallel", "parallel", "arbitrary")),
            name=name)
        return lambda: call(*([qt] * n_str), kh, vt, *([proj] * n_str))

    tk = max(t for t in ATTN_KEY_CHUNKS if kv_rows % t == 0)
    online = build(_attn_kernel, ATTN_STREAMS,
                   lambda n: [pltpu.VMEM((n, 1, mq), F32), pltpu.VMEM((n, VT_ROWS, mq), F32),
                              pltpu.VMEM((2 * n, tk, mq), F32), pltpu.VMEM((2 * n, 1, mq), F32)],
                   "attention_online")
    tkb = max(t for t in ATTN_BOUNDED_KEY_CHUNKS if kv_rows % t == 0)
    fast = build(functools.partial(_attn_bounded_kernel, tk=tkb), ATTN_BOUNDED_STREAMS,
                 lambda n: [pltpu.VMEM((n, VT_ROWS, mq), F32)], "attention")
    return lax.cond(bounded, fast, online)


def _halo_specs(width, col_block, geo, first, row):
    per = ROW_TILE // SUBLANES
    last = geo.rows // SUBLANES - 1
    prev = pl.BlockSpec((SUBLANES, width), lambda b, j: (jnp.maximum(row(b, j) * per - 1, 0), col_block))
    nxt = pl.BlockSpec((SUBLANES, width), lambda b, j: (jnp.minimum((row(b, j) + 1) * per, last), col_block))
    return prev, nxt


def _fill_ext(ext_ref, cur, prev, nxt, j, geo):
    t = cur.shape[0]
    seg_first = jnp.logical_or(j == 0, j == geo.ctx_tiles)
    seg_last = jnp.logical_or(j == geo.ctx_tiles - 1, j == geo.tpb - 1)
    ext_ref[0:SUBLANES, :] = jnp.where(seg_first, 0.0, prev)
    ext_ref[SUBLANES:SUBLANES + t, :] = cur
    ext_ref[SUBLANES + t:2 * SUBLANES + t, :] = jnp.where(seg_last, 0.0, nxt)


def _ssd_conv_kernel(cur_ref, prev_ref, next_ref, nar_ref, w_ref, b_ref, dtb_ref, shift_ref, xbc_ref, dt_ref, ext_ref,
                     *, geo):
    j = pl.program_id(1)
    cur = cur_ref[...]
    cur_f = cur.astype(F32)
    _fill_ext(ext_ref, cur_f, prev_ref[...].astype(F32), next_ref[...].astype(F32), j, geo)
    t, half = cur_ref.shape[0], SSD_CONV // 2
    acc = b_ref[...] + w_ref[half:half + 1, :] * cur_f
    taps = [k for k in range(SSD_CONV) if k != half]
    for i, k in enumerate(taps):
        acc = acc + w_ref[k:k + 1, :] * jnp.dot(shift_ref[i], cur, preferred_element_type=F32)

    def edge(r0):
        e = jnp.zeros((SUBLANES, cur_ref.shape[1]), F32) + b_ref[...]
        for k in range(SSD_CONV):
            e = e + w_ref[k:k + 1, :] * ext_ref[SUBLANES + r0 + k - half:2 * SUBLANES + r0 + k - half, :]
        return e

    acc = jnp.concatenate([edge(0), acc[SUBLANES:t - SUBLANES], edge(t - SUBLANES)], axis=0)
    xbc_ref[...] = _silu(acc).astype(xbc_ref.dtype)
    lane = lax.broadcasted_iota(jnp.int32, dt_ref.shape, 1)
    dt = _softplus(nar_ref[...].astype(F32) + dtb_ref[...])
    dt_ref[...] = jnp.where(lane < 2 * SSD_HEADS, dt, 0.0)


def _ssd_conv(proj, conv_w, conv_b, dt_bias_row, geo):
    width = BRANCH_WIDTH + 2 * SSD_GROUPS * SSD_STATE
    row = lambda b, j: b * geo.tpb + j
    cb = _COL["B_X"] // width
    prev, nxt = _halo_specs(width, cb, geo, 0, row)
    t_idx = np.arange(ROW_TILE)
    offsets = [k - SSD_CONV // 2 for k in range(SSD_CONV) if k != SSD_CONV // 2]
    shifts = jnp.asarray(np.stack([(t_idx[None, :] == t_idx[:, None] + d) for d in offsets]).astype(np.float32), BF16)
    return pl.pallas_call(
        functools.partial(_ssd_conv_kernel, geo=geo),
        grid=(geo.batch, geo.tpb),
        in_specs=[pl.BlockSpec((ROW_TILE, width), lambda b, j: (row(b, j), cb)), prev, nxt,
                  pl.BlockSpec((ROW_TILE, LANES), lambda b, j: (row(b, j), _COL["NARROW"] // LANES)),
                  pl.BlockSpec((SSD_CONV, width), lambda b, j: (0, 0)),
                  pl.BlockSpec((1, width), lambda b, j: (0, 0)),
                  pl.BlockSpec((1, LANES), lambda b, j: (0, 0)),
                  pl.BlockSpec((len(offsets), ROW_TILE, ROW_TILE), lambda b, j: (0, 0, 0))],
        out_specs=[pl.BlockSpec((ROW_TILE, width), lambda b, j: (row(b, j), 0)),
                   pl.BlockSpec((ROW_TILE, LANES), lambda b, j: (row(b, j), 0))],
        out_shape=[jax.ShapeDtypeStruct((geo.rows, width), BF16),
                   jax.ShapeDtypeStruct((geo.rows, LANES), F32)],
        scratch_shapes=[pltpu.VMEM((ROW_TILE + 2 * SUBLANES, width), F32)],
        compiler_params=_cparams(("parallel", "parallel")),
        name="ssd_conv",
    )(proj, proj, proj, proj, conv_w, conv_b, dt_bias_row, shifts)


def _scan_chunk(s, n_chunks, n_ctx_chunks, reverse):
    if not reverse:
        return s
    return jnp.where(s < n_ctx_chunks, n_ctx_chunks - 1 - s, n_chunks + n_ctx_chunks - 1 - s)


def _tri(n, reverse):
    t = lax.broadcasted_iota(jnp.int32, (n, n), 0)
    s = lax.broadcasted_iota(jnp.int32, (n, n), 1)
    return (s >= t) if reverse else (s <= t)


def _split_bf16(x, pieces):
    out, rest = [], x
    for _ in range(pieces):
        p = rest.astype(BF16)
        out.append(p)
        rest = rest - p.astype(F32)
    return out


def _cumsum_rows(mask_bf16, x):
    w = x.shape[1]
    parts = jnp.dot(mask_bf16, jnp.concatenate(_split_bf16(x, 3), axis=1), preferred_element_type=F32)
    return parts[:, 0:w] + parts[:, w:2 * w] + parts[:, 2 * w:3 * w]


def _ssd_scan_kernel(xbc_ref, dt_ref, alog_ref, e_ref, *rest, reverse):
    if reverse:
        yf_ref, z_ref, skip_ref, g_ref, y_ref, h_ref = rest
    else:
        y_ref, h_ref = rest
    q = SSD_CHUNK
    gw = SSD_HPG * SSD_HEAD_DIM
    quad = 4
    qw = quad * SSD_HEAD_DIM
    lane0 = SSD_HEADS if reverse else 0
    n_batch, n_sub = xbc_ref.shape[0], xbc_ref.shape[1] // q

    @pl.when(pl.program_id(0) == 0)
    def _():
        h_ref[...] = jnp.zeros_like(h_ref)

    lane = lax.broadcasted_iota(jnp.int32, (1, LANES), 1)
    a = jnp.where(lane < 2 * SSD_HEADS, -jnp.exp(alog_ref[...]), 0.0)
    mask = _tri(q, reverse)
    mask_bf = jnp.where(mask, 1.0, 0.0).astype(BF16)
    e = e_ref[...]
    head_of_lane = lax.broadcasted_iota(jnp.int32, (1, qw), 1) // SSD_HEAD_DIM
    zero_bf = jnp.zeros((), BF16)

    order = range(n_sub - 1, -1, -1) if reverse else range(n_sub)
    for ci, b in [(ci, b) for ci in order for b in range(n_batch)]:
        rows = slice(ci * q, (ci + 1) * q)
        dt = dt_ref[b, rows, :]
        cs = _cumsum_rows(mask_bf, dt * a)
        cs_last = cs[0:1, :] if reverse else cs[q - 1:q, :]
        zeros = jnp.zeros((q, LANES), BF16)
        ecs_hi, ecs_lo = _split_bf16(jnp.exp(cs), 2)
        cd_hi, cd_mid, cd_lo = _split_bf16(jnp.broadcast_to(jnp.exp(cs_last), (BF16_SUBLANES, LANES)), 3)
        stack = jnp.concatenate([
            jnp.concatenate([(jnp.exp(cs_last - cs) * dt).astype(BF16), zeros], axis=1),
            jnp.concatenate([ecs_hi, ecs_lo], axis=1),
            jnp.concatenate([cd_hi, cd_mid], axis=1),
            jnp.concatenate([cd_lo, zeros[0:BF16_SUBLANES]], axis=1)], axis=0)
        big = jnp.dot(stack, e, preferred_element_type=F32)
        x_bf = xbc_ref[b, rows, 0:BRANCH_WIDTH]
        x = x_bf.astype(F32)
        wx = (big[0:q] * x).astype(BF16)
        ecs = big[q:2 * q]
        r0 = 2 * q
        chunk_decay = big[r0:r0 + 1] + big[r0 + BF16_SUBLANES:r0 + BF16_SUBLANES + 1]
        cs_t = cs.T
        dt_t = dt.T
        pieces = []
        for g in range(SSD_GROUPS):
            bm = xbc_ref[b, rows, BRANCH_WIDTH + g * SSD_STATE:BRANCH_WIDTH + (g + 1) * SSD_STATE]
            c0 = BRANCH_WIDTH + SSD_GROUPS * SSD_STATE + g * SSD_STATE
            cm = xbc_ref[b, rows, c0:c0 + SSD_STATE]
            cb = lax.dot_general(cm, bm, (((1,), (1,)), ((), ())), preferred_element_type=F32)
            h_in = h_ref[b, g]
            y_off = jnp.dot(cm, h_in.astype(BF16), preferred_element_type=F32) * ecs[:, g * gw:(g + 1) * gw]
            states = jnp.dot(bm.astype(F32).T.astype(BF16), wx[:, g * gw:(g + 1) * gw], preferred_element_type=F32)
            h_ref[b, g] = chunk_decay[:, g * gw:(g + 1) * gw] * h_in + states
            for hq in range(SSD_HPG // quad):
                ms = []
                for r in range(quad):
                    col = lane0 + g * SSD_HPG + hq * quad + r
                    seg = cs[:, col:col + 1] - cs_t[col:col + 1, :]
                    ms.append((cb * jnp.exp(jnp.where(mask, seg, -1e30)) * dt_t[col:col + 1, :]).astype(BF16))
                lo = g * gw + hq * qw
                slab = x_bf[:, lo:lo + qw]
                rhs = jnp.concatenate([jnp.where(head_of_lane == r, slab, zero_bf) for r in range(quad)], axis=0)
                y_diag = jnp.dot(jnp.concatenate(ms, axis=1), rhs, preferred_element_type=F32)
                pieces.append(y_diag + y_off[:, hq * qw:(hq + 1) * qw])
        y = jnp.concatenate(pieces, axis=1)
        if reverse:
            y = (yf_ref[b, rows, :] + y + skip_ref[...] * x) * _silu(z_ref[b, rows, :].astype(F32))
            y = y * lax.rsqrt(jnp.mean(y * y, axis=-1, keepdims=True) + NORM_EPS) * g_ref[...]
        y_ref[b, rows, :] = y.astype(y_ref.dtype)


def _ssd_scan(xbc, dt, a_log_row, expand_mat, geo, reverse, finish=None):
    nb, nbc = geo.lt // SCAN_BLOCK, geo.ctx_len // SCAN_BLOCK
    width, w, nbat = xbc.shape[1], BRANCH_WIDTH, geo.batch
    blk = lambda s: _scan_chunk(s, nb, nbc, reverse)
    per_batch = lambda a: a.reshape(nbat, geo.lt, a.shape[1])
    tok_spec = lambda cols, col_block: pl.BlockSpec((nbat, SCAN_BLOCK, cols), lambda s: (0, blk(s), col_block))
    const_spec = lambda shape: pl.BlockSpec(shape, lambda s: (0, 0))
    in_specs = [tok_spec(width, 0), tok_spec(LANES, 0), const_spec((1, LANES)), const_spec((2 * LANES, w))]
    args = [per_batch(xbc), per_batch(dt), a_log_row, expand_mat]
    if reverse:
        yf, proj, skip_row, norm_g = finish
        in_specs += [tok_spec(w, 0), tok_spec(w, _COL["B_Z"] // w), const_spec((1, w)), const_spec((1, w))]
        args += [per_batch(yf), per_batch(proj), skip_row, norm_g]
    out = pl.pallas_call(
        functools.partial(_ssd_scan_kernel, reverse=reverse),
        grid=(nb,),
        in_specs=in_specs,
        out_specs=tok_spec(w, 0),
        out_shape=jax.ShapeDtypeStruct((nbat, geo.lt, w), BF16 if reverse else F32),
        scratch_shapes=[pltpu.VMEM((nbat, SSD_GROUPS, SSD_STATE, SSD_HPG * SSD_HEAD_DIM), F32)],
        compiler_params=_cparams(("arbitrary",)),
        name="ssd_scan_bwd" if reverse else "ssd_scan_fwd",
    )(*args)
    return out.reshape(geo.rows, w)


def _gla_scan_kernel(q_ref, k_ref, v_ref, nar_ref, w2_ref, bf_ref, *rest, reverse):
    if reverse:
        of_ref, gate_ref, g_ref, o_ref, h_ref = rest
    else:
        o_ref, h_ref = rest
    n = GLA_CHUNK

    @pl.when(pl.program_id(0) == 0)
    def _():
        h_ref[...] = jnp.zeros_like(h_ref)

    n_batch, t = q_ref.shape[0], q_ref.shape[1]
    n_sub = t // n
    ri = lax.broadcasted_iota(jnp.int32, (t, t), 0)
    ci = lax.broadcasted_iota(jnp.int32, (t, t), 1)
    mask = jnp.logical_and(ri // n == ci // n, (ci >= ri) if reverse else (ci <= ri))
    mask_bf = jnp.where(mask, 1.0, 0.0).astype(BF16)
    for bi in range(n_batch):
        logit = jnp.dot(nar_ref[bi], w2_ref[...], preferred_element_type=F32) + bf_ref[...]
        gl = _log_sigmoid(logit) / GLA_TAU
        b = _cumsum_rows(mask_bf, gl)
        last = [b[c * n:c * n + 1, :] if reverse else b[(c + 1) * n - 1:(c + 1) * n, :] for c in range(n_sub)]
        b_last = jnp.concatenate([jnp.broadcast_to(r, (n, r.shape[1])) for r in last], axis=0)
        q = q_ref[bi].astype(F32) * (GLA_DK ** -0.5)
        k = k_ref[bi].astype(F32)
        qe = (q * jnp.exp(b)).astype(BF16)
        ke = (k * jnp.exp(-b)).astype(BF16)
        kd = (k * jnp.exp(b_last - b)).astype(BF16)
        for h in range(GLA_HEADS):
            ks = slice(h * GLA_DK, (h + 1) * GLA_DK)
            vs = slice(h * GLA_DV, (h + 1) * GLA_DV)
            v = v_ref[bi, :, vs]
            att = lax.dot_general(qe[:, ks], ke[:, ks], (((1,), (1,)), ((), ())), preferred_element_type=F32)
            o_intra = jnp.dot(jnp.where(mask, att, 0.0).astype(BF16), v, preferred_element_type=F32)
            o_inter = [None] * n_sub
            for c in (range(n_sub - 1, -1, -1) if reverse else range(n_sub)):
                rows = slice(c * n, (c + 1) * n)
                h_in = h_ref[bi, h]
                o_inter[c] = lax.dot_general(qe[rows, ks], h_in.astype(BF16), (((1,), (1,)), ((), ())),
                                             preferred_element_type=F32)
                upd = jnp.dot(v[rows].astype(F32).T.astype(BF16), kd[rows, ks], preferred_element_type=F32)
                h_ref[bi, h] = jnp.exp(last[c][:, ks]) * h_in + upd
            o = o_intra + jnp.concatenate(o_inter, axis=0)
            if reverse:
                o = o + of_ref[bi, :, vs]
                o = o * lax.rsqrt(jnp.mean(o * o, axis=-1, keepdims=True) + NORM_EPS) * g_ref[...]
                o = o * _silu(gate_ref[bi, :, vs].astype(F32))
            o_ref[bi, :, vs] = o.astype(o_ref.dtype)


def _gla_scan(proj, w2, b_f, geo, reverse, finish=None):
    nb, nbc, nbat = geo.lt // SCAN_BLOCK, geo.ctx_len // SCAN_BLOCK, geo.batch
    blk = lambda s: _scan_chunk(s, nb, nbc, reverse)
    kw, vw = GLA_HEADS * GLA_DK, GLA_HEADS * GLA_DV
    per_batch = lambda a: a.reshape(nbat, geo.lt, a.shape[1])
    tok_spec = lambda cols, col_block: pl.BlockSpec((nbat, SCAN_BLOCK, cols), lambda s: (0, blk(s), col_block))
    const_spec = lambda shape: pl.BlockSpec(shape, lambda s: (0, 0))
    proj3 = per_batch(proj)
    in_specs = [tok_spec(kw, _COL["C_Q"] // kw), tok_spec(kw, _COL["C_K"] // kw), tok_spec(vw, _COL["C_V"] // vw),
                tok_spec(LANES, _COL["NARROW"] // LANES), const_spec((LANES, kw)), const_spec((1, kw))]
    args = [proj3, proj3, proj3, proj3, w2, b_f]
    if reverse:
        of, norm_g = finish
        in_specs += [tok_spec(vw, 0), tok_spec(vw, _COL["C_G"] // vw), const_spec((1, GLA_DV))]
        args += [per_batch(of), proj3, norm_g]
    out = pl.pallas_call(
        functools.partial(_gla_scan_kernel, reverse=reverse),
        grid=(nb,),
        in_specs=in_specs,
        out_specs=tok_spec(vw, 0),
        out_shape=jax.ShapeDtypeStruct((nbat, geo.lt, vw), BF16 if reverse else F32),
        scratch_shapes=[pltpu.VMEM((nbat, GLA_HEADS, GLA_DV, GLA_DK), F32)],
        compiler_params=_cparams(("arbitrary",)),
        name="gla_scan_bwd" if reverse else "gla_scan_fwd",
    )(*args)
    return out.reshape(geo.rows, vw)


def _shortconv_tile(cur_ref, prev_ref, next_ref, w_ref, ext_ref, tile, geo):
    w = BRANCH_WIDTH
    u = lambda ref: ref[:, w:2 * w].astype(F32) * ref[:, 2 * w:3 * w].astype(F32)
    _fill_ext(ext_ref, u(cur_ref), u(prev_ref), u(next_ref), tile, geo)
    t = cur_ref.shape[0]
    acc = jnp.zeros((t, w), F32)
    for k in range(SC_CONV):
        acc = acc + w_ref[k:k + 1, :] * ext_ref[SUBLANES + k - SC_CONV // 2:SUBLANES + k - SC_CONV // 2 + t, :]
    return cur_ref[:, 0:w].astype(F32) * acc * _silu(cur_ref[:, 3 * w:4 * w].astype(F32))


def _merge_out_kernel(*refs, n_ya, n_src, geo, first, with_next):
    tile = first + pl.program_id(1)
    ext_ref = refs[-1]
    ya = _read_tokens(refs[0:n_ya], tile, geo)
    refs = refs[n_ya:-1]
    yd = _shortconv_tile(*refs[2:6], ext_ref, tile, geo).astype(BF16)
    ys = [ya, refs[0][...], refs[1][...], yd]
    refs = refs[6:]
    gates = refs[0:N_BRANCH]
    wb_ref, wo_ref = refs[N_BRANCH:N_BRANCH + 2]
    p = N_BRANCH + 2
    srcs = refs[p:p + n_src]
    gt_ref, gp_ref = refs[p + n_src:p + n_src + 2]
    rest = refs[p + n_src + 2:]
    m = None
    for n in (N_BRANCH - 1, 0, 1, 2):
        term = jax.nn.sigmoid(gates[n][...].astype(F32)) * jnp.dot(ys[n], wb_ref[n], preferred_element_type=F32)
        m = term if m is None else m + term
    out = jnp.dot(m.astype(BF16), wo_ref[...], preferred_element_type=F32)
    y = out * lax.rsqrt(jnp.mean(out * out, axis=-1, keepdims=True) + NORM_EPS) * gp_ref[...]
    x_new = _read_tokens(srcs, tile, geo) + gt_ref[0] * y
    if with_next:
        gn_ref, shn_ref, scn_ref, o_ref, h_ref = rest
        h_ref[...] = _adaln_prenorm(x_new, gn_ref[...], shn_ref[0], scn_ref[0]).astype(h_ref.dtype)
    else:
        (o_ref,) = rest
    o_ref[...] = x_new


def _merge_out(ys, proj, w_branch, w_out, layer, tokens, mod3, g_post, geo, with_ctx, nxt=None):
    first, nt = geo.span(with_ctx)
    w, d = BRANCH_WIDTH, D_MODEL
    row = lambda b, j: b * geo.tpb + first + j
    out_rows = geo.rows if with_ctx else geo.batch * geo.seq
    out_row = row if with_ctx else (lambda b, j: b * nt + j)
    y_spec = pl.BlockSpec((ROW_TILE, w), lambda b, j: (row(b, j), 0))
    gate_specs = [pl.BlockSpec((ROW_TILE, d), functools.partial(lambda b, j, n: (row(b, j), _COL["MG"] // d + n), n=n))
                  for n in range(N_BRANCH)]
    resident = pl.Buffered(1)
    mod_spec = lambda part: pl.BlockSpec((1, 1, d), lambda b, j: (geo.mod_row(b, first + j), 0, part))
    vec_spec = pl.BlockSpec((1, d), lambda b, j: (0, 0))
    ya, yb, yg, conv_w = ys
    dw = N_BRANCH * w
    d_block = _COL["D_ALL"] // dw
    d_prev, d_next = _halo_specs(dw, d_block, geo, first, row)
    in_specs = [*_token_specs(ya, geo, first), y_spec, y_spec,
                pl.BlockSpec((ROW_TILE, dw), lambda b, j: (row(b, j), d_block)), d_prev, d_next,
                pl.BlockSpec((SC_CONV, w), lambda b, j: (0, 0)), *gate_specs,
                pl.BlockSpec((None, N_BRANCH, w, d), lambda b, j: (layer, 0, 0, 0), pipeline_mode=resident),
                pl.BlockSpec((None, d, d), lambda b, j: (layer, 0, 0), pipeline_mode=resident),
                *_token_specs(tokens, geo, first), mod_spec(2), vec_spec]
    args = [*ya, yb, yg, proj, proj, proj, conv_w, proj, proj, proj, proj, w_branch, w_out, *tokens, mod3, g_post]
    out_specs = [pl.BlockSpec((ROW_TILE, d), lambda b, j: (out_row(b, j), 0))]
    out_shape = [jax.ShapeDtypeStruct((out_rows, d), F32)]
    if nxt is not None:
        assert with_ctx
        g_next, mod3_next = nxt
        in_specs += [vec_spec, mod_spec(0), mod_spec(1)]
        args += [g_next, mod3_next, mod3_next]
        out_specs.append(pl.BlockSpec((ROW_TILE, d), lambda b, j: (row(b, j), 0)))
        out_shape.append(jax.ShapeDtypeStruct((geo.rows, d), BF16))
    outs = pl.pallas_call(
        functools.partial(_merge_out_kernel, n_ya=len(ya), n_src=len(tokens), geo=geo, first=first,
                          with_next=nxt is not None),
        grid=(geo.batch, nt),
        in_specs=in_specs,
        out_specs=out_specs,
        out_shape=out_shape,
        scratch_shapes=[pltpu.VMEM((ROW_TILE + 2 * SUBLANES, w), F32)],
        compiler_params=_cparams(("parallel", "parallel")),
        name="merge_out",
    )(*args)
    return outs if nxt is not None else (outs[0], None)


def _regroup_plan():
    bw = BRANCH_WIDTH
    names = ("a_q", "a_k", "a_v", "a_g", "b_x", "b_z", "b_b", "b_c", "b_dt", "c_q", "c_k", "c_v", "c_g", "c_f",
             "d_all", "mg")
    widths = (ATT_HEADS * HEAD_DIM, ATT_KV_HEADS * HEAD_DIM, ATT_KV_HEADS * HEAD_DIM, bw,
              bw, bw, SSD_GROUPS * SSD_STATE, SSD_GROUPS * SSD_STATE, 2 * SSD_HEADS,
              GLA_HEADS * GLA_DK, GLA_HEADS * GLA_DK, GLA_HEADS * GLA_DV, bw, 2 * GLA_RANK,
              4 * bw, N_BRANCH * D_MODEL)
    src = dict(zip(names, np.concatenate([[0], np.cumsum(widths)[:-1]]).tolist()))
    wid = dict(zip(names, widths))
    dst = dict(a_q=_COL["A_Q"], a_g=_COL["A_G"], b_z=_COL["B_Z"], b_x=_COL["B_X"], b_b=_COL["B_B"], b_c=_COL["B_C"],
               a_k=_COL["A_K"], a_v=_COL["A_V"], c_v=_COL["C_V"], c_g=_COL["C_G"], c_q=_COL["C_Q"], c_k=_COL["C_K"],
               d_all=_COL["D_ALL"], mg=_COL["MG"], b_dt=_COL["NARROW"] + DT_LANE0, c_f=_COL["NARROW"] + F1_LANE0)
    return [(dst[n], src[n], wid[n], n in ("a_q", "a_k")) for n in names], sum(widths)


def _regroup_kernel(w_ref, o_ref):
    plan, _ = _regroup_plan()
    half = HEAD_DIM // 2
    for dst, src, width, deinterleave in plan:
        if deinterleave:
            for h in range(width // HEAD_DIM):
                s, d = src + h * HEAD_DIM, dst + h * HEAD_DIM
                o_ref[d:d + half, :] = w_ref[pl.ds(s, half, stride=2), :].astype(o_ref.dtype)
                o_ref[d + half:d + HEAD_DIM, :] = w_ref[pl.ds(s + 1, half, stride=2), :].astype(o_ref.dtype)
        else:
            o_ref[dst:dst + width, :] = w_ref[src:src + width, :].astype(o_ref.dtype)
    used = _COL["NARROW"] + F1_LANE0 + 2 * GLA_RANK
    o_ref[used:N_PROJ, :] = jnp.zeros((N_PROJ - used, o_ref.shape[1]), o_ref.dtype)


def _regroup_w_in(w_in, layer):
    w_t = jnp.swapaxes(w_in, 1, 2)
    _, n, k = w_t.shape
    assert n == _regroup_plan()[1]
    tk = LANES
    return pl.pallas_call(
        _regroup_kernel,
        grid=(k // tk,),
        in_specs=[pl.BlockSpec((None, n, tk), lambda i: (layer, 0, i))],
        out_specs=pl.BlockSpec((N_PROJ, tk), lambda i: (0, i)),
        out_shape=jax.ShapeDtypeStruct((N_PROJ, k), BF16),
        compiler_params=_cparams(("parallel",)),
        name="regroup_w_in",
    )(w_t)


def _deinterleave_vec(g):
    return g.reshape(HEAD_DIM // 2, 2).T.reshape(1, HEAD_DIM)


def _pad_lanes(v, lane0=0):
    return jnp.zeros((1, LANES), F32).at[0, lane0:lane0 + v.shape[0]].set(v.astype(F32))


def _head_expand_matrix(reverse):
    e = np.zeros((LANES, BRANCH_WIDTH), np.float32)
    lane0 = SSD_HEADS if reverse else 0
    for r in range(SSD_HEADS):
        e[lane0 + r, r * SSD_HEAD_DIM:(r + 1) * SSD_HEAD_DIM] = 1.0
    return jnp.asarray(np.concatenate([e, e], axis=0), BF16)


def _forget_weight(w_f2_dir, direction):
    lane0 = F1_LANE0 + direction * GLA_RANK
    return jnp.zeros((LANES, w_f2_dir.shape[1]), F32).at[lane0:lane0 + GLA_RANK].set(w_f2_dir).astype(BF16)


def kernel(x, c, ctx, c_ctx, w_mod, b_mod, g_pre, g_post, w_in, g_q, g_k, ssd_conv_w, ssd_conv_b,
           ssd_a_log, ssd_dt_bias, ssd_d, ssd_norm_g, gla_w_f2, gla_b_f, gla_norm_g, sc_conv_w,
           w_branch, w_out):
    batch, seq, d = x.shape
    ctx_len = ctx.shape[1]
    depth = w_in.shape[0]
    geo = _Geom(batch, ctx_len, seq)
    assert d == D_MODEL and batch + 1 <= SUBLANES and seq % GRID_W == 0

    cos_t, sin_t = _rope_tables(geo)
    c_rows = jnp.zeros((SUBLANES, d), F32).at[:batch].set(c).at[batch].set(c_ctx)
    tokens = (ctx.reshape(batch * ctx_len, d), x.reshape(batch * seq, d))
    e_fwd, e_bwd = _head_expand_matrix(False), _head_expand_matrix(True)
    w_branch_bf, w_out_bf = w_branch.astype(BF16), w_out.astype(BF16)
    mods =[_modulation(c_rows, w_mod, b_mod[l][None, :], l).reshape(SUBLANES, 1, 3 * d) for l in range(depth)]
    h = _prenorm(tokens, g_pre[0][None, :], mods[0], geo)

    for l in range(depth):
        need_ctx = l < depth - 1
        mod3 = mods[l]
        proj = _matmul(h, _regroup_w_in(w_in, l), PROJ_TN)

        shift = Q_SCALE * HEAD_DIM * jnp.max(jnp.abs(g_q[l])) * jnp.max(jnp.abs(g_k[l]))
        bounded = 2.0 * shift <= ATTN_SAFE_EXPONENT
        shift_row = jnp.full((1, MXU_WIDTH - HEAD_DIM), shift, F32)
        qt, kh, vt = _qk_prep(proj, cos_t, sin_t, _deinterleave_vec(g_q[l]), _deinterleave_vec(g_k[l]), shift_row, geo)
        ya = (_attention(qt, kh, vt, proj, geo, ctx_len, seq, geo.lt, bounded),)
        if need_ctx:
            ya = (_attention(qt, kh, vt, proj, geo, 0, ctx_len, ctx_len, bounded), *ya)

        xbc, dt = _ssd_conv(proj, ssd_conv_w[l], ssd_conv_b[l][None, :], _pad_lanes(ssd_dt_bias[l].reshape(-1)), geo)
        a_log_row = _pad_lanes(ssd_a_log[l].reshape(-1))
        ysf = _ssd_scan(xbc, dt, a_log_row, e_fwd, geo, False)
        skip_row = jnp.repeat(ssd_d[l], SSD_HEAD_DIM)[None, :]
        yb = _ssd_scan(xbc, dt, a_log_row, e_bwd, geo, True, (ysf, proj, skip_row, ssd_norm_g[l][None, :]))

        ogf = _gla_scan(proj, _forget_weight(gla_w_f2[l, 0], 0), gla_b_f[l, 0][None, :], geo, False)
        yg = _gla_scan(proj, _forget_weight(gla_w_f2[l, 1], 1), gla_b_f[l, 1][None, :], geo, True,
                       (ogf, gla_norm_g[l][None, :]))

        nxt = (g_pre[l + 1][None, :], mods[l + 1]) if need_ctx else None
        x_new, h = _merge_out((ya, yb, yg, sc_conv_w[l]), proj, w_branch_bf, w_out_bf, l, tokens, mod3,
                              g_post[l][None, :], geo, need_ctx, nxt)
        tokens = (x_new,)

    return x_new.reshape(batch, seq, d)
```

```python
import functools

import numpy as np
import jax
import jax.numpy as jnp
from jax import lax
from jax.experimental import pallas as pl
from jax.experimental.pallas import tpu as pltpu

F32 = jnp.float32
BF16 = jnp.bfloat16

D_MODEL = 2048
GRID_W = 64
BRANCH_WIDTH = D_MODEL // 2
N_BRANCH = 4
NORM_EPS = 1e-6
HEAD_DIM = 128
ATT_HEADS = BRANCH_WIDTH // HEAD_DIM
ATT_KV_HEADS = ATT_HEADS // 4
ATT_REP = ATT_HEADS // ATT_KV_HEADS
ROPE_THETA = 10000.0
SSD_HEAD_DIM = 64
SSD_HEADS = BRANCH_WIDTH // SSD_HEAD_DIM
SSD_GROUPS = 2
SSD_HPG = SSD_HEADS // SSD_GROUPS
SSD_STATE = 128
SSD_CONV = 5
SSD_CHUNK = 128
GLA_HEADS = 4
GLA_DV = BRANCH_WIDTH // GLA_HEADS
GLA_DK = GLA_DV // 2
GLA_RANK = 16
GLA_TAU = 16.0
GLA_CHUNK = 64
SC_CONV = 3

LANES = 128
SUBLANES = 8
ROW_TILE = 256
SCAN_BLOCK = ROW_TILE
VMEM_LIMIT = 56 * 1024 * 1024

_COL = dict(
    A_Q=0, A_G=1024, B_Z=2048, B_X=3072, B_B=4096, B_C=4352, A_K=4608, A_V=4864,
    C_V=5120, C_G=6144, C_Q=7168, C_K=7680,
    D_ALL=8192, MG=12288, NARROW=20480,
)
MXU_WIDTH = 256
N_PROJ = 20736
PROJ_TN = 2304
ATTN_UNROLL = 2
ATTN_STREAMS = 2
ATTN_BOUNDED_KEY_CHUNKS = (ROW_TILE, 3 * ROW_TILE)
ATTN_BOUNDED_UNROLL = 11
ATTN_BOUNDED_STREAMS = 2
ATTN_SAFE_EXPONENT = 100.0
ATTN_KEY_CHUNKS = (ROW_TILE, 2 * ROW_TILE, 3 * ROW_TILE)
BF16_SUBLANES = 16
VT_ROWS = HEAD_DIM + BF16_SUBLANES
Q_SCALE = HEAD_DIM ** -0.5 * float(np.log2(np.e))
DT_LANE0 = 0
F1_LANE0 = 32


def _cparams(sem, vmem=VMEM_LIMIT):
    return pltpu.CompilerParams(dimension_semantics=sem, vmem_limit_bytes=vmem)


def _silu(x):
    return x * jax.nn.sigmoid(x)


def _softplus(x):
    return jnp.maximum(x, 0.0) + jnp.log1p(jnp.exp(-jnp.abs(x)))


def _log_sigmoid(x):
    return jnp.minimum(x, 0.0) - jnp.log(1.0 + jnp.exp(-jnp.abs(x)))


def _mod_kernel(c_ref, w_ref, b_ref, o_ref):
    a = _silu(c_ref[...]).astype(BF16)
    o_ref[...] = jnp.dot(a, w_ref[...].astype(BF16), preferred_element_type=F32) + b_ref[...]


def _modulation(c_rows, w_mod, b_mod, layer):
    _, d, n = w_mod.shape
    tn = 1024
    return pl.pallas_call(
        _mod_kernel,
        grid=(n // tn,),
        in_specs=[pl.BlockSpec((SUBLANES, d), lambda j: (0, 0)),
                  pl.BlockSpec((None, d, tn), lambda j: (layer, 0, j)),
                  pl.BlockSpec((1, tn), lambda j: (0, j))],
        out_specs=pl.BlockSpec((SUBLANES, tn), lambda j: (0, j)),
        out_shape=jax.ShapeDtypeStruct((SUBLANES, n), F32),
        compiler_params=_cparams(("parallel",)),
        name="modulation",
    )(c_rows, w_mod, b_mod)


class _Geom:
    def __init__(self, batch, ctx_len, seq):
        self.batch, self.ctx_len, self.seq = batch, ctx_len, seq
        self.lt = ctx_len + seq
        self.rows = batch * self.lt
        assert ctx_len % ROW_TILE == 0 and seq % ROW_TILE == 0
        self.tpb = self.lt // ROW_TILE
        self.ctx_tiles = ctx_len // ROW_TILE
        self.lat_tiles = seq // ROW_TILE

    def span(self, with_ctx):
        return (0, self.tpb) if with_ctx else (self.ctx_tiles, self.lat_tiles)

    def mod_row(self, b, j):
        return jnp.where(j < self.ctx_tiles, self.batch, b)


def _token_specs(tokens, geo, first):
    c = tokens[0].shape[1]
    ct, lt = geo.ctx_tiles, geo.lat_tiles
    if len(tokens) == 1 and tokens[0].shape[0] == geo.rows:
        return [pl.BlockSpec((ROW_TILE, c), lambda b, j: (b * geo.tpb + first + j, 0))]
    if len(tokens) == 1:
        assert tokens[0].shape[0] == geo.batch * geo.seq and first >= ct
        return [pl.BlockSpec((ROW_TILE, c), lambda b, j: (b * lt + first + j - ct, 0))]
    return [pl.BlockSpec((ROW_TILE, c), lambda b, j: (b * ct + jnp.minimum(first + j, ct - 1), 0),
                         pipeline_mode=pl.Buffered(1)),
            pl.BlockSpec((ROW_TILE, c), lambda b, j: (b * lt + jnp.maximum(first + j - ct, 0), 0))]


def _read_tokens(refs, tile, geo):
    if len(refs) == 1:
        return refs[0][...]
    return jnp.where(tile < geo.ctx_tiles, refs[0][...], refs[1][...])


def _adaln_prenorm(x, g, sh, sc):
    y = x * lax.rsqrt(jnp.mean(x * x, axis=-1, keepdims=True) + NORM_EPS)
    return (y * g) * (1.0 + sc) + sh


def _prenorm_kernel(*refs, n_src, geo):
    g_ref, sh_ref, sc_ref, o_ref = refs[n_src:]
    x = _read_tokens(refs[:n_src], pl.program_id(1), geo)
    o_ref[...] = _adaln_prenorm(x, g_ref[...], sh_ref[0], sc_ref[0]).astype(o_ref.dtype)


def _prenorm(tokens, g_pre, mod3, geo):
    d = D_MODEL
    return pl.pallas_call(
        functools.partial(_prenorm_kernel, n_src=len(tokens), geo=geo),
        grid=(geo.batch, geo.tpb),
        in_specs=[*_token_specs(tokens, geo, 0),
                  pl.BlockSpec((1, d), lambda b, j: (0, 0)),
                  pl.BlockSpec((1, 1, d), lambda b, j: (geo.mod_row(b, j), 0, 0)),
                  pl.BlockSpec((1, 1, d), lambda b, j: (geo.mod_row(b, j), 0, 1))],
        out_specs=pl.BlockSpec((ROW_TILE, d), lambda b, j: (b * geo.tpb + j, 0)),
        out_shape=jax.ShapeDtypeStruct((geo.rows, d), BF16),
        compiler_params=_cparams(("parallel", "parallel")),
        name="prenorm",
    )(*tokens, g_pre, mod3, mod3)


def _matmul_kernel(a_ref, w_ref, o_ref):
    o_ref[...] = lax.dot_general(a_ref[...], w_ref[...], (((1,), (1,)), ((), ())),
                                 preferred_element_type=F32).astype(o_ref.dtype)


def _matmul(a, w_t, tn, out_dtype=BF16):
    m, k = a.shape
    n = w_t.shape[0]
    tm = next(t for t in (768, 512, 256) if m % t == 0)
    assert m % tm == 0 and n % tn == 0
    return pl.pallas_call(
        _matmul_kernel,
        grid=(n // tn, m // tm),
        in_specs=[pl.BlockSpec((tm, k), lambda j, i: (i, 0)),
                  pl.BlockSpec((tn, k), lambda j, i: (j, 0))],
        out_specs=pl.BlockSpec((tm, tn), lambda j, i: (i, j)),
        out_shape=jax.ShapeDtypeStruct((m, n), out_dtype),
        compiler_params=_cparams(("parallel", "parallel")),
        name="in_proj",
    )(a, w_t)


def _rope_tables(geo):
    f32 = np.float32
    rows = geo.seq // GRID_W
    t_row = np.repeat(np.arange(rows, dtype=f32), GRID_W)
    t_col = np.tile(np.arange(GRID_W, dtype=f32), rows)
    half = HEAD_DIM // 2
    freqs = f32(ROPE_THETA) ** (-(np.arange(0, half, 2, dtype=f32) / f32(half)))
    ang = np.concatenate([t_row[:, None] * freqs, t_col[:, None] * freqs], axis=-1).astype(f32)
    cos, sin = np.cos(ang), np.sin(ang)
    cos_l = np.concatenate([cos, cos], axis=-1)
    sin_l = np.concatenate([-sin, sin], axis=-1)
    cos_c = np.ones((geo.ctx_len, HEAD_DIM), f32)
    sin_c = np.zeros((geo.ctx_len, HEAD_DIM), f32)
    return (jnp.asarray(np.concatenate([cos_c, cos_l], axis=0), F32),
            jnp.asarray(np.concatenate([sin_c, sin_l], axis=0), F32))


def _qk_prep_kernel(q_ref, k_ref, v_ref, cos_ref, sin_ref, gq_ref, gk_ref, shift_ref, qt_ref, ko_ref, vt_ref):
    cos, sin = cos_ref[...], sin_ref[...]
    t = q_ref.shape[0]

    ones = jnp.ones((HEAD_DIM, HEAD_DIM), BF16)

    def norm_rope(x, g):
        x = x.astype(F32)
        hi, lo = _split_bf16(x * x, 2)
        ss = jnp.dot(hi, ones, preferred_element_type=F32) + jnp.dot(lo, ones, preferred_element_type=F32)
        y = x * lax.rsqrt(ss * (1.0 / HEAD_DIM) + NORM_EPS) * g
        return y * cos + pltpu.roll(y, HEAD_DIM // 2, 1) * sin

    gq, gk = gq_ref[...], gk_ref[...]
    for h in range(ATT_HEADS):
        g, r = divmod(h, ATT_REP)
        y = norm_rope(q_ref[:, h * HEAD_DIM:(h + 1) * HEAD_DIM], gq) * Q_SCALE
        qt_ref[g, 0, 0:HEAD_DIM, r * t:(r + 1) * t] = y.astype(qt_ref.dtype).T
    extra = MXU_WIDTH - HEAD_DIM
    first_row = lax.broadcasted_iota(jnp.int32, (extra, ATT_REP * t), 0) == 0
    first_lane = lax.broadcasted_iota(jnp.int32, (t, extra), 1) == 0
    for g in range(ATT_KV_HEADS):
        hs = slice(g * HEAD_DIM, (g + 1) * HEAD_DIM)
        qt_ref[g, 0, HEAD_DIM:MXU_WIDTH, :] = jnp.where(first_row, 1.0, 0.0).astype(qt_ref.dtype)
        ko_ref[g, :, 0:HEAD_DIM] = norm_rope(k_ref[:, hs], gk).astype(ko_ref.dtype)
        ko_ref[g, :, HEAD_DIM:MXU_WIDTH] = jnp.where(first_lane, -shift_ref[...], 0.0).astype(ko_ref.dtype)
        vt_ref[g, 0, 0:HEAD_DIM, :] = v_ref[:, hs].T
        vt_ref[g, 0, HEAD_DIM:VT_ROWS, :] = jnp.ones((VT_ROWS - HEAD_DIM, t), vt_ref.dtype)


def _qk_prep(proj, cos_t, sin_t, g_q, g_k, shift_row, geo):
    row = lambda b, j: b * geo.tpb + j
    nq, nk = ATT_HEADS * HEAD_DIM, ATT_KV_HEADS * HEAD_DIM
    kv, t, depth = ATT_KV_HEADS, ROW_TILE, MXU_WIDTH
    return pl.pallas_call(
        _qk_prep_kernel,
        grid=(geo.batch, geo.tpb),
        in_specs=[pl.BlockSpec((t, nq), lambda b, j: (row(b, j), _COL["A_Q"] // nq)),
                  pl.BlockSpec((t, nk), lambda b, j: (row(b, j), _COL["A_K"] // nk)),
                  pl.BlockSpec((t, nk), lambda b, j: (row(b, j), _COL["A_V"] // nk)),
                  pl.BlockSpec((t, HEAD_DIM), lambda b, j: (j, 0)),
                  pl.BlockSpec((t, HEAD_DIM), lambda b, j: (j, 0)),
                  pl.BlockSpec((1, HEAD_DIM), lambda b, j: (0, 0)),
                  pl.BlockSpec((1, HEAD_DIM), lambda b, j: (0, 0)),
                  pl.BlockSpec((1, depth - HEAD_DIM), lambda b, j: (0, 0))],
        out_specs=[pl.BlockSpec((kv, 1, depth, ATT_REP * t), lambda b, j: (b, j, 0, 0)),
                   pl.BlockSpec((kv, t, depth), lambda b, j: (b, j, 0)),
                   pl.BlockSpec((kv, 1, VT_ROWS, t), lambda b, j: (b, j, 0, 0))],
        out_shape=[jax.ShapeDtypeStruct((geo.batch * kv, geo.tpb, depth, ATT_REP * t), BF16),
                   jax.ShapeDtypeStruct((geo.batch * kv, geo.lt, depth), BF16),
                   jax.ShapeDtypeStruct((geo.batch * kv, geo.tpb, VT_ROWS, t), BF16)],
        compiler_params=_cparams(("parallel", "parallel")),
        name="qk_prep",
    )(proj, proj, proj, cos_t, sin_t, g_q, g_k, shift_row)


def _attn_kernel(*refs, n_str):
    qt_refs = refs[0:n_str]
    k_ref, vt_ref = refs[n_str:n_str + 2]
    gate_refs = refs[n_str + 2:2 * n_str + 2]
    o_ref, m_ref, acc_ref, s_ref, mx_ref = refs[2 * n_str + 2:]
    vt_tile = vt_ref.shape[2]
    tk = s_ref.shape[1]
    tiles = tk // vt_tile
    n_chunks = k_ref.shape[0] // tk
    tq = o_ref.shape[0] // n_str
    m_ref[...] = jnp.full(m_ref.shape, -jnp.inf, F32)
    acc_ref[...] = jnp.zeros(acc_ref.shape, F32)

    def scores(c, parity):
        k = k_ref[pl.ds(pl.multiple_of(c * tk, tk), tk), :]
        for st in range(n_str):
            s = jnp.dot(k, qt_refs[st][...], preferred_element_type=F32)
            s_ref[2 * st + parity] = s
            mx_ref[2 * st + parity] = jnp.max(s, axis=0, keepdims=True)

    def absorb(c, parity):
        for st in range(n_str):
            buf = 2 * st + parity
            m_old = m_ref[st]
            m_new = jnp.maximum(m_old, mx_ref[buf])
            alpha = jnp.exp2(m_old - m_new)
            pv = None
            for t in range(tiles):
                p = jnp.exp2((s_ref[buf, t * vt_tile:(t + 1) * vt_tile, :] - m_new).astype(BF16))
                part = jnp.dot(vt_ref[c * tiles + t], p, preferred_element_type=F32)
                pv = part if pv is None else pv + part
            acc_ref[st] = alpha * acc_ref[st] + pv
            m_ref[st] = m_new

    def group(i, carry):
        c0 = ATTN_UNROLL * i
        for u in range(ATTN_UNROLL):
            scores(c0 + u + 1, (u + 1) % 2)
            absorb(c0 + u, u % 2)
        return carry

    scores(0, 0)
    n_groups = (n_chunks - 1) // ATTN_UNROLL
    if n_groups > 0:
        lax.fori_loop(0, n_groups, group, 0)
    for c in range(n_groups * ATTN_UNROLL, n_chunks):
        if c + 1 < n_chunks:
            scores(c + 1, (c + 1) % 2)
        absorb(c, c % 2)
    _attn_epilogue(acc_ref, gate_refs, o_ref, n_str)


def _attn_epilogue(acc_ref, gate_refs, o_ref, n_str):
    tq = o_ref.shape[0] // n_str
    for st in range(n_str):
        o_t = acc_ref[st, 0:HEAD_DIM, :] / acc_ref[st, HEAD_DIM:HEAD_DIM + 1, :]
        o = jnp.concatenate([o_t[:, r * tq:(r + 1) * tq].T for r in range(ATT_REP)], axis=1)
        o_ref[st * tq:(st + 1) * tq, :] = (o * _silu(gate_refs[st][...].astype(F32))).astype(o_ref.dtype)


def _attn_bounded_kernel(*refs, n_str, tk):
    qt_refs = refs[0:n_str]
    k_ref, vt_ref = refs[n_str:n_str + 2]
    gate_refs = refs[n_str + 2:2 * n_str + 2]
    o_ref, acc_ref = refs[2 * n_str + 2:]
    vt_tile = vt_ref.shape[2]
    tiles = tk // vt_tile
    n_chunks = k_ref.shape[0] // tk
    acc_ref[...] = jnp.zeros(acc_ref.shape, F32)

    def chunk(c):
        k = k_ref[pl.ds(pl.multiple_of(c * tk, tk), tk), :]
        for st in range(n_str):
            p = jnp.exp2(jnp.dot(k, qt_refs[st][...], preferred_element_type=F32).astype(BF16))
            pv = None
            for t in range(tiles):
                part = jnp.dot(vt_ref[c * tiles + t, 0:HEAD_DIM, :], p[t * vt_tile:(t + 1) * vt_tile, :],
                               preferred_element_type=F32)
                pv = part if pv is None else pv + part
            acc_ref[st, 0:HEAD_DIM, :] += pv
            acc_ref[st, HEAD_DIM:HEAD_DIM + 1, :] += jnp.sum(p.astype(F32), axis=0, keepdims=True)

    def group(i, carry):
        for u in range(ATTN_BOUNDED_UNROLL):
            chunk(ATTN_BOUNDED_UNROLL * i + u)
        return carry

    n_groups = n_chunks // ATTN_BOUNDED_UNROLL
    if n_groups > 0:
        lax.fori_loop(0, n_groups, group, 0)
    for c in range(n_groups * ATTN_BOUNDED_UNROLL, n_chunks):
        chunk(c)
    _attn_epilogue(acc_ref, gate_refs, o_ref, n_str)


def _attention(qt, kh, vt, proj, geo, q_first_row, q_rows, kv_rows, bounded):
    tq = ROW_TILE
    assert q_first_row % tq == 0 and q_rows % tq == 0 and kv_rows % ROW_TILE == 0
    q0, n_tiles = q_first_row // tq, q_rows // tq
    gw = ATT_REP * HEAD_DIM
    mq = ATT_REP * tq
    depth = qt.shape[2]
    head = lambda b, g: b * ATT_KV_HEADS + g

    def build(kernel_fn, streams, scratch, name):
        n_str = streams if n_tiles % streams == 0 else 1
        tile = lambda i, st: q0 + n_str * i + st
        stream_specs = lambda make: [make(st) for st in range(n_str)]
        call = pl.pallas_call(
            functools.partial(kernel_fn, n_str=n_str),
            grid=(geo.batch, ATT_KV_HEADS, n_tiles // n_str),
            in_specs=[
                *stream_specs(lambda st: pl.BlockSpec((None, None, depth, mq),
                                                      lambda b, g, i: (head(b, g), tile(i, st), 0, 0))),
                pl.BlockSpec((None, kv_rows, depth), lambda b, g, i: (head(b, g), 0, 0)),
                pl.BlockSpec((None, kv_rows // ROW_TILE, VT_ROWS, ROW_TILE), lambda b, g, i: (head(b, g), 0, 0, 0)),
                *stream_specs(lambda st: pl.BlockSpec((tq, gw), lambda b, g, i: (b * geo.tpb + tile(i, st),
                                                                                 _COL["A_G"] // gw + g))),
            ],
            out_specs=pl.BlockSpec((n_str * tq, gw), lambda b, g, i: (b * (n_tiles // n_str) + i, g)),
            out_shape=jax.ShapeDtypeStruct((geo.batch * q_rows, BRANCH_WIDTH), BF16),
            scratch_shapes=scratch(n_str),
            compiler_params=_cparams(("parallel", "parallel", "arbitrary")),
            name=name)
        return lambda: call(*([qt] * n_str), kh, vt, *([proj] * n_str))

    tk = max(t for t in ATTN_KEY_CHUNKS if kv_rows % t == 0)
    online = build(_attn_kernel, ATTN_STREAMS,
                   lambda n: [pltpu.VMEM((n, 1, mq), F32), pltpu.VMEM((n, VT_ROWS, mq), F32),
                              pltpu.VMEM((2 * n, tk, mq), F32), pltpu.VMEM((2 * n, 1, mq), F32)],
                   "attention_online")
    tkb = max(t for t in ATTN_BOUNDED_KEY_CHUNKS if kv_rows % t == 0)
    fast = build(functools.partial(_attn_bounded_kernel, tk=tkb), ATTN_BOUNDED_STREAMS,
                 lambda n: [pltpu.VMEM((n, VT_ROWS, mq), F32)], "attention")
    return lax.cond(bounded, fast, online)


def _halo_specs(width, col_block, geo, first, row):
    per = ROW_TILE // SUBLANES
    last = geo.rows // SUBLANES - 1
    prev = pl.BlockSpec((SUBLANES, width), lambda b, j: (jnp.maximum(row(b, j) * per - 1, 0), col_block))
    nxt = pl.BlockSpec((SUBLANES, width), lambda b, j: (jnp.minimum((row(b, j) + 1) * per, last), col_block))
    return prev, nxt


def _fill_ext(ext_ref, cur, prev, nxt, j, geo):
    t = cur.shape[0]
    seg_first = jnp.logical_or(j == 0, j == geo.ctx_tiles)
    seg_last = jnp.logical_or(j == geo.ctx_tiles - 1, j == geo.tpb - 1)
    ext_ref[0:SUBLANES, :] = jnp.where(seg_first, 0.0, prev)
    ext_ref[SUBLANES:SUBLANES + t, :] = cur
    ext_ref[SUBLANES + t:2 * SUBLANES + t, :] = jnp.where(seg_last, 0.0, nxt)


def _ssd_conv_kernel(cur_ref, prev_ref, next_ref, nar_ref, w_ref, b_ref, dtb_ref, shift_ref, xbc_ref, dt_ref, ext_ref,
                     *, geo):
    j = pl.program_id(1)
    cur = cur_ref[...]
    cur_f = cur.astype(F32)
    _fill_ext(ext_ref, cur_f, prev_ref[...].astype(F32), next_ref[...].astype(F32), j, geo)
    t, half = cur_ref.shape[0], SSD_CONV // 2
    acc = b_ref[...] + w_ref[half:half + 1, :] * cur_f
    taps = [k for k in range(SSD_CONV) if k != half]
    for i, k in enumerate(taps):
        acc = acc + w_ref[k:k + 1, :] * jnp.dot(shift_ref[i], cur, preferred_element_type=F32)

    def edge(r0):
        e = jnp.zeros((SUBLANES, cur_ref.shape[1]), F32) + b_ref[...]
        for k in range(SSD_CONV):
            e = e + w_ref[k:k + 1, :] * ext_ref[SUBLANES + r0 + k - half:2 * SUBLANES + r0 + k - half, :]
        return e

    acc = jnp.concatenate([edge(0), acc[SUBLANES:t - SUBLANES], edge(t - SUBLANES)], axis=0)
    xbc_ref[...] = _silu(acc).astype(xbc_ref.dtype)
    lane = lax.broadcasted_iota(jnp.int32, dt_ref.shape, 1)
    dt = _softplus(nar_ref[...].astype(F32) + dtb_ref[...])
    dt_ref[...] = jnp.where(lane < 2 * SSD_HEADS, dt, 0.0)


def _ssd_conv(proj, conv_w, conv_b, dt_bias_row, geo):
    width = BRANCH_WIDTH + 2 * SSD_GROUPS * SSD_STATE
    row = lambda b, j: b * geo.tpb + j
    cb = _COL["B_X"] // width
    prev, nxt = _halo_specs(width, cb, geo, 0, row)
    t_idx = np.arange(ROW_TILE)
    offsets = [k - SSD_CONV // 2 for k in range(SSD_CONV) if k != SSD_CONV // 2]
    shifts = jnp.asarray(np.stack([(t_idx[None, :] == t_idx[:, None] + d) for d in offsets]).astype(np.float32), BF16)
    return pl.pallas_call(
        functools.partial(_ssd_conv_kernel, geo=geo),
        grid=(geo.batch, geo.tpb),
        in_specs=[pl.BlockSpec((ROW_TILE, width), lambda b, j: (row(b, j), cb)), prev, nxt,
                  pl.BlockSpec((ROW_TILE, LANES), lambda b, j: (row(b, j), _COL["NARROW"] // LANES)),
                  pl.BlockSpec((SSD_CONV, width), lambda b, j: (0, 0)),
                  pl.BlockSpec((1, width), lambda b, j: (0, 0)),
                  pl.BlockSpec((1, LANES), lambda b, j: (0, 0)),
                  pl.BlockSpec((len(offsets), ROW_TILE, ROW_TILE), lambda b, j: (0, 0, 0))],
        out_specs=[pl.BlockSpec((ROW_TILE, width), lambda b, j: (row(b, j), 0)),
                   pl.BlockSpec((ROW_TILE, LANES), lambda b, j: (row(b, j), 0))],
        out_shape=[jax.ShapeDtypeStruct((geo.rows, width), BF16),
                   jax.ShapeDtypeStruct((geo.rows, LANES), F32)],
        scratch_shapes=[pltpu.VMEM((ROW_TILE + 2 * SUBLANES, width), F32)],
        compiler_params=_cparams(("parallel", "parallel")),
        name="ssd_conv",
    )(proj, proj, proj, proj, conv_w, conv_b, dt_bias_row, shifts)


def _scan_chunk(s, n_chunks, n_ctx_chunks, reverse):
    if not reverse:
        return s
    return jnp.where(s < n_ctx_chunks, n_ctx_chunks - 1 - s, n_chunks + n_ctx_chunks - 1 - s)


def _tri(n, reverse):
    t = lax.broadcasted_iota(jnp.int32, (n, n), 0)
    s = lax.broadcasted_iota(jnp.int32, (n, n), 1)
    return (s >= t) if reverse else (s <= t)


def _split_bf16(x, pieces):
    out, rest = [], x
    for _ in range(pieces):
        p = rest.astype(BF16)
        out.append(p)
        rest = rest - p.astype(F32)
    return out


def _cumsum_rows(mask_bf16, x):
    w = x.shape[1]
    parts = jnp.dot(mask_bf16, jnp.concatenate(_split_bf16(x, 3), axis=1), preferred_element_type=F32)
    return parts[:, 0:w] + parts[:, w:2 * w] + parts[:, 2 * w:3 * w]


def _ssd_scan_kernel(xbc_ref, dt_ref, alog_ref, e_ref, *rest, reverse):
    if reverse:
        yf_ref, z_ref, skip_ref, g_ref, y_ref, h_ref = rest
    else:
        y_ref, h_ref = rest
    q = SSD_CHUNK
    gw = SSD_HPG * SSD_HEAD_DIM
    quad = 4
    qw = quad * SSD_HEAD_DIM
    lane0 = SSD_HEADS if reverse else 0
    n_batch, n_sub = xbc_ref.shape[0], xbc_ref.shape[1] // q

    @pl.when(pl.program_id(0) == 0)
    def _():
        h_ref[...] = jnp.zeros_like(h_ref)

    lane = lax.broadcasted_iota(jnp.int32, (1, LANES), 1)
    a = jnp.where(lane < 2 * SSD_HEADS, -jnp.exp(alog_ref[...]), 0.0)
    mask = _tri(q, reverse)
    mask_bf = jnp.where(mask, 1.0, 0.0).astype(BF16)
    e = e_ref[...]
    head_of_lane = lax.broadcasted_iota(jnp.int32, (1, qw), 1) // SSD_HEAD_DIM
    zero_bf = jnp.zeros((), BF16)

    order = range(n_sub - 1, -1, -1) if reverse else range(n_sub)
    for ci, b in [(ci, b) for ci in order for b in range(n_batch)]:
        rows = slice(ci * q, (ci + 1) * q)
        dt = dt_ref[b, rows, :]
        cs = _cumsum_rows(mask_bf, dt * a)
        cs_last = cs[0:1, :] if reverse else cs[q - 1:q, :]
        zeros = jnp.zeros((q, LANES), BF16)
        ecs_hi, ecs_lo = _split_bf16(jnp.exp(cs), 2)
        cd_hi, cd_mid, cd_lo = _split_bf16(jnp.broadcast_to(jnp.exp(cs_last), (BF16_SUBLANES, LANES)), 3)
        stack = jnp.concatenate([
            jnp.concatenate([(jnp.exp(cs_last - cs) * dt).astype(BF16), zeros], axis=1),
            jnp.concatenate([ecs_hi, ecs_lo], axis=1),
            jnp.concatenate([cd_hi, cd_mid], axis=1),
            jnp.concatenate([cd_lo, zeros[0:BF16_SUBLANES]], axis=1)], axis=0)
        big = jnp.dot(stack, e, preferred_element_type=F32)
        x_bf = xbc_ref[b, rows, 0:BRANCH_WIDTH]
        x = x_bf.astype(F32)
        wx = (big[0:q] * x).astype(BF16)
        ecs = big[q:2 * q]
        r0 = 2 * q
        chunk_decay = big[r0:r0 + 1] + big[r0 + BF16_SUBLANES:r0 + BF16_SUBLANES + 1]
        cs_t = cs.T
        dt_t = dt.T
        pieces = []
        for g in range(SSD_GROUPS):
            bm = xbc_ref[b, rows, BRANCH_WIDTH + g * SSD_STATE:BRANCH_WIDTH + (g + 1) * SSD_STATE]
            c0 = BRANCH_WIDTH + SSD_GROUPS * SSD_STATE + g * SSD_STATE
            cm = xbc_ref[b, rows, c0:c0 + SSD_STATE]
            cb = lax.dot_general(cm, bm, (((1,), (1,)), ((), ())), preferred_element_type=F32)
            h_in = h_ref[b, g]
            y_off = jnp.dot(cm, h_in.astype(BF16), preferred_element_type=F32) * ecs[:, g * gw:(g + 1) * gw]
            states = jnp.dot(bm.T, wx[:, g * gw:(g + 1) * gw], preferred_element_type=F32)
            h_ref[b, g] = chunk_decay[:, g * gw:(g + 1) * gw] * h_in + states
            for hq in range(SSD_HPG // quad):
                ms = []
                for r in range(quad):
                    col = lane0 + g * SSD_HPG + hq * quad + r
                    seg = cs[:, col:col + 1] - cs_t[col:col + 1, :]
                    ms.append((cb * jnp.exp(jnp.where(mask, seg, -1e30)) * dt_t[col:col + 1, :]).astype(BF16))
                lo = g * gw + hq * qw
                slab = x_bf[:, lo:lo + qw]
                rhs = jnp.concatenate([jnp.where(head_of_lane == r, slab, zero_bf) for r in range(quad)], axis=0)
                y_diag = jnp.dot(jnp.concatenate(ms, axis=1), rhs, preferred_element_type=F32)
                pieces.append(y_diag + y_off[:, hq * qw:(hq + 1) * qw])
        y = jnp.concatenate(pieces, axis=1)
        if reverse:
            y = (yf_ref[b, rows, :] + y + skip_ref[...] * x) * _silu(z_ref[b, rows, :].astype(F32))
            y = y * lax.rsqrt(jnp.mean(y * y, axis=-1, keepdims=True) + NORM_EPS) * g_ref[...]
        y_ref[b, rows, :] = y.astype(y_ref.dtype)


def _ssd_scan(xbc, dt, a_log_row, expand_mat, geo, reverse, finish=None):
    nb, nbc = geo.lt // SCAN_BLOCK, geo.ctx_len // SCAN_BLOCK
    width, w, nbat = xbc.shape[1], BRANCH_WIDTH, geo.batch
    blk = lambda s: _scan_chunk(s, nb, nbc, reverse)
    per_batch = lambda a: a.reshape(nbat, geo.lt, a.shape[1])
    tok_spec = lambda cols, col_block: pl.BlockSpec((nbat, SCAN_BLOCK, cols), lambda s: (0, blk(s), col_block))
    const_spec = lambda shape: pl.BlockSpec(shape, lambda s: (0, 0))
    in_specs = [tok_spec(width, 0), tok_spec(LANES, 0), const_spec((1, LANES)), const_spec((2 * LANES, w))]
    args = [per_batch(xbc), per_batch(dt), a_log_row, expand_mat]
    if reverse:
        yf, proj, skip_row, norm_g = finish
        in_specs += [tok_spec(w, 0), tok_spec(w, _COL["B_Z"] // w), const_spec((1, w)), const_spec((1, w))]
        args += [per_batch(yf), per_batch(proj), skip_row, norm_g]
    out = pl.pallas_call(
        functools.partial(_ssd_scan_kernel, reverse=reverse),
        grid=(nb,),
        in_specs=in_specs,
        out_specs=tok_spec(w, 0),
        out_shape=jax.ShapeDtypeStruct((nbat, geo.lt, w), BF16 if reverse else F32),
        scratch_shapes=[pltpu.VMEM((nbat, SSD_GROUPS, SSD_STATE, SSD_HPG * SSD_HEAD_DIM), F32)],
        compiler_params=_cparams(("arbitrary",)),
        name="ssd_scan_bwd" if reverse else "ssd_scan_fwd",
    )(*args)
    return out.reshape(geo.rows, w)


def _gla_scan_kernel(q_ref, k_ref, v_ref, nar_ref, w2_ref, bf_ref, *rest, reverse):
    if reverse:
        of_ref, gate_ref, g_ref, o_ref, h_ref = rest
    else:
        o_ref, h_ref = rest
    n = GLA_CHUNK

    @pl.when(pl.program_id(0) == 0)
    def _():
        h_ref[...] = jnp.zeros_like(h_ref)

    n_batch, t = q_ref.shape[0], q_ref.shape[1]
    n_sub = t // n
    ri = lax.broadcasted_iota(jnp.int32, (t, t), 0)
    ci = lax.broadcasted_iota(jnp.int32, (t, t), 1)
    mask = jnp.logical_and(ri // n == ci // n, (ci >= ri) if reverse else (ci <= ri))
    mask_bf = jnp.where(mask, 1.0, 0.0).astype(BF16)
    for bi in range(n_batch):
        logit = jnp.dot(nar_ref[bi], w2_ref[...], preferred_element_type=F32) + bf_ref[...]
        gl = _log_sigmoid(logit) / GLA_TAU
        b = _cumsum_rows(mask_bf, gl)
        last = [b[c * n:c * n + 1, :] if reverse else b[(c + 1) * n - 1:(c + 1) * n, :] for c in range(n_sub)]
        b_last = jnp.concatenate([jnp.broadcast_to(r, (n, r.shape[1])) for r in last], axis=0)
        q = q_ref[bi].astype(F32) * (GLA_DK ** -0.5)
        k = k_ref[bi].astype(F32)
        qe = (q * jnp.exp(b)).astype(BF16)
        ke = (k * jnp.exp(-b)).astype(BF16)
        kd = (k * jnp.exp(b_last - b)).astype(BF16)
        for h in range(GLA_HEADS):
            ks = slice(h * GLA_DK, (h + 1) * GLA_DK)
            vs = slice(h * GLA_DV, (h + 1) * GLA_DV)
            v = v_ref[bi, :, vs]
            att = lax.dot_general(qe[:, ks], ke[:, ks], (((1,), (1,)), ((), ())), preferred_element_type=F32)
            o_intra = jnp.dot(jnp.where(mask, att, 0.0).astype(BF16), v, preferred_element_type=F32)
            o_inter = [None] * n_sub
            for c in (range(n_sub - 1, -1, -1) if reverse else range(n_sub)):
                rows = slice(c * n, (c + 1) * n)
                h_in = h_ref[bi, h]
                o_inter[c] = lax.dot_general(qe[rows, ks], h_in.astype(BF16), (((1,), (1,)), ((), ())),
                                             preferred_element_type=F32)
                upd = jnp.dot(v[rows].T, kd[rows, ks], preferred_element_type=F32)
                h_ref[bi, h] = jnp.exp(last[c][:, ks]) * h_in + upd
            o = o_intra + jnp.concatenate(o_inter, axis=0)
            if reverse:
                o = o + of_ref[bi, :, vs]
                o = o * lax.rsqrt(jnp.mean(o * o, axis=-1, keepdims=True) + NORM_EPS) * g_ref[...]
                o = o * _silu(gate_ref[bi, :, vs].astype(F32))
            o_ref[bi, :, vs] = o.astype(o_ref.dtype)


def _gla_scan(proj, w2, b_f, geo, reverse, finish=None):
    nb, nbc, nbat = geo.lt // SCAN_BLOCK, geo.ctx_len // SCAN_BLOCK, geo.batch
    blk = lambda s: _scan_chunk(s, nb, nbc, reverse)
    kw, vw = GLA_HEADS * GLA_DK, GLA_HEADS * GLA_DV
    per_batch = lambda a: a.reshape(nbat, geo.lt, a.shape[1])
    tok_spec = lambda cols, col_block: pl.BlockSpec((nbat, SCAN_BLOCK, cols), lambda s: (0, blk(s), col_block))
    const_spec = lambda shape: pl.BlockSpec(shape, lambda s: (0, 0))
    proj3 = per_batch(proj)
    in_specs = [tok_spec(kw, _COL["C_Q"] // kw), tok_spec(kw, _COL["C_K"] // kw), tok_spec(vw, _COL["C_V"] // vw),
                tok_spec(LANES, _COL["NARROW"] // LANES), const_spec((LANES, kw)), const_spec((1, kw))]
    args = [proj3, proj3, proj3, proj3, w2, b_f]
    if reverse:
        of, norm_g = finish
        in_specs += [tok_spec(vw, 0), tok_spec(vw, _COL["C_G"] // vw), const_spec((1, GLA_DV))]
        args += [per_batch(of), proj3, norm_g]
    out = pl.pallas_call(
        functools.partial(_gla_scan_kernel, reverse=reverse),
        grid=(nb,),
        in_specs=in_specs,
        out_specs=tok_spec(vw, 0),
        out_shape=jax.ShapeDtypeStruct((nbat, geo.lt, vw), BF16 if reverse else F32),
        scratch_shapes=[pltpu.VMEM((nbat, GLA_HEADS, GLA_DV, GLA_DK), F32)],
        compiler_params=_cparams(("arbitrary",)),
        name="gla_scan_bwd" if reverse else "gla_scan_fwd",
    )(*args)
    return out.reshape(geo.rows, vw)


def _shortconv_tile(cur_ref, prev_ref, next_ref, w_ref, ext_ref, tile, geo):
    w = BRANCH_WIDTH
    u = lambda ref: ref[:, w:2 * w].astype(F32) * ref[:, 2 * w:3 * w].astype(F32)
    _fill_ext(ext_ref, u(cur_ref), u(prev_ref), u(next_ref), tile, geo)
    t = cur_ref.shape[0]
    acc = jnp.zeros((t, w), F32)
    for k in range(SC_CONV):
        acc = acc + w_ref[k:k + 1, :] * ext_ref[SUBLANES + k - SC_CONV // 2:SUBLANES + k - SC_CONV // 2 + t, :]
    return cur_ref[:, 0:w].astype(F32) * acc * _silu(cur_ref[:, 3 * w:4 * w].astype(F32))


def _merge_out_kernel(*refs, n_ya, n_src, geo, first, with_next):
    tile = first + pl.program_id(1)
    ext_ref = refs[-1]
    ya = _read_tokens(refs[0:n_ya], tile, geo)
    refs = refs[n_ya:-1]
    yd = _shortconv_tile(*refs[2:6], ext_ref, tile, geo).astype(BF16)
    ys = [ya, refs[0][...], refs[1][...], yd]
    refs = refs[6:]
    gates = refs[0:N_BRANCH]
    wb_ref, wo_ref = refs[N_BRANCH:N_BRANCH + 2]
    p = N_BRANCH + 2
    srcs = refs[p:p + n_src]
    gt_ref, gp_ref = refs[p + n_src:p + n_src + 2]
    rest = refs[p + n_src + 2:]
    m = None
    for n in (N_BRANCH - 1, 0, 1, 2):
        term = jax.nn.sigmoid(gates[n][...].astype(F32)) * jnp.dot(ys[n], wb_ref[n], preferred_element_type=F32)
        m = term if m is None else m + term
    out = jnp.dot(m.astype(BF16), wo_ref[...], preferred_element_type=F32)
    y = out * lax.rsqrt(jnp.mean(out * out, axis=-1, keepdims=True) + NORM_EPS) * gp_ref[...]
    x_new = _read_tokens(srcs, tile, geo) + gt_ref[0] * y
    if with_next:
        gn_ref, shn_ref, scn_ref, o_ref, h_ref = rest
        h_ref[...] = _adaln_prenorm(x_new, gn_ref[...], shn_ref[0], scn_ref[0]).astype(h_ref.dtype)
    else:
        (o_ref,) = rest
    o_ref[...] = x_new


def _merge_out(ys, proj, w_branch, w_out, layer, tokens, mod3, g_post, geo, with_ctx, nxt=None):
    first, nt = geo.span(with_ctx)
    w, d = BRANCH_WIDTH, D_MODEL
    row = lambda b, j: b * geo.tpb + first + j
    out_rows = geo.rows if with_ctx else geo.batch * geo.seq
    out_row = row if with_ctx else (lambda b, j: b * nt + j)
    y_spec = pl.BlockSpec((ROW_TILE, w), lambda b, j: (row(b, j), 0))
    gate_specs = [pl.BlockSpec((ROW_TILE, d), functools.partial(lambda b, j, n: (row(b, j), _COL["MG"] // d + n), n=n))
                  for n in range(N_BRANCH)]
    resident = pl.Buffered(1)
    mod_spec = lambda part: pl.BlockSpec((1, 1, d), lambda b, j: (geo.mod_row(b, first + j), 0, part))
    vec_spec = pl.BlockSpec((1, d), lambda b, j: (0, 0))
    ya, yb, yg, conv_w = ys
    dw = N_BRANCH * w
    d_block = _COL["D_ALL"] // dw
    d_prev, d_next = _halo_specs(dw, d_block, geo, first, row)
    in_specs = [*_token_specs(ya, geo, first), y_spec, y_spec,
                pl.BlockSpec((ROW_TILE, dw), lambda b, j: (row(b, j), d_block)), d_prev, d_next,
                pl.BlockSpec((SC_CONV, w), lambda b, j: (0, 0)), *gate_specs,
                pl.BlockSpec((None, N_BRANCH, w, d), lambda b, j: (layer, 0, 0, 0), pipeline_mode=resident),
                pl.BlockSpec((None, d, d), lambda b, j: (layer, 0, 0), pipeline_mode=resident),
                *_token_specs(tokens, geo, first), mod_spec(2), vec_spec]
    args = [*ya, yb, yg, proj, proj, proj, conv_w, proj, proj, proj, proj, w_branch, w_out, *tokens, mod3, g_post]
    out_specs = [pl.BlockSpec((ROW_TILE, d), lambda b, j: (out_row(b, j), 0))]
    out_shape = [jax.ShapeDtypeStruct((out_rows, d), F32)]
    if nxt is not None:
        assert with_ctx
        g_next, mod3_next = nxt
        in_specs += [vec_spec, mod_spec(0), mod_spec(1)]
        args += [g_next, mod3_next, mod3_next]
        out_specs.append(pl.BlockSpec((ROW_TILE, d), lambda b, j: (row(b, j), 0)))
        out_shape.append(jax.ShapeDtypeStruct((geo.rows, d), BF16))
    outs = pl.pallas_call(
        functools.partial(_merge_out_kernel, n_ya=len(ya), n_src=len(tokens), geo=geo, first=first,
                          with_next=nxt is not None),
        grid=(geo.batch, nt),
        in_specs=in_specs,
        out_specs=out_specs,
        out_shape=out_shape,
        scratch_shapes=[pltpu.VMEM((ROW_TILE + 2 * SUBLANES, w), F32)],
        compiler_params=_cparams(("parallel", "parallel")),
        name="merge_out",
    )(*args)
    return outs if nxt is not None else (outs[0], None)


def _regroup_plan():
    bw = BRANCH_WIDTH
    names = ("a_q", "a_k", "a_v", "a_g", "b_x", "b_z", "b_b", "b_c", "b_dt", "c_q", "c_k", "c_v", "c_g", "c_f",
             "d_all", "mg")
    widths = (ATT_HEADS * HEAD_DIM, ATT_KV_HEADS * HEAD_DIM, ATT_KV_HEADS * HEAD_DIM, bw,
              bw, bw, SSD_GROUPS * SSD_STATE, SSD_GROUPS * SSD_STATE, 2 * SSD_HEADS,
              GLA_HEADS * GLA_DK, GLA_HEADS * GLA_DK, GLA_HEADS * GLA_DV, bw, 2 * GLA_RANK,
              4 * bw, N_BRANCH * D_MODEL)
    src = dict(zip(names, np.concatenate([[0], np.cumsum(widths)[:-1]]).tolist()))
    wid = dict(zip(names, widths))
    dst = dict(a_q=_COL["A_Q"], a_g=_COL["A_G"], b_z=_COL["B_Z"], b_x=_COL["B_X"], b_b=_COL["B_B"], b_c=_COL["B_C"],
               a_k=_COL["A_K"], a_v=_COL["A_V"], c_v=_COL["C_V"], c_g=_COL["C_G"], c_q=_COL["C_Q"], c_k=_COL["C_K"],
               d_all=_COL["D_ALL"], mg=_COL["MG"], b_dt=_COL["NARROW"] + DT_LANE0, c_f=_COL["NARROW"] + F1_LANE0)
    return [(dst[n], src[n], wid[n], n in ("a_q", "a_k")) for n in names], sum(widths)


def _regroup_kernel(w_ref, o_ref):
    plan, _ = _regroup_plan()
    half = HEAD_DIM // 2
    for dst, src, width, deinterleave in plan:
        if deinterleave:
            for h in range(width // HEAD_DIM):
                s, d = src + h * HEAD_DIM, dst + h * HEAD_DIM
                o_ref[d:d + half, :] = w_ref[pl.ds(s, half, stride=2), :].astype(o_ref.dtype)
                o_ref[d + half:d + HEAD_DIM, :] = w_ref[pl.ds(s + 1, half, stride=2), :].astype(o_ref.dtype)
        else:
            o_ref[dst:dst + width, :] = w_ref[src:src + width, :].astype(o_ref.dtype)
    used = _COL["NARROW"] + F1_LANE0 + 2 * GLA_RANK
    o_ref[used:N_PROJ, :] = jnp.zeros((N_PROJ - used, o_ref.shape[1]), o_ref.dtype)


def _regroup_w_in(w_in, layer):
    w_t = jnp.swapaxes(w_in, 1, 2)
    _, n, k = w_t.shape
    assert n == _regroup_plan()[1]
    tk = LANES
    return pl.pallas_call(
        _regroup_kernel,
        grid=(k // tk,),
        in_specs=[pl.BlockSpec((None, n, tk), lambda i: (layer, 0, i))],
        out_specs=pl.BlockSpec((N_PROJ, tk), lambda i: (0, i)),
        out_shape=jax.ShapeDtypeStruct((N_PROJ, k), BF16),
        compiler_params=_cparams(("parallel",)),
        name="regroup_w_in",
    )(w_t)


def _deinterleave_vec(g):
    return g.reshape(HEAD_DIM // 2, 2).T.reshape(1, HEAD_DIM)


def _pad_lanes(v, lane0=0):
    return jnp.zeros((1, LANES), F32).at[0, lane0:lane0 + v.shape[0]].set(v.astype(F32))


def _head_expand_matrix(reverse):
    e = np.zeros((LANES, BRANCH_WIDTH), np.float32)
    lane0 = SSD_HEADS if reverse else 0
    for r in range(SSD_HEADS):
        e[lane0 + r, r * SSD_HEAD_DIM:(r + 1) * SSD_HEAD_DIM] = 1.0
    return jnp.asarray(np.concatenate([e, e], axis=0), BF16)


def _forget_weight(w_f2_dir, direction):
    lane0 = F1_LANE0 + direction * GLA_RANK
    return jnp.zeros((LANES, w_f2_dir.shape[1]), F32).at[lane0:lane0 + GLA_RANK].set(w_f2_dir).astype(BF16)


def kernel(x, c, ctx, c_ctx, w_mod, b_mod, g_pre, g_post, w_in, g_q, g_k, ssd_conv_w, ssd_conv_b,
           ssd_a_log, ssd_dt_bias, ssd_d, ssd_norm_g, gla_w_f2, gla_b_f, gla_norm_g, sc_conv_w,
           w_branch, w_out):
    batch, seq, d = x.shape
    ctx_len = ctx.shape[1]
    depth = w_in.shape[0]
    geo = _Geom(batch, ctx_len, seq)
    assert d == D_MODEL and batch + 1 <= SUBLANES and seq % GRID_W == 0

    cos_t, sin_t = _rope_tables(geo)
    c_rows = jnp.zeros((SUBLANES, d), F32).at[:batch].set(c).at[batch].set(c_ctx)
    tokens = (ctx.reshape(batch * ctx_len, d), x.reshape(batch * seq, d))
    e_fwd, e_bwd = _head_expand_matrix(False), _head_expand_matrix(True)
    w_branch_bf, w_out_bf = w_branch.astype(BF16), w_out.astype(BF16)
    mods =[_modulation(c_rows, w_mod, b_mod[l][None, :], l).reshape(SUBLANES, 1, 3 * d) for l in range(depth)]
    h = _prenorm(tokens, g_pre[0][None, :], mods[0], geo)

    for l in range(depth):
        need_ctx = l < depth - 1
        mod3 = mods[l]
        proj = _matmul(h, _regroup_w_in(w_in, l), PROJ_TN)

        shift = Q_SCALE * HEAD_DIM * jnp.max(jnp.abs(g_q[l])) * jnp.max(jnp.abs(g_k[l]))
        bounded = 2.0 * shift <= ATTN_SAFE_EXPONENT
        shift_row = jnp.full((1, MXU_WIDTH - HEAD_DIM), shift, F32)
        qt, kh, vt = _qk_prep(proj, cos_t, sin_t, _deinterleave_vec(g_q[l]), _deinterleave_vec(g_k[l]), shift_row, geo)
        ya = (_attention(qt, kh, vt, proj, geo, ctx_len, seq, geo.lt, bounded),)
        if need_ctx:
            ya = (_attention(qt, kh, vt, proj, geo, 0, ctx_len, ctx_len, bounded), *ya)

        xbc, dt = _ssd_conv(proj, ssd_conv_w[l], ssd_conv_b[l][None, :], _pad_lanes(ssd_dt_bias[l].reshape(-1)), geo)
        a_log_row = _pad_lanes(ssd_a_log[l].reshape(-1))
        ysf = _ssd_scan(xbc, dt, a_log_row, e_fwd, geo, False)
        skip_row = jnp.repeat(ssd_d[l], SSD_HEAD_DIM)[None, :]
        yb = _ssd_scan(xbc, dt, a_log_row, e_bwd, geo, True, (ysf, proj, skip_row, ssd_norm_g[l][None, :]))

        ogf = _gla_scan(proj, _forget_weight(gla_w_f2[l, 0], 0), gla_b_f[l, 0][None, :], geo, False)
        yg = _gla_scan(proj, _forget_weight(gla_w_f2[l, 1], 1), gla_b_f[l, 1][None, :], geo, True,
                       (ogf, gla_norm_g[l][None, :]))

        nxt = (g_pre[l + 1][None, :], mods[l + 1]) if need_ctx else None
        x_new, h = _merge_out((ya, yb, yg, sc_conv_w[l]), proj, w_branch_bf, w_out_bf, l, tokens, mod3,
                              g_post[l][None, :], geo, need_ctx, nxt)
        tokens = (x_new,)

    return x_new.reshape(batch, seq, d)
```

```python
import functools

import numpy as np
import jax
import jax.numpy as jnp
from jax import lax
from jax.experimental import pallas as pl
from jax.experimental.pallas import tpu as pltpu

F32 = jnp.float32
BF16 = jnp.bfloat16

D_MODEL = 2048
GRID_W = 64
BRANCH_WIDTH = D_MODEL // 2
N_BRANCH = 4
NORM_EPS = 1e-6
HEAD_DIM = 128
ATT_HEADS = BRANCH_WIDTH // HEAD_DIM
ATT_KV_HEADS = ATT_HEADS // 4
ATT_REP = ATT_HEADS // ATT_KV_HEADS
ROPE_THETA = 10000.0
SSD_HEAD_DIM = 64
SSD_HEADS = BRANCH_WIDTH // SSD_HEAD_DIM
SSD_GROUPS = 2
SSD_HPG = SSD_HEADS // SSD_GROUPS
SSD_STATE = 128
SSD_CONV = 5
SSD_CHUNK = 128
GLA_HEADS = 4
GLA_DV = BRANCH_WIDTH // GLA_HEADS
GLA_DK = GLA_DV // 2
GLA_RANK = 16
GLA_TAU = 16.0
GLA_CHUNK = 64
SC_CONV = 3

LANES = 128
SUBLANES = 8
ROW_TILE = 256
SCAN_BLOCK = ROW_TILE
VMEM_LIMIT = 56 * 1024 * 1024

_COL = dict(
    A_Q=0, A_G=1024, B_Z=2048, B_X=3072, B_B=4096, B_C=4352, A_K=4608, A_V=4864,
    C_V=5120, C_G=6144, C_Q=7168, C_K=7680,
    D_ALL=8192, MG=12288, NARROW=20480,
)
MXU_WIDTH = 256
N_PROJ = 20736
PROJ_TN = 2304
ATTN_UNROLL = 2
ATTN_STREAMS = 2
ATTN_BOUNDED_KEY_CHUNKS = (ROW_TILE, 3 * ROW_TILE)
ATTN_BOUNDED_UNROLL = 11
ATTN_BOUNDED_STREAMS = 2
ATTN_SAFE_EXPONENT = 100.0
ATTN_KEY_CHUNKS = (ROW_TILE, 2 * ROW_TILE, 3 * ROW_TILE)
BF16_SUBLANES = 16
VT_ROWS = HEAD_DIM + BF16_SUBLANES
Q_SCALE = HEAD_DIM ** -0.5 * float(np.log2(np.e))
DT_LANE0 = 0
F1_LANE0 = 32


def _cparams(sem, vmem=VMEM_LIMIT):
    return pltpu.CompilerParams(dimension_semantics=sem, vmem_limit_bytes=vmem)


def _silu(x):
    return x * jax.nn.sigmoid(x)


def _softplus(x):
    return jnp.maximum(x, 0.0) + jnp.log1p(jnp.exp(-jnp.abs(x)))


def _log_sigmoid(x):
    return jnp.minimum(x, 0.0) - jnp.log(1.0 + jnp.exp(-jnp.abs(x)))


def _mod_kernel(c_ref, w_ref, b_ref, o_ref):
    a = _silu(c_ref[...]).astype(BF16)
    o_ref[...] = jnp.dot(a, w_ref[...].astype(BF16), preferred_element_type=F32) + b_ref[...]


def _modulation(c_rows, w_mod, b_mod, layer):
    _, d, n = w_mod.shape
    tn = 1024
    return pl.pallas_call(
        _mod_kernel,
        grid=(n // tn,),
        in_specs=[pl.BlockSpec((SUBLANES, d), lambda j: (0, 0)),
                  pl.BlockSpec((None, d, tn), lambda j: (layer, 0, j)),
                  pl.BlockSpec((1, tn), lambda j: (0, j))],
        out_specs=pl.BlockSpec((SUBLANES, tn), lambda j: (0, j)),
        out_shape=jax.ShapeDtypeStruct((SUBLANES, n), F32),
        compiler_params=_cparams(("parallel",)),
        name="modulation",
    )(c_rows, w_mod, b_mod)


class _Geom:
    def __init__(self, batch, ctx_len, seq):
        self.batch, self.ctx_len, self.seq = batch, ctx_len, seq
        self.lt = ctx_len + seq
        self.rows = batch * self.lt
        assert ctx_len % ROW_TILE == 0 and seq % ROW_TILE == 0
        self.tpb = self.lt // ROW_TILE
        self.ctx_tiles = ctx_len // ROW_TILE
        self.lat_tiles = seq // ROW_TILE

    def span(self, with_ctx):
        return (0, self.tpb) if with_ctx else (self.ctx_tiles, self.lat_tiles)

    def mod_row(self, b, j):
        return jnp.where(j < self.ctx_tiles, self.batch, b)


def _token_specs(tokens, geo, first):
    c = tokens[0].shape[1]
    ct, lt = geo.ctx_tiles, geo.lat_tiles
    if len(tokens) == 1 and tokens[0].shape[0] == geo.rows:
        return [pl.BlockSpec((ROW_TILE, c), lambda b, j: (b * geo.tpb + first + j, 0))]
    if len(tokens) == 1:
        assert tokens[0].shape[0] == geo.batch * geo.seq and first >= ct
        return [pl.BlockSpec((ROW_TILE, c), lambda b, j: (b * lt + first + j - ct, 0))]
    return [pl.BlockSpec((ROW_TILE, c), lambda b, j: (b * ct + jnp.minimum(first + j, ct - 1), 0),
                         pipeline_mode=pl.Buffered(1)),
            pl.BlockSpec((ROW_TILE, c), lambda b, j: (b * lt + jnp.maximum(first + j - ct, 0), 0))]


def _read_tokens(refs, tile, geo):
    if len(refs) == 1:
        return refs[0][...]
    return jnp.where(tile < geo.ctx_tiles, refs[0][...], refs[1][...])


def _adaln_prenorm(x, g, sh, sc):
    y = x * lax.rsqrt(jnp.mean(x * x, axis=-1, keepdims=True) + NORM_EPS)
    return (y * g) * (1.0 + sc) + sh


PRENORM_SLOTS = 3


def _prenorm_kernel(ctx_hbm, x_hbm, g_ref, sh_ref, sc_ref, o_ref, buf_ref, sem_ref, *, geo):
    s = pl.program_id(0)
    n_steps = pl.num_programs(0)
    ct, lt, t = geo.ctx_tiles, geo.lat_tiles, ROW_TILE

    def tile_copy(src_hbm, tile_index, step):
        slot = step % PRENORM_SLOTS
        row = tile_index * t if isinstance(tile_index, int) else pl.multiple_of(tile_index * t, t)
        return pltpu.make_async_copy(src_hbm.at[pl.ds(row, t)], buf_ref.at[slot], sem_ref.at[slot])

    def start(step):
        b, j = step // geo.tpb, step % geo.tpb

        @pl.when(j < ct)
        def _():
            tile_copy(ctx_hbm, b * ct + j, step).start()

        @pl.when(j >= ct)
        def _():
            tile_copy(x_hbm, b * lt + j - ct, step).start()

    @pl.when(s == 0)
    def _():
        start(s)
        if geo.batch * geo.tpb > 1:
            start(s + 1)

    @pl.when(s + 2 < n_steps)
    def _():
        start(s + 2)

    tile_copy(x_hbm, 0, s).wait()
    x = buf_ref[s % PRENORM_SLOTS]
    o_ref[...] = _adaln_prenorm(x, g_ref[...], sh_ref[0], sc_ref[0]).astype(o_ref.dtype)


def _prenorm(tokens, g_pre, mod3, geo):
    d = D_MODEL
    ctx2d, x2d = tokens
    mod_row = lambda s: geo.mod_row(s // geo.tpb, s % geo.tpb)
    return pl.pallas_call(
        functools.partial(_prenorm_kernel, geo=geo),
        grid=(geo.batch * geo.tpb,),
        in_specs=[pl.BlockSpec(memory_space=pl.ANY), pl.BlockSpec(memory_space=pl.ANY),
                  pl.BlockSpec((1, d), lambda s: (0, 0)),
                  pl.BlockSpec((1, 1, d), lambda s: (mod_row(s), 0, 0)),
                  pl.BlockSpec((1, 1, d), lambda s: (mod_row(s), 0, 1))],
        out_specs=pl.BlockSpec((ROW_TILE, d), lambda s: (s, 0)),
        out_shape=jax.ShapeDtypeStruct((geo.rows, d), BF16),
        scratch_shapes=[pltpu.VMEM((PRENORM_SLOTS, ROW_TILE, d), F32), pltpu.SemaphoreType.DMA((PRENORM_SLOTS,))],
        compiler_params=_cparams(("arbitrary",)),
        name="prenorm",
    )(ctx2d, x2d, g_pre, mod3, mod3)


def _matmul_kernel(a_ref, w_ref, o_ref):
    o_ref[...] = lax.dot_general(a_ref[...], w_ref[...], (((1,), (1,)), ((), ())),
                                 preferred_element_type=F32).astype(o_ref.dtype)


def _matmul(a, w_t, tn, out_dtype=BF16):
    m, k = a.shape
    n = w_t.shape[0]
    tm = next(t for t in (768, 512, 256) if m % t == 0)
    assert m % tm == 0 and n % tn == 0
    return pl.pallas_call(
        _matmul_kernel,
        grid=(n // tn, m // tm),
        in_specs=[pl.BlockSpec((tm, k), lambda j, i: (i, 0)),
                  pl.BlockSpec((tn, k), lambda j, i: (j, 0))],
        out_specs=pl.BlockSpec((tm, tn), lambda j, i: (i, j)),
        out_shape=jax.ShapeDtypeStruct((m, n), out_dtype),
        compiler_params=_cparams(("parallel", "parallel")),
        name="in_proj",
    )(a, w_t)


def _rope_tables(geo):
    f32 = np.float32
    rows = geo.seq // GRID_W
    t_row = np.repeat(np.arange(rows, dtype=f32), GRID_W)
    t_col = np.tile(np.arange(GRID_W, dtype=f32), rows)
    half = HEAD_DIM // 2
    freqs = f32(ROPE_THETA) ** (-(np.arange(0, half, 2, dtype=f32) / f32(half)))
    ang = np.concatenate([t_row[:, None] * freqs, t_col[:, None] * freqs], axis=-1).astype(f32)
    cos, sin = np.cos(ang), np.sin(ang)
    cos_l = np.concatenate([cos, cos], axis=-1)
    sin_l = np.concatenate([-sin, sin], axis=-1)
    cos_c = np.ones((geo.ctx_len, HEAD_DIM), f32)
    sin_c = np.zeros((geo.ctx_len, HEAD_DIM), f32)
    return (jnp.asarray(np.concatenate([cos_c, cos_l], axis=0), F32),
            jnp.asarray(np.concatenate([sin_c, sin_l], axis=0), F32))


def _qk_prep_kernel(q_ref, k_ref, v_ref, cos_ref, sin_ref, gq_ref, gk_ref, shift_ref, qt_ref, ko_ref, vt_ref):
    cos, sin = cos_ref[...], sin_ref[...]
    t = q_ref.shape[0]

    ones = jnp.ones((HEAD_DIM, HEAD_DIM), BF16)

    def norm_rope(x, g):
        x = x.astype(F32)
        hi, lo = _split_bf16(x * x, 2)
        ss = jnp.dot(hi, ones, preferred_element_type=F32) + jnp.dot(lo, ones, preferred_element_type=F32)
        y = x * lax.rsqrt(ss * (1.0 / HEAD_DIM) + NORM_EPS) * g
        return y * cos + pltpu.roll(y, HEAD_DIM // 2, 1) * sin

    gq, gk = gq_ref[...], gk_ref[...]
    for h in range(ATT_HEADS):
        g, r = divmod(h, ATT_REP)
        y = norm_rope(q_ref[:, h * HEAD_DIM:(h + 1) * HEAD_DIM], gq) * Q_SCALE
        qt_ref[g, 0, 0:HEAD_DIM, r * t:(r + 1) * t] = y.astype(qt_ref.dtype).T
    extra = MXU_WIDTH - HEAD_DIM
    first_row = lax.broadcasted_iota(jnp.int32, (extra, ATT_REP * t), 0) == 0
    first_lane = lax.broadcasted_iota(jnp.int32, (t, extra), 1) == 0
    for g in range(ATT_KV_HEADS):
        hs = slice(g * HEAD_DIM, (g + 1) * HEAD_DIM)
        qt_ref[g, 0, HEAD_DIM:MXU_WIDTH, :] = jnp.where(first_row, 1.0, 0.0).astype(qt_ref.dtype)
        ko_ref[g, :, 0:HEAD_DIM] = norm_rope(k_ref[:, hs], gk).astype(ko_ref.dtype)
        ko_ref[g, :, HEAD_DIM:MXU_WIDTH] = jnp.where(first_lane, -shift_ref[...], 0.0).astype(ko_ref.dtype)
        vt_ref[g, 0, 0:HEAD_DIM, :] = v_ref[:, hs].T
        vt_ref[g, 0, HEAD_DIM:VT_ROWS, :] = jnp.ones((VT_ROWS - HEAD_DIM, t), vt_ref.dtype)


def _qk_prep(proj, cos_t, sin_t, g_q, g_k, shift_row, geo):
    row = lambda b, j: b * geo.tpb + j
    nq, nk = ATT_HEADS * HEAD_DIM, ATT_KV_HEADS * HEAD_DIM
    kv, t, depth = ATT_KV_HEADS, ROW_TILE, MXU_WIDTH
    return pl.pallas_call(
        _qk_prep_kernel,
        grid=(geo.batch, geo.tpb),
        in_specs=[pl.BlockSpec((t, nq), lambda b, j: (row(b, j), _COL["A_Q"] // nq)),
                  pl.BlockSpec((t, nk), lambda b, j: (row(b, j), _COL["A_K"] // nk)),
                  pl.BlockSpec((t, nk), lambda b, j: (row(b, j), _COL["A_V"] // nk)),
                  pl.BlockSpec((t, HEAD_DIM), lambda b, j: (j, 0)),
                  pl.BlockSpec((t, HEAD_DIM), lambda b, j: (j, 0)),
                  pl.BlockSpec((1, HEAD_DIM), lambda b, j: (0, 0)),
                  pl.BlockSpec((1, HEAD_DIM), lambda b, j: (0, 0)),
                  pl.BlockSpec((1, depth - HEAD_DIM), lambda b, j: (0, 0))],
        out_specs=[pl.BlockSpec((kv, 1, depth, ATT_REP * t), lambda b, j: (b, j, 0, 0)),
                   pl.BlockSpec((kv, t, depth), lambda b, j: (b, j, 0)),
                   pl.BlockSpec((kv, 1, VT_ROWS, t), lambda b, j: (b, j, 0, 0))],
        out_shape=[jax.ShapeDtypeStruct((geo.batch * kv, geo.tpb, depth, ATT_REP * t), BF16),
                   jax.ShapeDtypeStruct((geo.batch * kv, geo.lt, depth), BF16),
                   jax.ShapeDtypeStruct((geo.batch * kv, geo.tpb, VT_ROWS, t), BF16)],
        compiler_params=_cparams(("parallel", "parallel")),
        name="qk_prep",
    )(proj, proj, proj, cos_t, sin_t, g_q, g_k, shift_row)


def _attn_kernel(*refs, n_str):
    qt_refs = refs[0:n_str]
    k_ref, vt_ref = refs[n_str:n_str + 2]
    gate_refs = refs[n_str + 2:2 * n_str + 2]
    o_ref, m_ref, acc_ref, s_ref, mx_ref = refs[2 * n_str + 2:]
    vt_tile = vt_ref.shape[2]
    tk = s_ref.shape[1]
    tiles = tk // vt_tile
    n_chunks = k_ref.shape[0] // tk
    tq = o_ref.shape[0] // n_str
    m_ref[...] = jnp.full(m_ref.shape, -jnp.inf, F32)
    acc_ref[...] = jnp.zeros(acc_ref.shape, F32)

    def scores(c, parity):
        k = k_ref[pl.ds(pl.multiple_of(c * tk, tk), tk), :]
        for st in range(n_str):
            s = jnp.dot(k, qt_refs[st][...], preferred_element_type=F32)
            s_ref[2 * st + parity] = s
            mx_ref[2 * st + parity] = jnp.max(s, axis=0, keepdims=True)

    def absorb(c, parity):
        for st in range(n_str):
            buf = 2 * st + parity
            m_old = m_ref[st]
            m_new = jnp.maximum(m_old, mx_ref[buf])
            alpha = jnp.exp2(m_old - m_new)
            pv = None
            for t in range(tiles):
                p = jnp.exp2((s_ref[buf, t * vt_tile:(t + 1) * vt_tile, :] - m_new).astype(BF16))
                part = jnp.dot(vt_ref[c * tiles + t], p, preferred_element_type=F32)
                pv = part if pv is None else pv + part
            acc_ref[st] = alpha * acc_ref[st] + pv
            m_ref[st] = m_new

    def group(i, carry):
        c0 = ATTN_UNROLL * i
        for u in range(ATTN_UNROLL):
            scores(c0 + u + 1, (u + 1) % 2)
            absorb(c0 + u, u % 2)
        return carry

    scores(0, 0)
    n_groups = (n_chunks - 1) // ATTN_UNROLL
    if n_groups > 0:
        lax.fori_loop(0, n_groups, group, 0)
    for c in range(n_groups * ATTN_UNROLL, n_chunks):
        if c + 1 < n_chunks:
            scores(c + 1, (c + 1) % 2)
        absorb(c, c % 2)
    _attn_epilogue(acc_ref, gate_refs, o_ref, n_str)


def _attn_epilogue(acc_ref, gate_refs, o_ref, n_str):
    tq = o_ref.shape[0] // n_str
    for st in range(n_str):
        o_t = acc_ref[st, 0:HEAD_DIM, :] / acc_ref[st, HEAD_DIM:HEAD_DIM + 1, :]
        o = jnp.concatenate([o_t[:, r * tq:(r + 1) * tq].T for r in range(ATT_REP)], axis=1)
        o_ref[st * tq:(st + 1) * tq, :] = (o * _silu(gate_refs[st][...].astype(F32))).astype(o_ref.dtype)


def _attn_bounded_kernel(*refs, n_str, tk):
    qt_refs = refs[0:n_str]
    k_ref, vt_ref = refs[n_str:n_str + 2]
    gate_refs = refs[n_str + 2:2 * n_str + 2]
    o_ref, acc_ref = refs[2 * n_str + 2:]
    vt_tile = vt_ref.shape[2]
    tiles = tk // vt_tile
    n_chunks = k_ref.shape[0] // tk
    acc_ref[...] = jnp.zeros(acc_ref.shape, F32)

    def chunk(c):
        k = k_ref[pl.ds(pl.multiple_of(c * tk, tk), tk), :]
        for st in range(n_str):
            p = jnp.exp2(jnp.dot(k, qt_refs[st][...], preferred_element_type=F32).astype(BF16))
            pv = None
            for t in range(tiles):
                part = jnp.dot(vt_ref[c * tiles + t, 0:HEAD_DIM, :], p[t * vt_tile:(t + 1) * vt_tile, :],
                               preferred_element_type=F32)
                pv = part if pv is None else pv + part
            acc_ref[st, 0:HEAD_DIM, :] += pv
            acc_ref[st, HEAD_DIM:HEAD_DIM + 1, :] += jnp.sum(p.astype(F32), axis=0, keepdims=True)

    def group(i, carry):
        for u in range(ATTN_BOUNDED_UNROLL):
            chunk(ATTN_BOUNDED_UNROLL * i + u)
        return carry

    n_groups = n_chunks // ATTN_BOUNDED_UNROLL
    if n_groups > 0:
        lax.fori_loop(0, n_groups, group, 0)
    for c in range(n_groups * ATTN_BOUNDED_UNROLL, n_chunks):
        chunk(c)
    _attn_epilogue(acc_ref, gate_refs, o_ref, n_str)


def _attention(qt, kh, vt, proj, geo, q_first_row, q_rows, kv_rows, bounded):
    tq = ROW_TILE
    assert q_first_row % tq == 0 and q_rows % tq == 0 and kv_rows % ROW_TILE == 0
    q0, n_tiles = q_first_row // tq, q_rows // tq
    gw = ATT_REP * HEAD_DIM
    mq = ATT_REP * tq
    depth = qt.shape[2]
    head = lambda b, g: b * ATT_KV_HEADS + g

    def build(kernel_fn, streams, scratch, name):
        n_str = streams if n_tiles % streams == 0 else 1
        tile = lambda i, st: q0 + n_str * i + st
        stream_specs = lambda make: [make(st) for st in range(n_str)]
        call = pl.pallas_call(
            functools.partial(kernel_fn, n_str=n_str),
            grid=(geo.batch, ATT_KV_HEADS, n_tiles // n_str),
            in_specs=[
                *stream_specs(lambda st: pl.BlockSpec((None, None, depth, mq),
                                                      lambda b, g, i: (head(b, g), tile(i, st), 0, 0))),
                pl.BlockSpec((None, kv_rows, depth), lambda b, g, i: (head(b, g), 0, 0)),
                pl.BlockSpec((None, kv_rows // ROW_TILE, VT_ROWS, ROW_TILE), lambda b, g, i: (head(b, g), 0, 0, 0)),
                *stream_specs(lambda st: pl.BlockSpec((tq, gw), lambda b, g, i: (b * geo.tpb + tile(i, st),
                                                                                 _COL["A_G"] // gw + g))),
            ],
            out_specs=pl.BlockSpec((n_str * tq, gw), lambda b, g, i: (b * (n_tiles // n_str) + i, g)),
            out_shape=jax.ShapeDtypeStruct((geo.batch * q_rows, BRANCH_WIDTH), BF16),
            scratch_shapes=scratch(n_str),
            compiler_params=_cparams(("parallel", "parallel", "arbitrary")),
            name=name)
        return lambda: call(*([qt] * n_str), kh, vt, *([proj] * n_str))

    tk = max(t for t in ATTN_KEY_CHUNKS if kv_rows % t == 0)
    online = build(_attn_kernel, ATTN_STREAMS,
                   lambda n: [pltpu.VMEM((n, 1, mq), F32), pltpu.VMEM((n, VT_ROWS, mq), F32),
                              pltpu.VMEM((2 * n, tk, mq), F32), pltpu.VMEM((2 * n, 1, mq), F32)],
                   "attention_online")
    tkb = max(t for t in ATTN_BOUNDED_KEY_CHUNKS if kv_rows % t == 0)
    fast = build(functools.partial(_attn_bounded_kernel, tk=tkb), ATTN_BOUNDED_STREAMS,
                 lambda n: [pltpu.VMEM((n, VT_ROWS, mq), F32)], "attention")
    return lax.cond(bounded, fast, online)


def _halo_specs(width, col_block, geo, first, row):
    per = ROW_TILE // SUBLANES
    last = geo.rows // SUBLANES - 1
    prev = pl.BlockSpec((SUBLANES, width), lambda b, j: (jnp.maximum(row(b, j) * per - 1, 0), col_block))
    nxt = pl.BlockSpec((SUBLANES, width), lambda b, j: (jnp.minimum((row(b, j) + 1) * per, last), col_block))
    return prev, nxt


def _fill_ext(ext_ref, cur, prev, nxt, j, geo):
    t = cur.shape[0]
    seg_first = jnp.logical_or(j == 0, j == geo.ctx_tiles)
    seg_last = jnp.logical_or(j == geo.ctx_tiles - 1, j == geo.tpb - 1)
    ext_ref[0:SUBLANES, :] = jnp.where(seg_first, 0.0, prev)
    ext_ref[SUBLANES:SUBLANES + t, :] = cur
    ext_ref[SUBLANES + t:2 * SUBLANES + t, :] = jnp.where(seg_last, 0.0, nxt)


def _ssd_conv_kernel(cur_ref, prev_ref, next_ref, nar_ref, w_ref, b_ref, dtb_ref, shift_ref, xbc_ref, dt_ref, ext_ref,
                     *, geo):
    j = pl.program_id(1)
    cur = cur_ref[...]
    cur_f = cur.astype(F32)
    _fill_ext(ext_ref, cur_f, prev_ref[...].astype(F32), next_ref[...].astype(F32), j, geo)
    t, half = cur_ref.shape[0], SSD_CONV // 2
    acc = b_ref[...] + w_ref[half:half + 1, :] * cur_f
    taps = [k for k in range(SSD_CONV) if k != half]
    for i, k in enumerate(taps):
        acc = acc + w_ref[k:k + 1, :] * jnp.dot(shift_ref[i], cur, preferred_element_type=F32)

    def edge(r0):
        e = jnp.zeros((SUBLANES, cur_ref.shape[1]), F32) + b_ref[...]
        for k in range(SSD_CONV):
            e = e + w_ref[k:k + 1, :] * ext_ref[SUBLANES + r0 + k - half:2 * SUBLANES + r0 + k - half, :]
        return e

    acc = jnp.concatenate([edge(0), acc[SUBLANES:t - SUBLANES], edge(t - SUBLANES)], axis=0)
    xbc_ref[...] = _silu(acc).astype(xbc_ref.dtype)
    lane = lax.broadcasted_iota(jnp.int32, dt_ref.shape, 1)
    dt = _softplus(nar_ref[...].astype(F32) + dtb_ref[...])
    dt_ref[...] = jnp.where(lane < 2 * SSD_HEADS, dt, 0.0)


def _ssd_conv(proj, conv_w, conv_b, dt_bias_row, geo):
    width = BRANCH_WIDTH + 2 * SSD_GROUPS * SSD_STATE
    row = lambda b, j: b * geo.tpb + j
    cb = _COL["B_X"] // width
    prev, nxt = _halo_specs(width, cb, geo, 0, row)
    t_idx = np.arange(ROW_TILE)
    offsets = [k - SSD_CONV // 2 for k in range(SSD_CONV) if k != SSD_CONV // 2]
    shifts = jnp.asarray(np.stack([(t_idx[None, :] == t_idx[:, None] + d) for d in offsets]).astype(np.float32), BF16)
    return pl.pallas_call(
        functools.partial(_ssd_conv_kernel, geo=geo),
        grid=(geo.batch, geo.tpb),
        in_specs=[pl.BlockSpec((ROW_TILE, width), lambda b, j: (row(b, j), cb)), prev, nxt,
                  pl.BlockSpec((ROW_TILE, LANES), lambda b, j: (row(b, j), _COL["NARROW"] // LANES)),
                  pl.BlockSpec((SSD_CONV, width), lambda b, j: (0, 0)),
                  pl.BlockSpec((1, width), lambda b, j: (0, 0)),
                  pl.BlockSpec((1, LANES), lambda b, j: (0, 0)),
                  pl.BlockSpec((len(offsets), ROW_TILE, ROW_TILE), lambda b, j: (0, 0, 0))],
        out_specs=[pl.BlockSpec((ROW_TILE, width), lambda b, j: (row(b, j), 0)),
                   pl.BlockSpec((ROW_TILE, LANES), lambda b, j: (row(b, j), 0))],
        out_shape=[jax.ShapeDtypeStruct((geo.rows, width), BF16),
                   jax.ShapeDtypeStruct((geo.rows, LANES), F32)],
        scratch_shapes=[pltpu.VMEM((ROW_TILE + 2 * SUBLANES, width), F32)],
        compiler_params=_cparams(("parallel", "parallel")),
        name="ssd_conv",
    )(proj, proj, proj, proj, conv_w, conv_b, dt_bias_row, shifts)


def _scan_chunk(s, n_chunks, n_ctx_chunks, reverse):
    if not reverse:
        return s
    return jnp.where(s < n_ctx_chunks, n_ctx_chunks - 1 - s, n_chunks + n_ctx_chunks - 1 - s)


def _tri(n, reverse):
    t = lax.broadcasted_iota(jnp.int32, (n, n), 0)
    s = lax.broadcasted_iota(jnp.int32, (n, n), 1)
    return (s >= t) if reverse else (s <= t)


def _split_bf16(x, pieces):
    out, rest = [], x
    for _ in range(pieces):
        p = rest.astype(BF16)
        out.append(p)
        rest = rest - p.astype(F32)
    return out


def _cumsum_rows(mask_bf16, x):
    w = x.shape[1]
    parts = jnp.dot(mask_bf16, jnp.concatenate(_split_bf16(x, 3), axis=1), preferred_element_type=F32)
    return parts[:, 0:w] + parts[:, w:2 * w] + parts[:, 2 * w:3 * w]


def _ssd_scan_kernel(xbc_ref, dt_ref, alog_ref, e_ref, *rest, reverse):
    if reverse:
        yf_ref, z_ref, skip_ref, g_ref, y_ref, h_ref = rest
    else:
        y_ref, h_ref = rest
    q = SSD_CHUNK
    gw = SSD_HPG * SSD_HEAD_DIM
    quad = 4
    qw = quad * SSD_HEAD_DIM
    lane0 = SSD_HEADS if reverse else 0
    n_batch, n_sub = xbc_ref.shape[0], xbc_ref.shape[1] // q

    @pl.when(pl.program_id(0) == 0)
    def _():
        h_ref[...] = jnp.zeros_like(h_ref)

    lane = lax.broadcasted_iota(jnp.int32, (1, LANES), 1)
    a = jnp.where(lane < 2 * SSD_HEADS, -jnp.exp(alog_ref[...]), 0.0)
    mask = _tri(q, reverse)
    mask_bf = jnp.where(mask, 1.0, 0.0).astype(BF16)
    e = e_ref[...]
    head_of_lane = lax.broadcasted_iota(jnp.int32, (1, qw), 1) // SSD_HEAD_DIM
    zero_bf = jnp.zeros((), BF16)

    order = range(n_sub - 1, -1, -1) if reverse else range(n_sub)
    for ci, b in [(ci, b) for ci in order for b in range(n_batch)]:
        rows = slice(ci * q, (ci + 1) * q)
        dt = dt_ref[b, rows, :]
        cs = _cumsum_rows(mask_bf, dt * a)
        cs_last = cs[0:1, :] if reverse else cs[q - 1:q, :]
        zeros = jnp.zeros((q, LANES), BF16)
        ecs_hi, ecs_lo = _split_bf16(jnp.exp(cs), 2)
        cd_hi, cd_mid, cd_lo = _split_bf16(jnp.broadcast_to(jnp.exp(cs_last), (BF16_SUBLANES, LANES)), 3)
        stack = jnp.concatenate([
            jnp.concatenate([(jnp.exp(cs_last - cs) * dt).astype(BF16), zeros], axis=1),
            jnp.concatenate([ecs_hi, ecs_lo], axis=1),
            jnp.concatenate([cd_hi, cd_mid], axis=1),
            jnp.concatenate([cd_lo, zeros[0:BF16_SUBLANES]], axis=1)], axis=0)
        big = jnp.dot(stack, e, preferred_element_type=F32)
        x_bf = xbc_ref[b, rows, 0:BRANCH_WIDTH]
        x = x_bf.astype(F32)
        wx = (big[0:q] * x).astype(BF16)
        ecs = big[q:2 * q]
        r0 = 2 * q
        chunk_decay = big[r0:r0 + 1] + big[r0 + BF16_SUBLANES:r0 + BF16_SUBLANES + 1]
        cs_t = cs.T
        dt_t = dt.T
        pieces = []
        for g in range(SSD_GROUPS):
            bm = xbc_ref[b, rows, BRANCH_WIDTH + g * SSD_STATE:BRANCH_WIDTH + (g + 1) * SSD_STATE]
            c0 = BRANCH_WIDTH + SSD_GROUPS * SSD_STATE + g * SSD_STATE
            cm = xbc_ref[b, rows, c0:c0 + SSD_STATE]
            cb = lax.dot_general(cm, bm, (((1,), (1,)), ((), ())), preferred_element_type=F32)
            h_in = h_ref[b, g]
            y_off = jnp.dot(cm, h_in.astype(BF16), preferred_element_type=F32) * ecs[:, g * gw:(g + 1) * gw]
            states = jnp.dot(bm.T, wx[:, g * gw:(g + 1) * gw], preferred_element_type=F32)
            h_ref[b, g] = chunk_decay[:, g * gw:(g + 1) * gw] * h_in + states
            for hq in range(SSD_HPG // quad):
                ms = []
                for r in range(quad):
                    col = lane0 + g * SSD_HPG + hq * quad + r
                    seg = cs[:, col:col + 1] - cs_t[col:col + 1, :]
                    ms.append((cb * jnp.exp(jnp.where(mask, seg, -1e30)) * dt_t[col:col + 1, :]).astype(BF16))
                lo = g * gw + hq * qw
                slab = x_bf[:, lo:lo + qw]
                rhs = jnp.concatenate([jnp.where(head_of_lane == r, slab, zero_bf) for r in range(quad)], axis=0)
                y_diag = jnp.dot(jnp.concatenate(ms, axis=1), rhs, preferred_element_type=F32)
                pieces.append(y_diag + y_off[:, hq * qw:(hq + 1) * qw])
        y = jnp.concatenate(pieces, axis=1)
        if reverse:
            y = (yf_ref[b, rows, :] + y + skip_ref[...] * x) * _silu(z_ref[b, rows, :].astype(F32))
            y = y * lax.rsqrt(jnp.mean(y * y, axis=-1, keepdims=True) + NORM_EPS) * g_ref[...]
        y_ref[b, rows, :] = y.astype(y_ref.dtype)


def _ssd_scan(xbc, dt, a_log_row, expand_mat, geo, reverse, finish=None):
    nb, nbc = geo.lt // SCAN_BLOCK, geo.ctx_len // SCAN_BLOCK
    width, w, nbat = xbc.shape[1], BRANCH_WIDTH, geo.batch
    blk = lambda s: _scan_chunk(s, nb, nbc, reverse)
    per_batch = lambda a: a.reshape(nbat, geo.lt, a.shape[1])
    tok_spec = lambda cols, col_block: pl.BlockSpec((nbat, SCAN_BLOCK, cols), lambda s: (0, blk(s), col_block))
    const_spec = lambda shape: pl.BlockSpec(shape, lambda s: (0, 0))
    in_specs = [tok_spec(width, 0), tok_spec(LANES, 0), const_spec((1, LANES)), const_spec((2 * LANES, w))]
    args = [per_batch(xbc), per_batch(dt), a_log_row, expand_mat]
    if reverse:
        yf, proj, skip_row, norm_g = finish
        in_specs += [tok_spec(w, 0), tok_spec(w, _COL["B_Z"] // w), const_spec((1, w)), const_spec((1, w))]
        args += [per_batch(yf), per_batch(proj), skip_row, norm_g]
    out = pl.pallas_call(
        functools.partial(_ssd_scan_kernel, reverse=reverse),
        grid=(nb,),
        in_specs=in_specs,
        out_specs=tok_spec(w, 0),
        out_shape=jax.ShapeDtypeStruct((nbat, geo.lt, w), BF16 if reverse else F32),
        scratch_shapes=[pltpu.VMEM((nbat, SSD_GROUPS, SSD_STATE, SSD_HPG * SSD_HEAD_DIM), F32)],
        compiler_params=_cparams(("arbitrary",)),
        name="ssd_scan_bwd" if reverse else "ssd_scan_fwd",
    )(*args)
    return out.reshape(geo.rows, w)


def _gla_scan_kernel(q_ref, k_ref, v_ref, nar_ref, w2_ref, bf_ref, *rest, reverse):
    if reverse:
        of_ref, gate_ref, g_ref, o_ref, h_ref = rest
    else:
        o_ref, h_ref = rest
    n = GLA_CHUNK

    @pl.when(pl.program_id(0) == 0)
    def _():
        h_ref[...] = jnp.zeros_like(h_ref)

    n_batch, t = q_ref.shape[0], q_ref.shape[1]
    n_sub = t // n
    ri = lax.broadcasted_iota(jnp.int32, (t, t), 0)
    ci = lax.broadcasted_iota(jnp.int32, (t, t), 1)
    mask = jnp.logical_and(ri // n == ci // n, (ci >= ri) if reverse else (ci <= ri))
    mask_bf = jnp.where(mask, 1.0, 0.0).astype(BF16)
    for bi in range(n_batch):
        logit = jnp.dot(nar_ref[bi], w2_ref[...], preferred_element_type=F32) + bf_ref[...]
        gl = _log_sigmoid(logit) / GLA_TAU
        b = _cumsum_rows(mask_bf, gl)
        last = [b[c * n:c * n + 1, :] if reverse else b[(c + 1) * n - 1:(c + 1) * n, :] for c in range(n_sub)]
        b_last = jnp.concatenate([jnp.broadcast_to(r, (n, r.shape[1])) for r in last], axis=0)
        q = q_ref[bi].astype(F32) * (GLA_DK ** -0.5)
        k = k_ref[bi].astype(F32)
        qe = (q * jnp.exp(b)).astype(BF16)
        ke = (k * jnp.exp(-b)).astype(BF16)
        kd = (k * jnp.exp(b_last - b)).astype(BF16)
        for h in range(GLA_HEADS):
            ks = slice(h * GLA_DK, (h + 1) * GLA_DK)
            vs = slice(h * GLA_DV, (h + 1) * GLA_DV)
            v = v_ref[bi, :, vs]
            att = lax.dot_general(qe[:, ks], ke[:, ks], (((1,), (1,)), ((), ())), preferred_element_type=F32)
            o_intra = jnp.dot(jnp.where(mask, att, 0.0).astype(BF16), v, preferred_element_type=F32)
            o_inter = [None] * n_sub
            for c in (range(n_sub - 1, -1, -1) if reverse else range(n_sub)):
                rows = slice(c * n, (c + 1) * n)
                h_in = h_ref[bi, h]
                o_inter[c] = lax.dot_general(qe[rows, ks], h_in.astype(BF16), (((1,), (1,)), ((), ())),
                                             preferred_element_type=F32)
                upd = jnp.dot(v[rows].T, kd[rows, ks], preferred_element_type=F32)
                h_ref[bi, h] = jnp.exp(last[c][:, ks]) * h_in + upd
            o = o_intra + jnp.concatenate(o_inter, axis=0)
            if reverse:
                o = o + of_ref[bi, :, vs]
                o = o * lax.rsqrt(jnp.mean(o * o, axis=-1, keepdims=True) + NORM_EPS) * g_ref[...]
                o = o * _silu(gate_ref[bi, :, vs].astype(F32))
            o_ref[bi, :, vs] = o.astype(o_ref.dtype)


def _gla_scan(proj, w2, b_f, geo, reverse, finish=None):
    nb, nbc, nbat = geo.lt // SCAN_BLOCK, geo.ctx_len // SCAN_BLOCK, geo.batch
    blk = lambda s: _scan_chunk(s, nb, nbc, reverse)
    kw, vw = GLA_HEADS * GLA_DK, GLA_HEADS * GLA_DV
    per_batch = lambda a: a.reshape(nbat, geo.lt, a.shape[1])
    tok_spec = lambda cols, col_block: pl.BlockSpec((nbat, SCAN_BLOCK, cols), lambda s: (0, blk(s), col_block))
    const_spec = lambda shape: pl.BlockSpec(shape, lambda s: (0, 0))
    proj3 = per_batch(proj)
    in_specs = [tok_spec(kw, _COL["C_Q"] // kw), tok_spec(kw, _COL["C_K"] // kw), tok_spec(vw, _COL["C_V"] // vw),
                tok_spec(LANES, _COL["NARROW"] // LANES), const_spec((LANES, kw)), const_spec((1, kw))]
    args = [proj3, proj3, proj3, proj3, w2, b_f]
    if reverse:
        of, norm_g = finish
        in_specs += [tok_spec(vw, 0), tok_spec(vw, _COL["C_G"] // vw), const_spec((1, GLA_DV))]
        args += [per_batch(of), proj3, norm_g]
    out = pl.pallas_call(
        functools.partial(_gla_scan_kernel, reverse=reverse),
        grid=(nb,),
        in_specs=in_specs,
        out_specs=tok_spec(vw, 0),
        out_shape=jax.ShapeDtypeStruct((nbat, geo.lt, vw), BF16 if reverse else F32),
        scratch_shapes=[pltpu.VMEM((nbat, GLA_HEADS, GLA_DV, GLA_DK), F32)],
        compiler_params=_cparams(("arbitrary",)),
        name="gla_scan_bwd" if reverse else "gla_scan_fwd",
    )(*args)
    return out.reshape(geo.rows, vw)


def _shortconv_tile(cur_ref, prev_ref, next_ref, w_ref, ext_ref, tile, geo):
    w = BRANCH_WIDTH
    u = lambda ref: ref[:, w:2 * w].astype(F32) * ref[:, 2 * w:3 * w].astype(F32)
    _fill_ext(ext_ref, u(cur_ref), u(prev_ref), u(next_ref), tile, geo)
    t = cur_ref.shape[0]
    acc = jnp.zeros((t, w), F32)
    for k in range(SC_CONV):
        acc = acc + w_ref[k:k + 1, :] * ext_ref[SUBLANES + k - SC_CONV // 2:SUBLANES + k - SC_CONV // 2 + t, :]
    return cur_ref[:, 0:w].astype(F32) * acc * _silu(cur_ref[:, 3 * w:4 * w].astype(F32))


def _merge_out_kernel(*refs, n_ya, n_src, geo, first, with_next):
    tile = first + pl.program_id(1)
    ext_ref = refs[-1]
    ya = _read_tokens(refs[0:n_ya], tile, geo)
    refs = refs[n_ya:-1]
    yd = _shortconv_tile(*refs[2:6], ext_ref, tile, geo).astype(BF16)
    ys = [ya, refs[0][...], refs[1][...], yd]
    refs = refs[6:]
    gates = refs[0:N_BRANCH]
    wb_ref, wo_ref = refs[N_BRANCH:N_BRANCH + 2]
    p = N_BRANCH + 2
    srcs = refs[p:p + n_src]
    gt_ref, gp_ref = refs[p + n_src:p + n_src + 2]
    rest = refs[p + n_src + 2:]
    m = None
    for n in (N_BRANCH - 1, 0, 1, 2):
        term = jax.nn.sigmoid(gates[n][...].astype(F32)) * jnp.dot(ys[n], wb_ref[n], preferred_element_type=F32)
        m = term if m is None else m + term
    out = jnp.dot(m.astype(BF16), wo_ref[...], preferred_element_type=F32)
    y = out * lax.rsqrt(jnp.mean(out * out, axis=-1, keepdims=True) + NORM_EPS) * gp_ref[...]
    x_new = _read_tokens(srcs, tile, geo) + gt_ref[0] * y
    if with_next:
        gn_ref, shn_ref, scn_ref, o_ref, h_ref = rest
        h_ref[...] = _adaln_prenorm(x_new, gn_ref[...], shn_ref[0], scn_ref[0]).astype(h_ref.dtype)
    else:
        (o_ref,) = rest
    o_ref[...] = x_new


def _merge_out(ys, proj, w_branch, w_out, layer, tokens, mod3, g_post, geo, with_ctx, nxt=None):
    first, nt = geo.span(with_ctx)
    w, d = BRANCH_WIDTH, D_MODEL
    row = lambda b, j: b * geo.tpb + first + j
    out_rows = geo.rows if with_ctx else geo.batch * geo.seq
    out_row = row if with_ctx else (lambda b, j: b * nt + j)
    y_spec = pl.BlockSpec((ROW_TILE, w), lambda b, j: (row(b, j), 0))
    gate_specs = [pl.BlockSpec((ROW_TILE, d), functools.partial(lambda b, j, n: (row(b, j), _COL["MG"] // d + n), n=n))
                  for n in range(N_BRANCH)]
    resident = pl.Buffered(1)
    mod_spec = lambda part: pl.BlockSpec((1, 1, d), lambda b, j: (geo.mod_row(b, first + j), 0, part))
    vec_spec = pl.BlockSpec((1, d), lambda b, j: (0, 0))
    ya, yb, yg, conv_w = ys
    dw = N_BRANCH * w
    d_block = _COL["D_ALL"] // dw
    d_prev, d_next = _halo_specs(dw, d_block, geo, first, row)
    in_specs = [*_token_specs(ya, geo, first), y_spec, y_spec,
                pl.BlockSpec((ROW_TILE, dw), lambda b, j: (row(b, j), d_block)), d_prev, d_next,
                pl.BlockSpec((SC_CONV, w), lambda b, j: (0, 0)), *gate_specs,
                pl.BlockSpec((None, N_BRANCH, w, d), lambda b, j: (layer, 0, 0, 0), pipeline_mode=resident),
                pl.BlockSpec((None, d, d), lambda b, j: (layer, 0, 0), pipeline_mode=resident),
                *_token_specs(tokens, geo, first), mod_spec(2), vec_spec]
    args = [*ya, yb, yg, proj, proj, proj, conv_w, proj, proj, proj, proj, w_branch, w_out, *tokens, mod3, g_post]
    out_specs = [pl.BlockSpec((ROW_TILE, d), lambda b, j: (out_row(b, j), 0))]
    out_shape = [jax.ShapeDtypeStruct((out_rows, d), F32)]
    if nxt is not None:
        assert with_ctx
        g_next, mod3_next = nxt
        in_specs += [vec_spec, mod_spec(0), mod_spec(1)]
        args += [g_next, mod3_next, mod3_next]
        out_specs.append(pl.BlockSpec((ROW_TILE, d), lambda b, j: (row(b, j), 0)))
        out_shape.append(jax.ShapeDtypeStruct((geo.rows, d), BF16))
    outs = pl.pallas_call(
        functools.partial(_merge_out_kernel, n_ya=len(ya), n_src=len(tokens), geo=geo, first=first,
                          with_next=nxt is not None),
        grid=(geo.batch, nt),
        in_specs=in_specs,
        out_specs=out_specs,
        out_shape=out_shape,
        scratch_shapes=[pltpu.VMEM((ROW_TILE + 2 * SUBLANES, w), F32)],
        compiler_params=_cparams(("parallel", "parallel")),
        name="merge_out",
    )(*args)
    return outs if nxt is not None else (outs[0], None)


def _regroup_plan():
    bw = BRANCH_WIDTH
    names = ("a_q", "a_k", "a_v", "a_g", "b_x", "b_z", "b_b", "b_c", "b_dt", "c_q", "c_k", "c_v", "c_g", "c_f",
             "d_all", "mg")
    widths = (ATT_HEADS * HEAD_DIM, ATT_KV_HEADS * HEAD_DIM, ATT_KV_HEADS * HEAD_DIM, bw,
              bw, bw, SSD_GROUPS * SSD_STATE, SSD_GROUPS * SSD_STATE, 2 * SSD_HEADS,
              GLA_HEADS * GLA_DK, GLA_HEADS * GLA_DK, GLA_HEADS * GLA_DV, bw, 2 * GLA_RANK,
              4 * bw, N_BRANCH * D_MODEL)
    src = dict(zip(names, np.concatenate([[0], np.cumsum(widths)[:-1]]).tolist()))
    wid = dict(zip(names, widths))
    dst = dict(a_q=_COL["A_Q"], a_g=_COL["A_G"], b_z=_COL["B_Z"], b_x=_COL["B_X"], b_b=_COL["B_B"], b_c=_COL["B_C"],
               a_k=_COL["A_K"], a_v=_COL["A_V"], c_v=_COL["C_V"], c_g=_COL["C_G"], c_q=_COL["C_Q"], c_k=_COL["C_K"],
               d_all=_COL["D_ALL"], mg=_COL["MG"], b_dt=_COL["NARROW"] + DT_LANE0, c_f=_COL["NARROW"] + F1_LANE0)
    return [(dst[n], src[n], wid[n], n in ("a_q", "a_k")) for n in names], sum(widths)


def _regroup_kernel(w_ref, o_ref):
    plan, _ = _regroup_plan()
    half = HEAD_DIM // 2
    for dst, src, width, deinterleave in plan:
        if deinterleave:
            for h in range(width // HEAD_DIM):
                s, d = src + h * HEAD_DIM, dst + h * HEAD_DIM
                o_ref[d:d + half, :] = w_ref[pl.ds(s, half, stride=2), :].astype(o_ref.dtype)
                o_ref[d + half:d + HEAD_DIM, :] = w_ref[pl.ds(s + 1, half, stride=2), :].astype(o_ref.dtype)
        else:
            o_ref[dst:dst + width, :] = w_ref[src:src + width, :].astype(o_ref.dtype)
    used = _COL["NARROW"] + F1_LANE0 + 2 * GLA_RANK
    o_ref[used:N_PROJ, :] = jnp.zeros((N_PROJ - used, o_ref.shape[1]), o_ref.dtype)


def _regroup_w_in(w_in, layer):
    w_t = jnp.swapaxes(w_in, 1, 2)
    _, n, k = w_t.shape
    assert n == _regroup_plan()[1]
    tk = LANES
    return pl.pallas_call(
        _regroup_kernel,
        grid=(k // tk,),
        in_specs=[pl.BlockSpec((None, n, tk), lambda i: (layer, 0, i))],
        out_specs=pl.BlockSpec((N_PROJ, tk), lambda i: (0, i)),
        out_shape=jax.ShapeDtypeStruct((N_PROJ, k), BF16),
        compiler_params=_cparams(("parallel",)),
        name="regroup_w_in",
    )(w_t)


def _deinterleave_vec(g):
    return g.reshape(HEAD_DIM // 2, 2).T.reshape(1, HEAD_DIM)


def _pad_lanes(v, lane0=0):
    return jnp.zeros((1, LANES), F32).at[0, lane0:lane0 + v.shape[0]].set(v.astype(F32))


def _head_expand_matrix(reverse):
    e = np.zeros((LANES, BRANCH_WIDTH), np.float32)
    lane0 = SSD_HEADS if reverse else 0
    for r in range(SSD_HEADS):
        e[lane0 + r, r * SSD_HEAD_DIM:(r + 1) * SSD_HEAD_DIM] = 1.0
    return jnp.asarray(np.concatenate([e, e], axis=0), BF16)


def _forget_weight(w_f2_dir, direction):
    lane0 = F1_LANE0 + direction * GLA_RANK
    return jnp.zeros((LANES, w_f2_dir.shape[1]), F32).at[lane0:lane0 + GLA_RANK].set(w_f2_dir).astype(BF16)


def kernel(x, c, ctx, c_ctx, w_mod, b_mod, g_pre, g_post, w_in, g_q, g_k, ssd_conv_w, ssd_conv_b,
           ssd_a_log, ssd_dt_bias, ssd_d, ssd_norm_g, gla_w_f2, gla_b_f, gla_norm_g, sc_conv_w,
           w_branch, w_out):
    batch, seq, d = x.shape
    ctx_len = ctx.shape[1]
    depth = w_in.shape[0]
    geo = _Geom(batch, ctx_len, seq)
    assert d == D_MODEL and batch + 1 <= SUBLANES and seq % GRID_W == 0

    cos_t, sin_t = _rope_tables(geo)
    c_rows = jnp.zeros((SUBLANES, d), F32).at[:batch].set(c).at[batch].set(c_ctx)
    tokens = (ctx.reshape(batch * ctx_len, d), x.reshape(batch * seq, d))
    e_fwd, e_bwd = _head_expand_matrix(False), _head_expand_matrix(True)
    w_branch_bf, w_out_bf = w_branch.astype(BF16), w_out.astype(BF16)
    mods =[_modulation(c_rows, w_mod, b_mod[l][None, :], l).reshape(SUBLANES, 1, 3 * d) for l in range(depth)]
    h = _prenorm(tokens, g_pre[0][None, :], mods[0], geo)

    for l in range(depth):
        need_ctx = l < depth - 1
        mod3 = mods[l]
        proj = _matmul(h, _regroup_w_in(w_in, l), PROJ_TN)

        shift = Q_SCALE * HEAD_DIM * jnp.max(jnp.abs(g_q[l])) * jnp.max(jnp.abs(g_k[l]))
        bounded = 2.0 * shift <= ATTN_SAFE_EXPONENT
        shift_row = jnp.full((1, MXU_WIDTH - HEAD_DIM), shift, F32)
        qt, kh, vt = _qk_prep(proj, cos_t, sin_t, _deinterleave_vec(g_q[l]), _deinterleave_vec(g_k[l]), shift_row, geo)
        ya = (_attention(qt, kh, vt, proj, geo, ctx_len, seq, geo.lt, bounded),)
        if need_ctx:
            ya = (_attention(qt, kh, vt, proj, geo, 0, ctx_len, ctx_len, bounded), *ya)

        xbc, dt = _ssd_conv(proj, ssd_conv_w[l], ssd_conv_b[l][None, :], _pad_lanes(ssd_dt_bias[l].reshape(-1)), geo)
        a_log_row = _pad_lanes(ssd_a_log[l].reshape(-1))
        ysf = _ssd_scan(xbc, dt, a_log_row, e_fwd, geo, False)
        skip_row = jnp.repeat(ssd_d[l], SSD_HEAD_DIM)[None, :]
        yb = _ssd_scan(xbc, dt, a_log_row, e_bwd, geo, True, (ysf, proj, skip_row, ssd_norm_g[l][None, :]))

        ogf = _gla_scan(proj, _forget_weight(gla_w_f2[l, 0], 0), gla_b_f[l, 0][None, :], geo, False)
        yg = _gla_scan(proj, _forget_weight(gla_w_f2[l, 1], 1), gla_b_f[l, 1][None, :], geo, True,
                       (ogf, gla_norm_g[l][None, :]))

        nxt = (g_pre[l + 1][None, :], mods[l + 1]) if need_ctx else None
        x_new, h = _merge_out((ya, yb, yg, sc_conv_w[l]), proj, w_branch_bf, w_out_bf, l, tokens, mod3,
                              g_post[l][None, :], geo, need_ctx, nxt)
        tokens = (x_new,)

    return x_new.reshape(batch, seq, d)
```

```python
import functools

import numpy as np
import jax
import jax.numpy as jnp
from jax import lax
from jax.experimental import pallas as pl
from jax.experimental.pallas import tpu as pltpu

F32 = jnp.float32
BF16 = jnp.bfloat16

D_MODEL = 2048
GRID_W = 64
BRANCH_WIDTH = D_MODEL // 2
N_BRANCH = 4
NORM_EPS = 1e-6
HEAD_DIM = 128
ATT_HEADS = BRANCH_WIDTH // HEAD_DIM
ATT_KV_HEADS = ATT_HEADS // 4
ATT_REP = ATT_HEADS // ATT_KV_HEADS
ROPE_THETA = 10000.0
SSD_HEAD_DIM = 64
SSD_HEADS = BRANCH_WIDTH // SSD_HEAD_DIM
SSD_GROUPS = 2
SSD_HPG = SSD_HEADS // SSD_GROUPS
SSD_STATE = 128
SSD_CONV = 5
SSD_CHUNK = 128
GLA_HEADS = 4
GLA_DV = BRANCH_WIDTH // GLA_HEADS
GLA_DK = GLA_DV // 2
GLA_RANK = 16
GLA_TAU = 16.0
GLA_CHUNK = 64
SC_CONV = 3

LANES = 128
SUBLANES = 8
ROW_TILE = 256
SCAN_BLOCK = ROW_TILE
VMEM_LIMIT = 56 * 1024 * 1024

_COL = dict(
    A_Q=0, A_G=1024, B_Z=2048, B_X=3072, B_B=4096, B_C=4352, A_K=4608, A_V=4864,
    C_V=5120, C_G=6144, C_Q=7168, C_K=7680,
    D_ALL=8192, MG=12288, NARROW=20480,
)
MXU_WIDTH = 256
N_PROJ = 20736
PROJ_TN = 2304
ATTN_UNROLL = 2
ATTN_STREAMS = 2
ATTN_BOUNDED_KEY_CHUNKS = (ROW_TILE, 3 * ROW_TILE)
ATTN_BOUNDED_UNROLL = 11
ATTN_BOUNDED_STREAMS = 2
ATTN_SAFE_EXPONENT = 100.0
ATTN_KEY_CHUNKS = (ROW_TILE, 2 * ROW_TILE, 3 * ROW_TILE)
BF16_SUBLANES = 16
VT_ROWS = HEAD_DIM + BF16_SUBLANES
Q_SCALE = HEAD_DIM ** -0.5 * float(np.log2(np.e))
DT_LANE0 = 0
F1_LANE0 = 32


def _cparams(sem, vmem=VMEM_LIMIT):
    return pltpu.CompilerParams(dimension_semantics=sem, vmem_limit_bytes=vmem)


def _silu(x):
    return x * jax.nn.sigmoid(x)


def _softplus(x):
    return jnp.maximum(x, 0.0) + jnp.log1p(jnp.exp(-jnp.abs(x)))


def _log_sigmoid(x):
    return jnp.minimum(x, 0.0) - jnp.log(1.0 + jnp.exp(-jnp.abs(x)))


def _mod_kernel(c_ref, w_ref, b_ref, o_ref):
    a = _silu(c_ref[...]).astype(BF16)
    o_ref[...] = jnp.dot(a, w_ref[...].astype(BF16), preferred_element_type=F32) + b_ref[...]


def _modulation(c_rows, w_mod, b_mod, layer):
    _, d, n = w_mod.shape
    tn = 1024
    return pl.pallas_call(
        _mod_kernel,
        grid=(n // tn,),
        in_specs=[pl.BlockSpec((SUBLANES, d), lambda j: (0, 0)),
                  pl.BlockSpec((None, d, tn), lambda j: (layer, 0, j)),
                  pl.BlockSpec((1, tn), lambda j: (0, j))],
        out_specs=pl.BlockSpec((SUBLANES, tn), lambda j: (0, j)),
        out_shape=jax.ShapeDtypeStruct((SUBLANES, n), F32),
        compiler_params=_cparams(("parallel",)),
        name="modulation",
    )(c_rows, w_mod, b_mod)


class _Geom:
    def __init__(self, batch, ctx_len, seq):
        self.batch, self.ctx_len, self.seq = batch, ctx_len, seq
        self.lt = ctx_len + seq
        self.rows = batch * self.lt
        assert ctx_len % ROW_TILE == 0 and seq % ROW_TILE == 0
        self.tpb = self.lt // ROW_TILE
        self.ctx_tiles = ctx_len // ROW_TILE
        self.lat_tiles = seq // ROW_TILE

    def span(self, with_ctx):
        return (0, self.tpb) if with_ctx else (self.ctx_tiles, self.lat_tiles)

    def mod_row(self, b, j):
        return jnp.where(j < self.ctx_tiles, self.batch, b)


def _token_specs(tokens, geo, first):
    c = tokens[0].shape[1]
    ct, lt = geo.ctx_tiles, geo.lat_tiles
    if len(tokens) == 1 and tokens[0].shape[0] == geo.rows:
        return [pl.BlockSpec((ROW_TILE, c), lambda b, j: (b * geo.tpb + first + j, 0))]
    if len(tokens) == 1:
        assert tokens[0].shape[0] == geo.batch * geo.seq and first >= ct
        return [pl.BlockSpec((ROW_TILE, c), lambda b, j: (b * lt + first + j - ct, 0))]
    return [pl.BlockSpec((ROW_TILE, c), lambda b, j: (b * ct + jnp.minimum(first + j, ct - 1), 0),
                         pipeline_mode=pl.Buffered(1)),
            pl.BlockSpec((ROW_TILE, c), lambda b, j: (b * lt + jnp.maximum(first + j - ct, 0), 0))]


def _read_tokens(refs, tile, geo):
    if len(refs) == 1:
        return refs[0][...]
    return jnp.where(tile < geo.ctx_tiles, refs[0][...], refs[1][...])


def _adaln_prenorm(x, g, sh, sc):
    y = x * lax.rsqrt(jnp.mean(x * x, axis=-1, keepdims=True) + NORM_EPS)
    return (y * g) * (1.0 + sc) + sh


PRENORM_SLOTS = 3


def _prenorm_kernel(ctx_hbm, x_hbm, g_ref, sh_ref, sc_ref, o_ref, buf_ref, sem_ref, *, geo):
    s = pl.program_id(0)
    n_steps = pl.num_programs(0)
    ct, lt, t = geo.ctx_tiles, geo.lat_tiles, ROW_TILE

    def tile_copy(src_hbm, tile_index, step):
        slot = step % PRENORM_SLOTS
        row = tile_index * t if isinstance(tile_index, int) else pl.multiple_of(tile_index * t, t)
        return pltpu.make_async_copy(src_hbm.at[pl.ds(row, t)], buf_ref.at[slot], sem_ref.at[slot])

    def start(step):
        b, j = step // geo.tpb, step % geo.tpb

        @pl.when(j < ct)
        def _():
            tile_copy(ctx_hbm, b * ct + j, step).start()

        @pl.when(j >= ct)
        def _():
            tile_copy(x_hbm, b * lt + j - ct, step).start()

    @pl.when(s == 0)
    def _():
        start(s)
        if geo.batch * geo.tpb > 1:
            start(s + 1)

    @pl.when(s + 2 < n_steps)
    def _():
        start(s + 2)

    tile_copy(x_hbm, 0, s).wait()
    x = buf_ref[s % PRENORM_SLOTS]
    o_ref[...] = _adaln_prenorm(x, g_ref[...], sh_ref[0], sc_ref[0]).astype(o_ref.dtype)


def _prenorm(tokens, g_pre, mod3, geo):
    d = D_MODEL
    ctx2d, x2d = tokens
    mod_row = lambda s: geo.mod_row(s // geo.tpb, s % geo.tpb)
    return pl.pallas_call(
        functools.partial(_prenorm_kernel, geo=geo),
        grid=(geo.batch * geo.tpb,),
        in_specs=[pl.BlockSpec(memory_space=pl.ANY), pl.BlockSpec(memory_space=pl.ANY),
                  pl.BlockSpec((1, d), lambda s: (0, 0)),
                  pl.BlockSpec((1, 1, d), lambda s: (mod_row(s), 0, 0)),
                  pl.BlockSpec((1, 1, d), lambda s: (mod_row(s), 0, 1))],
        out_specs=pl.BlockSpec((ROW_TILE, d), lambda s: (s, 0)),
        out_shape=jax.ShapeDtypeStruct((geo.rows, d), BF16),
        scratch_shapes=[pltpu.VMEM((PRENORM_SLOTS, ROW_TILE, d), F32), pltpu.SemaphoreType.DMA((PRENORM_SLOTS,))],
        compiler_params=_cparams(("arbitrary",)),
        name="prenorm",
    )(ctx2d, x2d, g_pre, mod3, mod3)


def _matmul_kernel(a_ref, w_ref, o_ref):
    o_ref[...] = lax.dot_general(a_ref[...], w_ref[...], (((1,), (1,)), ((), ())),
                                 preferred_element_type=F32).astype(o_ref.dtype)


def _matmul(a, w_t, tn, out_dtype=BF16):
    m, k = a.shape
    n = w_t.shape[0]
    tm = next(t for t in (1056, 768, 512, 256) if m % t == 0)
    assert m % tm == 0 and n % tn == 0
    return pl.pallas_call(
        _matmul_kernel,
        grid=(n // tn, m // tm),
        in_specs=[pl.BlockSpec((tm, k), lambda j, i: (i, 0)),
                  pl.BlockSpec((tn, k), lambda j, i: (j, 0))],
        out_specs=pl.BlockSpec((tm, tn), lambda j, i: (i, j)),
        out_shape=jax.ShapeDtypeStruct((m, n), out_dtype),
        compiler_params=_cparams(("parallel", "parallel")),
        name="in_proj",
    )(a, w_t)


def _rope_tables(geo):
    f32 = np.float32
    rows = geo.seq // GRID_W
    t_row = np.repeat(np.arange(rows, dtype=f32), GRID_W)
    t_col = np.tile(np.arange(GRID_W, dtype=f32), rows)
    half = HEAD_DIM // 2
    freqs = f32(ROPE_THETA) ** (-(np.arange(0, half, 2, dtype=f32) / f32(half)))
    ang = np.concatenate([t_row[:, None] * freqs, t_col[:, None] * freqs], axis=-1).astype(f32)
    cos, sin = np.cos(ang), np.sin(ang)
    cos_l = np.concatenate([cos, cos], axis=-1)
    sin_l = np.concatenate([-sin, sin], axis=-1)
    cos_c = np.ones((geo.ctx_len, HEAD_DIM), f32)
    sin_c = np.zeros((geo.ctx_len, HEAD_DIM), f32)
    return (jnp.asarray(np.concatenate([cos_c, cos_l], axis=0), F32),
            jnp.asarray(np.concatenate([sin_c, sin_l], axis=0), F32))


def _qk_prep_kernel(q_ref, k_ref, v_ref, cos_ref, sin_ref, gq_ref, gk_ref, shift_ref, qt_ref, ko_ref, vt_ref):
    cos, sin = cos_ref[...], sin_ref[...]
    t = q_ref.shape[0]

    ones = jnp.ones((HEAD_DIM, HEAD_DIM), BF16)

    def norm_rope(x, g):
        x = x.astype(F32)
        hi, lo = _split_bf16(x * x, 2)
        ss = jnp.dot(hi, ones, preferred_element_type=F32) + jnp.dot(lo, ones, preferred_element_type=F32)
        y = x * lax.rsqrt(ss * (1.0 / HEAD_DIM) + NORM_EPS) * g
        return y * cos + pltpu.roll(y, HEAD_DIM // 2, 1) * sin

    gq, gk = gq_ref[...], gk_ref[...]
    for h in range(ATT_HEADS):
        g, r = divmod(h, ATT_REP)
        y = norm_rope(q_ref[:, h * HEAD_DIM:(h + 1) * HEAD_DIM], gq) * Q_SCALE
        qt_ref[g, 0, 0:HEAD_DIM, r * t:(r + 1) * t] = y.astype(qt_ref.dtype).T
    extra = MXU_WIDTH - HEAD_DIM
    first_row = lax.broadcasted_iota(jnp.int32, (extra, ATT_REP * t), 0) == 0
    first_lane = lax.broadcasted_iota(jnp.int32, (t, extra), 1) == 0
    for g in range(ATT_KV_HEADS):
        hs = slice(g * HEAD_DIM, (g + 1) * HEAD_DIM)
        qt_ref[g, 0, HEAD_DIM:MXU_WIDTH, :] = jnp.where(first_row, 1.0, 0.0).astype(qt_ref.dtype)
        ko_ref[g, :, 0:HEAD_DIM] = norm_rope(k_ref[:, hs], gk).astype(ko_ref.dtype)
        ko_ref[g, :, HEAD_DIM:MXU_WIDTH] = jnp.where(first_lane, -shift_ref[...], 0.0).astype(ko_ref.dtype)
        vt_ref[g, 0, 0:HEAD_DIM, :] = v_ref[:, hs].T
        vt_ref[g, 0, HEAD_DIM:VT_ROWS, :] = jnp.ones((VT_ROWS - HEAD_DIM, t), vt_ref.dtype)


def _qk_prep(proj, cos_t, sin_t, g_q, g_k, shift_row, geo):
    row = lambda b, j: b * geo.tpb + j
    nq, nk = ATT_HEADS * HEAD_DIM, ATT_KV_HEADS * HEAD_DIM
    kv, t, depth = ATT_KV_HEADS, ROW_TILE, MXU_WIDTH
    return pl.pallas_call(
        _qk_prep_kernel,
        grid=(geo.batch, geo.tpb),
        in_specs=[pl.BlockSpec((t, nq), lambda b, j: (row(b, j), _COL["A_Q"] // nq)),
                  pl.BlockSpec((t, nk), lambda b, j: (row(b, j), _COL["A_K"] // nk)),
                  pl.BlockSpec((t, nk), lambda b, j: (row(b, j), _COL["A_V"] // nk)),
                  pl.BlockSpec((t, HEAD_DIM), lambda b, j: (j, 0)),
                  pl.BlockSpec((t, HEAD_DIM), lambda b, j: (j, 0)),
                  pl.BlockSpec((1, HEAD_DIM), lambda b, j: (0, 0)),
                  pl.BlockSpec((1, HEAD_DIM), lambda b, j: (0, 0)),
                  pl.BlockSpec((1, depth - HEAD_DIM), lambda b, j: (0, 0))],
        out_specs=[pl.BlockSpec((kv, 1, depth, ATT_REP * t), lambda b, j: (b, j, 0, 0)),
                   pl.BlockSpec((kv, t, depth), lambda b, j: (b, j, 0)),
                   pl.BlockSpec((kv, 1, VT_ROWS, t), lambda b, j: (b, j, 0, 0))],
        out_shape=[jax.ShapeDtypeStruct((geo.batch * kv, geo.tpb, depth, ATT_REP * t), BF16),
                   jax.ShapeDtypeStruct((geo.batch * kv, geo.lt, depth), BF16),
                   jax.ShapeDtypeStruct((geo.batch * kv, geo.tpb, VT_ROWS, t), BF16)],
        compiler_params=_cparams(("parallel", "parallel")),
        name="qk_prep",
    )(proj, proj, proj, cos_t, sin_t, g_q, g_k, shift_row)


def _attn_kernel(*refs, n_str):
    qt_refs = refs[0:n_str]
    k_ref, vt_ref = refs[n_str:n_str + 2]
    gate_refs = refs[n_str + 2:2 * n_str + 2]
    o_ref, m_ref, acc_ref, s_ref, mx_ref = refs[2 * n_str + 2:]
    vt_tile = vt_ref.shape[2]
    tk = s_ref.shape[1]
    tiles = tk // vt_tile
    n_chunks = k_ref.shape[0] // tk
    tq = o_ref.shape[0] // n_str
    m_ref[...] = jnp.full(m_ref.shape, -jnp.inf, F32)
    acc_ref[...] = jnp.zeros(acc_ref.shape, F32)

    def scores(c, parity):
        k = k_ref[pl.ds(pl.multiple_of(c * tk, tk), tk), :]
        for st in range(n_str):
            s = jnp.dot(k, qt_refs[st][...], preferred_element_type=F32)
            s_ref[2 * st + parity] = s
            mx_ref[2 * st + parity] = jnp.max(s, axis=0, keepdims=True)

    def absorb(c, parity):
        for st in range(n_str):
            buf = 2 * st + parity
            m_old = m_ref[st]
            m_new = jnp.maximum(m_old, mx_ref[buf])
            alpha = jnp.exp2(m_old - m_new)
            pv = None
            for t in range(tiles):
                p = jnp.exp2((s_ref[buf, t * vt_tile:(t + 1) * vt_tile, :] - m_new).astype(BF16))
                part = jnp.dot(vt_ref[c * tiles + t], p, preferred_element_type=F32)
                pv = part if pv is None else pv + part
            acc_ref[st] = alpha * acc_ref[st] + pv
            m_ref[st] = m_new

    def group(i, carry):
        c0 = ATTN_UNROLL * i
        for u in range(ATTN_UNROLL):
            scores(c0 + u + 1, (u + 1) % 2)
            absorb(c0 + u, u % 2)
        return carry

    scores(0, 0)
    n_groups = (n_chunks - 1) // ATTN_UNROLL
    if n_groups > 0:
        lax.fori_loop(0, n_groups, group, 0)
    for c in range(n_groups * ATTN_UNROLL, n_chunks):
        if c + 1 < n_chunks:
            scores(c + 1, (c + 1) % 2)
        absorb(c, c % 2)
    _attn_epilogue(acc_ref, gate_refs, o_ref, n_str)


def _attn_epilogue(acc_ref, gate_refs, o_ref, n_str):
    tq = o_ref.shape[0] // n_str
    for st in range(n_str):
        o_t = acc_ref[st, 0:HEAD_DIM, :] / acc_ref[st, HEAD_DIM:HEAD_DIM + 1, :]
        o = jnp.concatenate([o_t[:, r * tq:(r + 1) * tq].T for r in range(ATT_REP)], axis=1)
        o_ref[st * tq:(st + 1) * tq, :] = (o * _silu(gate_refs[st][...].astype(F32))).astype(o_ref.dtype)


def _attn_bounded_kernel(*refs, n_str, tk):
    qt_refs = refs[0:n_str]
    k_ref, vt_ref = refs[n_str:n_str + 2]
    gate_refs = refs[n_str + 2:2 * n_str + 2]
    o_ref, acc_ref = refs[2 * n_str + 2:]
    vt_tile = vt_ref.shape[2]
    tiles = tk // vt_tile
    n_chunks = k_ref.shape[0] // tk
    acc_ref[...] = jnp.zeros(acc_ref.shape, F32)

    def chunk(c):
        k = k_ref[pl.ds(pl.multiple_of(c * tk, tk), tk), :]
        for st in range(n_str):
            p = jnp.exp2(jnp.dot(k, qt_refs[st][...], preferred_element_type=F32).astype(BF16))
            pv = None
            for t in range(tiles):
                part = jnp.dot(vt_ref[c * tiles + t, 0:HEAD_DIM, :], p[t * vt_tile:(t + 1) * vt_tile, :],
                               preferred_element_type=F32)
                pv = part if pv is None else pv + part
            acc_ref[st, 0:HEAD_DIM, :] += pv
            acc_ref[st, HEAD_DIM:HEAD_DIM + 1, :] += jnp.sum(p.astype(F32), axis=0, keepdims=True)

    def group(i, carry):
        for u in range(ATTN_BOUNDED_UNROLL):
            chunk(ATTN_BOUNDED_UNROLL * i + u)
        return carry

    n_groups = n_chunks // ATTN_BOUNDED_UNROLL
    if n_groups > 0:
        lax.fori_loop(0, n_groups, group, 0)
    for c in range(n_groups * ATTN_BOUNDED_UNROLL, n_chunks):
        chunk(c)
    _attn_epilogue(acc_ref, gate_refs, o_ref, n_str)


def _attention(qt, kh, vt, proj, geo, q_first_row, q_rows, kv_rows, bounded):
    tq = ROW_TILE
    assert q_first_row % tq == 0 and q_rows % tq == 0 and kv_rows % ROW_TILE == 0
    q0, n_tiles = q_first_row // tq, q_rows // tq
    gw = ATT_REP * HEAD_DIM
    mq = ATT_REP * tq
    depth = qt.shape[2]
    head = lambda b, g: b * ATT_KV_HEADS + g

    def build(kernel_fn, streams, scratch, name):
        n_str = streams if n_tiles % streams == 0 else 1
        tile = lambda i, st: q0 + n_str * i + st
        stream_specs = lambda make: [make(st) for st in range(n_str)]
        call = pl.pallas_call(
            functools.partial(kernel_fn, n_str=n_str),
            grid=(geo.batch, ATT_KV_HEADS, n_tiles // n_str),
            in_specs=[
                *stream_specs(lambda st: pl.BlockSpec((None, None, depth, mq),
                                                      lambda b, g, i: (head(b, g), tile(i, st), 0, 0))),
                pl.BlockSpec((None, kv_rows, depth), lambda b, g, i: (head(b, g), 0, 0)),
                pl.BlockSpec((None, kv_rows // ROW_TILE, VT_ROWS, ROW_TILE), lambda b, g, i: (head(b, g), 0, 0, 0)),
                *stream_specs(lambda st: pl.BlockSpec((tq, gw), lambda b, g, i: (b * geo.tpb + tile(i, st),
                                                                                 _COL["A_G"] // gw + g))),
            ],
            out_specs=pl.BlockSpec((n_str * tq, gw), lambda b, g, i: (b * (n_tiles // n_str) + i, g)),
            out_shape=jax.ShapeDtypeStruct((geo.batch * q_rows, BRANCH_WIDTH), BF16),
            scratch_shapes=scratch(n_str),
            compiler_params=_cparams(("parallel", "parallel", "arbitrary")),
            name=name)
        return lambda: call(*([qt] * n_str), kh, vt, *([proj] * n_str))

    tk = max(t for t in ATTN_KEY_CHUNKS if kv_rows % t == 0)
    online = build(_attn_kernel, ATTN_STREAMS,
                   lambda n: [pltpu.VMEM((n, 1, mq), F32), pltpu.VMEM((n, VT_ROWS, mq), F32),
                              pltpu.VMEM((2 * n, tk, mq), F32), pltpu.VMEM((2 * n, 1, mq), F32)],
                   "attention_online")
    tkb = max(t for t in ATTN_BOUNDED_KEY_CHUNKS if kv_rows % t == 0)
    fast = build(functools.partial(_attn_bounded_kernel, tk=tkb), ATTN_BOUNDED_STREAMS,
                 lambda n: [pltpu.VMEM((n, VT_ROWS, mq), F32)], "attention")
    return lax.cond(bounded, fast, online)


def _halo_specs(width, col_block, geo, first, row):
    per = ROW_TILE // SUBLANES
    last = geo.rows // SUBLANES - 1
    prev = pl.BlockSpec((SUBLANES, width), lambda b, j: (jnp.maximum(row(b, j) * per - 1, 0), col_block))
    nxt = pl.BlockSpec((SUBLANES, width), lambda b, j: (jnp.minimum((row(b, j) + 1) * per, last), col_block))
    return prev, nxt


def _fill_ext(ext_ref, cur, prev, nxt, j, geo):
    t = cur.shape[0]
    seg_first = jnp.logical_or(j == 0, j == geo.ctx_tiles)
    seg_last = jnp.logical_or(j == geo.ctx_tiles - 1, j == geo.tpb - 1)
    ext_ref[0:SUBLANES, :] = jnp.where(seg_first, 0.0, prev)
    ext_ref[SUBLANES:SUBLANES + t, :] = cur
    ext_ref[SUBLANES + t:2 * SUBLANES + t, :] = jnp.where(seg_last, 0.0, nxt)


def _ssd_conv_kernel(cur_ref, prev_ref, next_ref, nar_ref, w_ref, b_ref, dtb_ref, shift_ref, xbc_ref, dt_ref, ext_ref,
                     *, geo):
    j = pl.program_id(1)
    cur = cur_ref[...]
    cur_f = cur.astype(F32)
    _fill_ext(ext_ref, cur_f, prev_ref[...].astype(F32), next_ref[...].astype(F32), j, geo)
    t, half = cur_ref.shape[0], SSD_CONV // 2
    acc = b_ref[...] + w_ref[half:half + 1, :] * cur_f
    taps = [k for k in range(SSD_CONV) if k != half]
    for i, k in enumerate(taps):
        acc = acc + w_ref[k:k + 1, :] * jnp.dot(shift_ref[i], cur, preferred_element_type=F32)

    def edge(r0):
        e = jnp.zeros((SUBLANES, cur_ref.shape[1]), F32) + b_ref[...]
        for k in range(SSD_CONV):
            e = e + w_ref[k:k + 1, :] * ext_ref[SUBLANES + r0 + k - half:2 * SUBLANES + r0 + k - half, :]
        return e

    acc = jnp.concatenate([edge(0), acc[SUBLANES:t - SUBLANES], edge(t - SUBLANES)], axis=0)
    xbc_ref[...] = _silu(acc).astype(xbc_ref.dtype)
    lane = lax.broadcasted_iota(jnp.int32, dt_ref.shape, 1)
    dt = _softplus(nar_ref[...].astype(F32) + dtb_ref[...])
    dt_ref[...] = jnp.where(lane < 2 * SSD_HEADS, dt, 0.0)


def _ssd_conv(proj, conv_w, conv_b, dt_bias_row, geo):
    width = BRANCH_WIDTH + 2 * SSD_GROUPS * SSD_STATE
    row = lambda b, j: b * geo.tpb + j
    cb = _COL["B_X"] // width
    prev, nxt = _halo_specs(width, cb, geo, 0, row)
    t_idx = np.arange(ROW_TILE)
    offsets = [k - SSD_CONV // 2 for k in range(SSD_CONV) if k != SSD_CONV // 2]
    shifts = jnp.asarray(np.stack([(t_idx[None, :] == t_idx[:, None] + d) for d in offsets]).astype(np.float32), BF16)
    return pl.pallas_call(
        functools.partial(_ssd_conv_kernel, geo=geo),
        grid=(geo.batch, geo.tpb),
        in_specs=[pl.BlockSpec((ROW_TILE, width), lambda b, j: (row(b, j), cb)), prev, nxt,
                  pl.BlockSpec((ROW_TILE, LANES), lambda b, j: (row(b, j), _COL["NARROW"] // LANES)),
                  pl.BlockSpec((SSD_CONV, width), lambda b, j: (0, 0)),
                  pl.BlockSpec((1, width), lambda b, j: (0, 0)),
                  pl.BlockSpec((1, LANES), lambda b, j: (0, 0)),
                  pl.BlockSpec((len(offsets), ROW_TILE, ROW_TILE), lambda b, j: (0, 0, 0))],
        out_specs=[pl.BlockSpec((ROW_TILE, width), lambda b, j: (row(b, j), 0)),
                   pl.BlockSpec((ROW_TILE, LANES), lambda b, j: (row(b, j), 0))],
        out_shape=[jax.ShapeDtypeStruct((geo.rows, width), BF16),
                   jax.ShapeDtypeStruct((geo.rows, LANES), F32)],
        scratch_shapes=[pltpu.VMEM((ROW_TILE + 2 * SUBLANES, width), F32)],
        compiler_params=_cparams(("parallel", "parallel")),
        name="ssd_conv",
    )(proj, proj, proj, proj, conv_w, conv_b, dt_bias_row, shifts)


def _scan_chunk(s, n_chunks, n_ctx_chunks, reverse):
    if not reverse:
        return s
    return jnp.where(s < n_ctx_chunks, n_ctx_chunks - 1 - s, n_chunks + n_ctx_chunks - 1 - s)


def _tri(n, reverse):
    t = lax.broadcasted_iota(jnp.int32, (n, n), 0)
    s = lax.broadcasted_iota(jnp.int32, (n, n), 1)
    return (s >= t) if reverse else (s <= t)


def _split_bf16(x, pieces):
    out, rest = [], x
    for _ in range(pieces):
        p = rest.astype(BF16)
        out.append(p)
        rest = rest - p.astype(F32)
    return out


def _cumsum_rows(mask_bf16, x):
    w = x.shape[1]
    parts = jnp.dot(mask_bf16, jnp.concatenate(_split_bf16(x, 3), axis=1), preferred_element_type=F32)
    return parts[:, 0:w] + parts[:, w:2 * w] + parts[:, 2 * w:3 * w]


def _ssd_scan_kernel(xbc_ref, dt_ref, alog_ref, e_ref, *rest, reverse):
    if reverse:
        yf_ref, z_ref, skip_ref, g_ref, y_ref, h_ref = rest
    else:
        y_ref, h_ref = rest
    q = SSD_CHUNK
    gw = SSD_HPG * SSD_HEAD_DIM
    quad = 4
    qw = quad * SSD_HEAD_DIM
    lane0 = SSD_HEADS if reverse else 0
    n_batch, n_sub = xbc_ref.shape[0], xbc_ref.shape[1] // q

    @pl.when(pl.program_id(0) == 0)
    def _():
        h_ref[...] = jnp.zeros_like(h_ref)

    lane = lax.broadcasted_iota(jnp.int32, (1, LANES), 1)
    a = jnp.where(lane < 2 * SSD_HEADS, -jnp.exp(alog_ref[...]), 0.0)
    mask = _tri(q, reverse)
    mask_bf = jnp.where(mask, 1.0, 0.0).astype(BF16)
    e = e_ref[...]
    head_of_lane = lax.broadcasted_iota(jnp.int32, (1, qw), 1) // SSD_HEAD_DIM
    zero_bf = jnp.zeros((), BF16)

    order = range(n_sub - 1, -1, -1) if reverse else range(n_sub)
    for ci, b in [(ci, b) for ci in order for b in range(n_batch)]:
        rows = slice(ci * q, (ci + 1) * q)
        dt = dt_ref[b, rows, :]
        cs = _cumsum_rows(mask_bf, dt * a)
        cs_last = cs[0:1, :] if reverse else cs[q - 1:q, :]
        zeros = jnp.zeros((q, LANES), BF16)
        ecs_hi, ecs_lo = _split_bf16(jnp.exp(cs), 2)
        cd_hi, cd_mid, cd_lo = _split_bf16(jnp.broadcast_to(jnp.exp(cs_last), (BF16_SUBLANES, LANES)), 3)
        stack = jnp.concatenate([
            jnp.concatenate([(jnp.exp(cs_last - cs) * dt).astype(BF16), zeros], axis=1),
            jnp.concatenate([ecs_hi, ecs_lo], axis=1),
            jnp.concatenate([cd_hi, cd_mid], axis=1),
            jnp.concatenate([cd_lo, zeros[0:BF16_SUBLANES]], axis=1)], axis=0)
        big = jnp.dot(stack, e, preferred_element_type=F32)
        x_bf = xbc_ref[b, rows, 0:BRANCH_WIDTH]
        x = x_bf.astype(F32)
        wx = (big[0:q] * x).astype(BF16)
        ecs = big[q:2 * q]
        r0 = 2 * q
        chunk_decay = big[r0:r0 + 1] + big[r0 + BF16_SUBLANES:r0 + BF16_SUBLANES + 1]
        cs_t = cs.T
        dt_t = dt.T
        pieces = []
        for g in range(SSD_GROUPS):
            bm = xbc_ref[b, rows, BRANCH_WIDTH + g * SSD_STATE:BRANCH_WIDTH + (g + 1) * SSD_STATE]
            c0 = BRANCH_WIDTH + SSD_GROUPS * SSD_STATE + g * SSD_STATE
            cm = xbc_ref[b, rows, c0:c0 + SSD_STATE]
            cb = lax.dot_general(cm, bm, (((1,), (1,)), ((), ())), preferred_element_type=F32)
            h_in = h_ref[b, g]
            y_off = jnp.dot(cm, h_in.astype(BF16), preferred_element_type=F32) * ecs[:, g * gw:(g + 1) * gw]
            states = jnp.dot(bm.T, wx[:, g * gw:(g + 1) * gw], preferred_element_type=F32)
            h_ref[b, g] = chunk_decay[:, g * gw:(g + 1) * gw] * h_in + states
            for hq in range(SSD_HPG // quad):
                ms = []
                for r in range(quad):
                    col = lane0 + g * SSD_HPG + hq * quad + r
                    seg = cs[:, col:col + 1] - cs_t[col:col + 1, :]
                    ms.append((cb * jnp.exp(jnp.where(mask, seg, -1e30)) * dt_t[col:col + 1, :]).astype(BF16))
                lo = g * gw + hq * qw
                slab = x_bf[:, lo:lo + qw]
                rhs = jnp.concatenate([jnp.where(head_of_lane == r, slab, zero_bf) for r in range(quad)], axis=0)
                y_diag = jnp.dot(jnp.concatenate(ms, axis=1), rhs, preferred_element_type=F32)
                pieces.append(y_diag + y_off[:, hq * qw:(hq + 1) * qw])
        y = jnp.concatenate(pieces, axis=1)
        if reverse:
            y = (yf_ref[b, rows, :] + y + skip_ref[...] * x) * _silu(z_ref[b, rows, :].astype(F32))
            y = y * lax.rsqrt(jnp.mean(y * y, axis=-1, keepdims=True) + NORM_EPS) * g_ref[...]
        y_ref[b, rows, :] = y.astype(y_ref.dtype)


def _ssd_scan(xbc, dt, a_log_row, expand_mat, geo, reverse, finish=None):
    nb, nbc = geo.lt // SCAN_BLOCK, geo.ctx_len // SCAN_BLOCK
    width, w, nbat = xbc.shape[1], BRANCH_WIDTH, geo.batch
    blk = lambda s: _scan_chunk(s, nb, nbc, reverse)
    per_batch = lambda a: a.reshape(nbat, geo.lt, a.shape[1])
    tok_spec = lambda cols, col_block: pl.BlockSpec((nbat, SCAN_BLOCK, cols), lambda s: (0, blk(s), col_block))
    const_spec = lambda shape: pl.BlockSpec(shape, lambda s: (0, 0))
    in_specs = [tok_spec(width, 0), tok_spec(LANES, 0), const_spec((1, LANES)), const_spec((2 * LANES, w))]
    args = [per_batch(xbc), per_batch(dt), a_log_row, expand_mat]
    if reverse:
        yf, proj, skip_row, norm_g = finish
        in_specs += [tok_spec(w, 0), tok_spec(w, _COL["B_Z"] // w), const_spec((1, w)), const_spec((1, w))]
        args += [per_batch(yf), per_batch(proj), skip_row, norm_g]
    out = pl.pallas_call(
        functools.partial(_ssd_scan_kernel, reverse=reverse),
        grid=(nb,),
        in_specs=in_specs,
        out_specs=tok_spec(w, 0),
        out_shape=jax.ShapeDtypeStruct((nbat, geo.lt, w), BF16 if reverse else F32),
        scratch_shapes=[pltpu.VMEM((nbat, SSD_GROUPS, SSD_STATE, SSD_HPG * SSD_HEAD_DIM), F32)],
        compiler_params=_cparams(("arbitrary",)),
        name="ssd_scan_bwd" if reverse else "ssd_scan_fwd",
    )(*args)
    return out.reshape(geo.rows, w)


def _gla_scan_kernel(q_ref, k_ref, v_ref, nar_ref, w2_ref, bf_ref, *rest, reverse):
    if reverse:
        of_ref, gate_ref, g_ref, o_ref, h_ref = rest
    else:
        o_ref, h_ref = rest
    n = GLA_CHUNK

    @pl.when(pl.program_id(0) == 0)
    def _():
        h_ref[...] = jnp.zeros_like(h_ref)

    n_batch, t = q_ref.shape[0], q_ref.shape[1]
    n_sub = t // n
    ri = lax.broadcasted_iota(jnp.int32, (t, t), 0)
    ci = lax.broadcasted_iota(jnp.int32, (t, t), 1)
    mask = jnp.logical_and(ri // n == ci // n, (ci >= ri) if reverse else (ci <= ri))
    mask_bf = jnp.where(mask, 1.0, 0.0).astype(BF16)
    for bi in range(n_batch):
        logit = jnp.dot(nar_ref[bi], w2_ref[...], preferred_element_type=F32) + bf_ref[...]
        gl = _log_sigmoid(logit) / GLA_TAU
        b = _cumsum_rows(mask_bf, gl)
        last = [b[c * n:c * n + 1, :] if reverse else b[(c + 1) * n - 1:(c + 1) * n, :] for c in range(n_sub)]
        b_last = jnp.concatenate([jnp.broadcast_to(r, (n, r.shape[1])) for r in last], axis=0)
        q = q_ref[bi].astype(F32) * (GLA_DK ** -0.5)
        k = k_ref[bi].astype(F32)
        qe = (q * jnp.exp(b)).astype(BF16)
        ke = (k * jnp.exp(-b)).astype(BF16)
        kd = (k * jnp.exp(b_last - b)).astype(BF16)
        for h in range(GLA_HEADS):
            ks = slice(h * GLA_DK, (h + 1) * GLA_DK)
            vs = slice(h * GLA_DV, (h + 1) * GLA_DV)
            v = v_ref[bi, :, vs]
            att = lax.dot_general(qe[:, ks], ke[:, ks], (((1,), (1,)), ((), ())), preferred_element_type=F32)
            o_intra = jnp.dot(jnp.where(mask, att, 0.0).astype(BF16), v, preferred_element_type=F32)
            o_inter = [None] * n_sub
            for c in (range(n_sub - 1, -1, -1) if reverse else range(n_sub)):
                rows = slice(c * n, (c + 1) * n)
                h_in = h_ref[bi, h]
                o_inter[c] = lax.dot_general(qe[rows, ks], h_in.astype(BF16), (((1,), (1,)), ((), ())),
                                             preferred_element_type=F32)
                upd = jnp.dot(v[rows].T, kd[rows, ks], preferred_element_type=F32)
                h_ref[bi, h] = jnp.exp(last[c][:, ks]) * h_in + upd
            o = o_intra + jnp.concatenate(o_inter, axis=0)
            if reverse:
                o = o + of_ref[bi, :, vs]
                o = o * lax.rsqrt(jnp.mean(o * o, axis=-1, keepdims=True) + NORM_EPS) * g_ref[...]
                o = o * _silu(gate_ref[bi, :, vs].astype(F32))
            o_ref[bi, :, vs] = o.astype(o_ref.dtype)


def _gla_scan(proj, w2, b_f, geo, reverse, finish=None):
    nb, nbc, nbat = geo.lt // SCAN_BLOCK, geo.ctx_len // SCAN_BLOCK, geo.batch
    blk = lambda s: _scan_chunk(s, nb, nbc, reverse)
    kw, vw = GLA_HEADS * GLA_DK, GLA_HEADS * GLA_DV
    per_batch = lambda a: a.reshape(nbat, geo.lt, a.shape[1])
    tok_spec = lambda cols, col_block: pl.BlockSpec((nbat, SCAN_BLOCK, cols), lambda s: (0, blk(s), col_block))
    const_spec = lambda shape: pl.BlockSpec(shape, lambda s: (0, 0))
    proj3 = per_batch(proj)
    in_specs = [tok_spec(kw, _COL["C_Q"] // kw), tok_spec(kw, _COL["C_K"] // kw), tok_spec(vw, _COL["C_V"] // vw),
                tok_spec(LANES, _COL["NARROW"] // LANES), const_spec((LANES, kw)), const_spec((1, kw))]
    args = [proj3, proj3, proj3, proj3, w2, b_f]
    if reverse:
        of, norm_g = finish
        in_specs += [tok_spec(vw, 0), tok_spec(vw, _COL["C_G"] // vw), const_spec((1, GLA_DV))]
        args += [per_batch(of), proj3, norm_g]
    out = pl.pallas_call(
        functools.partial(_gla_scan_kernel, reverse=reverse),
        grid=(nb,),
        in_specs=in_specs,
        out_specs=tok_spec(vw, 0),
        out_shape=jax.ShapeDtypeStruct((nbat, geo.lt, vw), BF16 if reverse else F32),
        scratch_shapes=[pltpu.VMEM((nbat, GLA_HEADS, GLA_DV, GLA_DK), F32)],
        compiler_params=_cparams(("arbitrary",)),
        name="gla_scan_bwd" if reverse else "gla_scan_fwd",
    )(*args)
    return out.reshape(geo.rows, vw)


def _shortconv_tile(cur_ref, prev_ref, next_ref, w_ref, ext_ref, tile, geo):
    w = BRANCH_WIDTH
    u = lambda ref: ref[:, w:2 * w].astype(F32) * ref[:, 2 * w:3 * w].astype(F32)
    _fill_ext(ext_ref, u(cur_ref), u(prev_ref), u(next_ref), tile, geo)
    t = cur_ref.shape[0]
    acc = jnp.zeros((t, w), F32)
    for k in range(SC_CONV):
        acc = acc + w_ref[k:k + 1, :] * ext_ref[SUBLANES + k - SC_CONV // 2:SUBLANES + k - SC_CONV // 2 + t, :]
    return cur_ref[:, 0:w].astype(F32) * acc * _silu(cur_ref[:, 3 * w:4 * w].astype(F32))


def _merge_out_kernel(*refs, n_ya, n_src, geo, first, with_next):
    tile = first + pl.program_id(1)
    ext_ref = refs[-1]
    ya = _read_tokens(refs[0:n_ya], tile, geo)
    refs = refs[n_ya:-1]
    yd = _shortconv_tile(*refs[2:6], ext_ref, tile, geo).astype(BF16)
    ys = [ya, refs[0][...], refs[1][...], yd]
    refs = refs[6:]
    gates = refs[0:N_BRANCH]
    wb_ref, wo_ref = refs[N_BRANCH:N_BRANCH + 2]
    p = N_BRANCH + 2
    srcs = refs[p:p + n_src]
    gt_ref, gp_ref = refs[p + n_src:p + n_src + 2]
    rest = refs[p + n_src + 2:]
    m = None
    for n in (N_BRANCH - 1, 0, 1, 2):
        term = jax.nn.sigmoid(gates[n][...].astype(F32)) * jnp.dot(ys[n], wb_ref[n], preferred_element_type=F32)
        m = term if m is None else m + term
    out = jnp.dot(m.astype(BF16), wo_ref[...], preferred_element_type=F32)
    y = out * lax.rsqrt(jnp.mean(out * out, axis=-1, keepdims=True) + NORM_EPS) * gp_ref[...]
    x_new = _read_tokens(srcs, tile, geo) + gt_ref[0] * y
    if with_next:
        gn_ref, shn_ref, scn_ref, o_ref, h_ref = rest
        h_ref[...] = _adaln_prenorm(x_new, gn_ref[...], shn_ref[0], scn_ref[0]).astype(h_ref.dtype)
    else:
        (o_ref,) = rest
    o_ref[...] = x_new


def _merge_out(ys, proj, w_branch, w_out, layer, tokens, mod3, g_post, geo, with_ctx, nxt=None):
    first, nt = geo.span(with_ctx)
    w, d = BRANCH_WIDTH, D_MODEL
    row = lambda b, j: b * geo.tpb + first + j
    out_rows = geo.rows if with_ctx else geo.batch * geo.seq
    out_row = row if with_ctx else (lambda b, j: b * nt + j)
    y_spec = pl.BlockSpec((ROW_TILE, w), lambda b, j: (row(b, j), 0))
    gate_specs = [pl.BlockSpec((ROW_TILE, d), functools.partial(lambda b, j, n: (row(b, j), _COL["MG"] // d + n), n=n))
                  for n in range(N_BRANCH)]
    resident = pl.Buffered(1)
    mod_spec = lambda part: pl.BlockSpec((1, 1, d), lambda b, j: (geo.mod_row(b, first + j), 0, part))
    vec_spec = pl.BlockSpec((1, d), lambda b, j: (0, 0))
    ya, yb, yg, conv_w = ys
    dw = N_BRANCH * w
    d_block = _COL["D_ALL"] // dw
    d_prev, d_next = _halo_specs(dw, d_block, geo, first, row)
    in_specs = [*_token_specs(ya, geo, first), y_spec, y_spec,
                pl.BlockSpec((ROW_TILE, dw), lambda b, j: (row(b, j), d_block)), d_prev, d_next,
                pl.BlockSpec((SC_CONV, w), lambda b, j: (0, 0)), *gate_specs,
                pl.BlockSpec((None, N_BRANCH, w, d), lambda b, j: (layer, 0, 0, 0), pipeline_mode=resident),
                pl.BlockSpec((None, d, d), lambda b, j: (layer, 0, 0), pipeline_mode=resident),
                *_token_specs(tokens, geo, first), mod_spec(2), vec_spec]
    args = [*ya, yb, yg, proj, proj, proj, conv_w, proj, proj, proj, proj, w_branch, w_out, *tokens, mod3, g_post]
    out_specs = [pl.BlockSpec((ROW_TILE, d), lambda b, j: (out_row(b, j), 0))]
    out_shape = [jax.ShapeDtypeStruct((out_rows, d), F32)]
    if nxt is not None:
        assert with_ctx
        g_next, mod3_next = nxt
        in_specs += [vec_spec, mod_spec(0), mod_spec(1)]
        args += [g_next, mod3_next, mod3_next]
        out_specs.append(pl.BlockSpec((ROW_TILE, d), lambda b, j: (row(b, j), 0)))
        out_shape.append(jax.ShapeDtypeStruct((geo.rows, d), BF16))
    outs = pl.pallas_call(
        functools.partial(_merge_out_kernel, n_ya=len(ya), n_src=len(tokens), geo=geo, first=first,
                          with_next=nxt is not None),
        grid=(geo.batch, nt),
        in_specs=in_specs,
        out_specs=out_specs,
        out_shape=out_shape,
        scratch_shapes=[pltpu.VMEM((ROW_TILE + 2 * SUBLANES, w), F32)],
        compiler_params=_cparams(("parallel", "parallel")),
        name="merge_out",
    )(*args)
    return outs if nxt is not None else (outs[0], None)


def _regroup_plan():
    bw = BRANCH_WIDTH
    names = ("a_q", "a_k", "a_v", "a_g", "b_x", "b_z", "b_b", "b_c", "b_dt", "c_q", "c_k", "c_v", "c_g", "c_f",
             "d_all", "mg")
    widths = (ATT_HEADS * HEAD_DIM, ATT_KV_HEADS * HEAD_DIM, ATT_KV_HEADS * HEAD_DIM, bw,
              bw, bw, SSD_GROUPS * SSD_STATE, SSD_GROUPS * SSD_STATE, 2 * SSD_HEADS,
              GLA_HEADS * GLA_DK, GLA_HEADS * GLA_DK, GLA_HEADS * GLA_DV, bw, 2 * GLA_RANK,
              4 * bw, N_BRANCH * D_MODEL)
    src = dict(zip(names, np.concatenate([[0], np.cumsum(widths)[:-1]]).tolist()))
    wid = dict(zip(names, widths))
    dst = dict(a_q=_COL["A_Q"], a_g=_COL["A_G"], b_z=_COL["B_Z"], b_x=_COL["B_X"], b_b=_COL["B_B"], b_c=_COL["B_C"],
               a_k=_COL["A_K"], a_v=_COL["A_V"], c_v=_COL["C_V"], c_g=_COL["C_G"], c_q=_COL["C_Q"], c_k=_COL["C_K"],
               d_all=_COL["D_ALL"], mg=_COL["MG"], b_dt=_COL["NARROW"] + DT_LANE0, c_f=_COL["NARROW"] + F1_LANE0)
    return [(dst[n], src[n], wid[n], n in ("a_q", "a_k")) for n in names], sum(widths)


def _regroup_kernel(w_ref, o_ref):
    plan, _ = _regroup_plan()
    half = HEAD_DIM // 2
    for dst, src, width, deinterleave in plan:
        if deinterleave:
            for h in range(width // HEAD_DIM):
                s, d = src + h * HEAD_DIM, dst + h * HEAD_DIM
                o_ref[d:d + half, :] = w_ref[pl.ds(s, half, stride=2), :].astype(o_ref.dtype)
                o_ref[d + half:d + HEAD_DIM, :] = w_ref[pl.ds(s + 1, half, stride=2), :].astype(o_ref.dtype)
        else:
            o_ref[dst:dst + width, :] = w_ref[src:src + width, :].astype(o_ref.dtype)
    used = _COL["NARROW"] + F1_LANE0 + 2 * GLA_RANK
    o_ref[used:N_PROJ, :] = jnp.zeros((N_PROJ - used, o_ref.shape[1]), o_ref.dtype)


def _regroup_w_in(w_in, layer):
    w_t = jnp.swapaxes(w_in, 1, 2)
    _, n, k = w_t.shape
    assert n == _regroup_plan()[1]
    tk = LANES
    return pl.pallas_call(
        _regroup_kernel,
        grid=(k // tk,),
        in_specs=[pl.BlockSpec((None, n, tk), lambda i: (layer, 0, i))],
        out_specs=pl.BlockSpec((N_PROJ, tk), lambda i: (0, i)),
        out_shape=jax.ShapeDtypeStruct((N_PROJ, k), BF16),
        compiler_params=_cparams(("parallel",)),
        name="regroup_w_in",
    )(w_t)


def _deinterleave_vec(g):
    return g.reshape(HEAD_DIM // 2, 2).T.reshape(1, HEAD_DIM)


def _pad_lanes(v, lane0=0):
    return jnp.zeros((1, LANES), F32).at[0, lane0:lane0 + v.shape[0]].set(v.astype(F32))


def _head_expand_matrix(reverse):
    e = np.zeros((LANES, BRANCH_WIDTH), np.float32)
    lane0 = SSD_HEADS if reverse else 0
    for r in range(SSD_HEADS):
        e[lane0 + r, r * SSD_HEAD_DIM:(r + 1) * SSD_HEAD_DIM] = 1.0
    return jnp.asarray(np.concatenate([e, e], axis=0), BF16)


def _forget_weight(w_f2_dir, direction):
    lane0 = F1_LANE0 + direction * GLA_RANK
    return jnp.zeros((LANES, w_f2_dir.shape[1]), F32).at[lane0:lane0 + GLA_RANK].set(w_f2_dir).astype(BF16)


def kernel(x, c, ctx, c_ctx, w_mod, b_mod, g_pre, g_post, w_in, g_q, g_k, ssd_conv_w, ssd_conv_b,
           ssd_a_log, ssd_dt_bias, ssd_d, ssd_norm_g, gla_w_f2, gla_b_f, gla_norm_g, sc_conv_w,
           w_branch, w_out):
    batch, seq, d = x.shape
    ctx_len = ctx.shape[1]
    depth = w_in.shape[0]
    geo = _Geom(batch, ctx_len, seq)
    assert d == D_MODEL and batch + 1 <= SUBLANES and seq % GRID_W == 0

    cos_t, sin_t = _rope_tables(geo)
    c_rows = jnp.zeros((SUBLANES, d), F32).at[:batch].set(c).at[batch].set(c_ctx)
    tokens = (ctx.reshape(batch * ctx_len, d), x.reshape(batch * seq, d))
    e_fwd, e_bwd = _head_expand_matrix(False), _head_expand_matrix(True)
    w_branch_bf, w_out_bf = w_branch.astype(BF16), w_out.astype(BF16)
    mods =[_modulation(c_rows, w_mod, b_mod[l][None, :], l).reshape(SUBLANES, 1, 3 * d) for l in range(depth)]
    h = _prenorm(tokens, g_pre[0][None, :], mods[0], geo)

    for l in range(depth):
        need_ctx = l < depth - 1
        mod3 = mods[l]
        proj = _matmul(h, _regroup_w_in(w_in, l), PROJ_TN)

        shift = Q_SCALE * HEAD_DIM * jnp.max(jnp.abs(g_q[l])) * jnp.max(jnp.abs(g_k[l]))
        bounded = 2.0 * shift <= ATTN_SAFE_EXPONENT
        shift_row = jnp.full((1, MXU_WIDTH - HEAD_DIM), shift, F32)
        qt, kh, vt = _qk_prep(proj, cos_t, sin_t, _deinterleave_vec(g_q[l]), _deinterleave_vec(g_k[l]), shift_row, geo)
        ya = (_attention(qt, kh, vt, proj, geo, ctx_len, seq, geo.lt, bounded),)
        if need_ctx:
            ya = (_attention(qt, kh, vt, proj, geo, 0, ctx_len, ctx_len, bounded), *ya)

        xbc, dt = _ssd_conv(proj, ssd_conv_w[l], ssd_conv_b[l][None, :], _pad_lanes(ssd_dt_bias[l].reshape(-1)), geo)
        a_log_row = _pad_lanes(ssd_a_log[l].reshape(-1))
        ysf = _ssd_scan(xbc, dt, a_log_row, e_fwd, geo, False)
        skip_row = jnp.repeat(ssd_d[l], SSD_HEAD_DIM)[None, :]
        yb = _ssd_scan(xbc, dt, a_log_row, e_bwd, geo, True, (ysf, proj, skip_row, ssd_norm_g[l][None, :]))

        ogf = _gla_scan(proj, _forget_weight(gla_w_f2[l, 0], 0), gla_b_f[l, 0][None, :], geo, False)
        yg = _gla_scan(proj, _forget_weight(gla_w_f2[l, 1], 1), gla_b_f[l, 1][None, :], geo, True,
                       (ogf, gla_norm_g[l][None, :]))

        nxt = (g_pre[l + 1][None, :], mods[l + 1]) if need_ctx else None
        x_new, h = _merge_out((ya, yb, yg, sc_conv_w[l]), proj, w_branch_bf, w_out_bf, l, tokens, mod3,
                              g_post[l][None, :], geo, need_ctx, nxt)
        tokens = (x_new,)

    return x_new.reshape(batch, seq, d)
```
